```python
import math
import jax
import jax.numpy as jnp
from jax import lax
import numpy as np

D_MODEL = 1024
BATCH = 2
SEQ = 8192
DEPTH = 1
DEC_BATCH = 32
DEC_SEQ = 1
PAST_LEN = 8192
PAGE_SIZE = 128

N_MEM = 256
NSA_HEADS = 8
NSA_KV_HEADS = 2
NSA_GROUP = NSA_HEADS // NSA_KV_HEADS
HEAD_DIM = 64
CMP_BLOCK = 32
CMP_STRIDE = 16
SEL_BLOCK = 64
SEL_TOP = 16
WINDOW = 512
Q_BLOCK = 128
N_KV_SLOTS = 4
FORCE_BONUS = 1e4
GLA_HEADS = 4
GLA_DK = 64
GLA_DV = 128
GLA_GATE_RANK = 16
GLA_GATE_TAU = 16.0
GLA_CHUNK = 64
X_HEADS = 4
X_HEAD_DIM = 128
N_BUCKETS = 32
MAX_DISTANCE = 128
D_FF = 2816
CONV_W = 3
N_BRANCH = 3
EPS = 1e-6
NEG_INF = -1e30
TINY = 1e-30

NSA_QW = NSA_HEADS * HEAD_DIM
NSA_KVW = 6 * NSA_KV_HEADS * HEAD_DIM
NSA_GW = 3 * NSA_HEADS
GLA_KW = GLA_HEADS * GLA_DK
GLA_VW = GLA_HEADS * GLA_DV
X_W = X_HEADS * X_HEAD_DIM
IN_SIZES = (NSA_QW, NSA_KVW, NSA_GW, GLA_KW, GLA_KW, GLA_VW, GLA_GATE_RANK, GLA_VW, X_W, N_BRANCH * D_MODEL)
D_IN = sum(IN_SIZES)

kernel_name = 'nsa_gla_memory_hybrid_step'


def _rmsnorm(x, g):
    xf = x.astype(jnp.float32)
    y = xf * lax.rsqrt(jnp.mean(xf * xf, axis=-1, keepdims=True) + EPS)
    return (y * g.astype(jnp.float32)).astype(x.dtype)


def _t5_bucket(rel):
    n = jnp.maximum(rel, 0)
    max_exact = N_BUCKETS // 2
    nf = jnp.maximum(n, 1).astype(jnp.float32)
    large = max_exact + (jnp.log(nf / max_exact) / math.log(MAX_DISTANCE / max_exact)
                         * (N_BUCKETS - max_exact)).astype(jnp.int32)
    return jnp.where(n < max_exact, n, jnp.minimum(large, N_BUCKETS - 1))


def _masked_softmax(s, mask, axis):
    s = jnp.where(mask, s, NEG_INF)
    m = jnp.max(s, axis=axis, keepdims=True)
    p = jnp.where(mask, jnp.exp(s - m), 0.0)
    return p / jnp.maximum(jnp.sum(p, axis=axis, keepdims=True), TINY)


def _compress(rows, pe, w1, w2):
    b, length, nkv, dh = rows.shape
    n_chunks = length // CMP_STRIDE
    n_cmp = (length - CMP_BLOCK) // CMP_STRIDE + 1
    chunks = rows[:, :n_chunks * CMP_STRIDE].reshape(b, n_chunks, CMP_STRIDE, nkv, dh)
    hid = None
    for r in range(CMP_BLOCK // CMP_STRIDE):
        sl = slice(r * CMP_STRIDE, (r + 1) * CMP_STRIDE)
        part = jnp.einsum('bcjkd,jde->bcke', chunks + pe[sl][None, None, :, None, :], w1[sl])[:, r:r + n_cmp]
        hid = part if hid is None else hid + part
    return jax.nn.gelu(hid) @ w2


def _cmp_sel_overlap(n_cmp, n_sel):
    cs = (jnp.arange(n_cmp) * CMP_STRIDE)[:, None]
    ss = (jnp.arange(n_sel) * SEL_BLOCK)[None, :]
    return ((cs < ss + SEL_BLOCK) & (cs + CMP_BLOCK > ss)).astype(jnp.float32)


def _nsa_attend(q, gates, k_cmp, v_cmp, k_sel, v_sel, k_win, v_win, q0, pw0, rel_bias):
    b, tq = q.shape[:2]
    length = k_sel.shape[1]
    lw = k_win.shape[1]
    n_cmp = k_cmp.shape[1]
    n_sel = -(-length // SEL_BLOCK)
    top = min(SEL_TOP, n_sel)
    qb = min(Q_BLOCK, tq)
    nqb = -(-tq // qb)
    tp = nqb * qb
    f32 = jnp.float32
    scale = HEAD_DIM ** -0.5

    def blocks(a):
        a = jnp.pad(a, [(0, 0), (0, tp - tq)] + [(0, 0)] * (a.ndim - 2))
        return jnp.moveaxis(a.reshape((b, nqb, qb) + a.shape[2:]), 1, 0)

    q_blk = blocks(q.reshape(b, tq, NSA_KV_HEADS, NSA_GROUP, HEAD_DIM))
    g_blk = blocks(gates.reshape(b, tq, NSA_KV_HEADS, NSA_GROUP, 3))
    pad_sel = n_sel * SEL_BLOCK - length
    ks_blk = jnp.pad(k_sel, ((0, 0), (0, pad_sel), (0, 0), (0, 0))).reshape(b, n_sel, SEL_BLOCK, NSA_KV_HEADS, HEAD_DIM)
    vs_blk = jnp.pad(v_sel, ((0, 0), (0, pad_sel), (0, 0), (0, 0))).reshape(b, n_sel, SEL_BLOCK, NSA_KV_HEADS, HEAD_DIM)
    kw_pad = jnp.pad(k_win, ((0, 0), (WINDOW, qb), (0, 0), (0, 0)))
    vw_pad = jnp.pad(v_win, ((0, 0), (WINDOW, qb), (0, 0), (0, 0)))
    kc = k_cmp.astype(f32)
    vc = v_cmp.astype(f32)
    cmp_end = jnp.arange(n_cmp) * CMP_STRIDE + (CMP_BLOCK - 1)
    sel_id = jnp.arange(n_sel)
    overlap = _cmp_sel_overlap(n_cmp, n_sel)
    bias_hg = rel_bias.astype(f32).reshape(N_BUCKETS, NSA_KV_HEADS, NSA_GROUP)
    bias_kg = jnp.transpose(bias_hg, (1, 0, 2))
    b_ix = jnp.arange(b)[:, None, None, None]
    k_ix = jnp.arange(NSA_KV_HEADS)[None, None, :, None]
    win_len = WINDOW + qb

    def one_block(args):
        n, qn, gn = args
        qs = q0 + n * qb
        tpos = qs + jnp.arange(qb)
        qf = qn.astype(f32) * scale
        s_c = jnp.einsum('bqkgd,bckd->bqkgc', qf, kc)
        bias_c = bias_hg[_t5_bucket(tpos[:, None] - cmp_end[None, :])]
        s_c = s_c + jnp.transpose(bias_c, (0, 2, 3, 1))[None]
        mask_c = (cmp_end[None, :] <= tpos[:, None])[None, :, None, None, :]
        p_c = _masked_softmax(s_c, mask_c, -1)
        o_c = jnp.einsum('bqkgc,bckd->bqkgd', p_c, vc)
        imp = jnp.einsum('bqkgc,cj->bqkj', p_c, overlap)
        tblk = (tpos // SEL_BLOCK)[:, None]
        valid = sel_id[None, :] * SEL_BLOCK <= tpos[:, None]
        forced = (sel_id[None, :] == 0) | (sel_id[None, :] == tblk) | (sel_id[None, :] == tblk - 1)
        score = jnp.where(valid[None, :, None, :],
                          imp + jnp.where(forced, FORCE_BONUS, 0.0)[None, :, None, :], NEG_INF)
        _, idx = lax.top_k(score, top)
        k_g = ks_blk[b_ix, idx, :, k_ix, :].astype(f32)
        v_g = vs_blk[b_ix, idx, :, k_ix, :].astype(f32)
        pos = idx[..., None] * SEL_BLOCK + jnp.arange(SEL_BLOCK)
        rel = tpos[None, :, None, None, None] - pos
        s_s = jnp.einsum('bqkgd,bqktsd->bqkgts', qf, k_g)
        bias_s = bias_kg[k_ix[..., None], _t5_bucket(rel)]
        s_s = s_s + jnp.moveaxis(bias_s, -1, 3)
        mask_s = ((rel >= 0) & (pos < length))[:, :, :, None]
        p_s = _masked_softmax(s_s, mask_s, (-2, -1))
        o_s = jnp.einsum('bqkgts,bqktsd->bqkgd', p_s, v_g)
        start = qs - pw0
        k_w = lax.dynamic_slice_in_dim(kw_pad, start, win_len, axis=1).astype(f32)
        v_w = lax.dynamic_slice_in_dim(vw_pad, start, win_len, axis=1).astype(f32)
        wpos = qs - WINDOW + jnp.arange(win_len)
        rel_w = tpos[:, None] - wpos[None, :]
        s_w = jnp.einsum('bqkgd,bskd->bqkgs', qf, k_w)
        s_w = s_w + jnp.transpose(bias_hg[_t5_bucket(rel_w)], (0, 2, 3, 1))[None]
        mask_w = ((rel_w >= 0) & (rel_w < WINDOW) & (wpos[None, :] >= pw0)
                  & (wpos[None, :] < pw0 + lw))[None, :, None, None, :]
        p_w = _masked_softmax(s_w, mask_w, -1)
        o_w = jnp.einsum('bqkgs,bskd->bqkgd', p_w, v_w)
        gf = gn.astype(f32)
        o = gf[..., 0:1] * o_c + gf[..., 1:2] * o_s + gf[..., 2:3] * o_w
        return o.astype(q.dtype)

    out = lax.map(one_block, (jnp.arange(nqb), q_blk, g_blk))
    return jnp.moveaxis(out, 0, 1).reshape(b, tp, NSA_HEADS, HEAD_DIM)[:, :tq]


def _gla(q, k, v, log_a, s0):
    b, t, nh, _ = q.shape
    dv = v.shape[-1]
    c = min(GLA_CHUNK, t)
    nc = -(-t // c)
    tp = nc * c

    def prep(a):
        a = jnp.pad(a.astype(jnp.float32), ((0, 0), (0, tp - t), (0, 0), (0, 0)))
        return jnp.transpose(a.reshape(b, nc, c, nh, a.shape[-1]), (1, 0, 3, 2, 4))

    causal = jnp.tril(jnp.ones((c, c), dtype=bool))[None, None, :, :, None]

    def step(s, inp):
        qc, kc, vc, ac = inp
        cb = jnp.cumsum(ac, axis=2)
        diff = cb[:, :, :, None, :] - cb[:, :, None, :, :]
        decay = jnp.exp(jnp.where(causal, diff, -jnp.inf))
        att = jnp.einsum('bhtd,bhsd,bhtsd->bhts', qc, kc, decay)
        o = jnp.einsum('bhtd,bhde->bhte', qc * jnp.exp(cb), s) + jnp.einsum('bhts,bhse->bhte', att, vc)
        last = cb[:, :, -1, :]
        s = jnp.exp(last)[..., None] * s + jnp.einsum('bhsd,bhse->bhde', kc * jnp.exp(last[:, :, None, :] - cb), vc)
        return s, o

    s_new, o = lax.scan(step, s0.astype(jnp.float32), (prep(q), prep(k), prep(v), prep(log_a)))
    o = jnp.transpose(o, (1, 0, 3, 2, 4)).reshape(b, tp, nh, dv)[:, :t]
    return o, s_new


def _conv_ffn(h, conv_past, w_up, conv_w, conv_b, w_down):
    t = h.shape[1]
    u, g = jnp.split(h @ w_up, 2, axis=-1)
    g_ext = jnp.concatenate([conv_past.astype(g.dtype), g], axis=1)
    gc = conv_b + sum(conv_w[j] * g_ext[:, j:j + t] for j in range(CONV_W))
    out = (jax.nn.gelu(gc) * u) @ w_down
    return out, g_ext[:, g_ext.shape[1] - (CONV_W - 1):]


def _memory_kv(mem, g_mem, w_mem_kv, g_x_k):
    b, m, _ = mem.shape
    kv = (_rmsnorm(mem, g_mem) @ w_mem_kv).reshape(b, m, 2, X_HEADS, X_HEAD_DIM)
    return jnp.stack([_rmsnorm(kv[:, :, 0], g_x_k), kv[:, :, 1]], axis=2)


def _layer(x, past_rows, win_past, gla_s0, conv_past, mem_kv, q0, p):
    b, t, _ = x.shape
    f32 = jnp.float32
    h = _rmsnorm(x, p['g_mix'])
    splits = [int(s) for s in np.cumsum(IN_SIZES)[:-1]]
    (nsa_q, nsa_kv, nsa_g, gla_q, gla_k, gla_v, gla_lr, gla_r, x_q, merge_g) = jnp.split(h @ p['w_in'], splits, axis=-1)

    g_k = p['g_nsa_k']
    q = _rmsnorm(nsa_q.reshape(b, t, NSA_HEADS, HEAD_DIM), p['g_nsa_q'])
    kv = nsa_kv.reshape(b, t, 6, NSA_KV_HEADS, HEAD_DIM)
    new_rows = jnp.stack([kv[:, :, 0], kv[:, :, 1], _rmsnorm(kv[:, :, 2], g_k[1]), kv[:, :, 3]], axis=2)
    new_win = jnp.stack([_rmsnorm(kv[:, :, 4], g_k[2]), kv[:, :, 5]], axis=2)
    rows = jnp.concatenate([past_rows.astype(x.dtype), new_rows], axis=1)
    win_all = jnp.concatenate([win_past.astype(x.dtype), new_win], axis=1)
    k_cmp = _rmsnorm(_compress(rows[:, :, 0], p['cmp_k_pe'], p['cmp_k_w1'], p['cmp_k_w2']), g_k[0])
    v_cmp = _compress(rows[:, :, 1], p['cmp_v_pe'], p['cmp_v_w1'], p['cmp_v_w2'])
    nsa_gates = jax.nn.sigmoid(nsa_g.reshape(b, t, NSA_HEADS, 3))
    o_nsa = _nsa_attend(q, nsa_gates, k_cmp, v_cmp, rows[:, :, 2], rows[:, :, 3],
                        win_all[:, :, 0], win_all[:, :, 1], q0, q0 - win_past.shape[1], p['rel_bias'])
    new_win_state = win_all[:, win_all.shape[1] - min(WINDOW, q0 + t):]

    gq = gla_q.reshape(b, t, GLA_HEADS, GLA_DK) * (GLA_DK ** -0.5)
    gk = gla_k.reshape(b, t, GLA_HEADS, GLA_DK)
    gv = gla_v.reshape(b, t, GLA_HEADS, GLA_DV)
    log_a = jax.nn.log_sigmoid((gla_lr @ p['w_gla_gate'] + p['b_gla_gate']).astype(f32)) / GLA_GATE_TAU
    o_gla, s_new = _gla(gq, gk, gv, log_a.reshape(b, t, GLA_HEADS, GLA_DK), gla_s0)
    o_gla = _rmsnorm(o_gla.astype(x.dtype), p['g_gla_o']).reshape(b, t, GLA_VW) * jax.nn.silu(gla_r)

    xq = _rmsnorm(x_q.reshape(b, t, X_HEADS, X_HEAD_DIM), p['g_x_q'])
    s_x = jnp.einsum('bthd,bmhd->bhtm', xq.astype(f32) * (X_HEAD_DIM ** -0.5), mem_kv[:, :, 0].astype(f32))
    o_x = jnp.einsum('bhtm,bmhd->bthd', jax.nn.softmax(s_x, axis=-1), mem_kv[:, :, 1].astype(f32))
    o_x = o_x.astype(x.dtype).reshape(b, t, X_W)

    mg = jax.nn.sigmoid(merge_g.reshape(b, t, N_BRANCH, D_MODEL))
    merged = (mg[:, :, 0] * (o_nsa.reshape(b, t, NSA_QW) @ p['w_nsa_out'])
              + mg[:, :, 1] * (o_gla @ p['w_gla_out'])
              + mg[:, :, 2] * (o_x @ p['w_x_out']))
    x1 = x + merged @ p['w_o']

    ffn, conv_state = _conv_ffn(_rmsnorm(x1, p['g_ffn']), conv_past, p['w_up'], p['conv_w'], p['conv_b'], p['w_down'])
    y = x1 + ffn
    return y, new_rows, new_win_state, s_new.astype(gla_s0.dtype), conv_state


def setup_inputs(seed: int = 0) -> dict:
    key = jax.random.key(seed)
    keys = iter(jax.random.split(key, 48))
    f32 = jnp.float32

    def nrm(shape, scale):
        return jax.random.normal(next(keys), shape, f32) * scale

    def gain(shape):
        return 1.0 + 0.05 * jax.random.normal(next(keys), shape, f32)

    n_pages = PAST_LEN // PAGE_SIZE
    n_used = DEC_BATCH * n_pages
    n_phys = n_used + max(1, n_used // 4)
    win_buf = min(WINDOW, PAST_LEN)
    d = D_MODEL
    x_prompt = nrm((BATCH, SEQ, d), 1.0)
    x_sample = nrm((DEC_BATCH, DEC_SEQ, d), 1.0)
    cache_kv = nrm((n_phys, PAGE_SIZE, N_KV_SLOTS, NSA_KV_HEADS, HEAD_DIM), 1.0)
    cache_win = nrm((DEC_BATCH, win_buf, 2, NSA_KV_HEADS, HEAD_DIM), 1.0)
    state_gla = nrm((DEC_BATCH, GLA_HEADS, GLA_DK, GLA_DV), 1.0)
    state_conv = nrm((DEC_BATCH, CONV_W - 1, D_FF), 1.0)
    cache_mem = nrm((DEC_BATCH, N_MEM, 2, X_HEADS, X_HEAD_DIM), 1.0)
    page_table = jax.random.permutation(next(keys), n_phys)[:n_used].reshape(DEC_BATCH, n_pages).astype(jnp.int32)
    mem_prompt = nrm((BATCH, N_MEM, d), 1.0)
    return {
        'x_prompt': x_prompt, 'x_sample': x_sample, 'cache_kv': cache_kv, 'cache_win': cache_win,
        'state_gla': state_gla, 'state_conv': state_conv, 'cache_mem': cache_mem,
        'page_table': page_table, 'mem_prompt': mem_prompt,
        'g_mix': gain((d,)),
        'w_in': nrm((d, D_IN), d ** -0.5),
        'g_nsa_q': gain((HEAD_DIM,)),
        'g_nsa_k': gain((3, HEAD_DIM)),
        'cmp_k_pe': nrm((CMP_BLOCK, HEAD_DIM), 0.1),
        'cmp_k_w1': nrm((CMP_BLOCK, HEAD_DIM, HEAD_DIM), (CMP_BLOCK * HEAD_DIM) ** -0.5),
        'cmp_k_w2': nrm((HEAD_DIM, HEAD_DIM), HEAD_DIM ** -0.5),
        'cmp_v_pe': nrm((CMP_BLOCK, HEAD_DIM), 0.1),
        'cmp_v_w1': nrm((CMP_BLOCK, HEAD_DIM, HEAD_DIM), (CMP_BLOCK * HEAD_DIM) ** -0.5),
        'cmp_v_w2': nrm((HEAD_DIM, HEAD_DIM), HEAD_DIM ** -0.5),
        'rel_bias': nrm((N_BUCKETS, NSA_HEADS), 0.5),
        'w_gla_gate': nrm((GLA_GATE_RANK, GLA_KW), GLA_GATE_RANK ** -0.5),
        'b_gla_gate': nrm((GLA_KW,), 0.1),
        'g_gla_o': gain((GLA_DV,)),
        'g_mem': gain((d,)),
        'w_mem_kv': nrm((d, 2 * X_W), d ** -0.5),
        'g_x_q': gain((X_HEAD_DIM,)),
        'g_x_k': gain((X_HEAD_DIM,)),
        'w_nsa_out': nrm((NSA_QW, d), NSA_QW ** -0.5),
        'w_gla_out': nrm((GLA_VW, d), GLA_VW ** -0.5),
        'w_x_out': nrm((X_W, d), X_W ** -0.5),
        'w_o': nrm((d, d), d ** -0.5),
        'g_ffn': gain((d,)),
        'w_up': nrm((d, 2 * D_FF), d ** -0.5),
        'conv_w': nrm((CONV_W, D_FF), CONV_W ** -0.5),
        'conv_b': nrm((D_FF,), 0.02),
        'w_down': nrm((D_FF, d), D_FF ** -0.5),
    }


def reference(x_prompt, x_sample, cache_kv, cache_win, state_gla, state_conv, cache_mem, page_table, mem_prompt,
              g_mix, w_in, g_nsa_q, g_nsa_k, cmp_k_pe, cmp_k_w1, cmp_k_w2, cmp_v_pe, cmp_v_w1, cmp_v_w2,
              rel_bias, w_gla_gate, b_gla_gate, g_gla_o, g_mem, w_mem_kv, g_x_q, g_x_k,
              w_nsa_out, w_gla_out, w_x_out, w_o, g_ffn, w_up, conv_w, conv_b, w_down):
    p = dict(g_mix=g_mix, w_in=w_in, g_nsa_q=g_nsa_q, g_nsa_k=g_nsa_k,
             cmp_k_pe=cmp_k_pe, cmp_k_w1=cmp_k_w1, cmp_k_w2=cmp_k_w2,
             cmp_v_pe=cmp_v_pe, cmp_v_w1=cmp_v_w1, cmp_v_w2=cmp_v_w2,
             rel_bias=rel_bias, w_gla_gate=w_gla_gate, b_gla_gate=b_gla_gate, g_gla_o=g_gla_o,
             g_x_q=g_x_q, w_nsa_out=w_nsa_out, w_gla_out=w_gla_out, w_x_out=w_x_out, w_o=w_o,
             g_ffn=g_ffn, w_up=w_up, conv_w=conv_w, conv_b=conv_b, w_down=w_down)
    dt = x_prompt.dtype
    bp = x_prompt.shape[0]
    db = x_sample.shape[0]

    mem_kv_p = _memory_kv(mem_prompt, g_mem, w_mem_kv, g_x_k)
    y_p, rows_p, win_p, gla_p, conv_p = _layer(
        x_prompt,
        jnp.zeros((bp, 0, N_KV_SLOTS, NSA_KV_HEADS, HEAD_DIM), dt),
        jnp.zeros((bp, 0, 2, NSA_KV_HEADS, HEAD_DIM), dt),
        jnp.zeros((bp, GLA_HEADS, GLA_DK, GLA_DV), dt),
        jnp.zeros((bp, CONV_W - 1, D_FF), dt),
        mem_kv_p, 0, p)

    n_pages = page_table.shape[1]
    past_len = n_pages * cache_kv.shape[1]
    past_rows = cache_kv[page_table].reshape((db, past_len) + cache_kv.shape[2:])
    y_s, rows_s, win_s, gla_s, conv_s = _layer(
        x_sample, past_rows, cache_win, state_gla, state_conv, cache_mem, past_len, p)

    return (y_p, y_s, rows_p, win_p, gla_p, conv_p, mem_kv_p, rows_s, win_s, gla_s, conv_s)
```

```python
import functools
import math

import numpy as np
import jax
import jax.numpy as jnp
from jax import lax
from jax.experimental import pallas as pl
from jax.experimental.pallas import tpu as pltpu

F32 = jnp.float32
BF16 = jnp.bfloat16

NSA_HEADS = 8
NSA_KV = 2
NSA_GROUP = 4
HEAD_DIM = 64
CMP_BLOCK = 32
CMP_STRIDE = 16
SEL_BLOCK = 64
SEL_TOP = 16
WINDOW = 512
Q_BLOCK = 128
FORCE_BONUS = 1e4
GLA_HEADS = 4
GLA_DK = 64
GLA_DV = 128
GLA_RANK = 16
GLA_TAU = 16.0
GLA_CHUNK = 64
X_HEADS = 4
X_DIM = 128
N_BUCKETS = 32
MAX_DISTANCE = 128
EPS = 1e-6
NEG = -1e30
TINY = 1e-30
SEL_PENALTY = -1e9
MASKED_BELOW = -5e29
EXP_CLAMP = 80.0

IN_SIZES = (512, 768, 24, 256, 256, 512, 16, 512, 512, 3072)
PAD_SIZES = (512, 768, 128, 256, 256, 512, 128, 512, 512, 3072)
PAD_OFFS = tuple(int(v) for v in np.cumsum((0,) + PAD_SIZES))
D_IN_PAD = PAD_OFFS[-1]
GLA_IN_W = 256 + 256 + 512 + 128 + 512

VMEM_LIMIT = 56 * 1024 * 1024
KEY_TILE = 128


def _cparams(n_axes):
    return pltpu.CompilerParams(dimension_semantics=("arbitrary",) * n_axes, vmem_limit_bytes=VMEM_LIMIT)


def _dot(a, b):
    return jnp.dot(a, b, preferred_element_type=F32)


def _dot_nt(a, b):
    return lax.dot_general(a, b, (((1,), (1,)), ((), ())), preferred_element_type=F32)


def _dot_tn(a, b):
    return lax.dot_general(a, b, (((0,), (0,)), ((), ())), preferred_element_type=F32)


def _split_dot(x, m):
    hi = x.astype(BF16)
    lo = (x - hi.astype(F32)).astype(BF16)
    return _dot(hi, m) + _dot(lo, m)


def _rms(x, g):
    return x * lax.rsqrt(jnp.mean(x * x, axis=-1, keepdims=True) + EPS) * g


def _group_rms(x, pmat, gsize, g):
    ss = _split_dot(x * x, pmat)
    return x * lax.rsqrt(ss * (1.0 / gsize) + EPS) * g


def _gelu(x):
    return 0.5 * x * (1.0 + jnp.tanh(math.sqrt(2.0 / math.pi) * (x + 0.044715 * (x * x * x))))


def _sigmoid(x):
    return 1.0 / (1.0 + jnp.exp(-x))


def _block_ones(n, gsize):
    i = np.arange(n) // gsize
    return jnp.asarray((i[:, None] == i[None, :]).astype(np.float32), dtype=BF16)


def _proj_kernel(x_ref, gmix_ref, w_ref, gq_ref, gk1_ref, gk2_ref, gxq_ref, p64_ref, p128_ref,
                 rows_ref, win_ref, qn_ref, katt_ref, gates_ref, gla_ref, xq_ref, mg_ref):
    x = x_ref[0]
    h = _rms(x, gmix_ref[...]).astype(BF16)
    o = PAD_OFFS

    def seg(i):
        return _dot(h, w_ref[:, o[i]:o[i + 1]])

    p64 = p64_ref[...]
    p64s = p64_ref[0:128, 0:128]
    qn = _group_rms(seg(0), p64, HEAD_DIM, gq_ref[...]) * (HEAD_DIM ** -0.5)
    qn_ref[0] = qn.astype(BF16)

    kv = seg(1)
    k_sel = _group_rms(kv[:, 256:384], p64s, HEAD_DIM, gk1_ref[...])
    k_win = _group_rms(kv[:, 512:640], p64s, HEAD_DIM, gk2_ref[...])
    rows_ref[0, :, 0:256] = kv[:, 0:256]
    rows_ref[0, :, 256:384] = k_sel
    rows_ref[0, :, 384:512] = kv[:, 384:512]
    win_ref[0, :, 0:128] = k_win
    win_ref[0, :, 128:256] = kv[:, 640:768]
    katt_ref[0, :, 0:128] = k_sel.astype(BF16)
    katt_ref[0, :, 128:256] = kv[:, 384:512].astype(BF16)
    katt_ref[0, :, 256:384] = k_win.astype(BF16)
    katt_ref[0, :, 384:512] = kv[:, 640:768].astype(BF16)

    gates_ref[0] = _sigmoid(seg(2))
    gla_ref[0, :, 0:256] = seg(3) * (GLA_DK ** -0.5)
    gla_ref[0, :, 256:512] = seg(4)
    gla_ref[0, :, 512:1024] = seg(5)
    gla_ref[0, :, 1024:1152] = seg(6)
    gla_ref[0, :, 1152:1664] = seg(7)
    xq = _group_rms(seg(8), p128_ref[...], X_DIM, gxq_ref[...]) * (X_DIM ** -0.5)
    xq_ref[0] = xq.astype(BF16)
    mg_ref[0] = _sigmoid(seg(9)).astype(BF16)


def _proj_in(x, g_mix, w_pad, gq, gk1, gk2, gxq, p64, p128):
    b, t, d = x.shape
    tm = min(256, t)
    assert t % tm == 0
    widths = (512, 256, 512, 512, 128, GLA_IN_W, 512, 3072)
    dtypes = (F32, F32, BF16, BF16, F32, F32, BF16, BF16)
    const = lambda shape: pl.BlockSpec(shape, lambda i, j: (0,) * len(shape))
    return pl.pallas_call(
        _proj_kernel,
        grid=(b, t // tm),
        in_specs=[pl.BlockSpec((1, tm, d), lambda i, j: (i, j, 0)),
                  const((1, d)), const((d, D_IN_PAD)), const((1, 512)), const((1, 128)), const((1, 128)),
                  const((1, 512)), const((512, 512)), const((512, 512))],
        out_specs=[pl.BlockSpec((1, tm, w), lambda i, j: (i, j, 0)) for w in widths],
        out_shape=[jax.ShapeDtypeStruct((b, t, w), dt) for w, dt in zip(widths, dtypes)],
        compiler_params=_cparams(2),
    )(x, g_mix, w_pad, gq, gk1, gk2, gxq, p64, p128)


def _memkv_kernel(m_ref, g_ref, w_ref, gk_ref, p128_ref, o_ref):
    h = _rms(m_ref[0], g_ref[...]).astype(BF16)
    kv = _dot(h, w_ref[...])
    o_ref[0, :, 0:512] = _group_rms(kv[:, 0:512], p128_ref[...], X_DIM, gk_ref[...])
    o_ref[0, :, 512:1024] = kv[:, 512:1024]


def _memory_kv(mem, g_mem, w_mem, gxk, p128):
    b, m, d = mem.shape
    const = lambda shape: pl.BlockSpec(shape, lambda i: (0,) * len(shape))
    return pl.pallas_call(
        _memkv_kernel,
        grid=(b,),
        in_specs=[pl.BlockSpec((1, m, d), lambda i: (i, 0, 0)), const((1, d)), const((d, 1024)),
                  const((1, 512)), const((512, 512))],
        out_specs=pl.BlockSpec((1, m, 1024), lambda i: (i, 0, 0)),
        out_shape=jax.ShapeDtypeStruct((b, m, 1024), F32),
        compiler_params=_cparams(1),
    )(mem, g_mem, w_mem, gxk, p128)


def _compress_core(load_j, n_chunks, pe_ref, w1_ref, w2_ref, gk0_ref, p64_ref):
    outs = []
    for kind in range(2):
        acc0 = jnp.zeros((n_chunks, 128), F32)
        acc1 = jnp.zeros((n_chunks, 128), F32)
        for j in range(CMP_STRIDE):
            xj = load_j(kind, j)
            acc0 = acc0 + _dot((xj + pe_ref[kind, 0, j]).astype(BF16), w1_ref[kind, 0, j])
            acc1 = acc1 + _dot((xj + pe_ref[kind, 1, j]).astype(BF16), w1_ref[kind, 1, j])
        hid = acc0 + pltpu.roll(acc1, n_chunks - 1, 0)
        outs.append(_dot(_gelu(hid).astype(BF16), w2_ref[kind]))
    row = lax.broadcasted_iota(jnp.int32, (n_chunks, 128), 0)
    live = row < n_chunks - 1
    kc = _group_rms(outs[0], p64_ref[0:128, 0:128], HEAD_DIM, gk0_ref[...])
    return jnp.where(live, kc, 0.0), jnp.where(live, outs[1], 0.0)


def _compress_kernel(rk_ref, rv_ref, pe_ref, w1_ref, w2_ref, gk0_ref, p64_ref, kc_ref, vc_ref, *, n_chunks):
    refs = (rk_ref, rv_ref)
    load_j = lambda kind, j: refs[kind][0, pl.ds(j, n_chunks, stride=CMP_STRIDE), :]
    kc, vc = _compress_core(load_j, n_chunks, pe_ref, w1_ref, w2_ref, gk0_ref, p64_ref)
    kc_ref[0] = kc.astype(BF16)
    vc_ref[0] = vc.astype(BF16)


def _compress(rows, pe, w1, w2, gk0, p64):
    b, t, _ = rows.shape
    n_chunks = t // CMP_STRIDE
    const = lambda shape: pl.BlockSpec(shape, lambda i: (0,) * len(shape))
    return pl.pallas_call(
        functools.partial(_compress_kernel, n_chunks=n_chunks),
        grid=(b,),
        in_specs=[pl.BlockSpec((1, t, 128), lambda i: (i, 0, 0)), pl.BlockSpec((1, t, 128), lambda i: (i, 0, 1)),
                  const((2, 2, CMP_STRIDE, 1, 128)), const((2, 2, CMP_STRIDE, 128, 128)), const((2, 128, 128)),
                  const((1, 128)), const((512, 512))],
        out_specs=[pl.BlockSpec((1, n_chunks, 128), lambda i: (i, 0, 0))] * 2,
        out_shape=[jax.ShapeDtypeStruct((b, n_chunks, 128), BF16)] * 2,
        compiler_params=_cparams(1),
    )(rows, rows, pe, w1, w2, gk0, p64)


def _masked_softmax_rows(s):
    valid = s > MASKED_BELOW
    m = jnp.max(s, axis=-1, keepdims=True)
    p = jnp.where(valid, jnp.exp(s - m), 0.0)
    return p / jnp.maximum(jnp.sum(p, axis=-1, keepdims=True), TINY)


def _select_blocks(score, top):
    lanes = lax.broadcasted_iota(jnp.int32, score.shape, 1).astype(F32)
    sel = jnp.zeros(score.shape, jnp.bool_)
    for _ in range(top):
        mx = jnp.max(score, axis=-1, keepdims=True)
        idx = jnp.min(jnp.where(score == mx, lanes, 1e9), axis=-1, keepdims=True)
        hit = lanes == idx
        sel = jnp.logical_or(sel, hit)
        score = jnp.where(hit, -3e38, score)
    return sel


def _flash_step(q, kt, vt, extra, carry):
    m, l, acc = carry
    s = _dot_nt(q, kt) + extra
    m_new = jnp.maximum(m, jnp.max(s, axis=-1, keepdims=True))
    alpha = jnp.exp(m - m_new)
    p = jnp.exp(s - m_new)
    l = alpha * l + jnp.sum(p, axis=-1, keepdims=True)
    acc = alpha * acc + _dot(p.astype(BF16), vt)
    return m_new, l, acc


def _attn_kernel(q_ref, kc_ref, vc_ref, ks_ref, vs_ref, kw_ref, vw_ref, gt_ref, tbl_ref, acmp_ref, ov_ref,
                 o_ref, *, top):
    k = pl.program_id(1)
    n = pl.program_id(2)
    rows4 = NSA_GROUP * Q_BLOCK
    qb = q_ref[0]
    zero = jnp.zeros((Q_BLOCK, HEAD_DIM), BF16)
    parts = []
    for g in range(NSA_GROUP):
        piece = qb[:, HEAD_DIM * g:HEAD_DIM * (g + 1)]
        parts.append(jnp.where(k == 0, jnp.concatenate([piece, zero], axis=1),
                               jnp.concatenate([zero, piece], axis=1)))
    q4 = jnp.concatenate(parts, axis=0)

    def half(x):
        return jnp.where(k == 0, x[:, 0:HEAD_DIM], x[:, HEAD_DIM:2 * HEAD_DIM])

    nc = kc_ref.shape[1]
    far = tbl_ref[0, 2][:, 0:1]
    r32 = lax.broadcasted_iota(jnp.int32, (32, nc), 0)
    c32 = lax.broadcasted_iota(jnp.int32, (32, nc), 1)
    onehot = jnp.where((c32 - 8 * n + 9) == (r32 & 15), 1.0, 0.0).astype(BF16)
    band = _dot(acmp_ref[0], onehot)
    cp = lax.broadcasted_iota(jnp.int32, (rows4, nc), 1) - 8 * n + 9
    s_c = _dot_nt(q4, kc_ref[0]) + jnp.where(cp < 0, far, jnp.where(cp > 15, NEG, band))
    p_c = _masked_softmax_rows(s_c).astype(BF16)
    o_c = half(_dot(p_c, vc_ref[0]))
    imp4 = _dot(p_c, ov_ref[...])
    imp = imp4[0:128] + imp4[128:256] + imp4[256:384] + imp4[384:512]

    jj = lax.broadcasted_iota(jnp.int32, (Q_BLOCK, 128), 1)
    tpos = n * Q_BLOCK + lax.broadcasted_iota(jnp.int32, (Q_BLOCK, 128), 0)
    tblk = tpos // SEL_BLOCK
    forced = (jj == 0) | (jj == tblk) | (jj == tblk - 1)
    score = jnp.where(jj * SEL_BLOCK <= tpos, imp + jnp.where(forced, FORCE_BONUS, 0.0), NEG)
    sel = _select_blocks(score, top)
    pen = jnp.where(sel, 0.0, SEL_PENALTY).astype(BF16)
    pen4 = jnp.concatenate([pen] * NSA_GROUP, axis=0)

    init = (jnp.full((rows4, 1), NEG, F32), jnp.zeros((rows4, 1), F32), jnp.zeros((rows4, 128), F32))

    er = lax.broadcasted_iota(jnp.int32, (128, KEY_TILE), 0)
    ec = lax.broadcasted_iota(jnp.int32, (128, KEY_TILE), 1) // SEL_BLOCK

    def sel_body(j, carry):
        start = pl.multiple_of(j * KEY_TILE, KEY_TILE)
        expand = jnp.where(er == ec + (KEY_TILE // SEL_BLOCK) * j, 1.0, 0.0).astype(BF16)
        extra = _dot(pen4, expand) + tbl_ref[0, jnp.minimum(n - j, 2)]
        return _flash_step(q4, ks_ref[0, pl.ds(start, KEY_TILE), :], vs_ref[0, pl.ds(start, KEY_TILE), :],
                           extra, carry)

    _, l_s, acc_s = lax.fori_loop(0, n + 1, sel_body, init)
    o_s = half(acc_s / l_s)

    carry = init
    for d, ti in enumerate((0, 1, 2, 2, 3)):
        j = n - d
        start = pl.multiple_of(jnp.maximum(j, 0) * KEY_TILE, KEY_TILE)
        extra = tbl_ref[0, ti] + jnp.where(j < 0, NEG, 0.0)
        carry = _flash_step(q4, kw_ref[0, pl.ds(start, KEY_TILE), :], vw_ref[0, pl.ds(start, KEY_TILE), :],
                            extra, carry)
    o_w = half(carry[2] / carry[1])

    gt = gt_ref[0]
    gtk = jnp.where(k == 0, gt[:, 0:12], gt[:, 12:24])
    outs = []
    for g in range(NSA_GROUP):
        r = slice(Q_BLOCK * g, Q_BLOCK * (g + 1))
        outs.append(gtk[:, 3 * g:3 * g + 1] * o_c[r] + gtk[:, 3 * g + 1:3 * g + 2] * o_s[r]
                    + gtk[:, 3 * g + 2:3 * g + 3] * o_w[r])
    o_ref[0] = jnp.concatenate(outs, axis=1).astype(BF16)


def _nsa_prompt(qn, kc, vc, katt, gates, tbl, acmp, ov):
    b, t, _ = qn.shape
    nqb = t // Q_BLOCK
    nc = kc.shape[1]
    top = min(SEL_TOP, -(-t // SEL_BLOCK))
    kspec = lambda c: pl.BlockSpec((1, t, 128), lambda i, k, n: (i, 0, c))
    return pl.pallas_call(
        functools.partial(_attn_kernel, top=top),
        grid=(b, NSA_KV, nqb),
        in_specs=[pl.BlockSpec((1, Q_BLOCK, 256), lambda i, k, n: (i, n, k)),
                  pl.BlockSpec((1, nc, 128), lambda i, k, n: (i, 0, 0)),
                  pl.BlockSpec((1, nc, 128), lambda i, k, n: (i, 0, 0)),
                  kspec(0), kspec(1), kspec(2), kspec(3),
                  pl.BlockSpec((1, Q_BLOCK, 128), lambda i, k, n: (i, n, 0)),
                  pl.BlockSpec((1, 4, 512, 128), lambda i, k, n: (k, 0, 0, 0)),
                  pl.BlockSpec((1, 512, 32), lambda i, k, n: (k, 0, 0)),
                  pl.BlockSpec((nc, 128), lambda i, k, n: (0, 0))],
        out_specs=pl.BlockSpec((1, Q_BLOCK, 256), lambda i, k, n: (i, n, k)),
        out_shape=jax.ShapeDtypeStruct((b, t, 512), BF16),
        compiler_params=_cparams(3),
    )(qn, kc, vc, katt, katt, katt, katt, gates, tbl, acmp, ov)


def _log_sigmoid(z):
    return jnp.minimum(z, 0.0) - jnp.log1p(jnp.exp(-jnp.abs(z)))


def _gla_kernel(x_ref, s0_ref, wg_ref, bg_ref, ggo_ref, tri_ref, o_ref, st_ref, s_scr, *, n_chunks):
    @pl.when(pl.program_id(1) == 0)
    def _():
        s_scr[...] = s0_ref[0]

    c_len = GLA_CHUNK
    tri = tri_ref[...]
    ti = lax.broadcasted_iota(jnp.int32, (c_len, c_len), 0)
    si = lax.broadcasted_iota(jnp.int32, (c_len, c_len), 1)
    causal = si <= ti
    for c in range(n_chunks):
        rs = slice(c_len * c, c_len * (c + 1))
        q = x_ref[0, rs, 0:256]
        kk = x_ref[0, rs, 256:512]
        v = x_ref[0, rs, 512:1024]
        lr = x_ref[0, rs, 1024:1152]
        r = x_ref[0, rs, 1152:1664]
        la = _log_sigmoid(_dot(lr.astype(BF16), wg_ref[...]) + bg_ref[...]) * (1.0 / GLA_TAU)
        a1 = la.astype(BF16)
        r1 = la - a1.astype(F32)
        a2 = r1.astype(BF16)
        a3 = (r1 - a2.astype(F32)).astype(BF16)
        cb = _dot(tri, a1) + _dot(tri, a2) + _dot(tri, a3)
        last = cb[c_len - 1:c_len, :]
        mid = cb[c_len // 2:c_len // 2 + 1, :]
        qe = (q * jnp.exp(cb)).astype(BF16)
        qa = (q * jnp.exp(jnp.minimum(cb - mid, EXP_CLAMP))).astype(BF16)
        kb = (kk * jnp.exp(jnp.minimum(mid - cb, EXP_CLAMP))).astype(BF16)
        ke = (kk * jnp.exp(last - cb)).astype(BF16)
        dec = jnp.exp(last)
        for h in range(GLA_HEADS):
            ks = slice(GLA_DK * h, GLA_DK * (h + 1))
            vs = slice(GLA_DV * h, GLA_DV * (h + 1))
            att = jnp.where(causal, _dot_nt(qa[:, ks], kb[:, ks]), 0.0)
            vh = v[:, vs].astype(BF16)
            st = s_scr[h]
            o = _dot(att.astype(BF16), vh) + _dot_nt(qe[:, ks], st.astype(BF16))
            s_scr[h] = st * dec[:, ks] + _dot_tn(vh, ke[:, ks])
            on = _rms(o, ggo_ref[...])
            rh = r[:, vs]
            o_ref[0, rs, vs] = (on * (rh * _sigmoid(rh))).astype(BF16)
    st_ref[0] = s_scr[...]


def _gla_prompt(gla_in, s0t, wg, bg, ggo, tri):
    b, t, w = gla_in.shape
    ct = min(256, t)
    assert t % ct == 0 and ct % GLA_CHUNK == 0
    const = lambda shape: pl.BlockSpec(shape, lambda i, j: (0,) * len(shape))
    return pl.pallas_call(
        functools.partial(_gla_kernel, n_chunks=ct // GLA_CHUNK),
        grid=(b, t // ct),
        in_specs=[pl.BlockSpec((1, ct, w), lambda i, j: (i, j, 0)),
                  pl.BlockSpec((1, GLA_HEADS, GLA_DV, GLA_DK), lambda i, j: (i, 0, 0, 0)),
                  const((128, 256)), const((1, 256)), const((1, 128)), const((GLA_CHUNK, GLA_CHUNK))],
        out_specs=[pl.BlockSpec((1, ct, 512), lambda i, j: (i, j, 0)),
                   pl.BlockSpec((1, GLA_HEADS, GLA_DV, GLA_DK), lambda i, j: (i, 0, 0, 0))],
        out_shape=[jax.ShapeDtypeStruct((b, t, 512), BF16),
                   jax.ShapeDtypeStruct((b, GLA_HEADS, GLA_DV, GLA_DK), F32)],
        scratch_shapes=[pltpu.VMEM((GLA_HEADS, GLA_DV, GLA_DK), F32)],
        compiler_params=_cparams(2),
    )(gla_in, s0t, wg, bg, ggo, tri)


def _gla_step_kernel(q_ref, k_ref, lr_ref, v_ref, r_ref, s_ref, wgt_ref, bgt_ref, ggo_ref, o_ref, sn_ref):
    lr = lr_ref[0]
    for h in range(GLA_HEADS):
        z = jnp.sum(wgt_ref[h] * lr, axis=-1, keepdims=True) + bgt_ref[h]
        a = jnp.exp(_log_sigmoid(z) * (1.0 / GLA_TAU))
        s0 = s_ref[0, h]
        kh = k_ref[0, h]
        qh = q_ref[0, h]
        vh = v_ref[0, h]
        sn_ref[0, h] = a * s0 + kh * vh
        o = jnp.sum((qh * a) * s0, axis=0, keepdims=True) + jnp.sum(qh * kh, axis=0, keepdims=True) * vh
        on = _rms(o, ggo_ref[...])
        rh = r_ref[0, h]
        o_ref[0, h] = on * (rh * _sigmoid(rh))


def _gla_step(q_col, k_col, lr, v_row, r_row, s0, wgt, bgt, ggo):
    b = q_col.shape[0]
    const = lambda shape: pl.BlockSpec(shape, lambda i: (0,) * len(shape))
    per_b = lambda shape: pl.BlockSpec((1,) + shape, lambda i: (i,) + (0,) * len(shape))
    return pl.pallas_call(
        _gla_step_kernel,
        grid=(b,),
        in_specs=[per_b((GLA_HEADS, GLA_DK, 1)), per_b((GLA_HEADS, GLA_DK, 1)), per_b((1, 128)),
                  per_b((GLA_HEADS, 1, GLA_DV)), per_b((GLA_HEADS, 1, GLA_DV)), per_b((GLA_HEADS, GLA_DK, GLA_DV)),
                  const((GLA_HEADS, GLA_DK, 128)), const((GLA_HEADS, GLA_DK, 1)), const((1, 128))],
        out_specs=[per_b((GLA_HEADS, 1, GLA_DV)), per_b((GLA_HEADS, GLA_DK, GLA_DV))],
        out_shape=[jax.ShapeDtypeStruct((b, GLA_HEADS, 1, GLA_DV), F32),
                   jax.ShapeDtypeStruct((b, GLA_HEADS, GLA_DK, GLA_DV), F32)],
        compiler_params=_cparams(1),
    )(q_col, k_col, lr, v_row, r_row, s0, wgt, bgt, ggo)


def _xatt_kernel(xq_ref, mem_ref, o_ref):
    for h in range(X_HEADS):
        ls = slice(X_DIM * h, X_DIM * (h + 1))
        kh = mem_ref[0, :, ls].astype(BF16)
        vh = mem_ref[0, :, 512 + X_DIM * h:512 + X_DIM * (h + 1)].astype(BF16)
        s = _dot_nt(xq_ref[0, :, ls], kh)
        p = jnp.exp(s - jnp.max(s, axis=-1, keepdims=True))
        p = p / jnp.sum(p, axis=-1, keepdims=True)
        o_ref[0, :, ls] = _dot(p.astype(BF16), vh).astype(BF16)


def _xatt(xq, memkv):
    b, t, _ = xq.shape
    m = memkv.shape[1]
    tq = min(512, t)
    assert t % tq == 0
    return pl.pallas_call(
        _xatt_kernel,
        grid=(b, t // tq),
        in_specs=[pl.BlockSpec((1, tq, 512), lambda i, j: (i, j, 0)),
                  pl.BlockSpec((1, m, 1024), lambda i, j: (i, 0, 0))],
        out_specs=pl.BlockSpec((1, tq, 512), lambda i, j: (i, j, 0)),
        out_shape=jax.ShapeDtypeStruct((b, t, 512), BF16),
        compiler_params=_cparams(2),
    )(xq, memkv)


def _merge_kernel(on_ref, og_ref, ox_ref, mg_ref, x_ref, wn_ref, wg_ref, wx_ref, wo_ref, x1_ref):
    d = x_ref.shape[-1]
    merged = (mg_ref[:, 0:d].astype(F32) * _dot(on_ref[...], wn_ref[...])
              + mg_ref[:, d:2 * d].astype(F32) * _dot(og_ref[...], wg_ref[...])
              + mg_ref[:, 2 * d:3 * d].astype(F32) * _dot(ox_ref[...], wx_ref[...]))
    x1_ref[...] = x_ref[...] + _dot(merged.astype(BF16), wo_ref[...])


def _merge(o_nsa, o_gla, o_x, mg, x, wn, wg, wx, wo):
    m, d = x.shape
    tm = min(512, m)
    assert m % tm == 0
    row = lambda w: pl.BlockSpec((tm, w), lambda i: (i, 0))
    const = lambda shape: pl.BlockSpec(shape, lambda i: (0,) * len(shape))
    return pl.pallas_call(
        _merge_kernel,
        grid=(m // tm,),
        in_specs=[row(512), row(512), row(512), row(3 * d), row(d),
                  const((512, d)), const((512, d)), const((512, d)), const((d, d))],
        out_specs=row(d),
        out_shape=jax.ShapeDtypeStruct((m, d), F32),
        compiler_params=_cparams(1),
    )(o_nsa, o_gla, o_x, mg, x, wn, wg, wx, wo)


def _ffn_seq_kernel(x_ref, past_ref, g_ref, wup_ref, cw_ref, cb_ref, wdn_ref, y_ref, tail_ref, carry_ref):
    f = cw_ref.shape[-1]
    tm = x_ref.shape[1]

    @pl.when(pl.program_id(1) == 0)
    def _():
        carry_ref[...] = jnp.zeros(carry_ref.shape, F32)
        carry_ref[6:8, :] = past_ref[0]

    x1 = x_ref[0]
    ug = _dot(_rms(x1, g_ref[...]).astype(BF16), wup_ref[...])
    u = ug[:, 0:f]
    g = ug[:, f:2 * f]
    row = lax.broadcasted_iota(jnp.int32, (tm, f), 0)
    p1 = carry_ref[7:8, :]
    p2 = carry_ref[6:7, :]
    gm1 = jnp.where(row == 0, p1, pltpu.roll(g, 1, 0))
    gm2 = jnp.where(row == 0, p2, jnp.where(row == 1, p1, pltpu.roll(g, 2, 0)))
    gc = cb_ref[...] + cw_ref[0:1, :] * gm2 + cw_ref[1:2, :] * gm1 + cw_ref[2:3, :] * g
    y_ref[0] = x1 + _dot((_gelu(gc) * u).astype(BF16), wdn_ref[...])
    carry_ref[...] = g[tm - 8:tm, :]
    tail_ref[0] = g[tm - 8:tm, :]


def _ffn_seq(x1, conv_past, g_ffn, w_up, conv_w, conv_b, w_down):
    b, t, d = x1.shape
    f = conv_w.shape[-1]
    tm = min(256, t)
    assert t % tm == 0 and tm >= 8
    const = lambda shape: pl.BlockSpec(shape, lambda i, j: (0,) * len(shape))
    return pl.pallas_call(
        _ffn_seq_kernel,
        grid=(b, t // tm),
        in_specs=[pl.BlockSpec((1, tm, d), lambda i, j: (i, j, 0)),
                  pl.BlockSpec((1, 2, f), lambda i, j: (i, 0, 0)),
                  const((1, d)), const((d, 2 * f)), const((3, f)), const((1, f)), const((f, d))],
        out_specs=[pl.BlockSpec((1, tm, d), lambda i, j: (i, j, 0)),
                   pl.BlockSpec((1, 8, f), lambda i, j: (i, 0, 0))],
        out_shape=[jax.ShapeDtypeStruct((b, t, d), F32), jax.ShapeDtypeStruct((b, 8, f), F32)],
        scratch_shapes=[pltpu.VMEM((8, f), F32)],
        compiler_params=_cparams(2),
    )(x1, conv_past, g_ffn, w_up, conv_w, conv_b, w_down)


def _ffn_step_kernel(x_ref, p0_ref, p1_ref, g_ref, wup_ref, cw_ref, cb_ref, wdn_ref, y_ref, gnew_ref):
    f = cw_ref.shape[-1]
    x1 = x_ref[...]
    ug = _dot(_rms(x1, g_ref[...]).astype(BF16), wup_ref[...])
    u = ug[:, 0:f]
    g = ug[:, f:2 * f]
    gc = cb_ref[...] + cw_ref[0:1, :] * p0_ref[...] + cw_ref[1:2, :] * p1_ref[...] + cw_ref[2:3, :] * g
    y_ref[...] = x1 + _dot((_gelu(gc) * u).astype(BF16), wdn_ref[...])
    gnew_ref[...] = g


def _ffn_step(x1, p0, p1, g_ffn, w_up, conv_w, conv_b, w_down):
    m, d = x1.shape
    f = conv_w.shape[-1]
    full = lambda shape: pl.BlockSpec(shape, lambda i: (0,) * len(shape))
    return pl.pallas_call(
        _ffn_step_kernel,
        grid=(1,),
        in_specs=[full((m, d)), full((m, f)), full((m, f)), full((1, d)), full((d, 2 * f)), full((3, f)),
                  full((1, f)), full((f, d))],
        out_specs=[full((m, d)), full((m, f))],
        out_shape=[jax.ShapeDtypeStruct((m, d), F32), jax.ShapeDtypeStruct((m, f), F32)],
        compiler_params=_cparams(1),
    )(x1, p0, p1, g_ffn, w_up, conv_w, conv_b, w_down)


def _decode_kernel(pt_ref, page_ref, q8_ref, new_ref, neww_ref, cwin_ref, gt_ref, bc_ref, bs_ref, bw_ref, b0_ref,
                   ov_ref, et_ref, pe_ref, w1_ref, w2_ref, gk0_ref, p64_ref, o_ref, wout_ref, xs_ref, *, top, n_sel):
    del pt_ref
    p = pl.program_id(1)
    n_pages = pl.num_programs(1)
    page = page_ref.shape[1]
    start = pl.multiple_of(p * page, page)
    for c in range(4):
        xs_ref[c, pl.ds(start, page), :] = page_ref[0, :, 128 * c:128 * (c + 1)]

    @pl.when(p == n_pages - 1)
    def _():
        length = xs_ref.shape[1]
        n_chunks = length // CMP_STRIDE
        q8 = q8_ref[0]
        q8f = q8.astype(F32)
        rowk = lax.broadcasted_iota(jnp.int32, (8, 128), 0) // NSA_GROUP
        lane_half = lax.broadcasted_iota(jnp.int32, (8, 128), 1) // HEAD_DIM

        def half_mask(x):
            return jnp.where(rowk == lane_half, x, 0.0)

        load_j = lambda kind, j: xs_ref[kind, pl.ds(j, n_chunks, stride=CMP_STRIDE), :]
        kc, vc = _compress_core(load_j, n_chunks, pe_ref, w1_ref, w2_ref, gk0_ref, p64_ref)
        kc = kc.astype(BF16)
        vc = vc.astype(BF16)

        p_c = _masked_softmax_rows(_dot_nt(q8, kc) + bc_ref[...]).astype(BF16)
        o_c = half_mask(_dot(p_c, vc))
        imp8 = _dot(p_c, ov_ref[...])
        nsp = imp8.shape[1]
        imp = jnp.concatenate([jnp.sum(imp8[0:4], axis=0, keepdims=True),
                               jnp.sum(imp8[4:8], axis=0, keepdims=True)], axis=0)
        jj = lax.broadcasted_iota(jnp.int32, (2, nsp), 1)
        tblk = length // SEL_BLOCK
        forced = (jj == 0) | (jj == tblk) | (jj == tblk - 1)
        score = jnp.where(jj * SEL_BLOCK <= length, imp + jnp.where(forced, FORCE_BONUS, 0.0), NEG)
        score = jnp.where(jj < n_sel, score, -2e38)
        sel = _select_blocks(score, top)
        pen = jnp.where(sel, 0.0, SEL_PENALTY)
        pen8 = jnp.concatenate([jnp.broadcast_to(pen[0:1], (4, nsp)), jnp.broadcast_to(pen[1:2], (4, nsp))], axis=0)

        new = new_ref[0]
        neww = neww_ref[0]
        b0 = b0_ref[...]

        def attend(s_past, s_new, v_past, v_new):
            m = jnp.maximum(jnp.max(s_past, axis=-1, keepdims=True), s_new)
            pp = jnp.exp(s_past - m)
            pn = jnp.exp(s_new - m)
            l = jnp.sum(pp, axis=-1, keepdims=True) + pn
            o = _dot(pp.astype(BF16), v_past) + pn.astype(BF16).astype(F32) * v_new.astype(BF16).astype(F32)
            return half_mask(o / l)

        def new_score(k_new):
            return jnp.sum(q8f * k_new.astype(BF16).astype(F32), axis=-1, keepdims=True) + b0

        k_past = xs_ref[2].astype(BF16)
        v_past = xs_ref[3].astype(BF16)
        s_past = _dot_nt(q8, k_past) + _dot_nt(pen8[:, 0:128].astype(BF16), et_ref[...]) + bs_ref[...]
        lane = lax.broadcasted_iota(jnp.int32, (8, nsp), 1)
        pen_new = jnp.sum(jnp.where(lane == tblk, pen8, 0.0), axis=-1, keepdims=True)
        o_s = attend(s_past, new_score(new[:, 256:384]) + pen_new, v_past, new[:, 384:512])

        cw = cwin_ref[0]
        s_w = _dot_nt(q8, cw[:, 0:128].astype(BF16)) + bw_ref[...]
        o_w = attend(s_w, new_score(neww[:, 0:128]), cw[:, 128:256].astype(BF16), neww[:, 128:256])

        gt = gt_ref[0]
        o_ref[0] = gt[:, 0:1] * o_c + gt[:, 1:2] * o_s + gt[:, 2:3] * o_w

        wl = cw.shape[0]
        wrow = lax.broadcasted_iota(jnp.int32, cw.shape, 0)
        wout_ref[0] = jnp.where(wrow == wl - 1, neww, pltpu.roll(cw, wl - 1, 0))


def _nsa_decode(page_table, cache2d, q8, new_rows, new_win, cache_win, gates8, bc, bs, bw, b0, ov, et,
                pe, w1, w2, gk0, p64):
    db, n_pages = page_table.shape
    page = cache2d.shape[1]
    length = n_pages * page
    n_sel = -(-(length + 1) // SEL_BLOCK)
    top = min(SEL_TOP, n_sel)
    wl = cache_win.shape[1]
    n_chunks = length // CMP_STRIDE
    nsp = ov.shape[1]
    const = lambda shape: pl.BlockSpec(shape, lambda i, j, pt: (0,) * len(shape))
    per_b = lambda shape: pl.BlockSpec((1,) + shape, lambda i, j, pt: (i,) + (0,) * len(shape))
    grid_spec = pltpu.PrefetchScalarGridSpec(
        num_scalar_prefetch=1,
        grid=(db, n_pages),
        in_specs=[pl.BlockSpec((1, page, 512), lambda i, j, pt: (pt[i, j], 0, 0)),
                  per_b((8, 128)), per_b((1, 512)), per_b((1, 256)), per_b((wl, 256)), per_b((8, 128)),
                  const((8, n_chunks)), const((8, length)), const((8, wl)), const((8, 1)),
                  const((n_chunks, nsp)), const((length, 128)),
                  const((2, 2, CMP_STRIDE, 1, 128)), const((2, 2, CMP_STRIDE, 128, 128)), const((2, 128, 128)),
                  const((1, 128)), const((512, 512))],
        out_specs=[per_b((8, 128)), per_b((wl, 256))],
        scratch_shapes=[pltpu.VMEM((4, length, 128), F32)],
    )
    return pl.pallas_call(
        functools.partial(_decode_kernel, top=top, n_sel=n_sel),
        grid_spec=grid_spec,
        out_shape=[jax.ShapeDtypeStruct((db, 8, 128), F32), jax.ShapeDtypeStruct((db, wl, 256), F32)],
        compiler_params=_cparams(2),
    )(page_table, cache2d, q8, new_rows, new_win, cache_win, gates8, bc, bs, bw, b0, ov, et, pe, w1, w2, gk0, p64)


def _bucket_table():
    n = np.arange(MAX_DISTANCE + 1)
    max_exact = N_BUCKETS // 2
    nf = np.maximum(n, 1).astype(np.float32)
    large = max_exact + (np.log(nf / np.float32(max_exact)) / np.float32(math.log(MAX_DISTANCE / max_exact))
                         * np.float32(N_BUCKETS - max_exact)).astype(np.int32)
    return np.where(n < max_exact, n, np.minimum(large, N_BUCKETS - 1)).astype(np.int32)


def _bias_lookup(tb, rel, valid):
    idx = np.clip(rel, 0, MAX_DISTANCE)
    vals = jnp.moveaxis(tb[idx], -1, 0)
    return jnp.where(jnp.asarray(valid)[None], vals, NEG)


def _overlap(n_cmp_pad, n_cmp, n_sel_pad, n_sel):
    cs = (np.arange(n_cmp_pad) * CMP_STRIDE)[:, None]
    ss = (np.arange(n_sel_pad) * SEL_BLOCK)[None, :]
    ov = (cs < ss + SEL_BLOCK) & (cs + CMP_BLOCK > ss)
    ov &= (np.arange(n_cmp_pad) < n_cmp)[:, None] & (np.arange(n_sel_pad) < n_sel)[None, :]
    return jnp.asarray(ov.astype(np.float32), dtype=BF16)


def _stack_rows(x):
    return x.reshape(NSA_KV, NSA_GROUP * Q_BLOCK, x.shape[-1])


def _prompt_tables(tb):
    i = np.arange(Q_BLOCK)[:, None]
    c = np.arange(KEY_TILE)[None, :]
    d0 = _bias_lookup(tb, i - c, i - c >= 0)
    d1 = _bias_lookup(tb, Q_BLOCK + i - c, np.ones((Q_BLOCK, KEY_TILE), bool))
    d2 = _bias_lookup(tb, np.full((Q_BLOCK, KEY_TILE), MAX_DISTANCE), np.ones((Q_BLOCK, KEY_TILE), bool))
    d4 = _bias_lookup(tb, np.full((Q_BLOCK, KEY_TILE), MAX_DISTANCE), i < c)
    tbl = jnp.stack([_stack_rows(d) for d in (d0, d1, d2, d4)], axis=1)
    w = np.arange(16)[None, :] - 9
    rel = i - CMP_STRIDE * w - (CMP_BLOCK - 1)
    a = _stack_rows(_bias_lookup(tb, rel, rel >= 0))
    hi = a.astype(BF16)
    lo = (a - hi.astype(F32)).astype(BF16)
    return tbl, jnp.concatenate([hi, lo], axis=-1)


def kernel(x_prompt, x_sample, cache_kv, cache_win, state_gla, state_conv, cache_mem, page_table, mem_prompt,
           g_mix, w_in, g_nsa_q, g_nsa_k, cmp_k_pe, cmp_k_w1, cmp_k_w2, cmp_v_pe, cmp_v_w1, cmp_v_w2,
           rel_bias, w_gla_gate, b_gla_gate, g_gla_o, g_mem, w_mem_kv, g_x_q, g_x_k,
           w_nsa_out, w_gla_out, w_x_out, w_o, g_ffn, w_up, conv_w, conv_b, w_down):
    bp, t, d = x_prompt.shape
    db = x_sample.shape[0]
    f = conv_w.shape[-1]

    offs = np.cumsum((0,) + IN_SIZES)
    segs = [w_in[:, offs[i]:offs[i + 1]] for i in range(len(IN_SIZES))]
    w_pad = jnp.concatenate([jnp.pad(s, ((0, 0), (0, pw - s.shape[1]))) for s, pw in zip(segs, PAD_SIZES)],
                            axis=1).astype(BF16)
    row = lambda v: v.reshape(1, -1).astype(F32)
    gq = row(jnp.tile(g_nsa_q, NSA_HEADS))
    gk0 = row(jnp.tile(g_nsa_k[0], NSA_KV))
    gk1 = row(jnp.tile(g_nsa_k[1], NSA_KV))
    gk2 = row(jnp.tile(g_nsa_k[2], NSA_KV))
    gxq = row(jnp.tile(g_x_q, X_HEADS))
    gxk = row(jnp.tile(g_x_k, X_HEADS))
    p64 = _block_ones(512, HEAD_DIM)
    p128 = _block_ones(512, X_DIM)
    bd2 = lambda a: jnp.concatenate([jnp.concatenate([a, jnp.zeros_like(a)], -1),
                                     jnp.concatenate([jnp.zeros_like(a), a], -1)], -2)
    pe = jnp.stack([jnp.tile(v, (1, NSA_KV)) for v in (cmp_k_pe, cmp_v_pe)]).reshape(2, 2, CMP_STRIDE, 1, 128)
    w1 = jnp.stack([bd2(v) for v in (cmp_k_w1, cmp_v_w1)]).reshape(2, 2, CMP_STRIDE, 128, 128).astype(BF16)
    w2 = jnp.stack([bd2(v) for v in (cmp_k_w2, cmp_v_w2)]).astype(BF16)
    wg_pad = jnp.pad(w_gla_gate, ((0, 128 - GLA_RANK), (0, 0))).astype(BF16)
    tri = jnp.asarray(np.tril(np.ones((GLA_CHUNK, GLA_CHUNK), np.float32)), dtype=BF16)
    tb = rel_bias.astype(F32)[_bucket_table()]
    wn, wgo, wx, wo = (w.astype(BF16) for w in (w_nsa_out, w_gla_out, w_x_out, w_o))
    wup = w_up.astype(BF16)
    wdn = w_down.astype(BF16)
    ggo = row(g_gla_o)

    rows_p, win_p, qn, katt, gates, gla_in, xq, mg = _proj_in(x_prompt, row(g_mix), w_pad, gq, gk1, gk2, gxq, p64, p128)
    memkv_p = _memory_kv(mem_prompt, row(g_mem), w_mem_kv.astype(BF16), gxk, p128)
    kc, vc = _compress(rows_p, pe, w1, w2, gk0, p64)
    n_chunks = t // CMP_STRIDE
    n_sel = -(-t // SEL_BLOCK)
    tbl, acmp = _prompt_tables(tb)
    ov = _overlap(n_chunks, n_chunks - 1, 128, n_sel)
    o_nsa = _nsa_prompt(qn, kc, vc, katt, gates, tbl, acmp, ov)
    s0t = jnp.zeros((bp, GLA_HEADS, GLA_DV, GLA_DK), F32)
    o_gla, st = _gla_prompt(gla_in, s0t, wg_pad, row(b_gla_gate), ggo, tri)
    o_x = _xatt(xq, memkv_p)
    m = bp * t
    x1 = _merge(o_nsa.reshape(m, 512), o_gla.reshape(m, 512), o_x.reshape(m, 512), mg.reshape(m, 3 * d),
                x_prompt.reshape(m, d), wn, wgo, wx, wo)
    y_p, tail = _ffn_seq(x1.reshape(bp, t, d), jnp.zeros((bp, 2, f), F32), row(g_ffn), wup, conv_w, row(conv_b), wdn)
    wl_p = min(WINDOW, t)
    out_rows_p = rows_p.reshape(bp, t, 4, NSA_KV, HEAD_DIM)
    out_win_p = win_p[:, t - wl_p:].reshape(bp, wl_p, 2, NSA_KV, HEAD_DIM)
    out_gla_p = jnp.swapaxes(st, 2, 3)
    out_conv_p = tail[:, 6:8]
    out_mem_p = memkv_p.reshape(bp, -1, 2, X_HEADS, X_DIM)

    n_pages = page_table.shape[1]
    page = cache_kv.shape[1]
    length = n_pages * page
    wl = cache_win.shape[1]
    rows_s, win_s, qn_s, _, gates_s, gla_s, xq_s, mg_s = _proj_in(
        x_sample.reshape(1, db, d), row(g_mix), w_pad, gq, gk1, gk2, gxq, p64, p128)
    rows_s, win_s, qn_s, gates_s, gla_s, xq_s, mg_s = (a[0] for a in (rows_s, win_s, qn_s, gates_s, gla_s, xq_s, mg_s))

    eye = jnp.eye(NSA_KV, dtype=BF16)
    q8 = (qn_s.reshape(db, NSA_KV, NSA_GROUP, 1, HEAD_DIM) * eye[None, :, None, :, None]).reshape(db, 8, 128)
    gates8 = jnp.pad(gates_s[:, 0:24].reshape(db, 8, 3), ((0, 0), (0, 0), (0, 125)))
    n_chunks_s = length // CMP_STRIDE
    n_sel_s = -(-(length + 1) // SEL_BLOCK)
    nsp = -(-n_sel_s // 128) * 128
    cidx = np.arange(n_chunks_s)
    rel_c = length - (cidx * CMP_STRIDE + CMP_BLOCK - 1)
    bc = _bias_lookup(tb, rel_c, (rel_c >= 0) & (cidx < n_chunks_s - 1))
    kpos = np.arange(length)
    bs = _bias_lookup(tb, length - kpos, np.ones(length, bool))
    wpos = length - wl + np.arange(wl)
    bw = _bias_lookup(tb, length - wpos, (length - wpos) < WINDOW)
    b0 = tb[0].reshape(8, 1)
    ov_s = _overlap(n_chunks_s, n_chunks_s - 1, nsp, n_sel_s)
    et = jnp.asarray((kpos[:, None] // SEL_BLOCK == np.arange(128)[None, :]).astype(np.float32), dtype=BF16)
    o8, win_new = _nsa_decode(page_table, cache_kv.reshape(cache_kv.shape[0], page, 512), q8,
                              rows_s.reshape(db, 1, 512), win_s.reshape(db, 1, 256), cache_win.reshape(db, wl, 256),
                              gates8, bc, bs, bw, b0, ov_s, et, pe, w1, w2, gk0, p64)
    o8 = o8.reshape(db, NSA_KV, NSA_GROUP, NSA_KV, HEAD_DIM)
    o_nsa_s = jnp.stack([o8[:, 0, :, 0], o8[:, 1, :, 1]], axis=1).reshape(db, 512).astype(BF16)

    wgt = jnp.pad(w_gla_gate.T, ((0, 0), (0, 128 - GLA_RANK))).reshape(GLA_HEADS, GLA_DK, 128)
    o_gla_s, gla_state_s = _gla_step(
        gla_s[:, 0:256].reshape(db, GLA_HEADS, GLA_DK, 1), gla_s[:, 256:512].reshape(db, GLA_HEADS, GLA_DK, 1),
        gla_s[:, 1024:1152].reshape(db, 1, 128), gla_s[:, 512:1024].reshape(db, GLA_HEADS, 1, GLA_DV),
        gla_s[:, 1152:1664].reshape(db, GLA_HEADS, 1, GLA_DV), state_gla.astype(F32), wgt,
        b_gla_gate.reshape(GLA_HEADS, GLA_DK, 1), ggo)
    o_gla_s = o_gla_s.reshape(db, 512).astype(BF16)

    xq_pad = jnp.pad(xq_s.reshape(db, 1, 512), ((0, 0), (0, 15), (0, 0)))
    o_x_s = _xatt(xq_pad, cache_mem.reshape(db, -1, 1024))[:, 0]
    x1_s = _merge(o_nsa_s, o_gla_s, o_x_s, mg_s, x_sample.reshape(db, d), wn, wgo, wx, wo)
    y_s, g_new = _ffn_step(x1_s, state_conv[:, 0], state_conv[:, 1], row(g_ffn), wup, conv_w, row(conv_b), wdn)

    out_rows_s = rows_s.reshape(db, 1, 4, NSA_KV, HEAD_DIM)
    out_win_s = win_new.reshape(db, wl, 2, NSA_KV, HEAD_DIM)
    out_conv_s = jnp.stack([state_conv[:, 1], g_new], axis=1)
    return (y_p, y_s.reshape(db, 1, d), out_rows_p, out_win_p, out_gla_p, out_conv_p, out_mem_p,
            out_rows_s, out_win_s, gla_state_s, out_conv_s)
```

```python
import functools
import math

import numpy as np
import jax
import jax.numpy as jnp
from jax import lax
from jax.experimental import pallas as pl
from jax.experimental.pallas import tpu as pltpu

F32 = jnp.float32
BF16 = jnp.bfloat16

NSA_HEADS = 8
NSA_KV = 2
NSA_GROUP = 4
HEAD_DIM = 64
CMP_BLOCK = 32
CMP_STRIDE = 16
SEL_BLOCK = 64
SEL_TOP = 16
WINDOW = 512
Q_BLOCK = 128
FORCE_BONUS = 1e4
GLA_HEADS = 4
GLA_DK = 64
GLA_DV = 128
GLA_RANK = 16
GLA_TAU = 16.0
GLA_CHUNK = 64
X_HEADS = 4
X_DIM = 128
N_BUCKETS = 32
MAX_DISTANCE = 128
EPS = 1e-6
NEG = -1e30
TINY = 1e-30
SEL_PENALTY = -1e9
MASKED_BELOW = -5e29
EXP_CLAMP = 80.0

IN_SIZES = (512, 768, 24, 256, 256, 512, 16, 512, 512, 3072)
PAD_SIZES = (512, 768, 128, 256, 256, 512, 128, 512, 512, 3072)
PAD_OFFS = tuple(int(v) for v in np.cumsum((0,) + PAD_SIZES))
D_IN_PAD = PAD_OFFS[-1]
GLA_IN_W = 256 + 256 + 512 + 128 + 512

VMEM_LIMIT = 56 * 1024 * 1024
FAR_TILE = 256
NEAR_TILE = 512
WIN_TILE = WINDOW + Q_BLOCK
KATT_W = 640


def _cparams(n_axes):
    return pltpu.CompilerParams(dimension_semantics=("arbitrary",) * n_axes, vmem_limit_bytes=VMEM_LIMIT)


def _dot(a, b):
    return jnp.dot(a, b, preferred_element_type=F32)


def _dot_nt(a, b):
    return lax.dot_general(a, b, (((1,), (1,)), ((), ())), preferred_element_type=F32)


def _dot_tn(a, b):
    return lax.dot_general(a, b, (((0,), (0,)), ((), ())), preferred_element_type=F32)


def _split_dot(x, m):
    hi = x.astype(BF16)
    lo = (x - hi.astype(F32)).astype(BF16)
    return _dot(hi, m) + _dot(lo, m)


def _rms(x, g):
    return x * lax.rsqrt(jnp.mean(x * x, axis=-1, keepdims=True) + EPS) * g


def _group_rms(x, pmat, gsize, g):
    ss = _split_dot(x * x, pmat)
    return x * lax.rsqrt(ss * (1.0 / gsize) + EPS) * g


def _gelu(x):
    return 0.5 * x * (1.0 + jnp.tanh(math.sqrt(2.0 / math.pi) * (x + 0.044715 * (x * x * x))))


def _sigmoid(x):
    return 1.0 / (1.0 + jnp.exp(-x))


def _block_ones(n, gsize):
    i = np.arange(n) // gsize
    return jnp.asarray((i[:, None] == i[None, :]).astype(np.float32), dtype=BF16)


def _proj_kernel(x_ref, gmix_ref, w_ref, gq_ref, gk1_ref, gk2_ref, gxq_ref, p64_ref, p128_ref,
                 rows_ref, win_ref, qn_ref, katt_ref, gates_ref, gla_ref, xq_ref, mg_ref):
    x = x_ref[0]
    h = _rms(x, gmix_ref[...]).astype(BF16)
    o = PAD_OFFS

    def seg(i):
        return _dot(h, w_ref[:, o[i]:o[i + 1]])

    p64 = p64_ref[...]
    p64s = p64_ref[0:128, 0:128]
    qn = _group_rms(seg(0), p64, HEAD_DIM, gq_ref[...]) * (HEAD_DIM ** -0.5)
    qn_ref[0] = qn.astype(BF16)

    kv = seg(1)
    k_sel = _group_rms(kv[:, 256:384], p64s, HEAD_DIM, gk1_ref[...])
    k_win = _group_rms(kv[:, 512:640], p64s, HEAD_DIM, gk2_ref[...])
    rows_ref[0, :, 0:256] = kv[:, 0:256]
    rows_ref[0, :, 256:384] = k_sel
    rows_ref[0, :, 384:512] = kv[:, 384:512]
    win_ref[0, :, 0:128] = k_win
    win_ref[0, :, 128:256] = kv[:, 640:768]
    tm = x.shape[0]
    tpos = pl.program_id(1) * tm + lax.broadcasted_iota(jnp.int32, (tm, 128), 0)
    blk = lax.broadcasted_iota(jnp.int32, (tm, 128), 1)
    katt_ref[0, :, 0:128] = k_sel.astype(BF16)
    katt_ref[0, :, 128:256] = jnp.where(tpos // SEL_BLOCK == blk, 1.0, 0.0).astype(BF16)
    katt_ref[0, :, 256:384] = kv[:, 384:512].astype(BF16)
    katt_ref[0, :, 384:512] = k_win.astype(BF16)
    katt_ref[0, :, 512:640] = kv[:, 640:768].astype(BF16)

    gates_ref[0] = _sigmoid(seg(2))
    gla_ref[0, :, 0:256] = seg(3) * (GLA_DK ** -0.5)
    gla_ref[0, :, 256:512] = seg(4)
    gla_ref[0, :, 512:1024] = seg(5)
    gla_ref[0, :, 1024:1152] = seg(6)
    gla_ref[0, :, 1152:1664] = seg(7)
    xq = _group_rms(seg(8), p128_ref[...], X_DIM, gxq_ref[...]) * (X_DIM ** -0.5)
    xq_ref[0] = xq.astype(BF16)
    mg_ref[0] = _sigmoid(seg(9)).astype(BF16)


def _proj_in(x, g_mix, w_pad, gq, gk1, gk2, gxq, p64, p128):
    b, t, d = x.shape
    tm = min(256, t)
    assert t % tm == 0
    widths = (512, 256, 512, KATT_W, 128, GLA_IN_W, 512, 3072)
    dtypes = (F32, F32, BF16, BF16, F32, F32, BF16, BF16)
    const = lambda shape: pl.BlockSpec(shape, lambda i, j: (0,) * len(shape))
    return pl.pallas_call(
        _proj_kernel,
        grid=(b, t // tm),
        in_specs=[pl.BlockSpec((1, tm, d), lambda i, j: (i, j, 0)),
                  const((1, d)), const((d, D_IN_PAD)), const((1, 512)), const((1, 128)), const((1, 128)),
                  const((1, 512)), const((512, 512)), const((512, 512))],
        out_specs=[pl.BlockSpec((1, tm, w), lambda i, j: (i, j, 0)) for w in widths],
        out_shape=[jax.ShapeDtypeStruct((b, t, w), dt) for w, dt in zip(widths, dtypes)],
        compiler_params=_cparams(2),
    )(x, g_mix, w_pad, gq, gk1, gk2, gxq, p64, p128)


def _memkv_kernel(m_ref, g_ref, w_ref, gk_ref, p128_ref, o_ref):
    h = _rms(m_ref[0], g_ref[...]).astype(BF16)
    kv = _dot(h, w_ref[...])
    o_ref[0, :, 0:512] = _group_rms(kv[:, 0:512], p128_ref[...], X_DIM, gk_ref[...])
    o_ref[0, :, 512:1024] = kv[:, 512:1024]


def _memory_kv(mem, g_mem, w_mem, gxk, p128):
    b, m, d = mem.shape
    const = lambda shape: pl.BlockSpec(shape, lambda i: (0,) * len(shape))
    return pl.pallas_call(
        _memkv_kernel,
        grid=(b,),
        in_specs=[pl.BlockSpec((1, m, d), lambda i: (i, 0, 0)), const((1, d)), const((d, 1024)),
                  const((1, 512)), const((512, 512))],
        out_specs=pl.BlockSpec((1, m, 1024), lambda i: (i, 0, 0)),
        out_shape=jax.ShapeDtypeStruct((b, m, 1024), F32),
        compiler_params=_cparams(1),
    )(mem, g_mem, w_mem, gxk, p128)


def _compress_core(load_j, n_chunks, pe_ref, w1_ref, w2_ref, gk0_ref, p64_ref):
    outs = []
    for kind in range(2):
        acc0 = jnp.zeros((n_chunks, 128), F32)
        acc1 = jnp.zeros((n_chunks, 128), F32)
        for j in range(CMP_STRIDE):
            xj = load_j(kind, j)
            acc0 = acc0 + _dot((xj + pe_ref[kind, 0, j]).astype(BF16), w1_ref[kind, 0, j])
            acc1 = acc1 + _dot((xj + pe_ref[kind, 1, j]).astype(BF16), w1_ref[kind, 1, j])
        hid = acc0 + pltpu.roll(acc1, n_chunks - 1, 0)
        outs.append(_dot(_gelu(hid).astype(BF16), w2_ref[kind]))
    row = lax.broadcasted_iota(jnp.int32, (n_chunks, 128), 0)
    live = row < n_chunks - 1
    kc = _group_rms(outs[0], p64_ref[0:128, 0:128], HEAD_DIM, gk0_ref[...])
    return jnp.where(live, kc, 0.0), jnp.where(live, outs[1], 0.0)


def _compress_kernel(rk_ref, rv_ref, pe_ref, w1_ref, w2_ref, gk0_ref, p64_ref, kc_ref, vc_ref, *, n_chunks):
    refs = (rk_ref, rv_ref)
    load_j = lambda kind, j: refs[kind][0, pl.ds(j, n_chunks, stride=CMP_STRIDE), :]
    kc, vc = _compress_core(load_j, n_chunks, pe_ref, w1_ref, w2_ref, gk0_ref, p64_ref)
    kc_ref[0] = kc.astype(BF16)
    vc_ref[0] = vc.astype(BF16)


def _compress(rows, pe, w1, w2, gk0, p64):
    b, t, _ = rows.shape
    n_chunks = t // CMP_STRIDE
    const = lambda shape: pl.BlockSpec(shape, lambda i: (0,) * len(shape))
    return pl.pallas_call(
        functools.partial(_compress_kernel, n_chunks=n_chunks),
        grid=(b,),
        in_specs=[pl.BlockSpec((1, t, 128), lambda i: (i, 0, 0)), pl.BlockSpec((1, t, 128), lambda i: (i, 0, 1)),
                  const((2, 2, CMP_STRIDE, 1, 128)), const((2, 2, CMP_STRIDE, 128, 128)), const((2, 128, 128)),
                  const((1, 128)), const((512, 512))],
        out_specs=[pl.BlockSpec((1, n_chunks, 128), lambda i: (i, 0, 0))] * 2,
        out_shape=[jax.ShapeDtypeStruct((b, n_chunks, 128), BF16)] * 2,
        compiler_params=_cparams(1),
    )(rows, rows, pe, w1, w2, gk0, p64)


def _masked_softmax_rows(s):
    valid = s > MASKED_BELOW
    m = jnp.max(s, axis=-1, keepdims=True)
    p = jnp.where(valid, jnp.exp(s - m), 0.0)
    return p / jnp.maximum(jnp.sum(p, axis=-1, keepdims=True), TINY)


def _select_blocks(score, top, axis):
    pos = lax.broadcasted_iota(jnp.int32, score.shape, axis).astype(F32)
    sel = jnp.zeros(score.shape, jnp.bool_)
    for _ in range(top):
        mx = jnp.max(score, axis=axis, keepdims=True)
        idx = jnp.min(jnp.where(score == mx, pos, 1e9), axis=axis, keepdims=True)
        hit = pos == idx
        sel = jnp.logical_or(sel, hit)
        score = jnp.where(hit, -3e38, score)
    return sel


def _softmax_update(s, vt, carry):
    m, l, acc = carry
    m_new = jnp.maximum(m, jnp.max(s, axis=-1, keepdims=True))
    alpha = jnp.exp(m - m_new)
    p = jnp.exp(s - m_new)
    l = alpha * l + jnp.sum(p, axis=-1, keepdims=True)
    acc = alpha * acc + _dot(p.astype(BF16), vt)
    return m_new, l, acc


def _attn_kernel(q_ref, kc_ref, vc_ref, ka_ref, vs_ref, kw_ref, vw_ref, gt_ref, tsel_ref, twin_ref, acmp_ref, ovt_ref,
                 o_ref, *, top):
    k = pl.program_id(1)
    n = pl.program_id(2)
    rows4 = NSA_GROUP * Q_BLOCK
    qb = q_ref[0]
    zero = jnp.zeros((Q_BLOCK, HEAD_DIM), BF16)
    parts = []
    for g in range(NSA_GROUP):
        piece = qb[:, HEAD_DIM * g:HEAD_DIM * (g + 1)]
        parts.append(jnp.where(k == 0, jnp.concatenate([piece, zero], axis=1),
                               jnp.concatenate([zero, piece], axis=1)))
    q4 = jnp.concatenate(parts, axis=0)

    def half(x):
        return jnp.where(k == 0, x[:, 0:HEAD_DIM], x[:, HEAD_DIM:2 * HEAD_DIM])

    nc = kc_ref.shape[1]
    r32 = lax.broadcasted_iota(jnp.int32, (32, nc), 0)
    c32 = lax.broadcasted_iota(jnp.int32, (32, nc), 1)
    onehot = jnp.where((c32 - 8 * n + 9) == (r32 & 15), 1.0, 0.0).astype(BF16)
    band = _dot(acmp_ref[0], onehot)
    cp = lax.broadcasted_iota(jnp.int32, (rows4, nc), 1) - 8 * n + 9
    s_c = _dot_nt(q4, kc_ref[0]) + jnp.where(cp < 0, 0.0, jnp.where(cp > 15, NEG, band))
    p_c = _masked_softmax_rows(s_c).astype(BF16)
    o_c = half(_dot(p_c, vc_ref[0]))
    imp4 = _dot_nt(ovt_ref[...], p_c)
    imp = imp4[:, 0:128] + imp4[:, 128:256] + imp4[:, 256:384] + imp4[:, 384:512]

    jj = lax.broadcasted_iota(jnp.int32, (128, Q_BLOCK), 0)
    tpos = n * Q_BLOCK + lax.broadcasted_iota(jnp.int32, (128, Q_BLOCK), 1)
    tblk = tpos // SEL_BLOCK
    forced = (jj == 0) | (jj == tblk) | (jj == tblk - 1)
    score = jnp.where(jj * SEL_BLOCK <= tpos, imp + jnp.where(forced, FORCE_BONUS, 0.0), NEG)
    sel_t = _select_blocks(score, top, 0)
    pen = jnp.where(sel_t, 0.0, SEL_PENALTY).T.astype(BF16)
    lhs = jnp.concatenate([q4, jnp.concatenate([pen] * NSA_GROUP, axis=0)], axis=1)

    groups = [slice(Q_BLOCK * g, Q_BLOCK * (g + 1)) for g in range(NSA_GROUP)]
    lhs_g = [lhs[r] for r in groups]
    n_far = jnp.maximum(n - 2, 0) // 2
    r_near = n - 2 * n_far

    def scores(tile):
        kt = ka_ref[0, pl.ds(pl.multiple_of(tile * FAR_TILE, FAR_TILE), FAR_TILE), :]
        return tuple(_dot_nt(x, kt) for x in lhs_g)

    def reduce_tile(tile, s_g, stats, half_idx):
        vt = vs_ref[0, pl.ds(pl.multiple_of(tile * FAR_TILE, FAR_TILE), FAR_TILE), :]
        out = []
        for g, r in enumerate(groups):
            s = s_g[g]
            if half_idx is not None:
                s = s + tsel_ref[0, r_near, r, FAR_TILE * half_idx:FAR_TILE * (half_idx + 1)]
            out.append(_softmax_update(s, vt, stats[g]))
        return tuple(out)

    def far_body(jt, carry):
        stats, s_cur = carry
        s_next = scores(jt + 1)
        return reduce_tile(jt, s_cur, stats, None), s_next

    stats0 = tuple((jnp.full((Q_BLOCK, 1), NEG, F32), jnp.zeros((Q_BLOCK, 1), F32), jnp.zeros((Q_BLOCK, 128), F32))
                   for _ in groups)
    stats, s_a = lax.fori_loop(0, n_far, far_body, (stats0, scores(0)))
    s_b = scores(n_far + 1)
    stats = reduce_tile(n_far, s_a, stats, 0)
    stats = reduce_tile(n_far + 1, s_b, stats, 1)
    o_s = half(jnp.concatenate([acc / l for _, l, acc in stats], axis=0))

    ws = pl.multiple_of(jnp.maximum(n - WINDOW // Q_BLOCK, 0) * Q_BLOCK, Q_BLOCK)
    s_w = _dot_nt(q4, kw_ref[0, pl.ds(ws, WIN_TILE), :]) + twin_ref[0, 0]
    p_w = jnp.exp(s_w - jnp.max(s_w, axis=-1, keepdims=True))
    o_w = half(_dot(p_w.astype(BF16), vw_ref[0, pl.ds(ws, WIN_TILE), :]) / jnp.sum(p_w, axis=-1, keepdims=True))

    gt = gt_ref[0]
    gtk = jnp.where(k == 0, gt[:, 0:12], gt[:, 12:24])
    outs = []
    for g in range(NSA_GROUP):
        r = slice(Q_BLOCK * g, Q_BLOCK * (g + 1))
        outs.append(gtk[:, 3 * g:3 * g + 1] * o_c[r] + gtk[:, 3 * g + 1:3 * g + 2] * o_s[r]
                    + gtk[:, 3 * g + 2:3 * g + 3] * o_w[r])
    o_ref[0] = jnp.concatenate(outs, axis=1).astype(BF16)


def _nsa_prompt(qn, kc, vc, katt, gates, tsel, twin, acmp, ovt):
    b, t, _ = qn.shape
    nqb = t // Q_BLOCK
    nc = kc.shape[1]
    top = min(SEL_TOP, -(-t // SEL_BLOCK))
    assert t >= WIN_TILE and t % FAR_TILE == 0 and t // SEL_BLOCK <= 128
    n_win = WINDOW // Q_BLOCK
    kspec = lambda c: pl.BlockSpec((1, t, 128), lambda i, k, n: (i, 0, c))
    return pl.pallas_call(
        functools.partial(_attn_kernel, top=top),
        grid=(b, NSA_KV, nqb),
        in_specs=[pl.BlockSpec((1, Q_BLOCK, 256), lambda i, k, n: (i, n, k)),
                  pl.BlockSpec((1, nc, 128), lambda i, k, n: (i, 0, 0)),
                  pl.BlockSpec((1, nc, 128), lambda i, k, n: (i, 0, 0)),
                  pl.BlockSpec((1, t, 256), lambda i, k, n: (i, 0, 0)), kspec(2), kspec(3), kspec(4),
                  pl.BlockSpec((1, Q_BLOCK, 128), lambda i, k, n: (i, n, 0)),
                  pl.BlockSpec((1, 4, 512, NEAR_TILE), lambda i, k, n: (k, 0, 0, 0)),
                  pl.BlockSpec((1, 1, 512, WIN_TILE), lambda i, k, n: (k, jnp.minimum(n, n_win), 0, 0)),
                  pl.BlockSpec((1, 512, 32), lambda i, k, n: (k, 0, 0)),
                  pl.BlockSpec((128, nc), lambda i, k, n: (0, 0))],
        out_specs=pl.BlockSpec((1, Q_BLOCK, 256), lambda i, k, n: (i, n, k)),
        out_shape=jax.ShapeDtypeStruct((b, t, 512), BF16),
        compiler_params=_cparams(3),
    )(qn, kc, vc, katt, katt, katt, katt, gates, tsel, twin, acmp, ovt)


def _log_sigmoid(z):
    return jnp.minimum(z, 0.0) - jnp.log1p(jnp.exp(-jnp.abs(z)))


def _gla_kernel(x_ref, s0_ref, wg_ref, bg_ref, ggo_ref, tri_ref, o_ref, st_ref, s_scr, *, n_chunks):
    @pl.when(pl.program_id(1) == 0)
    def _():
        s_scr[...] = s0_ref[0]

    c_len = GLA_CHUNK
    tri = tri_ref[...]
    ti = lax.broadcasted_iota(jnp.int32, (c_len, c_len), 0)
    si = lax.broadcasted_iota(jnp.int32, (c_len, c_len), 1)
    causal = si <= ti
    for c in range(n_chunks):
        rs = slice(c_len * c, c_len * (c + 1))
        q = x_ref[0, rs, 0:256]
        kk = x_ref[0, rs, 256:512]
        v = x_ref[0, rs, 512:1024]
        lr = x_ref[0, rs, 1024:1152]
        r = x_ref[0, rs, 1152:1664]
        la = _log_sigmoid(_dot(lr.astype(BF16), wg_ref[...]) + bg_ref[...]) * (1.0 / GLA_TAU)
        a1 = la.astype(BF16)
        r1 = la - a1.astype(F32)
        a2 = r1.astype(BF16)
        a3 = (r1 - a2.astype(F32)).astype(BF16)
        cb = _dot(tri, a1) + _dot(tri, a2) + _dot(tri, a3)
        last = cb[c_len - 1:c_len, :]
        mid = cb[c_len // 2:c_len // 2 + 1, :]
        qe = (q * jnp.exp(cb)).astype(BF16)
        qa = (q * jnp.exp(jnp.minimum(cb - mid, EXP_CLAMP))).astype(BF16)
        kb = (kk * jnp.exp(jnp.minimum(mid - cb, EXP_CLAMP))).astype(BF16)
        ke = (kk * jnp.exp(last - cb)).astype(BF16)
        dec = jnp.exp(last)
        for h in range(GLA_HEADS):
            ks = slice(GLA_DK * h, GLA_DK * (h + 1))
            vs = slice(GLA_DV * h, GLA_DV * (h + 1))
            att = jnp.where(causal, _dot_nt(qa[:, ks], kb[:, ks]), 0.0)
            vh = v[:, vs].astype(BF16)
            st = s_scr[h]
            o = _dot(att.astype(BF16), vh) + _dot_nt(qe[:, ks], st.astype(BF16))
            s_scr[h] = st * dec[:, ks] + _dot_tn(vh, ke[:, ks])
            on = _rms(o, ggo_ref[...])
            rh = r[:, vs]
            o_ref[0, rs, vs] = (on * (rh * _sigmoid(rh))).astype(BF16)
    st_ref[0] = s_scr[...]


def _gla_prompt(gla_in, s0t, wg, bg, ggo, tri):
    b, t, w = gla_in.shape
    ct = min(256, t)
    assert t % ct == 0 and ct % GLA_CHUNK == 0
    const = lambda shape: pl.BlockSpec(shape, lambda i, j: (0,) * len(shape))
    return pl.pallas_call(
        functools.partial(_gla_kernel, n_chunks=ct // GLA_CHUNK),
        grid=(b, t // ct),
        in_specs=[pl.BlockSpec((1, ct, w), lambda i, j: (i, j, 0)),
                  pl.BlockSpec((1, GLA_HEADS, GLA_DV, GLA_DK), lambda i, j: (i, 0, 0, 0)),
                  const((128, 256)), const((1, 256)), const((1, 128)), const((GLA_CHUNK, GLA_CHUNK))],
        out_specs=[pl.BlockSpec((1, ct, 512), lambda i, j: (i, j, 0)),
                   pl.BlockSpec((1, GLA_HEADS, GLA_DV, GLA_DK), lambda i, j: (i, 0, 0, 0))],
        out_shape=[jax.ShapeDtypeStruct((b, t, 512), BF16),
                   jax.ShapeDtypeStruct((b, GLA_HEADS, GLA_DV, GLA_DK), F32)],
        scratch_shapes=[pltpu.VMEM((GLA_HEADS, GLA_DV, GLA_DK), F32)],
        compiler_params=_cparams(2),
    )(gla_in, s0t, wg, bg, ggo, tri)


def _gla_step_kernel(q_ref, k_ref, lr_ref, v_ref, r_ref, s_ref, wgt_ref, bgt_ref, ggo_ref, o_ref, sn_ref):
    lr = lr_ref[0]
    for h in range(GLA_HEADS):
        z = jnp.sum(wgt_ref[h] * lr, axis=-1, keepdims=True) + bgt_ref[h]
        a = jnp.exp(_log_sigmoid(z) * (1.0 / GLA_TAU))
        s0 = s_ref[0, h]
        kh = k_ref[0, h]
        qh = q_ref[0, h]
        vh = v_ref[0, h]
        sn_ref[0, h] = a * s0 + kh * vh
        o = jnp.sum((qh * a) * s0, axis=0, keepdims=True) + jnp.sum(qh * kh, axis=0, keepdims=True) * vh
        on = _rms(o, ggo_ref[...])
        rh = r_ref[0, h]
        o_ref[0, h] = on * (rh * _sigmoid(rh))


def _gla_step(q_col, k_col, lr, v_row, r_row, s0, wgt, bgt, ggo):
    b = q_col.shape[0]
    const = lambda shape: pl.BlockSpec(shape, lambda i: (0,) * len(shape))
    per_b = lambda shape: pl.BlockSpec((1,) + shape, lambda i: (i,) + (0,) * len(shape))
    return pl.pallas_call(
        _gla_step_kernel,
        grid=(b,),
        in_specs=[per_b((GLA_HEADS, GLA_DK, 1)), per_b((GLA_HEADS, GLA_DK, 1)), per_b((1, 128)),
                  per_b((GLA_HEADS, 1, GLA_DV)), per_b((GLA_HEADS, 1, GLA_DV)), per_b((GLA_HEADS, GLA_DK, GLA_DV)),
                  const((GLA_HEADS, GLA_DK, 128)), const((GLA_HEADS, GLA_DK, 1)), const((1, 128))],
        out_specs=[per_b((GLA_HEADS, 1, GLA_DV)), per_b((GLA_HEADS, GLA_DK, GLA_DV))],
        out_shape=[jax.ShapeDtypeStruct((b, GLA_HEADS, 1, GLA_DV), F32),
                   jax.ShapeDtypeStruct((b, GLA_HEADS, GLA_DK, GLA_DV), F32)],
        compiler_params=_cparams(1),
    )(q_col, k_col, lr, v_row, r_row, s0, wgt, bgt, ggo)


def _xatt_kernel(xq_ref, mem_ref, o_ref):
    for h in range(X_HEADS):
        ls = slice(X_DIM * h, X_DIM * (h + 1))
        kh = mem_ref[0, :, ls].astype(BF16)
        vh = mem_ref[0, :, 512 + X_DIM * h:512 + X_DIM * (h + 1)].astype(BF16)
        s = _dot_nt(xq_ref[0, :, ls], kh)
        p = jnp.exp(s - jnp.max(s, axis=-1, keepdims=True))
        p = p / jnp.sum(p, axis=-1, keepdims=True)
        o_ref[0, :, ls] = _dot(p.astype(BF16), vh).astype(BF16)


def _xatt(xq, memkv):
    b, t, _ = xq.shape
    m = memkv.shape[1]
    tq = min(512, t)
    assert t % tq == 0
    return pl.pallas_call(
        _xatt_kernel,
        grid=(b, t // tq),
        in_specs=[pl.BlockSpec((1, tq, 512), lambda i, j: (i, j, 0)),
                  pl.BlockSpec((1, m, 1024), lambda i, j: (i, 0, 0))],
        out_specs=pl.BlockSpec((1, tq, 512), lambda i, j: (i, j, 0)),
        out_shape=jax.ShapeDtypeStruct((b, t, 512), BF16),
        compiler_params=_cparams(2),
    )(xq, memkv)


def _merge_kernel(on_ref, og_ref, ox_ref, mg_ref, x_ref, wn_ref, wg_ref, wx_ref, wo_ref, x1_ref):
    d = x_ref.shape[-1]
    merged = (mg_ref[:, 0:d].astype(F32) * _dot(on_ref[...], wn_ref[...])
              + mg_ref[:, d:2 * d].astype(F32) * _dot(og_ref[...], wg_ref[...])
              + mg_ref[:, 2 * d:3 * d].astype(F32) * _dot(ox_ref[...], wx_ref[...]))
    x1_ref[...] = x_ref[...] + _dot(merged.astype(BF16), wo_ref[...])


def _merge(o_nsa, o_gla, o_x, mg, x, wn, wg, wx, wo):
    m, d = x.shape
    tm = min(512, m)
    assert m % tm == 0
    row = lambda w: pl.BlockSpec((tm, w), lambda i: (i, 0))
    const = lambda shape: pl.BlockSpec(shape, lambda i: (0,) * len(shape))
    return pl.pallas_call(
        _merge_kernel,
        grid=(m // tm,),
        in_specs=[row(512), row(512), row(512), row(3 * d), row(d),
                  const((512, d)), const((512, d)), const((512, d)), const((d, d))],
        out_specs=row(d),
        out_shape=jax.ShapeDtypeStruct((m, d), F32),
        compiler_params=_cparams(1),
    )(o_nsa, o_gla, o_x, mg, x, wn, wg, wx, wo)


def _ffn_seq_kernel(x_ref, past_ref, g_ref, wup_ref, cw_ref, cb_ref, wdn_ref, y_ref, tail_ref, carry_ref):
    f = cw_ref.shape[-1]
    tm = x_ref.shape[1]

    @pl.when(pl.program_id(1) == 0)
    def _():
        carry_ref[...] = jnp.zeros(carry_ref.shape, F32)
        carry_ref[6:8, :] = past_ref[0]

    x1 = x_ref[0]
    ug = _dot(_rms(x1, g_ref[...]).astype(BF16), wup_ref[...])
    u = ug[:, 0:f]
    g = ug[:, f:2 * f]
    row = lax.broadcasted_iota(jnp.int32, (tm, f), 0)
    p1 = carry_ref[7:8, :]
    p2 = carry_ref[6:7, :]
    gm1 = jnp.where(row == 0, p1, pltpu.roll(g, 1, 0))
    gm2 = jnp.where(row == 0, p2, jnp.where(row == 1, p1, pltpu.roll(g, 2, 0)))
    gc = cb_ref[...] + cw_ref[0:1, :] * gm2 + cw_ref[1:2, :] * gm1 + cw_ref[2:3, :] * g
    y_ref[0] = x1 + _dot((_gelu(gc) * u).astype(BF16), wdn_ref[...])
    carry_ref[...] = g[tm - 8:tm, :]
    tail_ref[0] = g[tm - 8:tm, :]


def _ffn_seq(x1, conv_past, g_ffn, w_up, conv_w, conv_b, w_down):
    b, t, d = x1.shape
    f = conv_w.shape[-1]
    tm = min(256, t)
    assert t % tm == 0 and tm >= 8
    const = lambda shape: pl.BlockSpec(shape, lambda i, j: (0,) * len(shape))
    return pl.pallas_call(
        _ffn_seq_kernel,
        grid=(b, t // tm),
        in_specs=[pl.BlockSpec((1, tm, d), lambda i, j: (i, j, 0)),
                  pl.BlockSpec((1, 2, f), lambda i, j: (i, 0, 0)),
                  const((1, d)), const((d, 2 * f)), const((3, f)), const((1, f)), const((f, d))],
        out_specs=[pl.BlockSpec((1, tm, d), lambda i, j: (i, j, 0)),
                   pl.BlockSpec((1, 8, f), lambda i, j: (i, 0, 0))],
        out_shape=[jax.ShapeDtypeStruct((b, t, d), F32), jax.ShapeDtypeStruct((b, 8, f), F32)],
        scratch_shapes=[pltpu.VMEM((8, f), F32)],
        compiler_params=_cparams(2),
    )(x1, conv_past, g_ffn, w_up, conv_w, conv_b, w_down)


def _ffn_step_kernel(x_ref, p0_ref, p1_ref, g_ref, wup_ref, cw_ref, cb_ref, wdn_ref, y_ref, gnew_ref):
    f = cw_ref.shape[-1]
    x1 = x_ref[...]
    ug = _dot(_rms(x1, g_ref[...]).astype(BF16), wup_ref[...])
    u = ug[:, 0:f]
    g = ug[:, f:2 * f]
    gc = cb_ref[...] + cw_ref[0:1, :] * p0_ref[...] + cw_ref[1:2, :] * p1_ref[...] + cw_ref[2:3, :] * g
    y_ref[...] = x1 + _dot((_gelu(gc) * u).astype(BF16), wdn_ref[...])
    gnew_ref[...] = g


def _ffn_step(x1, p0, p1, g_ffn, w_up, conv_w, conv_b, w_down):
    m, d = x1.shape
    f = conv_w.shape[-1]
    full = lambda shape: pl.BlockSpec(shape, lambda i: (0,) * len(shape))
    return pl.pallas_call(
        _ffn_step_kernel,
        grid=(1,),
        in_specs=[full((m, d)), full((m, f)), full((m, f)), full((1, d)), full((d, 2 * f)), full((3, f)),
                  full((1, f)), full((f, d))],
        out_specs=[full((m, d)), full((m, f))],
        out_shape=[jax.ShapeDtypeStruct((m, d), F32), jax.ShapeDtypeStruct((m, f), F32)],
        compiler_params=_cparams(1),
    )(x1, p0, p1, g_ffn, w_up, conv_w, conv_b, w_down)


def _decode_kernel(pt_ref, page_ref, q8_ref, new_ref, neww_ref, cwin_ref, gt_ref, bc_ref, bs_ref, bw_ref, b0_ref,
                   ov_ref, et_ref, pe_ref, w1_ref, w2_ref, gk0_ref, p64_ref, o_ref, wout_ref, xs_ref, *, top, n_sel):
    del pt_ref
    p = pl.program_id(1)
    n_pages = pl.num_programs(1)
    page = page_ref.shape[1]
    start = pl.multiple_of(p * page, page)
    for c in range(4):
        xs_ref[c, pl.ds(start, page), :] = page_ref[0, :, 128 * c:128 * (c + 1)]

    @pl.when(p == n_pages - 1)
    def _():
        length = xs_ref.shape[1]
        n_chunks = length // CMP_STRIDE
        q8 = q8_ref[0]
        q8f = q8.astype(F32)
        rowk = lax.broadcasted_iota(jnp.int32, (8, 128), 0) // NSA_GROUP
        lane_half = lax.broadcasted_iota(jnp.int32, (8, 128), 1) // HEAD_DIM

        def half_mask(x):
            return jnp.where(rowk == lane_half, x, 0.0)

        load_j = lambda kind, j: xs_ref[kind, pl.ds(j, n_chunks, stride=CMP_STRIDE), :]
        kc, vc = _compress_core(load_j, n_chunks, pe_ref, w1_ref, w2_ref, gk0_ref, p64_ref)
        kc = kc.astype(BF16)
        vc = vc.astype(BF16)

        p_c = _masked_softmax_rows(_dot_nt(q8, kc) + bc_ref[...]).astype(BF16)
        o_c = half_mask(_dot(p_c, vc))
        imp8 = _dot(p_c, ov_ref[...])
        nsp = imp8.shape[1]
        imp = jnp.concatenate([jnp.sum(imp8[0:4], axis=0, keepdims=True),
                               jnp.sum(imp8[4:8], axis=0, keepdims=True)], axis=0)
        jj = lax.broadcasted_iota(jnp.int32, (2, nsp), 1)
        tblk = length // SEL_BLOCK
        forced = (jj == 0) | (jj == tblk) | (jj == tblk - 1)
        score = jnp.where(jj * SEL_BLOCK <= length, imp + jnp.where(forced, FORCE_BONUS, 0.0), NEG)
        score = jnp.where(jj < n_sel, score, -2e38)
        sel = _select_blocks(score, top, 1)
        pen = jnp.where(sel, 0.0, SEL_PENALTY)
        pen8 = jnp.concatenate([jnp.broadcast_to(pen[0:1], (4, nsp)), jnp.broadcast_to(pen[1:2], (4, nsp))], axis=0)

        new = new_ref[0]
        neww = neww_ref[0]
        b0 = b0_ref[...]

        def attend(s_past, s_new, v_past, v_new):
            m = jnp.maximum(jnp.max(s_past, axis=-1, keepdims=True), s_new)
            pp = jnp.exp(s_past - m)
            pn = jnp.exp(s_new - m)
            l = jnp.sum(pp, axis=-1, keepdims=True) + pn
            o = _dot(pp.astype(BF16), v_past) + pn.astype(BF16).astype(F32) * v_new.astype(BF16).astype(F32)
            return half_mask(o / l)

        def new_score(k_new):
            return jnp.sum(q8f * k_new.astype(BF16).astype(F32), axis=-1, keepdims=True) + b0

        k_past = xs_ref[2].astype(BF16)
        v_past = xs_ref[3].astype(BF16)
        s_past = _dot_nt(q8, k_past) + _dot_nt(pen8[:, 0:128].astype(BF16), et_ref[...]) + bs_ref[...]
        lane = lax.broadcasted_iota(jnp.int32, (8, nsp), 1)
        pen_new = jnp.sum(jnp.where(lane == tblk, pen8, 0.0), axis=-1, keepdims=True)
        o_s = attend(s_past, new_score(new[:, 256:384]) + pen_new, v_past, new[:, 384:512])

        cw = cwin_ref[0]
        s_w = _dot_nt(q8, cw[:, 0:128].astype(BF16)) + bw_ref[...]
        o_w = attend(s_w, new_score(neww[:, 0:128]), cw[:, 128:256].astype(BF16), neww[:, 128:256])

        gt = gt_ref[0]
        o_ref[0] = gt[:, 0:1] * o_c + gt[:, 1:2] * o_s + gt[:, 2:3] * o_w

        wl = cw.shape[0]
        wrow = lax.broadcasted_iota(jnp.int32, cw.shape, 0)
        wout_ref[0] = jnp.where(wrow == wl - 1, neww, pltpu.roll(cw, wl - 1, 0))


def _nsa_decode(page_table, cache2d, q8, new_rows, new_win, cache_win, gates8, bc, bs, bw, b0, ov, et,
                pe, w1, w2, gk0, p64):
    db, n_pages = page_table.shape
    page = cache2d.shape[1]
    length = n_pages * page
    n_sel = -(-(length + 1) // SEL_BLOCK)
    top = min(SEL_TOP, n_sel)
    wl = cache_win.shape[1]
    n_chunks = length // CMP_STRIDE
    nsp = ov.shape[1]
    const = lambda shape: pl.BlockSpec(shape, lambda i, j, pt: (0,) * len(shape))
    per_b = lambda shape: pl.BlockSpec((1,) + shape, lambda i, j, pt: (i,) + (0,) * len(shape))
    grid_spec = pltpu.PrefetchScalarGridSpec(
        num_scalar_prefetch=1,
        grid=(db, n_pages),
        in_specs=[pl.BlockSpec((1, page, 512), lambda i, j, pt: (pt[i, j], 0, 0)),
                  per_b((8, 128)), per_b((1, 512)), per_b((1, 256)), per_b((wl, 256)), per_b((8, 128)),
                  const((8, n_chunks)), const((8, length)), const((8, wl)), const((8, 1)),
                  const((n_chunks, nsp)), const((length, 128)),
                  const((2, 2, CMP_STRIDE, 1, 128)), const((2, 2, CMP_STRIDE, 128, 128)), const((2, 128, 128)),
                  const((1, 128)), const((512, 512))],
        out_specs=[per_b((8, 128)), per_b((wl, 256))],
        scratch_shapes=[pltpu.VMEM((4, length, 128), F32)],
    )
    return pl.pallas_call(
        functools.partial(_decode_kernel, top=top, n_sel=n_sel),
        grid_spec=grid_spec,
        out_shape=[jax.ShapeDtypeStruct((db, 8, 128), F32), jax.ShapeDtypeStruct((db, wl, 256), F32)],
        compiler_params=_cparams(2),
    )(page_table, cache2d, q8, new_rows, new_win, cache_win, gates8, bc, bs, bw, b0, ov, et, pe, w1, w2, gk0, p64)


def _bucket_table():
    n = np.arange(MAX_DISTANCE + 1)
    max_exact = N_BUCKETS // 2
    nf = np.maximum(n, 1).astype(np.float32)
    large = max_exact + (np.log(nf / np.float32(max_exact)) / np.float32(math.log(MAX_DISTANCE / max_exact))
                         * np.float32(N_BUCKETS - max_exact)).astype(np.int32)
    return np.where(n < max_exact, n, np.minimum(large, N_BUCKETS - 1)).astype(np.int32)


def _bias_lookup(tb, rel, valid):
    idx = np.clip(rel, 0, MAX_DISTANCE)
    vals = jnp.moveaxis(tb[idx], -1, 0)
    return jnp.where(jnp.asarray(valid)[None], vals, NEG)


def _overlap(n_cmp_pad, n_cmp, n_sel_pad, n_sel):
    cs = (np.arange(n_cmp_pad) * CMP_STRIDE)[:, None]
    ss = (np.arange(n_sel_pad) * SEL_BLOCK)[None, :]
    ov = (cs < ss + SEL_BLOCK) & (cs + CMP_BLOCK > ss)
    ov &= (np.arange(n_cmp_pad) < n_cmp)[:, None] & (np.arange(n_sel_pad) < n_sel)[None, :]
    return jnp.asarray(ov.astype(np.float32), dtype=BF16)


def _stack_rows(x):
    return x.reshape(NSA_KV, NSA_GROUP * Q_BLOCK, x.shape[-1])


def _prompt_tables(tb):
    tbr = tb - tb[MAX_DISTANCE][None, :]
    i = np.arange(Q_BLOCK)[:, None]

    def table(rel, valid):
        return _stack_rows(_bias_lookup(tbr, rel, valid))

    c = np.arange(NEAR_TILE)[None, :]
    tsel = jnp.stack([table(Q_BLOCK * r + i - c, Q_BLOCK * r + i - c >= 0) for r in range(4)], axis=1)
    c = np.arange(WIN_TILE)[None, :]
    rels = [Q_BLOCK * v + i - c for v in range(WINDOW // Q_BLOCK + 1)]
    twin = jnp.stack([table(rel, (rel >= 0) & (rel < WINDOW)) for rel in rels], axis=1)
    w = np.arange(16)[None, :] - 9
    rel = i - CMP_STRIDE * w - (CMP_BLOCK - 1)
    a = table(rel, rel >= 0)
    hi = a.astype(BF16)
    lo = (a - hi.astype(F32)).astype(BF16)
    return tsel, twin, jnp.concatenate([hi, lo], axis=-1)


def kernel(x_prompt, x_sample, cache_kv, cache_win, state_gla, state_conv, cache_mem, page_table, mem_prompt,
           g_mix, w_in, g_nsa_q, g_nsa_k, cmp_k_pe, cmp_k_w1, cmp_k_w2, cmp_v_pe, cmp_v_w1, cmp_v_w2,
           rel_bias, w_gla_gate, b_gla_gate, g_gla_o, g_mem, w_mem_kv, g_x_q, g_x_k,
           w_nsa_out, w_gla_out, w_x_out, w_o, g_ffn, w_up, conv_w, conv_b, w_down):
    bp, t, d = x_prompt.shape
    db = x_sample.shape[0]
    f = conv_w.shape[-1]

    offs = np.cumsum((0,) + IN_SIZES)
    segs = [w_in[:, offs[i]:offs[i + 1]] for i in range(len(IN_SIZES))]
    w_pad = jnp.concatenate([jnp.pad(s, ((0, 0), (0, pw - s.shape[1]))) for s, pw in zip(segs, PAD_SIZES)],
                            axis=1).astype(BF16)
    row = lambda v: v.reshape(1, -1).astype(F32)
    gq = row(jnp.tile(g_nsa_q, NSA_HEADS))
    gk0 = row(jnp.tile(g_nsa_k[0], NSA_KV))
    gk1 = row(jnp.tile(g_nsa_k[1], NSA_KV))
    gk2 = row(jnp.tile(g_nsa_k[2], NSA_KV))
    gxq = row(jnp.tile(g_x_q, X_HEADS))
    gxk = row(jnp.tile(g_x_k, X_HEADS))
    p64 = _block_ones(512, HEAD_DIM)
    p128 = _block_ones(512, X_DIM)
    bd2 = lambda a: jnp.concatenate([jnp.concatenate([a, jnp.zeros_like(a)], -1),
                                     jnp.concatenate([jnp.zeros_like(a), a], -1)], -2)
    pe = jnp.stack([jnp.tile(v, (1, NSA_KV)) for v in (cmp_k_pe, cmp_v_pe)]).reshape(2, 2, CMP_STRIDE, 1, 128)
    w1 = jnp.stack([bd2(v) for v in (cmp_k_w1, cmp_v_w1)]).reshape(2, 2, CMP_STRIDE, 128, 128).astype(BF16)
    w2 = jnp.stack([bd2(v) for v in (cmp_k_w2, cmp_v_w2)]).astype(BF16)
    wg_pad = jnp.pad(w_gla_gate, ((0, 128 - GLA_RANK), (0, 0))).astype(BF16)
    tri = jnp.asarray(np.tril(np.ones((GLA_CHUNK, GLA_CHUNK), np.float32)), dtype=BF16)
    tb = rel_bias.astype(F32)[_bucket_table()]
    wn, wgo, wx, wo = (w.astype(BF16) for w in (w_nsa_out, w_gla_out, w_x_out, w_o))
    wup = w_up.astype(BF16)
    wdn = w_down.astype(BF16)
    ggo = row(g_gla_o)

    rows_p, win_p, qn, katt, gates, gla_in, xq, mg = _proj_in(x_prompt, row(g_mix), w_pad, gq, gk1, gk2, gxq, p64, p128)
    memkv_p = _memory_kv(mem_prompt, row(g_mem), w_mem_kv.astype(BF16), gxk, p128)
    kc, vc = _compress(rows_p, pe, w1, w2, gk0, p64)
    n_chunks = t // CMP_STRIDE
    n_sel = -(-t // SEL_BLOCK)
    tsel, twin, acmp = _prompt_tables(tb)
    ovt = _overlap(n_chunks, n_chunks - 1, 128, n_sel).T
    o_nsa = _nsa_prompt(qn, kc, vc, katt, gates, tsel, twin, acmp, ovt)
    s0t = jnp.zeros((bp, GLA_HEADS, GLA_DV, GLA_DK), F32)
    o_gla, st = _gla_prompt(gla_in, s0t, wg_pad, row(b_gla_gate), ggo, tri)
    o_x = _xatt(xq, memkv_p)
    m = bp * t
    x1 = _merge(o_nsa.reshape(m, 512), o_gla.reshape(m, 512), o_x.reshape(m, 512), mg.reshape(m, 3 * d),
                x_prompt.reshape(m, d), wn, wgo, wx, wo)
    y_p, tail = _ffn_seq(x1.reshape(bp, t, d), jnp.zeros((bp, 2, f), F32), row(g_ffn), wup, conv_w, row(conv_b), wdn)
    wl_p = min(WINDOW, t)
    out_rows_p = rows_p.reshape(bp, t, 4, NSA_KV, HEAD_DIM)
    out_win_p = win_p[:, t - wl_p:].reshape(bp, wl_p, 2, NSA_KV, HEAD_DIM)
    out_gla_p = jnp.swapaxes(st, 2, 3)
    out_conv_p = tail[:, 6:8]
    out_mem_p = memkv_p.reshape(bp, -1, 2, X_HEADS, X_DIM)

    n_pages = page_table.shape[1]
    page = cache_kv.shape[1]
    length = n_pages * page
    wl = cache_win.shape[1]
    rows_s, win_s, qn_s, _, gates_s, gla_s, xq_s, mg_s = _proj_in(
        x_sample.reshape(1, db, d), row(g_mix), w_pad, gq, gk1, gk2, gxq, p64, p128)
    rows_s, win_s, qn_s, gates_s, gla_s, xq_s, mg_s = (a[0] for a in (rows_s, win_s, qn_s, gates_s, gla_s, xq_s, mg_s))

    eye = jnp.eye(NSA_KV, dtype=BF16)
    q8 = (qn_s.reshape(db, NSA_KV, NSA_GROUP, 1, HEAD_DIM) * eye[None, :, None, :, None]).reshape(db, 8, 128)
    gates8 = jnp.pad(gates_s[:, 0:24].reshape(db, 8, 3), ((0, 0), (0, 0), (0, 125)))
    n_chunks_s = length // CMP_STRIDE
    n_sel_s = -(-(length + 1) // SEL_BLOCK)
    nsp = -(-n_sel_s // 128) * 128
    cidx = np.arange(n_chunks_s)
    rel_c = length - (cidx * CMP_STRIDE + CMP_BLOCK - 1)
    bc = _bias_lookup(tb, rel_c, (rel_c >= 0) & (cidx < n_chunks_s - 1))
    kpos = np.arange(length)
    bs = _bias_lookup(tb, length - kpos, np.ones(length, bool))
    wpos = length - wl + np.arange(wl)
    bw = _bias_lookup(tb, length - wpos, (length - wpos) < WINDOW)
    b0 = tb[0].reshape(8, 1)
    ov_s = _overlap(n_chunks_s, n_chunks_s - 1, nsp, n_sel_s)
    et = jnp.asarray((kpos[:, None] // SEL_BLOCK == np.arange(128)[None, :]).astype(np.float32), dtype=BF16)
    o8, win_new = _nsa_decode(page_table, cache_kv.reshape(cache_kv.shape[0], page, 512), q8,
                              rows_s.reshape(db, 1, 512), win_s.reshape(db, 1, 256), cache_win.reshape(db, wl, 256),
                              gates8, bc, bs, bw, b0, ov_s, et, pe, w1, w2, gk0, p64)
    o8 = o8.reshape(db, NSA_KV, NSA_GROUP, NSA_KV, HEAD_DIM)
    o_nsa_s = jnp.stack([o8[:, 0, :, 0], o8[:, 1, :, 1]], axis=1).reshape(db, 512).astype(BF16)

    wgt = jnp.pad(w_gla_gate.T, ((0, 0), (0, 128 - GLA_RANK))).reshape(GLA_HEADS, GLA_DK, 128)
    o_gla_s, gla_state_s = _gla_step(
        gla_s[:, 0:256].reshape(db, GLA_HEADS, GLA_DK, 1), gla_s[:, 256:512].reshape(db, GLA_HEADS, GLA_DK, 1),
        gla_s[:, 1024:1152].reshape(db, 1, 128), gla_s[:, 512:1024].reshape(db, GLA_HEADS, 1, GLA_DV),
        gla_s[:, 1152:1664].reshape(db, GLA_HEADS, 1, GLA_DV), state_gla.astype(F32), wgt,
        b_gla_gate.reshape(GLA_HEADS, GLA_DK, 1), ggo)
    o_gla_s = o_gla_s.reshape(db, 512).astype(BF16)

    xq_pad = jnp.pad(xq_s.reshape(db, 1, 512), ((0, 0), (0, 15), (0, 0)))
    o_x_s = _xatt(xq_pad, cache_mem.reshape(db, -1, 1024))[:, 0]
    x1_s = _merge(o_nsa_s, o_gla_s, o_x_s, mg_s, x_sample.reshape(db, d), wn, wgo, wx, wo)
    y_s, g_new = _ffn_step(x1_s, state_conv[:, 0], state_conv[:, 1], row(g_ffn), wup, conv_w, row(conv_b), wdn)

    out_rows_s = rows_s.reshape(db, 1, 4, NSA_KV, HEAD_DIM)
    out_win_s = win_new.reshape(db, wl, 2, NSA_KV, HEAD_DIM)
    out_conv_s = jnp.stack([state_conv[:, 1], g_new], axis=1)
    return (y_p, y_s.reshape(db, 1, d), out_rows_p, out_win_p, out_gla_p, out_conv_p, out_mem_p,
            out_rows_s, out_win_s, gla_state_s, out_conv_s)
```

```python
import functools
import math

import numpy as np
import jax
import jax.numpy as jnp
from jax import lax
from jax.experimental import pallas as pl
from jax.experimental.pallas import tpu as pltpu

F32 = jnp.float32
BF16 = jnp.bfloat16

NSA_HEADS = 8
NSA_KV = 2
NSA_GROUP = 4
HEAD_DIM = 64
CMP_BLOCK = 32
CMP_STRIDE = 16
SEL_BLOCK = 64
SEL_TOP = 16
WINDOW = 512
Q_BLOCK = 128
FORCE_BONUS = 1e4
GLA_HEADS = 4
GLA_DK = 64
GLA_DV = 128
GLA_RANK = 16
GLA_TAU = 16.0
GLA_CHUNK = 64
X_HEADS = 4
X_DIM = 128
N_BUCKETS = 32
MAX_DISTANCE = 128
EPS = 1e-6
NEG = -1e30
TINY = 1e-30
SEL_PENALTY = -1e9
MASKED_BELOW = -5e29
EXP_CLAMP = 80.0

IN_SIZES = (512, 768, 24, 256, 256, 512, 16, 512, 512, 3072)
PAD_SIZES = (512, 768, 128, 256, 256, 512, 128, 512, 512, 3072)
PAD_OFFS = tuple(int(v) for v in np.cumsum((0,) + PAD_SIZES))
D_IN_PAD = PAD_OFFS[-1]
GLA_IN_W = 256 + 256 + 512 + 128 + 512

VMEM_LIMIT = 56 * 1024 * 1024
FAR_TILE = 256
NEAR_TILE = 512
WIN_TILE = WINDOW + Q_BLOCK
KATT_W = 640
DECODE_PAGES_PER_STEP = 8


def _cparams(n_axes):
    return pltpu.CompilerParams(dimension_semantics=("arbitrary",) * n_axes, vmem_limit_bytes=VMEM_LIMIT)


def _dot(a, b):
    return jnp.dot(a, b, preferred_element_type=F32)


def _dot_nt(a, b):
    return lax.dot_general(a, b, (((1,), (1,)), ((), ())), preferred_element_type=F32)


def _dot_tn(a, b):
    return lax.dot_general(a, b, (((0,), (0,)), ((), ())), preferred_element_type=F32)


def _split_dot(x, m):
    hi = x.astype(BF16)
    lo = (x - hi.astype(F32)).astype(BF16)
    return _dot(hi, m) + _dot(lo, m)


def _rms(x, g):
    return x * lax.rsqrt(jnp.mean(x * x, axis=-1, keepdims=True) + EPS) * g


def _group_rms(x, pmat, gsize, g):
    ss = _split_dot(x * x, pmat)
    return x * lax.rsqrt(ss * (1.0 / gsize) + EPS) * g


def _gelu(x):
    return 0.5 * x * (1.0 + jnp.tanh(math.sqrt(2.0 / math.pi) * (x + 0.044715 * (x * x * x))))


def _sigmoid(x):
    return 1.0 / (1.0 + jnp.exp(-x))


def _block_ones(n, gsize):
    i = np.arange(n) // gsize
    return jnp.asarray((i[:, None] == i[None, :]).astype(np.float32), dtype=BF16)


def _proj_kernel(x_ref, gmix_ref, w_ref, gq_ref, gk1_ref, gk2_ref, gxq_ref, p64_ref, p128_ref,
                 rows_ref, win_ref, qn_ref, katt_ref, gates_ref, gla_ref, xq_ref, mg_ref):
    x = x_ref[0]
    h = _rms(x, gmix_ref[...]).astype(BF16)
    o = PAD_OFFS

    def seg(i):
        return _dot(h, w_ref[:, o[i]:o[i + 1]])

    p64 = p64_ref[...]
    p64s = p64_ref[0:128, 0:128]
    qn = _group_rms(seg(0), p64, HEAD_DIM, gq_ref[...]) * (HEAD_DIM ** -0.5)
    qn_ref[0] = qn.astype(BF16)

    kv = seg(1)
    k_sel = _group_rms(kv[:, 256:384], p64s, HEAD_DIM, gk1_ref[...])
    k_win = _group_rms(kv[:, 512:640], p64s, HEAD_DIM, gk2_ref[...])
    rows_ref[0, :, 0:256] = kv[:, 0:256]
    rows_ref[0, :, 256:384] = k_sel
    rows_ref[0, :, 384:512] = kv[:, 384:512]
    win_ref[0, :, 0:128] = k_win
    win_ref[0, :, 128:256] = kv[:, 640:768]
    tm = x.shape[0]
    tpos = pl.program_id(1) * tm + lax.broadcasted_iota(jnp.int32, (tm, 128), 0)
    blk = lax.broadcasted_iota(jnp.int32, (tm, 128), 1)
    katt_ref[0, :, 0:128] = k_sel.astype(BF16)
    katt_ref[0, :, 128:256] = jnp.where(tpos // SEL_BLOCK == blk, 1.0, 0.0).astype(BF16)
    katt_ref[0, :, 256:384] = kv[:, 384:512].astype(BF16)
    katt_ref[0, :, 384:512] = k_win.astype(BF16)
    katt_ref[0, :, 512:640] = kv[:, 640:768].astype(BF16)

    gates_ref[0] = _sigmoid(seg(2))
    gla_ref[0, :, 0:256] = seg(3) * (GLA_DK ** -0.5)
    gla_ref[0, :, 256:512] = seg(4)
    gla_ref[0, :, 512:1024] = seg(5)
    gla_ref[0, :, 1024:1152] = seg(6)
    gla_ref[0, :, 1152:1664] = seg(7)
    xq = _group_rms(seg(8), p128_ref[...], X_DIM, gxq_ref[...]) * (X_DIM ** -0.5)
    xq_ref[0] = xq.astype(BF16)
    mg_ref[0] = _sigmoid(seg(9)).astype(BF16)


def _proj_in(x, g_mix, w_pad, gq, gk1, gk2, gxq, p64, p128):
    b, t, d = x.shape
    tm = min(256, t)
    assert t % tm == 0
    widths = (512, 256, 512, KATT_W, 128, GLA_IN_W, 512, 3072)
    dtypes = (F32, F32, BF16, BF16, F32, F32, BF16, BF16)
    const = lambda shape: pl.BlockSpec(shape, lambda i, j: (0,) * len(shape))
    return pl.pallas_call(
        _proj_kernel,
        grid=(b, t // tm),
        in_specs=[pl.BlockSpec((1, tm, d), lambda i, j: (i, j, 0)),
                  const((1, d)), const((d, D_IN_PAD)), const((1, 512)), const((1, 128)), const((1, 128)),
                  const((1, 512)), const((512, 512)), const((512, 512))],
        out_specs=[pl.BlockSpec((1, tm, w), lambda i, j: (i, j, 0)) for w in widths],
        out_shape=[jax.ShapeDtypeStruct((b, t, w), dt) for w, dt in zip(widths, dtypes)],
        compiler_params=_cparams(2),
    )(x, g_mix, w_pad, gq, gk1, gk2, gxq, p64, p128)


def _memkv_kernel(m_ref, g_ref, w_ref, gk_ref, p128_ref, o_ref):
    h = _rms(m_ref[0], g_ref[...]).astype(BF16)
    kv = _dot(h, w_ref[...])
    o_ref[0, :, 0:512] = _group_rms(kv[:, 0:512], p128_ref[...], X_DIM, gk_ref[...])
    o_ref[0, :, 512:1024] = kv[:, 512:1024]


def _memory_kv(mem, g_mem, w_mem, gxk, p128):
    b, m, d = mem.shape
    const = lambda shape: pl.BlockSpec(shape, lambda i: (0,) * len(shape))
    return pl.pallas_call(
        _memkv_kernel,
        grid=(b,),
        in_specs=[pl.BlockSpec((1, m, d), lambda i: (i, 0, 0)), const((1, d)), const((d, 1024)),
                  const((1, 512)), const((512, 512))],
        out_specs=pl.BlockSpec((1, m, 1024), lambda i: (i, 0, 0)),
        out_shape=jax.ShapeDtypeStruct((b, m, 1024), F32),
        compiler_params=_cparams(1),
    )(mem, g_mem, w_mem, gxk, p128)


def _compress_core(load_j, n_chunks, pe_ref, w1_ref, w2_ref, gk0_ref, p64_ref):
    outs = []
    for kind in range(2):
        acc0 = jnp.zeros((n_chunks, 128), F32)
        acc1 = jnp.zeros((n_chunks, 128), F32)
        for j in range(CMP_STRIDE):
            xj = load_j(kind, j)
            acc0 = acc0 + _dot((xj + pe_ref[kind, 0, j]).astype(BF16), w1_ref[kind, 0, j])
            acc1 = acc1 + _dot((xj + pe_ref[kind, 1, j]).astype(BF16), w1_ref[kind, 1, j])
        hid = acc0 + pltpu.roll(acc1, n_chunks - 1, 0)
        outs.append(_dot(_gelu(hid).astype(BF16), w2_ref[kind]))
    row = lax.broadcasted_iota(jnp.int32, (n_chunks, 128), 0)
    live = row < n_chunks - 1
    kc = _group_rms(outs[0], p64_ref[0:128, 0:128], HEAD_DIM, gk0_ref[...])
    return jnp.where(live, kc, 0.0), jnp.where(live, outs[1], 0.0)


def _compress_kernel(rk_ref, rv_ref, pe_ref, w1_ref, w2_ref, gk0_ref, p64_ref, kc_ref, vc_ref, *, n_chunks):
    refs = (rk_ref, rv_ref)
    load_j = lambda kind, j: refs[kind][0, pl.ds(j, n_chunks, stride=CMP_STRIDE), :]
    kc, vc = _compress_core(load_j, n_chunks, pe_ref, w1_ref, w2_ref, gk0_ref, p64_ref)
    kc_ref[0] = kc.astype(BF16)
    vc_ref[0] = vc.astype(BF16)


def _compress(rows, pe, w1, w2, gk0, p64):
    b, t, _ = rows.shape
    n_chunks = t // CMP_STRIDE
    const = lambda shape: pl.BlockSpec(shape, lambda i: (0,) * len(shape))
    return pl.pallas_call(
        functools.partial(_compress_kernel, n_chunks=n_chunks),
        grid=(b,),
        in_specs=[pl.BlockSpec((1, t, 128), lambda i: (i, 0, 0)), pl.BlockSpec((1, t, 128), lambda i: (i, 0, 1)),
                  const((2, 2, CMP_STRIDE, 1, 128)), const((2, 2, CMP_STRIDE, 128, 128)), const((2, 128, 128)),
                  const((1, 128)), const((512, 512))],
        out_specs=[pl.BlockSpec((1, n_chunks, 128), lambda i: (i, 0, 0))] * 2,
        out_shape=[jax.ShapeDtypeStruct((b, n_chunks, 128), BF16)] * 2,
        compiler_params=_cparams(1),
    )(rows, rows, pe, w1, w2, gk0, p64)


def _masked_softmax_rows(s):
    valid = s > MASKED_BELOW
    m = jnp.max(s, axis=-1, keepdims=True)
    p = jnp.where(valid, jnp.exp(s - m), 0.0)
    return p / jnp.maximum(jnp.sum(p, axis=-1, keepdims=True), TINY)


def _select_blocks(score, top, axis):
    pos = lax.broadcasted_iota(jnp.int32, score.shape, axis).astype(F32)
    sel = jnp.zeros(score.shape, jnp.bool_)
    for _ in range(top):
        mx = jnp.max(score, axis=axis, keepdims=True)
        idx = jnp.min(jnp.where(score == mx, pos, 1e9), axis=axis, keepdims=True)
        hit = pos == idx
        sel = jnp.logical_or(sel, hit)
        score = jnp.where(hit, -3e38, score)
    return sel


def _softmax_update(s, vt, carry):
    m, l, acc = carry
    m_new = jnp.maximum(m, jnp.max(s, axis=-1, keepdims=True))
    alpha = jnp.exp(m - m_new)
    p = jnp.exp(s - m_new)
    l = alpha * l + jnp.sum(p, axis=-1, keepdims=True)
    acc = alpha * acc + _dot(p.astype(BF16), vt)
    return m_new, l, acc


def _attn_kernel(q_ref, kc_ref, vc_ref, ka_ref, vs_ref, kw_ref, vw_ref, gt_ref, tsel_ref, twin_ref, acmp_ref, ovt_ref,
                 o_ref, *, top):
    k = pl.program_id(1)
    n = pl.program_id(2)
    rows4 = NSA_GROUP * Q_BLOCK
    qb = q_ref[0]
    zero = jnp.zeros((Q_BLOCK, HEAD_DIM), BF16)
    parts = []
    for g in range(NSA_GROUP):
        piece = qb[:, HEAD_DIM * g:HEAD_DIM * (g + 1)]
        parts.append(jnp.where(k == 0, jnp.concatenate([piece, zero], axis=1),
                               jnp.concatenate([zero, piece], axis=1)))
    q4 = jnp.concatenate(parts, axis=0)

    def half(x):
        return jnp.where(k == 0, x[:, 0:HEAD_DIM], x[:, HEAD_DIM:2 * HEAD_DIM])

    nc = kc_ref.shape[1]
    r32 = lax.broadcasted_iota(jnp.int32, (32, nc), 0)
    c32 = lax.broadcasted_iota(jnp.int32, (32, nc), 1)
    onehot = jnp.where((c32 - 8 * n + 9) == (r32 & 15), 1.0, 0.0).astype(BF16)
    band = _dot(acmp_ref[0], onehot)
    cp = lax.broadcasted_iota(jnp.int32, (rows4, nc), 1) - 8 * n + 9
    s_c = _dot_nt(q4, kc_ref[0]) + jnp.where(cp < 0, 0.0, jnp.where(cp > 15, NEG, band))
    p_c = _masked_softmax_rows(s_c).astype(BF16)
    o_c = half(_dot(p_c, vc_ref[0]))
    imp4 = _dot_nt(ovt_ref[...], p_c)
    imp = imp4[:, 0:128] + imp4[:, 128:256] + imp4[:, 256:384] + imp4[:, 384:512]

    jj = lax.broadcasted_iota(jnp.int32, (128, Q_BLOCK), 0)
    tpos = n * Q_BLOCK + lax.broadcasted_iota(jnp.int32, (128, Q_BLOCK), 1)
    tblk = tpos // SEL_BLOCK
    forced = (jj == 0) | (jj == tblk) | (jj == tblk - 1)
    score = jnp.where(jj * SEL_BLOCK <= tpos, imp + jnp.where(forced, FORCE_BONUS, 0.0), NEG)
    sel_t = _select_blocks(score, top, 0)
    pen = jnp.where(sel_t, 0.0, SEL_PENALTY).T.astype(BF16)
    lhs = jnp.concatenate([q4, jnp.concatenate([pen] * NSA_GROUP, axis=0)], axis=1)

    groups = [slice(Q_BLOCK * g, Q_BLOCK * (g + 1)) for g in range(NSA_GROUP)]
    lhs_g = [lhs[r] for r in groups]
    n_far = jnp.maximum(n - 2, 0) // 2
    r_near = n - 2 * n_far

    def scores(tile):
        kt = ka_ref[0, pl.ds(pl.multiple_of(tile * FAR_TILE, FAR_TILE), FAR_TILE), :]
        return tuple(_dot_nt(x, kt) for x in lhs_g)

    def reduce_tile(tile, s_g, stats, half_idx):
        vt = vs_ref[0, pl.ds(pl.multiple_of(tile * FAR_TILE, FAR_TILE), FAR_TILE), :]
        out = []
        for g, r in enumerate(groups):
            s = s_g[g]
            if half_idx is not None:
                s = s + tsel_ref[0, r_near, r, FAR_TILE * half_idx:FAR_TILE * (half_idx + 1)]
            out.append(_softmax_update(s, vt, stats[g]))
        return tuple(out)

    def far_body(jt, carry):
        stats, s_cur = carry
        s_next = scores(jt + 1)
        return reduce_tile(jt, s_cur, stats, None), s_next

    stats0 = tuple((jnp.full((Q_BLOCK, 1), NEG, F32), jnp.zeros((Q_BLOCK, 1), F32), jnp.zeros((Q_BLOCK, 128), F32))
                   for _ in groups)
    stats, s_a = lax.fori_loop(0, n_far, far_body, (stats0, scores(0)))
    s_b = scores(n_far + 1)
    stats = reduce_tile(n_far, s_a, stats, 0)
    stats = reduce_tile(n_far + 1, s_b, stats, 1)
    o_s = half(jnp.concatenate([acc / l for _, l, acc in stats], axis=0))

    ws = pl.multiple_of(jnp.maximum(n - WINDOW // Q_BLOCK, 0) * Q_BLOCK, Q_BLOCK)
    s_w = _dot_nt(q4, kw_ref[0, pl.ds(ws, WIN_TILE), :]) + twin_ref[0, 0]
    p_w = jnp.exp(s_w - jnp.max(s_w, axis=-1, keepdims=True))
    o_w = half(_dot(p_w.astype(BF16), vw_ref[0, pl.ds(ws, WIN_TILE), :]) / jnp.sum(p_w, axis=-1, keepdims=True))

    gt = gt_ref[0]
    gtk = jnp.where(k == 0, gt[:, 0:12], gt[:, 12:24])
    outs = []
    for g in range(NSA_GROUP):
        r = slice(Q_BLOCK * g, Q_BLOCK * (g + 1))
        outs.append(gtk[:, 3 * g:3 * g + 1] * o_c[r] + gtk[:, 3 * g + 1:3 * g + 2] * o_s[r]
                    + gtk[:, 3 * g + 2:3 * g + 3] * o_w[r])
    o_ref[0] = jnp.concatenate(outs, axis=1).astype(BF16)


def _nsa_prompt(qn, kc, vc, katt, gates, tsel, twin, acmp, ovt):
    b, t, _ = qn.shape
    nqb = t // Q_BLOCK
    nc = kc.shape[1]
    top = min(SEL_TOP, -(-t // SEL_BLOCK))
    assert t >= WIN_TILE and t % FAR_TILE == 0 and t // SEL_BLOCK <= 128
    n_win = WINDOW // Q_BLOCK
    kspec = lambda c: pl.BlockSpec((1, t, 128), lambda i, k, n: (i, 0, c))
    return pl.pallas_call(
        functools.partial(_attn_kernel, top=top),
        grid=(b, NSA_KV, nqb),
        in_specs=[pl.BlockSpec((1, Q_BLOCK, 256), lambda i, k, n: (i, n, k)),
                  pl.BlockSpec((1, nc, 128), lambda i, k, n: (i, 0, 0)),
                  pl.BlockSpec((1, nc, 128), lambda i, k, n: (i, 0, 0)),
                  pl.BlockSpec((1, t, 256), lambda i, k, n: (i, 0, 0)), kspec(2), kspec(3), kspec(4),
                  pl.BlockSpec((1, Q_BLOCK, 128), lambda i, k, n: (i, n, 0)),
                  pl.BlockSpec((1, 4, 512, NEAR_TILE), lambda i, k, n: (k, 0, 0, 0)),
                  pl.BlockSpec((1, 1, 512, WIN_TILE), lambda i, k, n: (k, jnp.minimum(n, n_win), 0, 0)),
                  pl.BlockSpec((1, 512, 32), lambda i, k, n: (k, 0, 0)),
                  pl.BlockSpec((128, nc), lambda i, k, n: (0, 0))],
        out_specs=pl.BlockSpec((1, Q_BLOCK, 256), lambda i, k, n: (i, n, k)),
        out_shape=jax.ShapeDtypeStruct((b, t, 512), BF16),
        compiler_params=_cparams(3),
    )(qn, kc, vc, katt, katt, katt, katt, gates, tsel, twin, acmp, ovt)


def _log_sigmoid(z):
    return jnp.minimum(z, 0.0) - jnp.log1p(jnp.exp(-jnp.abs(z)))


def _gla_kernel(x_ref, s0_ref, wg_ref, bg_ref, ggo_ref, tri_ref, o_ref, st_ref, s_scr, *, n_chunks):
    @pl.when(pl.program_id(1) == 0)
    def _():
        s_scr[...] = s0_ref[0]

    c_len = GLA_CHUNK
    tri = tri_ref[...]
    ti = lax.broadcasted_iota(jnp.int32, (c_len, c_len), 0)
    si = lax.broadcasted_iota(jnp.int32, (c_len, c_len), 1)
    causal = si <= ti
    for c in range(n_chunks):
        rs = slice(c_len * c, c_len * (c + 1))
        q = x_ref[0, rs, 0:256]
        kk = x_ref[0, rs, 256:512]
        v = x_ref[0, rs, 512:1024]
        lr = x_ref[0, rs, 1024:1152]
        r = x_ref[0, rs, 1152:1664]
        la = _log_sigmoid(_dot(lr.astype(BF16), wg_ref[...]) + bg_ref[...]) * (1.0 / GLA_TAU)
        a1 = la.astype(BF16)
        r1 = la - a1.astype(F32)
        a2 = r1.astype(BF16)
        a3 = (r1 - a2.astype(F32)).astype(BF16)
        cb = _dot(tri, a1) + _dot(tri, a2) + _dot(tri, a3)
        last = cb[c_len - 1:c_len, :]
        mid = cb[c_len // 2:c_len // 2 + 1, :]
        qe = (q * jnp.exp(cb)).astype(BF16)
        qa = (q * jnp.exp(jnp.minimum(cb - mid, EXP_CLAMP))).astype(BF16)
        kb = (kk * jnp.exp(jnp.minimum(mid - cb, EXP_CLAMP))).astype(BF16)
        ke = (kk * jnp.exp(last - cb)).astype(BF16)
        dec = jnp.exp(last)
        for h in range(GLA_HEADS):
            ks = slice(GLA_DK * h, GLA_DK * (h + 1))
            vs = slice(GLA_DV * h, GLA_DV * (h + 1))
            att = jnp.where(causal, _dot_nt(qa[:, ks], kb[:, ks]), 0.0)
            vh = v[:, vs].astype(BF16)
            st = s_scr[h]
            o = _dot(att.astype(BF16), vh) + _dot_nt(qe[:, ks], st.astype(BF16))
            s_scr[h] = st * dec[:, ks] + _dot_tn(vh, ke[:, ks])
            on = _rms(o, ggo_ref[...])
            rh = r[:, vs]
            o_ref[0, rs, vs] = (on * (rh * _sigmoid(rh))).astype(BF16)
    st_ref[0] = s_scr[...]


def _gla_prompt(gla_in, s0t, wg, bg, ggo, tri):
    b, t, w = gla_in.shape
    ct = min(256, t)
    assert t % ct == 0 and ct % GLA_CHUNK == 0
    const = lambda shape: pl.BlockSpec(shape, lambda i, j: (0,) * len(shape))
    return pl.pallas_call(
        functools.partial(_gla_kernel, n_chunks=ct // GLA_CHUNK),
        grid=(b, t // ct),
        in_specs=[pl.BlockSpec((1, ct, w), lambda i, j: (i, j, 0)),
                  pl.BlockSpec((1, GLA_HEADS, GLA_DV, GLA_DK), lambda i, j: (i, 0, 0, 0)),
                  const((128, 256)), const((1, 256)), const((1, 128)), const((GLA_CHUNK, GLA_CHUNK))],
        out_specs=[pl.BlockSpec((1, ct, 512), lambda i, j: (i, j, 0)),
                   pl.BlockSpec((1, GLA_HEADS, GLA_DV, GLA_DK), lambda i, j: (i, 0, 0, 0))],
        out_shape=[jax.ShapeDtypeStruct((b, t, 512), BF16),
                   jax.ShapeDtypeStruct((b, GLA_HEADS, GLA_DV, GLA_DK), F32)],
        scratch_shapes=[pltpu.VMEM((GLA_HEADS, GLA_DV, GLA_DK), F32)],
        compiler_params=_cparams(2),
    )(gla_in, s0t, wg, bg, ggo, tri)


def _gla_step_kernel(q_ref, k_ref, lr_ref, v_ref, r_ref, s_ref, wgt_ref, bgt_ref, ggo_ref, o_ref, sn_ref):
    lr = lr_ref[0]
    for h in range(GLA_HEADS):
        z = jnp.sum(wgt_ref[h] * lr, axis=-1, keepdims=True) + bgt_ref[h]
        a = jnp.exp(_log_sigmoid(z) * (1.0 / GLA_TAU))
        s0 = s_ref[0, h]
        kh = k_ref[0, h]
        qh = q_ref[0, h]
        vh = v_ref[0, h]
        sn_ref[0, h] = a * s0 + kh * vh
        o = jnp.sum((qh * a) * s0, axis=0, keepdims=True) + jnp.sum(qh * kh, axis=0, keepdims=True) * vh
        on = _rms(o, ggo_ref[...])
        rh = r_ref[0, h]
        o_ref[0, h] = on * (rh * _sigmoid(rh))


def _gla_step(q_col, k_col, lr, v_row, r_row, s0, wgt, bgt, ggo):
    b = q_col.shape[0]
    const = lambda shape: pl.BlockSpec(shape, lambda i: (0,) * len(shape))
    per_b = lambda shape: pl.BlockSpec((1,) + shape, lambda i: (i,) + (0,) * len(shape))
    return pl.pallas_call(
        _gla_step_kernel,
        grid=(b,),
        in_specs=[per_b((GLA_HEADS, GLA_DK, 1)), per_b((GLA_HEADS, GLA_DK, 1)), per_b((1, 128)),
                  per_b((GLA_HEADS, 1, GLA_DV)), per_b((GLA_HEADS, 1, GLA_DV)), per_b((GLA_HEADS, GLA_DK, GLA_DV)),
                  const((GLA_HEADS, GLA_DK, 128)), const((GLA_HEADS, GLA_DK, 1)), const((1, 128))],
        out_specs=[per_b((GLA_HEADS, 1, GLA_DV)), per_b((GLA_HEADS, GLA_DK, GLA_DV))],
        out_shape=[jax.ShapeDtypeStruct((b, GLA_HEADS, 1, GLA_DV), F32),
                   jax.ShapeDtypeStruct((b, GLA_HEADS, GLA_DK, GLA_DV), F32)],
        compiler_params=_cparams(1),
    )(q_col, k_col, lr, v_row, r_row, s0, wgt, bgt, ggo)


def _xatt_kernel(xq_ref, mem_ref, o_ref):
    for h in range(X_HEADS):
        ls = slice(X_DIM * h, X_DIM * (h + 1))
        kh = mem_ref[0, :, ls].astype(BF16)
        vh = mem_ref[0, :, 512 + X_DIM * h:512 + X_DIM * (h + 1)].astype(BF16)
        s = _dot_nt(xq_ref[0, :, ls], kh)
        p = jnp.exp(s - jnp.max(s, axis=-1, keepdims=True))
        p = p / jnp.sum(p, axis=-1, keepdims=True)
        o_ref[0, :, ls] = _dot(p.astype(BF16), vh).astype(BF16)


def _xatt(xq, memkv):
    b, t, _ = xq.shape
    m = memkv.shape[1]
    tq = min(512, t)
    assert t % tq == 0
    return pl.pallas_call(
        _xatt_kernel,
        grid=(b, t // tq),
        in_specs=[pl.BlockSpec((1, tq, 512), lambda i, j: (i, j, 0)),
                  pl.BlockSpec((1, m, 1024), lambda i, j: (i, 0, 0))],
        out_specs=pl.BlockSpec((1, tq, 512), lambda i, j: (i, j, 0)),
        out_shape=jax.ShapeDtypeStruct((b, t, 512), BF16),
        compiler_params=_cparams(2),
    )(xq, memkv)


def _merge_kernel(on_ref, og_ref, ox_ref, mg_ref, x_ref, wn_ref, wg_ref, wx_ref, wo_ref, x1_ref):
    d = x_ref.shape[-1]
    merged = (mg_ref[:, 0:d].astype(F32) * _dot(on_ref[...], wn_ref[...])
              + mg_ref[:, d:2 * d].astype(F32) * _dot(og_ref[...], wg_ref[...])
              + mg_ref[:, 2 * d:3 * d].astype(F32) * _dot(ox_ref[...], wx_ref[...]))
    x1_ref[...] = x_ref[...] + _dot(merged.astype(BF16), wo_ref[...])


def _merge(o_nsa, o_gla, o_x, mg, x, wn, wg, wx, wo):
    m, d = x.shape
    tm = min(512, m)
    assert m % tm == 0
    row = lambda w: pl.BlockSpec((tm, w), lambda i: (i, 0))
    const = lambda shape: pl.BlockSpec(shape, lambda i: (0,) * len(shape))
    return pl.pallas_call(
        _merge_kernel,
        grid=(m // tm,),
        in_specs=[row(512), row(512), row(512), row(3 * d), row(d),
                  const((512, d)), const((512, d)), const((512, d)), const((d, d))],
        out_specs=row(d),
        out_shape=jax.ShapeDtypeStruct((m, d), F32),
        compiler_params=_cparams(1),
    )(o_nsa, o_gla, o_x, mg, x, wn, wg, wx, wo)


def _ffn_seq_kernel(x_ref, past_ref, g_ref, wup_ref, cw_ref, cb_ref, wdn_ref, y_ref, tail_ref, carry_ref):
    f = cw_ref.shape[-1]
    tm = x_ref.shape[1]

    @pl.when(pl.program_id(1) == 0)
    def _():
        carry_ref[...] = jnp.zeros(carry_ref.shape, F32)
        carry_ref[6:8, :] = past_ref[0]

    x1 = x_ref[0]
    ug = _dot(_rms(x1, g_ref[...]).astype(BF16), wup_ref[...])
    u = ug[:, 0:f]
    g = ug[:, f:2 * f]
    row = lax.broadcasted_iota(jnp.int32, (tm, f), 0)
    p1 = carry_ref[7:8, :]
    p2 = carry_ref[6:7, :]
    gm1 = jnp.where(row == 0, p1, pltpu.roll(g, 1, 0))
    gm2 = jnp.where(row == 0, p2, jnp.where(row == 1, p1, pltpu.roll(g, 2, 0)))
    gc = cb_ref[...] + cw_ref[0:1, :] * gm2 + cw_ref[1:2, :] * gm1 + cw_ref[2:3, :] * g
    y_ref[0] = x1 + _dot((_gelu(gc) * u).astype(BF16), wdn_ref[...])
    carry_ref[...] = g[tm - 8:tm, :]
    tail_ref[0] = g[tm - 8:tm, :]


def _ffn_seq(x1, conv_past, g_ffn, w_up, conv_w, conv_b, w_down):
    b, t, d = x1.shape
    f = conv_w.shape[-1]
    tm = min(256, t)
    assert t % tm == 0 and tm >= 8
    const = lambda shape: pl.BlockSpec(shape, lambda i, j: (0,) * len(shape))
    return pl.pallas_call(
        _ffn_seq_kernel,
        grid=(b, t // tm),
        in_specs=[pl.BlockSpec((1, tm, d), lambda i, j: (i, j, 0)),
                  pl.BlockSpec((1, 2, f), lambda i, j: (i, 0, 0)),
                  const((1, d)), const((d, 2 * f)), const((3, f)), const((1, f)), const((f, d))],
        out_specs=[pl.BlockSpec((1, tm, d), lambda i, j: (i, j, 0)),
                   pl.BlockSpec((1, 8, f), lambda i, j: (i, 0, 0))],
        out_shape=[jax.ShapeDtypeStruct((b, t, d), F32), jax.ShapeDtypeStruct((b, 8, f), F32)],
        scratch_shapes=[pltpu.VMEM((8, f), F32)],
        compiler_params=_cparams(2),
    )(x1, conv_past, g_ffn, w_up, conv_w, conv_b, w_down)


def _ffn_step_kernel(x_ref, p0_ref, p1_ref, g_ref, wup_ref, cw_ref, cb_ref, wdn_ref, y_ref, gnew_ref):
    f = cw_ref.shape[-1]
    x1 = x_ref[...]
    ug = _dot(_rms(x1, g_ref[...]).astype(BF16), wup_ref[...])
    u = ug[:, 0:f]
    g = ug[:, f:2 * f]
    gc = cb_ref[...] + cw_ref[0:1, :] * p0_ref[...] + cw_ref[1:2, :] * p1_ref[...] + cw_ref[2:3, :] * g
    y_ref[...] = x1 + _dot((_gelu(gc) * u).astype(BF16), wdn_ref[...])
    gnew_ref[...] = g


def _ffn_step(x1, p0, p1, g_ffn, w_up, conv_w, conv_b, w_down):
    m, d = x1.shape
    f = conv_w.shape[-1]
    full = lambda shape: pl.BlockSpec(shape, lambda i: (0,) * len(shape))
    return pl.pallas_call(
        _ffn_step_kernel,
        grid=(1,),
        in_specs=[full((m, d)), full((m, f)), full((m, f)), full((1, d)), full((d, 2 * f)), full((3, f)),
                  full((1, f)), full((f, d))],
        out_specs=[full((m, d)), full((m, f))],
        out_shape=[jax.ShapeDtypeStruct((m, d), F32), jax.ShapeDtypeStruct((m, f), F32)],
        compiler_params=_cparams(1),
    )(x1, p0, p1, g_ffn, w_up, conv_w, conv_b, w_down)


def _decode_kernel(pt_ref, *refs, top, n_sel, pages_per_step):
    del pt_ref
    page_refs = refs[:pages_per_step]
    (q8_ref, new_ref, neww_ref, cwin_ref, gt_ref, bc_ref, bs_ref, bw_ref, b0_ref, ov_ref, et_ref, pe_ref, w1_ref,
     w2_ref, gk0_ref, p64_ref, o_ref, wout_ref, xs_ref, kv_ref) = refs[pages_per_step:]
    p = pl.program_id(1)
    page = page_refs[0].shape[1]
    for u, page_ref in enumerate(page_refs):
        start = pl.multiple_of((p * pages_per_step + u) * page, page)
        for c in range(2):
            xs_ref[c, pl.ds(start, page), :] = page_ref[0, :, 128 * c:128 * (c + 1)]
            kv_ref[c, pl.ds(start, page), :] = page_ref[0, :, 128 * (c + 2):128 * (c + 3)].astype(BF16)

    @pl.when(p == pl.num_programs(1) - 1)
    def _():
        length = xs_ref.shape[1]
        n_chunks = length // CMP_STRIDE
        q8 = q8_ref[0]
        q8f = q8.astype(F32)
        rowk = lax.broadcasted_iota(jnp.int32, (8, 128), 0) // NSA_GROUP
        lane_half = lax.broadcasted_iota(jnp.int32, (8, 128), 1) // HEAD_DIM

        def half_mask(x):
            return jnp.where(rowk == lane_half, x, 0.0)

        load_j = lambda kind, j: xs_ref[kind, pl.ds(j, n_chunks, stride=CMP_STRIDE), :]
        kc, vc = _compress_core(load_j, n_chunks, pe_ref, w1_ref, w2_ref, gk0_ref, p64_ref)
        kc = kc.astype(BF16)
        vc = vc.astype(BF16)

        p_c = _masked_softmax_rows(_dot_nt(q8, kc) + bc_ref[...]).astype(BF16)
        o_c = half_mask(_dot(p_c, vc))
        imp8 = _dot(p_c, ov_ref[...])
        nsp = imp8.shape[1]
        imp = jnp.concatenate([jnp.sum(imp8[0:4], axis=0, keepdims=True),
                               jnp.sum(imp8[4:8], axis=0, keepdims=True)], axis=0)
        jj = lax.broadcasted_iota(jnp.int32, (2, nsp), 1)
        tblk = length // SEL_BLOCK
        forced = (jj == 0) | (jj == tblk) | (jj == tblk - 1)
        score = jnp.where(jj * SEL_BLOCK <= length, imp + jnp.where(forced, FORCE_BONUS, 0.0), NEG)
        score = jnp.where(jj < n_sel, score, -2e38)
        sel = _select_blocks(score, top, 1)
        pen = jnp.where(sel, 0.0, SEL_PENALTY)
        pen8 = jnp.concatenate([jnp.broadcast_to(pen[0:1], (4, nsp)), jnp.broadcast_to(pen[1:2], (4, nsp))], axis=0)

        new = new_ref[0]
        neww = neww_ref[0]
        b0 = b0_ref[...]

        def attend(s_past, s_new, v_past, v_new):
            m = jnp.maximum(jnp.max(s_past, axis=-1, keepdims=True), s_new)
            pp = jnp.exp(s_past - m)
            pn = jnp.exp(s_new - m)
            l = jnp.sum(pp, axis=-1, keepdims=True) + pn
            o = _dot(pp.astype(BF16), v_past) + pn.astype(BF16).astype(F32) * v_new.astype(BF16).astype(F32)
            return half_mask(o / l)

        def new_score(k_new):
            return jnp.sum(q8f * k_new.astype(BF16).astype(F32), axis=-1, keepdims=True) + b0

        k_past = kv_ref[0]
        v_past = kv_ref[1]
        s_past = _dot_nt(q8, k_past) + _dot_nt(pen8[:, 0:128].astype(BF16), et_ref[...]) + bs_ref[...]
        lane = lax.broadcasted_iota(jnp.int32, (8, nsp), 1)
        pen_new = jnp.sum(jnp.where(lane == tblk, pen8, 0.0), axis=-1, keepdims=True)
        o_s = attend(s_past, new_score(new[:, 256:384]) + pen_new, v_past, new[:, 384:512])

        cw = cwin_ref[0]
        s_w = _dot_nt(q8, cw[:, 0:128].astype(BF16)) + bw_ref[...]
        o_w = attend(s_w, new_score(neww[:, 0:128]), cw[:, 128:256].astype(BF16), neww[:, 128:256])

        gt = gt_ref[0]
        o_ref[0] = gt[:, 0:1] * o_c + gt[:, 1:2] * o_s + gt[:, 2:3] * o_w

        wl = cw.shape[0]
        wrow = lax.broadcasted_iota(jnp.int32, cw.shape, 0)
        wout_ref[0] = jnp.where(wrow == wl - 1, neww, pltpu.roll(cw, wl - 1, 0))


def _nsa_decode(page_table, cache2d, q8, new_rows, new_win, cache_win, gates8, bc, bs, bw, b0, ov, et,
                pe, w1, w2, gk0, p64):
    db, n_pages = page_table.shape
    page = cache2d.shape[1]
    length = n_pages * page
    n_sel = -(-(length + 1) // SEL_BLOCK)
    top = min(SEL_TOP, n_sel)
    wl = cache_win.shape[1]
    n_chunks = length // CMP_STRIDE
    nsp = ov.shape[1]
    const = lambda shape: pl.BlockSpec(shape, lambda i, j, pt: (0,) * len(shape))
    per_b = lambda shape: pl.BlockSpec((1,) + shape, lambda i, j, pt: (i,) + (0,) * len(shape))
    pps = math.gcd(n_pages, DECODE_PAGES_PER_STEP)
    page_spec = lambda u: pl.BlockSpec((1, page, 512), lambda i, j, pt: (pt[i, j * pps + u], 0, 0))
    grid_spec = pltpu.PrefetchScalarGridSpec(
        num_scalar_prefetch=1,
        grid=(db, n_pages // pps),
        in_specs=[page_spec(u) for u in range(pps)] + [
                  per_b((8, 128)), per_b((1, 512)), per_b((1, 256)), per_b((wl, 256)), per_b((8, 128)),
                  const((8, n_chunks)), const((8, length)), const((8, wl)), const((8, 1)),
                  const((n_chunks, nsp)), const((length, 128)),
                  const((2, 2, CMP_STRIDE, 1, 128)), const((2, 2, CMP_STRIDE, 128, 128)), const((2, 128, 128)),
                  const((1, 128)), const((512, 512))],
        out_specs=[per_b((8, 128)), per_b((wl, 256))],
        scratch_shapes=[pltpu.VMEM((2, length, 128), F32), pltpu.VMEM((2, length, 128), BF16)],
    )
    return pl.pallas_call(
        functools.partial(_decode_kernel, top=top, n_sel=n_sel, pages_per_step=pps),
        grid_spec=grid_spec,
        out_shape=[jax.ShapeDtypeStruct((db, 8, 128), F32), jax.ShapeDtypeStruct((db, wl, 256), F32)],
        compiler_params=_cparams(2),
    )(page_table, *([cache2d] * pps), q8, new_rows, new_win, cache_win, gates8, bc, bs, bw, b0, ov, et, pe, w1, w2, gk0, p64)


def _bucket_table():
    n = np.arange(MAX_DISTANCE + 1)
    max_exact = N_BUCKETS // 2
    nf = np.maximum(n, 1).astype(np.float32)
    large = max_exact + (np.log(nf / np.float32(max_exact)) / np.float32(math.log(MAX_DISTANCE / max_exact))
                         * np.float32(N_BUCKETS - max_exact)).astype(np.int32)
    return np.where(n < max_exact, n, np.minimum(large, N_BUCKETS - 1)).astype(np.int32)


def _bias_lookup(tb, rel, valid):
    idx = np.clip(rel, 0, MAX_DISTANCE)
    vals = jnp.moveaxis(tb[idx], -1, 0)
    return jnp.where(jnp.asarray(valid)[None], vals, NEG)


def _overlap(n_cmp_pad, n_cmp, n_sel_pad, n_sel):
    cs = (np.arange(n_cmp_pad) * CMP_STRIDE)[:, None]
    ss = (np.arange(n_sel_pad) * SEL_BLOCK)[None, :]
    ov = (cs < ss + SEL_BLOCK) & (cs + CMP_BLOCK > ss)
    ov &= (np.arange(n_cmp_pad) < n_cmp)[:, None] & (np.arange(n_sel_pad) < n_sel)[None, :]
    return jnp.asarray(ov.astype(np.float32), dtype=BF16)


def _stack_rows(x):
    return x.reshape(NSA_KV, NSA_GROUP * Q_BLOCK, x.shape[-1])


def _toeplitz(tbr, shift, width, max_valid):
    n = width + Q_BLOCK - 1
    xs = shift - (width - 1) + np.arange(n)
    fvec = _bias_lookup(tbr, xs, (xs >= 0) & (xs <= max_valid))
    h = fvec.shape[0]
    hank = jnp.tile(fvec, (1, Q_BLOCK + 1))[:, :Q_BLOCK * (n + 1)].reshape(h, Q_BLOCK, n + 1)[:, :, :width]
    return hank[:, :, ::-1]


def _bias_descending(tb, top):
    far = jnp.broadcast_to(tb[MAX_DISTANCE][:, None], (tb.shape[1], top - MAX_DISTANCE + 1))
    return jnp.concatenate([far, tb[MAX_DISTANCE - 1:0:-1].T], axis=1)


def _prompt_tables(tb):
    tbr = tb - tb[MAX_DISTANCE][None, :]
    i = np.arange(Q_BLOCK)[:, None]

    def table(rel, valid):
        return _stack_rows(_bias_lookup(tbr, rel, valid))

    n_r = 4
    m_sel = _stack_rows(_toeplitz(tbr, Q_BLOCK * (n_r - 1), NEAR_TILE + Q_BLOCK * (n_r - 1), 1 << 30))
    tsel = jnp.stack([m_sel[:, :, Q_BLOCK * (n_r - 1 - r):Q_BLOCK * (n_r - 1 - r) + NEAR_TILE] for r in range(n_r)],
                     axis=1)
    n_v = WINDOW // Q_BLOCK + 1
    m_win = _stack_rows(_toeplitz(tbr, Q_BLOCK * (n_v - 1), WIN_TILE + Q_BLOCK * (n_v - 1), WINDOW - 1))
    twin = jnp.stack([m_win[:, :, Q_BLOCK * (n_v - 1 - v):Q_BLOCK * (n_v - 1 - v) + WIN_TILE] for v in range(n_v)],
                     axis=1)
    w = np.arange(16)[None, :] - 9
    rel = i - CMP_STRIDE * w - (CMP_BLOCK - 1)
    a = table(rel, rel >= 0)
    hi = a.astype(BF16)
    lo = (a - hi.astype(F32)).astype(BF16)
    return tsel, twin, jnp.concatenate([hi, lo], axis=-1)


def kernel(x_prompt, x_sample, cache_kv, cache_win, state_gla, state_conv, cache_mem, page_table, mem_prompt,
           g_mix, w_in, g_nsa_q, g_nsa_k, cmp_k_pe, cmp_k_w1, cmp_k_w2, cmp_v_pe, cmp_v_w1, cmp_v_w2,
           rel_bias, w_gla_gate, b_gla_gate, g_gla_o, g_mem, w_mem_kv, g_x_q, g_x_k,
           w_nsa_out, w_gla_out, w_x_out, w_o, g_ffn, w_up, conv_w, conv_b, w_down):
    bp, t, d = x_prompt.shape
    db = x_sample.shape[0]
    f = conv_w.shape[-1]

    offs = np.cumsum((0,) + IN_SIZES)
    segs = [w_in[:, offs[i]:offs[i + 1]] for i in range(len(IN_SIZES))]
    w_pad = jnp.concatenate([jnp.pad(s, ((0, 0), (0, pw - s.shape[1]))) for s, pw in zip(segs, PAD_SIZES)],
                            axis=1).astype(BF16)
    row = lambda v: v.reshape(1, -1).astype(F32)
    gq = row(jnp.tile(g_nsa_q, NSA_HEADS))
    gk0 = row(jnp.tile(g_nsa_k[0], NSA_KV))
    gk1 = row(jnp.tile(g_nsa_k[1], NSA_KV))
    gk2 = row(jnp.tile(g_nsa_k[2], NSA_KV))
    gxq = row(jnp.tile(g_x_q, X_HEADS))
    gxk = row(jnp.tile(g_x_k, X_HEADS))
    p64 = _block_ones(512, HEAD_DIM)
    p128 = _block_ones(512, X_DIM)
    bd2 = lambda a: jnp.concatenate([jnp.concatenate([a, jnp.zeros_like(a)], -1),
                                     jnp.concatenate([jnp.zeros_like(a), a], -1)], -2)
    pe = jnp.stack([jnp.tile(v, (1, NSA_KV)) for v in (cmp_k_pe, cmp_v_pe)]).reshape(2, 2, CMP_STRIDE, 1, 128)
    w1 = jnp.stack([bd2(v) for v in (cmp_k_w1, cmp_v_w1)]).reshape(2, 2, CMP_STRIDE, 128, 128).astype(BF16)
    w2 = jnp.stack([bd2(v) for v in (cmp_k_w2, cmp_v_w2)]).astype(BF16)
    wg_pad = jnp.pad(w_gla_gate, ((0, 128 - GLA_RANK), (0, 0))).astype(BF16)
    tri = jnp.asarray(np.tril(np.ones((GLA_CHUNK, GLA_CHUNK), np.float32)), dtype=BF16)
    tb = rel_bias.astype(F32)[_bucket_table()]
    wn, wgo, wx, wo = (w.astype(BF16) for w in (w_nsa_out, w_gla_out, w_x_out, w_o))
    wup = w_up.astype(BF16)
    wdn = w_down.astype(BF16)
    ggo = row(g_gla_o)

    rows_p, win_p, qn, katt, gates, gla_in, xq, mg = _proj_in(x_prompt, row(g_mix), w_pad, gq, gk1, gk2, gxq, p64, p128)
    memkv_p = _memory_kv(mem_prompt, row(g_mem), w_mem_kv.astype(BF16), gxk, p128)
    kc, vc = _compress(rows_p, pe, w1, w2, gk0, p64)
    n_chunks = t // CMP_STRIDE
    n_sel = -(-t // SEL_BLOCK)
    tsel, twin, acmp = _prompt_tables(tb)
    ovt = _overlap(n_chunks, n_chunks - 1, 128, n_sel).T
    o_nsa = _nsa_prompt(qn, kc, vc, katt, gates, tsel, twin, acmp, ovt)
    s0t = jnp.zeros((bp, GLA_HEADS, GLA_DV, GLA_DK), F32)
    o_gla, st = _gla_prompt(gla_in, s0t, wg_pad, row(b_gla_gate), ggo, tri)
    o_x = _xatt(xq, memkv_p)
    m = bp * t
    x1 = _merge(o_nsa.reshape(m, 512), o_gla.reshape(m, 512), o_x.reshape(m, 512), mg.reshape(m, 3 * d),
                x_prompt.reshape(m, d), wn, wgo, wx, wo)
    y_p, tail = _ffn_seq(x1.reshape(bp, t, d), jnp.zeros((bp, 2, f), F32), row(g_ffn), wup, conv_w, row(conv_b), wdn)
    wl_p = min(WINDOW, t)
    out_rows_p = rows_p.reshape(bp, t, 4, NSA_KV, HEAD_DIM)
    out_win_p = win_p[:, t - wl_p:].reshape(bp, wl_p, 2, NSA_KV, HEAD_DIM)
    out_gla_p = jnp.swapaxes(st, 2, 3)
    out_conv_p = tail[:, 6:8]
    out_mem_p = memkv_p.reshape(bp, -1, 2, X_HEADS, X_DIM)

    n_pages = page_table.shape[1]
    page = cache_kv.shape[1]
    length = n_pages * page
    wl = cache_win.shape[1]
    rows_s, win_s, qn_s, _, gates_s, gla_s, xq_s, mg_s = _proj_in(
        x_sample.reshape(1, db, d), row(g_mix), w_pad, gq, gk1, gk2, gxq, p64, p128)
    rows_s, win_s, qn_s, gates_s, gla_s, xq_s, mg_s = (a[0] for a in (rows_s, win_s, qn_s, gates_s, gla_s, xq_s, mg_s))

    eye = jnp.eye(NSA_KV, dtype=BF16)
    q8 = (qn_s.reshape(db, NSA_KV, NSA_GROUP, 1, HEAD_DIM) * eye[None, :, None, :, None]).reshape(db, 8, 128)
    gates8 = jnp.pad(gates_s[:, 0:24].reshape(db, 8, 3), ((0, 0), (0, 0), (0, 125)))
    n_chunks_s = length // CMP_STRIDE
    n_sel_s = -(-(length + 1) // SEL_BLOCK)
    nsp = -(-n_sel_s // 128) * 128
    cidx = np.arange(n_chunks_s)
    rel_c = length - (cidx * CMP_STRIDE + CMP_BLOCK - 1)
    bc = _bias_lookup(tb, rel_c, (rel_c >= 0) & (cidx < n_chunks_s - 1))
    kpos = np.arange(length)
    bs = _bias_descending(tb, length)
    bw = jnp.where(jnp.asarray(np.arange(wl, 0, -1) < WINDOW)[None], _bias_descending(tb, wl), NEG)
    b0 = tb[0].reshape(8, 1)
    ov_s = _overlap(n_chunks_s, n_chunks_s - 1, nsp, n_sel_s)
    et = jnp.asarray((kpos[:, None] // SEL_BLOCK == np.arange(128)[None, :]).astype(np.float32), dtype=BF16)
    o8, win_new = _nsa_decode(page_table, cache_kv.reshape(cache_kv.shape[0], page, 512), q8,
                              rows_s.reshape(db, 1, 512), win_s.reshape(db, 1, 256), cache_win.reshape(db, wl, 256),
                              gates8, bc, bs, bw, b0, ov_s, et, pe, w1, w2, gk0, p64)
    o8 = o8.reshape(db, NSA_KV, NSA_GROUP, NSA_KV, HEAD_DIM)
    o_nsa_s = jnp.stack([o8[:, 0, :, 0], o8[:, 1, :, 1]], axis=1).reshape(db, 512).astype(BF16)

    wgt = jnp.pad(w_gla_gate.T, ((0, 0), (0, 128 - GLA_RANK))).reshape(GLA_HEADS, GLA_DK, 128)
    o_gla_s, gla_state_s = _gla_step(
        gla_s[:, 0:256].reshape(db, GLA_HEADS, GLA_DK, 1), gla_s[:, 256:512].reshape(db, GLA_HEADS, GLA_DK, 1),
        gla_s[:, 1024:1152].reshape(db, 1, 128), gla_s[:, 512:1024].reshape(db, GLA_HEADS, 1, GLA_DV),
        gla_s[:, 1152:1664].reshape(db, GLA_HEADS, 1, GLA_DV), state_gla.astype(F32), wgt,
        b_gla_gate.reshape(GLA_HEADS, GLA_DK, 1), ggo)
    o_gla_s = o_gla_s.reshape(db, 512).astype(BF16)

    xq_pad = jnp.pad(xq_s.reshape(db, 1, 512), ((0, 0), (0, 15), (0, 0)))
    o_x_s = _xatt(xq_pad, cache_mem.reshape(db, -1, 1024))[:, 0]
    x1_s = _merge(o_nsa_s, o_gla_s, o_x_s, mg_s, x_sample.reshape(db, d), wn, wgo, wx, wo)
    y_s, g_new = _ffn_step(x1_s, state_conv[:, 0], state_conv[:, 1], row(g_ffn), wup, conv_w, row(conv_b), wdn)

    out_rows_s = rows_s.reshape(db, 1, 4, NSA_KV, HEAD_DIM)
    out_win_s = win_new.reshape(db, wl, 2, NSA_KV, HEAD_DIM)
    out_conv_s = jnp.stack([state_conv[:, 1], g_new], axis=1)
    return (y_p, y_s.reshape(db, 1, d), out_rows_p, out_win_p, out_gla_p, out_conv_p, out_mem_p,
            out_rows_s, out_win_s, gla_state_s, out_conv_s)
```

```python
import functools
import math

import numpy as np
import jax
import jax.numpy as jnp
from jax import lax
from jax.experimental import pallas as pl
from jax.experimental.pallas import tpu as pltpu

F32 = jnp.float32
BF16 = jnp.bfloat16

NSA_HEADS = 8
NSA_KV = 2
NSA_GROUP = 4
HEAD_DIM = 64
CMP_BLOCK = 32
CMP_STRIDE = 16
SEL_BLOCK = 64
SEL_TOP = 16
WINDOW = 512
Q_BLOCK = 128
FORCE_BONUS = 1e4
GLA_HEADS = 4
GLA_DK = 64
GLA_DV = 128
GLA_RANK = 16
GLA_TAU = 16.0
GLA_CHUNK = 64
X_HEADS = 4
X_DIM = 128
N_BUCKETS = 32
MAX_DISTANCE = 128
EPS = 1e-6
LOG2E = math.log2(math.e)
NEG = -1e30
TINY = 1e-30
SEL_PENALTY = -1e9
MASKED_BELOW = -5e29
EXP_CLAMP = 80.0

IN_SIZES = (512, 768, 24, 256, 256, 512, 16, 512, 512, 3072)
PAD_SIZES = (512, 768, 128, 256, 256, 512, 128, 512, 512, 3072)
PAD_OFFS = tuple(int(v) for v in np.cumsum((0,) + PAD_SIZES))
D_IN_PAD = PAD_OFFS[-1]
GLA_IN_W = 256 + 256 + 512 + 128 + 512

VMEM_LIMIT = 56 * 1024 * 1024
FAR_TILE = 256
NEAR_TILE = 512
WIN_TILE = WINDOW + Q_BLOCK
KATT_W = 640
DECODE_PAGES_PER_STEP = 8


def _cparams(n_axes):
    return pltpu.CompilerParams(dimension_semantics=("arbitrary",) * n_axes, vmem_limit_bytes=VMEM_LIMIT)


def _dot(a, b):
    return jnp.dot(a, b, preferred_element_type=F32)


def _dot_nt(a, b):
    return lax.dot_general(a, b, (((1,), (1,)), ((), ())), preferred_element_type=F32)


def _dot_tn(a, b):
    return lax.dot_general(a, b, (((0,), (0,)), ((), ())), preferred_element_type=F32)


def _split_dot(x, m):
    hi = x.astype(BF16)
    lo = (x - hi.astype(F32)).astype(BF16)
    return _dot(hi, m) + _dot(lo, m)


def _rms(x, g):
    return x * lax.rsqrt(jnp.mean(x * x, axis=-1, keepdims=True) + EPS) * g


def _group_rms(x, pmat, gsize, g):
    ss = _split_dot(x * x, pmat)
    return x * lax.rsqrt(ss * (1.0 / gsize) + EPS) * g


def _gelu(x):
    return 0.5 * x * (1.0 + jnp.tanh(math.sqrt(2.0 / math.pi) * (x + 0.044715 * (x * x * x))))


def _sigmoid(x):
    return 1.0 / (1.0 + jnp.exp(-x))


def _block_ones(n, gsize):
    i = np.arange(n) // gsize
    return jnp.asarray((i[:, None] == i[None, :]).astype(np.float32), dtype=BF16)


def _proj_kernel(x_ref, gmix_ref, w_ref, gq_ref, gk1_ref, gk2_ref, gxq_ref, p64_ref, p128_ref,
                 rows_ref, win_ref, qn_ref, gates_ref, gla_ref, xq_ref, mg_ref, *attn_refs):
    x = x_ref[0]
    h = _rms(x, gmix_ref[...]).astype(BF16)
    o = PAD_OFFS

    def seg(i):
        return _dot(h, w_ref[:, o[i]:o[i + 1]])

    p64 = p64_ref[...]
    p64s = p64_ref[0:128, 0:128]
    qn = _group_rms(seg(0), p64, HEAD_DIM, gq_ref[...]) * (HEAD_DIM ** -0.5)
    qn_ref[0] = qn.astype(BF16)
    if attn_refs:
        qt_ref = attn_refs[4]
        for u in range(x.shape[0] // Q_BLOCK):
            qt_ref[0, u] = (qn[Q_BLOCK * u:Q_BLOCK * (u + 1)] * LOG2E).T.astype(BF16)

    kv = seg(1)
    k_sel = _group_rms(kv[:, 256:384], p64s, HEAD_DIM, gk1_ref[...])
    k_win = _group_rms(kv[:, 512:640], p64s, HEAD_DIM, gk2_ref[...])
    rows_ref[0, :, 0:256] = kv[:, 0:256]
    rows_ref[0, :, 256:384] = k_sel
    rows_ref[0, :, 384:512] = kv[:, 384:512]
    win_ref[0, :, 0:128] = k_win
    win_ref[0, :, 128:256] = kv[:, 640:768]
    if attn_refs:
        ka_ref, kw_ref, vst_ref, vwt_ref, _ = attn_refs
        tm = x.shape[0]
        tpos = pl.program_id(1) * tm + lax.broadcasted_iota(jnp.int32, (tm, 128), 0)
        blk = lax.broadcasted_iota(jnp.int32, (tm, 128), 1)
        ka_ref[0, :, 0:128] = k_sel.astype(BF16)
        ka_ref[0, :, 128:256] = jnp.where(tpos // SEL_BLOCK == blk, 1.0, 0.0).astype(BF16)
        kw_ref[0] = k_win.astype(BF16)
        for u in range(tm // FAR_TILE):
            vst_ref[0, u] = kv[FAR_TILE * u:FAR_TILE * (u + 1), 384:512].T.astype(BF16)
        for u in range(tm // Q_BLOCK):
            vwt_ref[0, u] = kv[Q_BLOCK * u:Q_BLOCK * (u + 1), 640:768].T.astype(BF16)

    gates_ref[0] = _sigmoid(seg(2))
    gla_ref[0, :, 0:256] = seg(3) * (GLA_DK ** -0.5)
    gla_ref[0, :, 256:512] = seg(4)
    gla_ref[0, :, 512:1024] = seg(5)
    gla_ref[0, :, 1024:1152] = seg(6)
    gla_ref[0, :, 1152:1664] = seg(7)
    xq = _group_rms(seg(8), p128_ref[...], X_DIM, gxq_ref[...]) * (X_DIM ** -0.5)
    xq_ref[0] = xq.astype(BF16)
    mg_ref[0] = _sigmoid(seg(9)).astype(BF16)


def _proj_in(x, g_mix, w_pad, gq, gk1, gk2, gxq, p64, p128, attn_layout):
    b, t, d = x.shape
    tm = min(256, t)
    assert t % tm == 0
    widths = [512, 256, 512, 128, GLA_IN_W, 512, 3072]
    dtypes = [F32, F32, BF16, F32, F32, BF16, BF16]
    out_specs = [pl.BlockSpec((1, tm, w), lambda i, j: (i, j, 0)) for w in widths]
    out_shape = [jax.ShapeDtypeStruct((b, t, w), dt) for w, dt in zip(widths, dtypes)]
    if attn_layout:
        assert tm % Q_BLOCK == 0
        for w in (256, 128):
            out_specs.append(pl.BlockSpec((1, tm, w), lambda i, j: (i, j, 0)))
            out_shape.append(jax.ShapeDtypeStruct((b, t, w), BF16))
        assert tm % FAR_TILE == 0
        for rows, width in ((128, FAR_TILE), (128, Q_BLOCK), (512, Q_BLOCK)):
            out_specs.append(pl.BlockSpec((1, tm // width, rows, width), lambda i, j: (i, j, 0, 0)))
            out_shape.append(jax.ShapeDtypeStruct((b, t // width, rows, width), BF16))
    const = lambda shape: pl.BlockSpec(shape, lambda i, j: (0,) * len(shape))
    return pl.pallas_call(
        _proj_kernel,
        grid=(b, t // tm),
        in_specs=[pl.BlockSpec((1, tm, d), lambda i, j: (i, j, 0)),
                  const((1, d)), const((d, D_IN_PAD)), const((1, 512)), const((1, 128)), const((1, 128)),
                  const((1, 512)), const((512, 512)), const((512, 512))],
        out_specs=out_specs,
        out_shape=out_shape,
        compiler_params=_cparams(2),
    )(x, g_mix, w_pad, gq, gk1, gk2, gxq, p64, p128)


def _memkv_kernel(m_ref, g_ref, w_ref, gk_ref, p128_ref, o_ref):
    h = _rms(m_ref[0], g_ref[...]).astype(BF16)
    kv = _dot(h, w_ref[...])
    o_ref[0, :, 0:512] = _group_rms(kv[:, 0:512], p128_ref[...], X_DIM, gk_ref[...])
    o_ref[0, :, 512:1024] = kv[:, 512:1024]


def _memory_kv(mem, g_mem, w_mem, gxk, p128):
    b, m, d = mem.shape
    const = lambda shape: pl.BlockSpec(shape, lambda i: (0,) * len(shape))
    return pl.pallas_call(
        _memkv_kernel,
        grid=(b,),
        in_specs=[pl.BlockSpec((1, m, d), lambda i: (i, 0, 0)), const((1, d)), const((d, 1024)),
                  const((1, 512)), const((512, 512))],
        out_specs=pl.BlockSpec((1, m, 1024), lambda i: (i, 0, 0)),
        out_shape=jax.ShapeDtypeStruct((b, m, 1024), F32),
        compiler_params=_cparams(1),
    )(mem, g_mem, w_mem, gxk, p128)


def _compress_core(load_j, n_chunks, pe_ref, w1_ref, w2_ref, gk0_ref, p64_ref):
    outs = []
    for kind in range(2):
        acc0 = jnp.zeros((n_chunks, 128), F32)
        acc1 = jnp.zeros((n_chunks, 128), F32)
        for j in range(CMP_STRIDE):
            xj = load_j(kind, j)
            acc0 = acc0 + _dot((xj + pe_ref[kind, 0, j]).astype(BF16), w1_ref[kind, 0, j])
            acc1 = acc1 + _dot((xj + pe_ref[kind, 1, j]).astype(BF16), w1_ref[kind, 1, j])
        hid = acc0 + pltpu.roll(acc1, n_chunks - 1, 0)
        outs.append(_dot(_gelu(hid).astype(BF16), w2_ref[kind]))
    row = lax.broadcasted_iota(jnp.int32, (n_chunks, 128), 0)
    live = row < n_chunks - 1
    kc = _group_rms(outs[0], p64_ref[0:128, 0:128], HEAD_DIM, gk0_ref[...])
    return jnp.where(live, kc, 0.0), jnp.where(live, outs[1], 0.0)


def _compress_kernel(rk_ref, rv_ref, pe_ref, w1_ref, w2_ref, gk0_ref, p64_ref, kc_ref, vc_ref, *, n_chunks):
    refs = (rk_ref, rv_ref)
    load_j = lambda kind, j: refs[kind][0, pl.ds(j, n_chunks, stride=CMP_STRIDE), :]
    kc, vc = _compress_core(load_j, n_chunks, pe_ref, w1_ref, w2_ref, gk0_ref, p64_ref)
    kc_ref[0] = kc.astype(BF16)
    vc_ref[0] = vc.T.astype(BF16)


def _compress(rows, pe, w1, w2, gk0, p64):
    b, t, _ = rows.shape
    n_chunks = t // CMP_STRIDE
    const = lambda shape: pl.BlockSpec(shape, lambda i: (0,) * len(shape))
    return pl.pallas_call(
        functools.partial(_compress_kernel, n_chunks=n_chunks),
        grid=(b,),
        in_specs=[pl.BlockSpec((1, t, 128), lambda i: (i, 0, 0)), pl.BlockSpec((1, t, 128), lambda i: (i, 0, 1)),
                  const((2, 2, CMP_STRIDE, 1, 128)), const((2, 2, CMP_STRIDE, 128, 128)), const((2, 128, 128)),
                  const((1, 128)), const((512, 512))],
        out_specs=[pl.BlockSpec((1, n_chunks, 128), lambda i: (i, 0, 0)),
                   pl.BlockSpec((1, 128, n_chunks), lambda i: (i, 0, 0))],
        out_shape=[jax.ShapeDtypeStruct((b, n_chunks, 128), BF16), jax.ShapeDtypeStruct((b, 128, n_chunks), BF16)],
        compiler_params=_cparams(1),
    )(rows, rows, pe, w1, w2, gk0, p64)


def _masked_softmax(s, axis, exp_fn=jnp.exp):
    valid = s > MASKED_BELOW
    m = jnp.max(s, axis=axis, keepdims=True)
    p = jnp.where(valid, exp_fn(s - m), 0.0)
    return p / jnp.maximum(jnp.sum(p, axis=axis, keepdims=True), TINY)


def _select_blocks(score, top, axis):
    pos = lax.broadcasted_iota(jnp.int32, score.shape, axis).astype(F32)
    sel = jnp.zeros(score.shape, jnp.bool_)
    for _ in range(top):
        mx = jnp.max(score, axis=axis, keepdims=True)
        idx = jnp.min(jnp.where(score == mx, pos, 1e9), axis=axis, keepdims=True)
        hit = pos == idx
        sel = jnp.logical_or(sel, hit)
        score = jnp.where(hit, -3e38, score)
    return sel


def _values_times_probs(vt_tiles, p):
    out = None
    start = 0
    for vt in vt_tiles:
        part = _dot(vt, p[start:start + vt.shape[1]])
        start += vt.shape[1]
        out = part if out is None else out + part
    return out


def _softmax_update_t(s_ref, bias, vt_tiles, m_ref, l_ref, acc_ref):
    a_parts, p_parts = [], []
    for g in range(s_ref.shape[1] // Q_BLOCK):
        cs = slice(Q_BLOCK * g, Q_BLOCK * (g + 1))
        sg = s_ref[:, cs]
        if bias is not None:
            sg = sg + bias(cs)
        m_old = m_ref[:, cs]
        m_new = jnp.maximum(m_old, jnp.max(sg, axis=0, keepdims=True))
        alpha = jnp.exp2(m_old - m_new)
        p = jnp.exp2(sg - m_new)
        m_ref[:, cs] = m_new
        l_ref[:, cs] = alpha * l_ref[:, cs] + jnp.sum(p, axis=0, keepdims=True)
        a_parts.append(alpha)
        p_parts.append(p.astype(BF16))
    acc_ref[...] = (jnp.concatenate(a_parts, axis=1) * acc_ref[...]
                    + _values_times_probs(vt_tiles, jnp.concatenate(p_parts, axis=1)))


def _attn_kernel(qt_ref, kc_ref, vct_ref, ka_ref, vst_ref, kw_ref, vwt_ref, gt_ref, tsel_ref, twin_ref, acmp_ref,
                 ovt_ref, o_ref, sa_scr, sb_scr, m_scr, l_scr, acc_scr, *, top):
    k = pl.program_id(1)
    n = pl.program_id(2)
    cols = NSA_GROUP * Q_BLOCK
    vrow = pl.multiple_of(k * HEAD_DIM, HEAD_DIM)
    zero = jnp.zeros((HEAD_DIM, Q_BLOCK), BF16)
    parts = []
    for g in range(NSA_GROUP):
        piece = qt_ref[0, 0, HEAD_DIM * g:HEAD_DIM * (g + 1), :]
        parts.append(jnp.where(k == 0, jnp.concatenate([piece, zero], axis=0),
                               jnp.concatenate([zero, piece], axis=0)))
    qt = jnp.concatenate(parts, axis=1)

    nc = kc_ref.shape[1]
    c32 = lax.broadcasted_iota(jnp.int32, (nc, 32), 0)
    r32 = lax.broadcasted_iota(jnp.int32, (nc, 32), 1)
    onehot = jnp.where((c32 - 8 * n + 9) == (r32 & 15), 1.0, 0.0).astype(BF16)
    band = _dot(onehot, acmp_ref[0])
    cp = lax.broadcasted_iota(jnp.int32, (nc, cols), 0) - 8 * n + 9
    s_c = _dot(kc_ref[0], qt) + jnp.where(cp < 0, 0.0, jnp.where(cp > 15, NEG, band))
    p_c = _masked_softmax(s_c, 0, jnp.exp2).astype(BF16)
    o_c = _dot(vct_ref[0, pl.ds(vrow, HEAD_DIM), :], p_c)
    imp4 = _dot(ovt_ref[...], p_c)
    imp = imp4[:, 0:128] + imp4[:, 128:256] + imp4[:, 256:384] + imp4[:, 384:512]

    jj = lax.broadcasted_iota(jnp.int32, (128, Q_BLOCK), 0)
    tpos = n * Q_BLOCK + lax.broadcasted_iota(jnp.int32, (128, Q_BLOCK), 1)
    tblk = tpos // SEL_BLOCK
    forced = (jj == 0) | (jj == tblk) | (jj == tblk - 1)
    score = jnp.where(jj * SEL_BLOCK <= tpos, imp + jnp.where(forced, FORCE_BONUS, 0.0), NEG)
    sel_t = _select_blocks(score, top, 0)
    pen_t = jnp.where(sel_t, 0.0, SEL_PENALTY).astype(BF16)
    rhs = jnp.concatenate([qt, jnp.concatenate([pen_t] * NSA_GROUP, axis=1)], axis=0)

    n_far = jnp.maximum(n - 2, 0) // 2
    r_near = n - 2 * n_far

    def value_tiles(ref, first, count):
        return [ref[0, first + d, pl.ds(vrow, HEAD_DIM), :] for d in range(count)]

    def issue_scores(tile, s_ref):
        s_ref[...] = _dot(ka_ref[0, pl.ds(pl.multiple_of(tile * FAR_TILE, FAR_TILE), FAR_TILE), :], rhs)

    def reduce_tile(tile, s_ref, half_idx):
        bias = None
        if half_idx is not None:
            bias = lambda cs: tsel_ref[0, r_near, FAR_TILE * half_idx:FAR_TILE * (half_idx + 1), cs]
        _softmax_update_t(s_ref, bias, value_tiles(vst_ref, tile, 1), m_scr, l_scr, acc_scr)

    m_scr[...] = jnp.full((1, cols), NEG, F32)
    l_scr[...] = jnp.zeros((1, cols), F32)
    acc_scr[...] = jnp.zeros((HEAD_DIM, cols), F32)
    odd = n_far % 2

    @pl.when(odd == 1)
    def _():
        issue_scores(0, sa_scr)
        reduce_tile(0, sa_scr, None)

    issue_scores(odd, sa_scr)

    @pl.loop(0, n_far // 2)
    def _(j):
        t0 = odd + 2 * j
        issue_scores(t0 + 1, sb_scr)
        reduce_tile(t0, sa_scr, None)
        issue_scores(t0 + 2, sa_scr)
        reduce_tile(t0 + 1, sb_scr, None)

    issue_scores(n_far + 1, sb_scr)
    reduce_tile(n_far, sa_scr, 0)
    reduce_tile(n_far + 1, sb_scr, 1)
    o_s = acc_scr[...] / l_scr[...]

    wt = jnp.maximum(n - WINDOW // Q_BLOCK, 0)
    s_w = _dot(kw_ref[0, pl.ds(pl.multiple_of(wt * Q_BLOCK, Q_BLOCK), WIN_TILE), :], qt) + twin_ref[0, 0]
    p_w = jnp.exp2(s_w - jnp.max(s_w, axis=0, keepdims=True))
    o_w = (_values_times_probs(value_tiles(vwt_ref, wt, WIN_TILE // Q_BLOCK), p_w.astype(BF16))
           / jnp.sum(p_w, axis=0, keepdims=True))

    gtt = gt_ref[0].T

    def gate_row(branch):
        rows = [jnp.where(k == 0, gtt[3 * g + branch:3 * g + branch + 1],
                          gtt[12 + 3 * g + branch:12 + 3 * g + branch + 1]) for g in range(NSA_GROUP)]
        return jnp.concatenate(rows, axis=1)

    o_t = gate_row(0) * o_c + gate_row(1) * o_s + gate_row(2) * o_w
    left = jnp.concatenate([o_t[:, 0:128], o_t[:, 128:256]], axis=0).T
    right = jnp.concatenate([o_t[:, 256:384], o_t[:, 384:512]], axis=0).T
    o_ref[0] = jnp.concatenate([left, right], axis=1).astype(BF16)


def _nsa_prompt(qt, kc, vct, ka, kw, vst, vwt, gates, tsel, twin, acmp, ovt):
    b, t, _ = ka.shape
    nqb = t // Q_BLOCK
    nc = kc.shape[1]
    top = min(SEL_TOP, -(-t // SEL_BLOCK))
    assert t >= WIN_TILE and t % FAR_TILE == 0 and t // SEL_BLOCK <= 128
    n_win = WINDOW // Q_BLOCK
    per_b = lambda shape: pl.BlockSpec((1,) + shape, lambda i, k, n: (i,) + (0,) * len(shape))
    return pl.pallas_call(
        functools.partial(_attn_kernel, top=top),
        grid=(b, NSA_KV, nqb),
        in_specs=[pl.BlockSpec((1, 1, NSA_GROUP * HEAD_DIM, Q_BLOCK), lambda i, k, n: (i, n, k, 0)),
                  per_b((nc, 128)), per_b((128, nc)),
                  per_b((t, 256)), per_b((t // FAR_TILE, 128, FAR_TILE)), per_b((t, 128)), per_b((nqb, 128, Q_BLOCK)),
                  pl.BlockSpec((1, Q_BLOCK, 128), lambda i, k, n: (i, n, 0)),
                  pl.BlockSpec((1, 4, NEAR_TILE, 512), lambda i, k, n: (k, 0, 0, 0)),
                  pl.BlockSpec((1, 1, WIN_TILE, 512), lambda i, k, n: (k, jnp.minimum(n, n_win), 0, 0)),
                  pl.BlockSpec((1, 32, 512), lambda i, k, n: (k, 0, 0)),
                  pl.BlockSpec((128, nc), lambda i, k, n: (0, 0))],
        out_specs=pl.BlockSpec((1, Q_BLOCK, 256), lambda i, k, n: (i, n, k)),
        out_shape=jax.ShapeDtypeStruct((b, t, 512), BF16),
        scratch_shapes=[pltpu.VMEM((FAR_TILE, NSA_GROUP * Q_BLOCK), F32), pltpu.VMEM((FAR_TILE, NSA_GROUP * Q_BLOCK), F32),
                        pltpu.VMEM((1, NSA_GROUP * Q_BLOCK), F32), pltpu.VMEM((1, NSA_GROUP * Q_BLOCK), F32),
                        pltpu.VMEM((HEAD_DIM, NSA_GROUP * Q_BLOCK), F32)],
        compiler_params=_cparams(3),
    )(qt, kc, vct, ka, vst, kw, vwt, gates, tsel, twin, acmp, ovt)


def _log_sigmoid(z):
    return jnp.minimum(z, 0.0) - jnp.log1p(jnp.exp(-jnp.abs(z)))


def _gla_kernel(x_ref, s0_ref, wg_ref, bg_ref, ggo_ref, tri_ref, o_ref, st_ref, s_scr, *, n_chunks):
    @pl.when(pl.program_id(1) == 0)
    def _():
        s_scr[...] = s0_ref[0]

    c_len = GLA_CHUNK
    tri = tri_ref[...]
    ti = lax.broadcasted_iota(jnp.int32, (c_len, c_len), 0)
    si = lax.broadcasted_iota(jnp.int32, (c_len, c_len), 1)
    causal = si <= ti
    for c in range(n_chunks):
        rs = slice(c_len * c, c_len * (c + 1))
        q = x_ref[0, rs, 0:256]
        kk = x_ref[0, rs, 256:512]
        v = x_ref[0, rs, 512:1024]
        lr = x_ref[0, rs, 1024:1152]
        r = x_ref[0, rs, 1152:1664]
        la = _log_sigmoid(_dot(lr.astype(BF16), wg_ref[...]) + bg_ref[...]) * (1.0 / GLA_TAU)
        a1 = la.astype(BF16)
        r1 = la - a1.astype(F32)
        a2 = r1.astype(BF16)
        a3 = (r1 - a2.astype(F32)).astype(BF16)
        cb = _dot(tri, a1) + _dot(tri, a2) + _dot(tri, a3)
        last = cb[c_len - 1:c_len, :]
        mid = cb[c_len // 2:c_len // 2 + 1, :]
        qe = (q * jnp.exp(cb)).astype(BF16)
        qa = (q * jnp.exp(jnp.minimum(cb - mid, EXP_CLAMP))).astype(BF16)
        kb = (kk * jnp.exp(jnp.minimum(mid - cb, EXP_CLAMP))).astype(BF16)
        ke = (kk * jnp.exp(last - cb)).astype(BF16)
        dec = jnp.exp(last)
        for h in range(GLA_HEADS):
            ks = slice(GLA_DK * h, GLA_DK * (h + 1))
            vs = slice(GLA_DV * h, GLA_DV * (h + 1))
            att = jnp.where(causal, _dot_nt(qa[:, ks], kb[:, ks]), 0.0)
            vh = v[:, vs].astype(BF16)
            st = s_scr[h]
            o = _dot(att.astype(BF16), vh) + _dot_nt(qe[:, ks], st.astype(BF16))
            s_scr[h] = st * dec[:, ks] + _dot_tn(vh, ke[:, ks])
            on = _rms(o, ggo_ref[...])
            rh = r[:, vs]
            o_ref[0, rs, vs] = (on * (rh * _sigmoid(rh))).astype(BF16)
    st_ref[0] = s_scr[...]


def _gla_prompt(gla_in, s0t, wg, bg, ggo, tri):
    b, t, w = gla_in.shape
    ct = min(256, t)
    assert t % ct == 0 and ct % GLA_CHUNK == 0
    const = lambda shape: pl.BlockSpec(shape, lambda i, j: (0,) * len(shape))
    return pl.pallas_call(
        functools.partial(_gla_kernel, n_chunks=ct // GLA_CHUNK),
        grid=(b, t // ct),
        in_specs=[pl.BlockSpec((1, ct, w), lambda i, j: (i, j, 0)),
                  pl.BlockSpec((1, GLA_HEADS, GLA_DV, GLA_DK), lambda i, j: (i, 0, 0, 0)),
                  const((128, 256)), const((1, 256)), const((1, 128)), const((GLA_CHUNK, GLA_CHUNK))],
        out_specs=[pl.BlockSpec((1, ct, 512), lambda i, j: (i, j, 0)),
                   pl.BlockSpec((1, GLA_HEADS, GLA_DV, GLA_DK), lambda i, j: (i, 0, 0, 0))],
        out_shape=[jax.ShapeDtypeStruct((b, t, 512), BF16),
                   jax.ShapeDtypeStruct((b, GLA_HEADS, GLA_DV, GLA_DK), F32)],
        scratch_shapes=[pltpu.VMEM((GLA_HEADS, GLA_DV, GLA_DK), F32)],
        compiler_params=_cparams(2),
    )(gla_in, s0t, wg, bg, ggo, tri)


def _gla_step_kernel(q_ref, k_ref, lr_ref, v_ref, r_ref, s_ref, wgt_ref, bgt_ref, ggo_ref, o_ref, sn_ref):
    lr = lr_ref[0]
    for h in range(GLA_HEADS):
        z = jnp.sum(wgt_ref[h] * lr, axis=-1, keepdims=True) + bgt_ref[h]
        a = jnp.exp(_log_sigmoid(z) * (1.0 / GLA_TAU))
        s0 = s_ref[0, h]
        kh = k_ref[0, h]
        qh = q_ref[0, h]
        vh = v_ref[0, h]
        sn_ref[0, h] = a * s0 + kh * vh
        o = jnp.sum((qh * a) * s0, axis=0, keepdims=True) + jnp.sum(qh * kh, axis=0, keepdims=True) * vh
        on = _rms(o, ggo_ref[...])
        rh = r_ref[0, h]
        o_ref[0, h] = on * (rh * _sigmoid(rh))


def _gla_step(q_col, k_col, lr, v_row, r_row, s0, wgt, bgt, ggo):
    b = q_col.shape[0]
    const = lambda shape: pl.BlockSpec(shape, lambda i: (0,) * len(shape))
    per_b = lambda shape: pl.BlockSpec((1,) + shape, lambda i: (i,) + (0,) * len(shape))
    return pl.pallas_call(
        _gla_step_kernel,
        grid=(b,),
        in_specs=[per_b((GLA_HEADS, GLA_DK, 1)), per_b((GLA_HEADS, GLA_DK, 1)), per_b((1, 128)),
                  per_b((GLA_HEADS, 1, GLA_DV)), per_b((GLA_HEADS, 1, GLA_DV)), per_b((GLA_HEADS, GLA_DK, GLA_DV)),
                  const((GLA_HEADS, GLA_DK, 128)), const((GLA_HEADS, GLA_DK, 1)), const((1, 128))],
        out_specs=[per_b((GLA_HEADS, 1, GLA_DV)), per_b((GLA_HEADS, GLA_DK, GLA_DV))],
        out_shape=[jax.ShapeDtypeStruct((b, GLA_HEADS, 1, GLA_DV), F32),
                   jax.ShapeDtypeStruct((b, GLA_HEADS, GLA_DK, GLA_DV), F32)],
        compiler_params=_cparams(1),
    )(q_col, k_col, lr, v_row, r_row, s0, wgt, bgt, ggo)


def _xatt_kernel(xq_ref, mem_ref, o_ref):
    for h in range(X_HEADS):
        ls = slice(X_DIM * h, X_DIM * (h + 1))
        kh = mem_ref[0, :, ls].astype(BF16)
        vh = mem_ref[0, :, 512 + X_DIM * h:512 + X_DIM * (h + 1)].astype(BF16)
        s = _dot_nt(xq_ref[0, :, ls], kh)
        p = jnp.exp(s - jnp.max(s, axis=-1, keepdims=True))
        p = p / jnp.sum(p, axis=-1, keepdims=True)
        o_ref[0, :, ls] = _dot(p.astype(BF16), vh).astype(BF16)


def _xatt(xq, memkv):
    b, t, _ = xq.shape
    m = memkv.shape[1]
    tq = min(512, t)
    assert t % tq == 0
    return pl.pallas_call(
        _xatt_kernel,
        grid=(b, t // tq),
        in_specs=[pl.BlockSpec((1, tq, 512), lambda i, j: (i, j, 0)),
                  pl.BlockSpec((1, m, 1024), lambda i, j: (i, 0, 0))],
        out_specs=pl.BlockSpec((1, tq, 512), lambda i, j: (i, j, 0)),
        out_shape=jax.ShapeDtypeStruct((b, t, 512), BF16),
        compiler_params=_cparams(2),
    )(xq, memkv)


def _merge_kernel(on_ref, og_ref, ox_ref, mg_ref, x_ref, wn_ref, wg_ref, wx_ref, wo_ref, x1_ref):
    d = x_ref.shape[-1]
    merged = (mg_ref[:, 0:d].astype(F32) * _dot(on_ref[...], wn_ref[...])
              + mg_ref[:, d:2 * d].astype(F32) * _dot(og_ref[...], wg_ref[...])
              + mg_ref[:, 2 * d:3 * d].astype(F32) * _dot(ox_ref[...], wx_ref[...]))
    x1_ref[...] = x_ref[...] + _dot(merged.astype(BF16), wo_ref[...])


def _merge(o_nsa, o_gla, o_x, mg, x, wn, wg, wx, wo):
    m, d = x.shape
    tm = min(512, m)
    assert m % tm == 0
    row = lambda w: pl.BlockSpec((tm, w), lambda i: (i, 0))
    const = lambda shape: pl.BlockSpec(shape, lambda i: (0,) * len(shape))
    return pl.pallas_call(
        _merge_kernel,
        grid=(m // tm,),
        in_specs=[row(512), row(512), row(512), row(3 * d), row(d),
                  const((512, d)), const((512, d)), const((512, d)), const((d, d))],
        out_specs=row(d),
        out_shape=jax.ShapeDtypeStruct((m, d), F32),
        compiler_params=_cparams(1),
    )(o_nsa, o_gla, o_x, mg, x, wn, wg, wx, wo)


def _ffn_seq_kernel(x_ref, past_ref, g_ref, wup_ref, cw_ref, cb_ref, wdn_ref, y_ref, tail_ref, carry_ref):
    f = cw_ref.shape[-1]
    tm = x_ref.shape[1]

    @pl.when(pl.program_id(1) == 0)
    def _():
        carry_ref[...] = jnp.zeros(carry_ref.shape, F32)
        carry_ref[6:8, :] = past_ref[0]

    x1 = x_ref[0]
    ug = _dot(_rms(x1, g_ref[...]).astype(BF16), wup_ref[...])
    u = ug[:, 0:f]
    g = ug[:, f:2 * f]
    row = lax.broadcasted_iota(jnp.int32, (tm, f), 0)
    p1 = carry_ref[7:8, :]
    p2 = carry_ref[6:7, :]
    gm1 = jnp.where(row == 0, p1, pltpu.roll(g, 1, 0))
    gm2 = jnp.where(row == 0, p2, jnp.where(row == 1, p1, pltpu.roll(g, 2, 0)))
    gc = cb_ref[...] + cw_ref[0:1, :] * gm2 + cw_ref[1:2, :] * gm1 + cw_ref[2:3, :] * g
    y_ref[0] = x1 + _dot((_gelu(gc) * u).astype(BF16), wdn_ref[...])
    carry_ref[...] = g[tm - 8:tm, :]
    tail_ref[0] = g[tm - 8:tm, :]


def _ffn_seq(x1, conv_past, g_ffn, w_up, conv_w, conv_b, w_down):
    b, t, d = x1.shape
    f = conv_w.shape[-1]
    tm = min(256, t)
    assert t % tm == 0 and tm >= 8
    const = lambda shape: pl.BlockSpec(shape, lambda i, j: (0,) * len(shape))
    return pl.pallas_call(
        _ffn_seq_kernel,
        grid=(b, t // tm),
        in_specs=[pl.BlockSpec((1, tm, d), lambda i, j: (i, j, 0)),
                  pl.BlockSpec((1, 2, f), lambda i, j: (i, 0, 0)),
                  const((1, d)), const((d, 2 * f)), const((3, f)), const((1, f)), const((f, d))],
        out_specs=[pl.BlockSpec((1, tm, d), lambda i, j: (i, j, 0)),
                   pl.BlockSpec((1, 8, f), lambda i, j: (i, 0, 0))],
        out_shape=[jax.ShapeDtypeStruct((b, t, d), F32), jax.ShapeDtypeStruct((b, 8, f), F32)],
        scratch_shapes=[pltpu.VMEM((8, f), F32)],
        compiler_params=_cparams(2),
    )(x1, conv_past, g_ffn, w_up, conv_w, conv_b, w_down)


def _ffn_step_kernel(x_ref, p0_ref, p1_ref, g_ref, wup_ref, cw_ref, cb_ref, wdn_ref, y_ref, gnew_ref):
    f = cw_ref.shape[-1]
    x1 = x_ref[...]
    ug = _dot(_rms(x1, g_ref[...]).astype(BF16), wup_ref[...])
    u = ug[:, 0:f]
    g = ug[:, f:2 * f]
    gc = cb_ref[...] + cw_ref[0:1, :] * p0_ref[...] + cw_ref[1:2, :] * p1_ref[...] + cw_ref[2:3, :] * g
    y_ref[...] = x1 + _dot((_gelu(gc) * u).astype(BF16), wdn_ref[...])
    gnew_ref[...] = g


def _ffn_step(x1, p0, p1, g_ffn, w_up, conv_w, conv_b, w_down):
    m, d = x1.shape
    f = conv_w.shape[-1]
    full = lambda shape: pl.BlockSpec(shape, lambda i: (0,) * len(shape))
    return pl.pallas_call(
        _ffn_step_kernel,
        grid=(1,),
        in_specs=[full((m, d)), full((m, f)), full((m, f)), full((1, d)), full((d, 2 * f)), full((3, f)),
                  full((1, f)), full((f, d))],
        out_specs=[full((m, d)), full((m, f))],
        out_shape=[jax.ShapeDtypeStruct((m, d), F32), jax.ShapeDtypeStruct((m, f), F32)],
        compiler_params=_cparams(1),
    )(x1, p0, p1, g_ffn, w_up, conv_w, conv_b, w_down)


def _decode_kernel(pt_ref, *refs, top, n_sel, pages_per_step):
    del pt_ref
    page_refs = refs[:pages_per_step]
    (q8_ref, new_ref, neww_ref, cwin_ref, gt_ref, bc_ref, bs_ref, bw_ref, b0_ref, ov_ref, et_ref, pe_ref, w1_ref,
     w2_ref, gk0_ref, p64_ref, o_ref, wout_ref, xs_ref, kv_ref) = refs[pages_per_step:]
    p = pl.program_id(1)
    page = page_refs[0].shape[1]
    for u, page_ref in enumerate(page_refs):
        start = pl.multiple_of((p * pages_per_step + u) * page, page)
        for c in range(2):
            xs_ref[c, pl.ds(start, page), :] = page_ref[0, :, 128 * c:128 * (c + 1)]
            kv_ref[c, pl.ds(start, page), :] = page_ref[0, :, 128 * (c + 2):128 * (c + 3)].astype(BF16)

    @pl.when(p == pl.num_programs(1) - 1)
    def _():
        length = xs_ref.shape[1]
        n_chunks = length // CMP_STRIDE
        q8 = q8_ref[0]
        q8f = q8.astype(F32)
        rowk = lax.broadcasted_iota(jnp.int32, (8, 128), 0) // NSA_GROUP
        lane_half = lax.broadcasted_iota(jnp.int32, (8, 128), 1) // HEAD_DIM

        def half_mask(x):
            return jnp.where(rowk == lane_half, x, 0.0)

        load_j = lambda kind, j: xs_ref[kind, pl.ds(j, n_chunks, stride=CMP_STRIDE), :]
        kc, vc = _compress_core(load_j, n_chunks, pe_ref, w1_ref, w2_ref, gk0_ref, p64_ref)
        kc = kc.astype(BF16)
        vc = vc.astype(BF16)

        p_c = _masked_softmax(_dot_nt(q8, kc) + bc_ref[...], 1).astype(BF16)
        o_c = half_mask(_dot(p_c, vc))
        imp8 = _dot(p_c, ov_ref[...])
        nsp = imp8.shape[1]
        imp = jnp.concatenate([jnp.sum(imp8[0:4], axis=0, keepdims=True),
                               jnp.sum(imp8[4:8], axis=0, keepdims=True)], axis=0)
        jj = lax.broadcasted_iota(jnp.int32, (2, nsp), 1)
        tblk = length // SEL_BLOCK
        forced = (jj == 0) | (jj == tblk) | (jj == tblk - 1)
        score = jnp.where(jj * SEL_BLOCK <= length, imp + jnp.where(forced, FORCE_BONUS, 0.0), NEG)
        score = jnp.where(jj < n_sel, score, -2e38)
        sel = _select_blocks(score, top, 1)
        pen = jnp.where(sel, 0.0, SEL_PENALTY)
        pen8 = jnp.concatenate([jnp.broadcast_to(pen[0:1], (4, nsp)), jnp.broadcast_to(pen[1:2], (4, nsp))], axis=0)

        new = new_ref[0]
        neww = neww_ref[0]
        b0 = b0_ref[...]

        def attend(s_past, s_new, v_past, v_new):
            m = jnp.maximum(jnp.max(s_past, axis=-1, keepdims=True), s_new)
            pp = jnp.exp(s_past - m)
            pn = jnp.exp(s_new - m)
            l = jnp.sum(pp, axis=-1, keepdims=True) + pn
            o = _dot(pp.astype(BF16), v_past) + pn.astype(BF16).astype(F32) * v_new.astype(BF16).astype(F32)
            return half_mask(o / l)

        def new_score(k_new):
            return jnp.sum(q8f * k_new.astype(BF16).astype(F32), axis=-1, keepdims=True) + b0

        k_past = kv_ref[0]
        v_past = kv_ref[1]
        s_past = _dot_nt(q8, k_past) + _dot_nt(pen8[:, 0:128].astype(BF16), et_ref[...]) + bs_ref[...]
        lane = lax.broadcasted_iota(jnp.int32, (8, nsp), 1)
        pen_new = jnp.sum(jnp.where(lane == tblk, pen8, 0.0), axis=-1, keepdims=True)
        o_s = attend(s_past, new_score(new[:, 256:384]) + pen_new, v_past, new[:, 384:512])

        cw = cwin_ref[0]
        s_w = _dot_nt(q8, cw[:, 0:128].astype(BF16)) + bw_ref[...]
        o_w = attend(s_w, new_score(neww[:, 0:128]), cw[:, 128:256].astype(BF16), neww[:, 128:256])

        gt = gt_ref[0]
        o_ref[0] = gt[:, 0:1] * o_c + gt[:, 1:2] * o_s + gt[:, 2:3] * o_w

        wl = cw.shape[0]
        wrow = lax.broadcasted_iota(jnp.int32, cw.shape, 0)
        wout_ref[0] = jnp.where(wrow == wl - 1, neww, pltpu.roll(cw, wl - 1, 0))


def _nsa_decode(page_table, cache2d, q8, new_rows, new_win, cache_win, gates8, bc, bs, bw, b0, ov, et,
                pe, w1, w2, gk0, p64):
    db, n_pages = page_table.shape
    page = cache2d.shape[1]
    length = n_pages * page
    n_sel = -(-(length + 1) // SEL_BLOCK)
    top = min(SEL_TOP, n_sel)
    wl = cache_win.shape[1]
    n_chunks = length // CMP_STRIDE
    nsp = ov.shape[1]
    const = lambda shape: pl.BlockSpec(shape, lambda i, j, pt: (0,) * len(shape))
    per_b = lambda shape: pl.BlockSpec((1,) + shape, lambda i, j, pt: (i,) + (0,) * len(shape))
    pps = math.gcd(n_pages, DECODE_PAGES_PER_STEP)
    page_spec = lambda u: pl.BlockSpec((1, page, 512), lambda i, j, pt: (pt[i, j * pps + u], 0, 0))
    grid_spec = pltpu.PrefetchScalarGridSpec(
        num_scalar_prefetch=1,
        grid=(db, n_pages // pps),
        in_specs=[page_spec(u) for u in range(pps)] + [
                  per_b((8, 128)), per_b((1, 512)), per_b((1, 256)), per_b((wl, 256)), per_b((8, 128)),
                  const((8, n_chunks)), const((8, length)), const((8, wl)), const((8, 1)),
                  const((n_chunks, nsp)), const((length, 128)),
                  const((2, 2, CMP_STRIDE, 1, 128)), const((2, 2, CMP_STRIDE, 128, 128)), const((2, 128, 128)),
                  const((1, 128)), const((512, 512))],
        out_specs=[per_b((8, 128)), per_b((wl, 256))],
        scratch_shapes=[pltpu.VMEM((2, length, 128), F32), pltpu.VMEM((2, length, 128), BF16)],
    )
    return pl.pallas_call(
        functools.partial(_decode_kernel, top=top, n_sel=n_sel, pages_per_step=pps),
        grid_spec=grid_spec,
        out_shape=[jax.ShapeDtypeStruct((db, 8, 128), F32), jax.ShapeDtypeStruct((db, wl, 256), F32)],
        compiler_params=_cparams(2),
    )(page_table, *([cache2d] * pps), q8, new_rows, new_win, cache_win, gates8, bc, bs, bw, b0, ov, et, pe, w1, w2, gk0, p64)


def _bucket_table():
    n = np.arange(MAX_DISTANCE + 1)
    max_exact = N_BUCKETS // 2
    nf = np.maximum(n, 1).astype(np.float32)
    large = max_exact + (np.log(nf / np.float32(max_exact)) / np.float32(math.log(MAX_DISTANCE / max_exact))
                         * np.float32(N_BUCKETS - max_exact)).astype(np.int32)
    return np.where(n < max_exact, n, np.minimum(large, N_BUCKETS - 1)).astype(np.int32)


def _bias_lookup(tb, rel, valid):
    idx = np.clip(rel, 0, MAX_DISTANCE)
    vals = jnp.moveaxis(tb[idx], -1, 0)
    return jnp.where(jnp.asarray(valid)[None], vals, NEG)


def _overlap(n_cmp_pad, n_cmp, n_sel_pad, n_sel):
    cs = (np.arange(n_cmp_pad) * CMP_STRIDE)[:, None]
    ss = (np.arange(n_sel_pad) * SEL_BLOCK)[None, :]
    ov = (cs < ss + SEL_BLOCK) & (cs + CMP_BLOCK > ss)
    ov &= (np.arange(n_cmp_pad) < n_cmp)[:, None] & (np.arange(n_sel_pad) < n_sel)[None, :]
    return jnp.asarray(ov.astype(np.float32), dtype=BF16)


def _stack_rows(x):
    return x.reshape(NSA_KV, NSA_GROUP * Q_BLOCK, x.shape[-1])


def _toeplitz(tbr, shift, width, max_valid):
    n = width + Q_BLOCK - 1
    xs = shift - (width - 1) + np.arange(n)
    fvec = _bias_lookup(tbr, xs, (xs >= 0) & (xs <= max_valid))
    h = fvec.shape[0]
    hank = jnp.tile(fvec, (1, Q_BLOCK + 1))[:, :Q_BLOCK * (n + 1)].reshape(h, Q_BLOCK, n + 1)[:, :, :width]
    return hank[:, :, ::-1]


def _bias_descending(tb, top):
    far = jnp.broadcast_to(tb[MAX_DISTANCE][:, None], (tb.shape[1], top - MAX_DISTANCE + 1))
    return jnp.concatenate([far, tb[MAX_DISTANCE - 1:0:-1].T], axis=1)


def _prompt_tables(tb):
    tbr = (tb - tb[MAX_DISTANCE][None, :]) * LOG2E
    i = np.arange(Q_BLOCK)[:, None]

    def table(rel, valid):
        return _stack_rows(_bias_lookup(tbr, rel, valid))

    n_r = 4
    m_sel = _stack_rows(_toeplitz(tbr, Q_BLOCK * (n_r - 1), NEAR_TILE + Q_BLOCK * (n_r - 1), 1 << 30))
    tsel = jnp.stack([m_sel[:, :, Q_BLOCK * (n_r - 1 - r):Q_BLOCK * (n_r - 1 - r) + NEAR_TILE] for r in range(n_r)],
                     axis=1)
    n_v = WINDOW // Q_BLOCK + 1
    m_win = _stack_rows(_toeplitz(tbr, Q_BLOCK * (n_v - 1), WIN_TILE + Q_BLOCK * (n_v - 1), WINDOW - 1))
    twin = jnp.stack([m_win[:, :, Q_BLOCK * (n_v - 1 - v):Q_BLOCK * (n_v - 1 - v) + WIN_TILE] for v in range(n_v)],
                     axis=1)
    w = np.arange(16)[None, :] - 9
    rel = i - CMP_STRIDE * w - (CMP_BLOCK - 1)
    a = table(rel, rel >= 0)
    hi = a.astype(BF16)
    lo = (a - hi.astype(F32)).astype(BF16)
    acmp = jnp.concatenate([hi, lo], axis=-1)
    return jnp.swapaxes(tsel, 2, 3), jnp.swapaxes(twin, 2, 3), jnp.swapaxes(acmp, 1, 2)


def kernel(x_prompt, x_sample, cache_kv, cache_win, state_gla, state_conv, cache_mem, page_table, mem_prompt,
           g_mix, w_in, g_nsa_q, g_nsa_k, cmp_k_pe, cmp_k_w1, cmp_k_w2, cmp_v_pe, cmp_v_w1, cmp_v_w2,
           rel_bias, w_gla_gate, b_gla_gate, g_gla_o, g_mem, w_mem_kv, g_x_q, g_x_k,
           w_nsa_out, w_gla_out, w_x_out, w_o, g_ffn, w_up, conv_w, conv_b, w_down):
    bp, t, d = x_prompt.shape
    db = x_sample.shape[0]
    f = conv_w.shape[-1]

    offs = np.cumsum((0,) + IN_SIZES)
    segs = [w_in[:, offs[i]:offs[i + 1]] for i in range(len(IN_SIZES))]
    w_pad = jnp.concatenate([jnp.pad(s, ((0, 0), (0, pw - s.shape[1]))) for s, pw in zip(segs, PAD_SIZES)],
                            axis=1).astype(BF16)
    row = lambda v: v.reshape(1, -1).astype(F32)
    gq = row(jnp.tile(g_nsa_q, NSA_HEADS))
    gk0 = row(jnp.tile(g_nsa_k[0], NSA_KV))
    gk1 = row(jnp.tile(g_nsa_k[1], NSA_KV))
    gk2 = row(jnp.tile(g_nsa_k[2], NSA_KV))
    gxq = row(jnp.tile(g_x_q, X_HEADS))
    gxk = row(jnp.tile(g_x_k, X_HEADS))
    p64 = _block_ones(512, HEAD_DIM)
    p128 = _block_ones(512, X_DIM)
    bd2 = lambda a: jnp.concatenate([jnp.concatenate([a, jnp.zeros_like(a)], -1),
                                     jnp.concatenate([jnp.zeros_like(a), a], -1)], -2)
    pe = jnp.stack([jnp.tile(v, (1, NSA_KV)) for v in (cmp_k_pe, cmp_v_pe)]).reshape(2, 2, CMP_STRIDE, 1, 128)
    w1 = jnp.stack([bd2(v) for v in (cmp_k_w1, cmp_v_w1)]).reshape(2, 2, CMP_STRIDE, 128, 128).astype(BF16)
    w2 = jnp.stack([bd2(v) for v in (cmp_k_w2, cmp_v_w2)]).astype(BF16)
    wg_pad = jnp.pad(w_gla_gate, ((0, 128 - GLA_RANK), (0, 0))).astype(BF16)
    tri = jnp.asarray(np.tril(np.ones((GLA_CHUNK, GLA_CHUNK), np.float32)), dtype=BF16)
    tb = rel_bias.astype(F32)[_bucket_table()]
    wn, wgo, wx, wo = (w.astype(BF16) for w in (w_nsa_out, w_gla_out, w_x_out, w_o))
    wup = w_up.astype(BF16)
    wdn = w_down.astype(BF16)
    ggo = row(g_gla_o)

    rows_p, win_p, _, gates, gla_in, xq, mg, ka, kw, vst, vwt, qt = _proj_in(
        x_prompt, row(g_mix), w_pad, gq, gk1, gk2, gxq, p64, p128, True)
    memkv_p = _memory_kv(mem_prompt, row(g_mem), w_mem_kv.astype(BF16), gxk, p128)
    kc, vct = _compress(rows_p, pe, w1, w2, gk0, p64)
    n_chunks = t // CMP_STRIDE
    n_sel = -(-t // SEL_BLOCK)
    tsel, twin, acmp = _prompt_tables(tb)
    ovt = _overlap(n_chunks, n_chunks - 1, 128, n_sel).T
    o_nsa = _nsa_prompt(qt, kc, vct, ka, kw, vst, vwt, gates, tsel, twin, acmp, ovt)
    s0t = jnp.zeros((bp, GLA_HEADS, GLA_DV, GLA_DK), F32)
    o_gla, st = _gla_prompt(gla_in, s0t, wg_pad, row(b_gla_gate), ggo, tri)
    o_x = _xatt(xq, memkv_p)
    m = bp * t
    x1 = _merge(o_nsa.reshape(m, 512), o_gla.reshape(m, 512), o_x.reshape(m, 512), mg.reshape(m, 3 * d),
                x_prompt.reshape(m, d), wn, wgo, wx, wo)
    y_p, tail = _ffn_seq(x1.reshape(bp, t, d), jnp.zeros((bp, 2, f), F32), row(g_ffn), wup, conv_w, row(conv_b), wdn)
    wl_p = min(WINDOW, t)
    out_rows_p = rows_p.reshape(bp, t, 4, NSA_KV, HEAD_DIM)
    out_win_p = win_p[:, t - wl_p:].reshape(bp, wl_p, 2, NSA_KV, HEAD_DIM)
    out_gla_p = jnp.swapaxes(st, 2, 3)
    out_conv_p = tail[:, 6:8]
    out_mem_p = memkv_p.reshape(bp, -1, 2, X_HEADS, X_DIM)

    n_pages = page_table.shape[1]
    page = cache_kv.shape[1]
    length = n_pages * page
    wl = cache_win.shape[1]
    rows_s, win_s, qn_s, gates_s, gla_s, xq_s, mg_s = (a[0] for a in _proj_in(
        x_sample.reshape(1, db, d), row(g_mix), w_pad, gq, gk1, gk2, gxq, p64, p128, False))

    eye = jnp.eye(NSA_KV, dtype=BF16)
    q8 = (qn_s.reshape(db, NSA_KV, NSA_GROUP, 1, HEAD_DIM) * eye[None, :, None, :, None]).reshape(db, 8, 128)
    gates8 = jnp.pad(gates_s[:, 0:24].reshape(db, 8, 3), ((0, 0), (0, 0), (0, 125)))
    n_chunks_s = length // CMP_STRIDE
    n_sel_s = -(-(length + 1) // SEL_BLOCK)
    nsp = -(-n_sel_s // 128) * 128
    cidx = np.arange(n_chunks_s)
    rel_c = length - (cidx * CMP_STRIDE + CMP_BLOCK - 1)
    bc = _bias_lookup(tb, rel_c, (rel_c >= 0) & (cidx < n_chunks_s - 1))
    kpos = np.arange(length)
    bs = _bias_descending(tb, length)
    bw = jnp.where(jnp.asarray(np.arange(wl, 0, -1) < WINDOW)[None], _bias_descending(tb, wl), NEG)
    b0 = tb[0].reshape(8, 1)
    ov_s = _overlap(n_chunks_s, n_chunks_s - 1, nsp, n_sel_s)
    et = jnp.asarray((kpos[:, None] // SEL_BLOCK == np.arange(128)[None, :]).astype(np.float32), dtype=BF16)
    o8, win_new = _nsa_decode(page_table, cache_kv.reshape(cache_kv.shape[0], page, 512), q8,
                              rows_s.reshape(db, 1, 512), win_s.reshape(db, 1, 256), cache_win.reshape(db, wl, 256),
                              gates8, bc, bs, bw, b0, ov_s, et, pe, w1, w2, gk0, p64)
    o8 = o8.reshape(db, NSA_KV, NSA_GROUP, NSA_KV, HEAD_DIM)
    o_nsa_s = jnp.stack([o8[:, 0, :, 0], o8[:, 1, :, 1]], axis=1).reshape(db, 512).astype(BF16)

    wgt = jnp.pad(w_gla_gate.T, ((0, 0), (0, 128 - GLA_RANK))).reshape(GLA_HEADS, GLA_DK, 128)
    o_gla_s, gla_state_s = _gla_step(
        gla_s[:, 0:256].reshape(db, GLA_HEADS, GLA_DK, 1), gla_s[:, 256:512].reshape(db, GLA_HEADS, GLA_DK, 1),
        gla_s[:, 1024:1152].reshape(db, 1, 128), gla_s[:, 512:1024].reshape(db, GLA_HEADS, 1, GLA_DV),
        gla_s[:, 1152:1664].reshape(db, GLA_HEADS, 1, GLA_DV), state_gla.astype(F32), wgt,
        b_gla_gate.reshape(GLA_HEADS, GLA_DK, 1), ggo)
    o_gla_s = o_gla_s.reshape(db, 512).astype(BF16)

    xq_pad = jnp.pad(xq_s.reshape(db, 1, 512), ((0, 0), (0, 15), (0, 0)))
    o_x_s = _xatt(xq_pad, cache_mem.reshape(db, -1, 1024))[:, 0]
    x1_s = _merge(o_nsa_s, o_gla_s, o_x_s, mg_s, x_sample.reshape(db, d), wn, wgo, wx, wo)
    y_s, g_new = _ffn_step(x1_s, state_conv[:, 0], state_conv[:, 1], row(g_ffn), wup, conv_w, row(conv_b), wdn)

    out_rows_s = rows_s.reshape(db, 1, 4, NSA_KV, HEAD_DIM)
    out_win_s = win_new.reshape(db, wl, 2, NSA_KV, HEAD_DIM)
    out_conv_s = jnp.stack([state_conv[:, 1], g_new], axis=1)
    return (y_p, y_s.reshape(db, 1, d), out_rows_p, out_win_p, out_gla_p, out_conv_p, out_mem_p,
            out_rows_s, out_win_s, gla_state_s, out_conv_s)
```

```python
import functools
import math

import numpy as np
import jax
import jax.numpy as jnp
from jax import lax
from jax.experimental import pallas as pl
from jax.experimental.pallas import tpu as pltpu

F32 = jnp.float32
BF16 = jnp.bfloat16

NSA_HEADS = 8
NSA_KV = 2
NSA_GROUP = 4
HEAD_DIM = 64
CMP_BLOCK = 32
CMP_STRIDE = 16
SEL_BLOCK = 64
SEL_TOP = 16
WINDOW = 512
Q_BLOCK = 128
GLA_HEADS = 4
GLA_DK = 64
GLA_DV = 128
GLA_RANK = 16
GLA_TAU = 16.0
GLA_CHUNK = 64
X_HEADS = 4
X_DIM = 128
N_BUCKETS = 32
MAX_DISTANCE = 128
EPS = 1e-6
LOG2E = math.log2(math.e)
NEG = -1e30
TINY = 1e-30
SEL_PENALTY = -1e9
MASKED_BELOW = -5e29
EXP_CLAMP = 80.0

IN_SIZES = (512, 768, 24, 256, 256, 512, 16, 512, 512, 3072)
PAD_SIZES = (512, 768, 128, 256, 256, 512, 128, 512, 512, 3072)
PAD_OFFS = tuple(int(v) for v in np.cumsum((0,) + PAD_SIZES))
D_IN_PAD = PAD_OFFS[-1]
GLA_IN_W = 256 + 256 + 512 + 128 + 512

VMEM_LIMIT = 56 * 1024 * 1024
FAR_TILE = 256
NEAR_TILE = 512
WIN_TILE = WINDOW + Q_BLOCK


def _cparams(n_axes):
    return pltpu.CompilerParams(dimension_semantics=("arbitrary",) * n_axes, vmem_limit_bytes=VMEM_LIMIT)


def _dot(a, b):
    return jnp.dot(a, b, preferred_element_type=F32)


def _dot_nt(a, b):
    return lax.dot_general(a, b, (((1,), (1,)), ((), ())), preferred_element_type=F32)


def _dot_tn(a, b):
    return lax.dot_general(a, b, (((0,), (0,)), ((), ())), preferred_element_type=F32)


def _split_dot(x, m):
    hi = x.astype(BF16)
    lo = (x - hi.astype(F32)).astype(BF16)
    return _dot(hi, m) + _dot(lo, m)


def _rms(x, g):
    return x * lax.rsqrt(jnp.mean(x * x, axis=-1, keepdims=True) + EPS) * g


def _group_rms(x, pmat, gsize, g):
    ss = _split_dot(x * x, pmat)
    return x * lax.rsqrt(ss * (1.0 / gsize) + EPS) * g


def _gelu(x):
    return 0.5 * x * (1.0 + jnp.tanh(math.sqrt(2.0 / math.pi) * (x + 0.044715 * (x * x * x))))


def _sigmoid(x):
    return 1.0 / (1.0 + jnp.exp(-x))


def _block_ones(n, gsize):
    i = np.arange(n) // gsize
    return jnp.asarray((i[:, None] == i[None, :]).astype(np.float32), dtype=BF16)


def _proj_kernel(x_ref, gmix_ref, w_ref, gq_ref, gk1_ref, gk2_ref, gxq_ref, p64_ref, p128_ref,
                 rows_ref, win_ref, qn_ref, gates_ref, gla_ref, xq_ref, mg_ref, *attn_refs):
    x = x_ref[0]
    h = _rms(x, gmix_ref[...]).astype(BF16)
    o = PAD_OFFS

    def seg(i):
        return _dot(h, w_ref[:, o[i]:o[i + 1]])

    p64 = p64_ref[...]
    p64s = p64_ref[0:128, 0:128]
    qn = _group_rms(seg(0), p64, HEAD_DIM, gq_ref[...]) * (HEAD_DIM ** -0.5)
    qn_ref[0] = qn.astype(BF16)
    if attn_refs:
        qt_ref = attn_refs[4]
        for u in range(x.shape[0] // Q_BLOCK):
            qt_ref[0, u] = (qn[Q_BLOCK * u:Q_BLOCK * (u + 1)] * LOG2E).T.astype(BF16)

    kv = seg(1)
    k_sel = _group_rms(kv[:, 256:384], p64s, HEAD_DIM, gk1_ref[...])
    k_win = _group_rms(kv[:, 512:640], p64s, HEAD_DIM, gk2_ref[...])
    rows_ref[0, :, 0:256] = kv[:, 0:256]
    rows_ref[0, :, 256:384] = k_sel
    rows_ref[0, :, 384:512] = kv[:, 384:512]
    win_ref[0, :, 0:128] = k_win
    win_ref[0, :, 128:256] = kv[:, 640:768]
    if attn_refs:
        ka_ref, kw_ref, vst_ref, vwt_ref, _ = attn_refs
        tm = x.shape[0]
        tpos = pl.program_id(1) * tm + lax.broadcasted_iota(jnp.int32, (tm, 128), 0)
        blk = lax.broadcasted_iota(jnp.int32, (tm, 128), 1)
        ka_ref[0, :, 0:128] = k_sel.astype(BF16)
        ka_ref[0, :, 128:256] = jnp.where(tpos // SEL_BLOCK == blk, 1.0, 0.0).astype(BF16)
        kw_ref[0] = k_win.astype(BF16)
        for u in range(tm // FAR_TILE):
            vst_ref[0, u] = kv[FAR_TILE * u:FAR_TILE * (u + 1), 384:512].T.astype(BF16)
        for u in range(tm // Q_BLOCK):
            vwt_ref[0, u] = kv[Q_BLOCK * u:Q_BLOCK * (u + 1), 640:768].T.astype(BF16)

    gates_ref[0] = _sigmoid(seg(2))
    gla_ref[0, :, 0:256] = seg(3) * (GLA_DK ** -0.5)
    gla_ref[0, :, 256:512] = seg(4)
    gla_ref[0, :, 512:1024] = seg(5)
    gla_ref[0, :, 1024:1152] = seg(6)
    gla_ref[0, :, 1152:1664] = seg(7)
    xq = _group_rms(seg(8), p128_ref[...], X_DIM, gxq_ref[...]) * (X_DIM ** -0.5)
    xq_ref[0] = xq.astype(BF16)
    mg_ref[0] = _sigmoid(seg(9)).astype(BF16)


def _proj_in(x, g_mix, w_pad, gq, gk1, gk2, gxq, p64, p128, attn_layout):
    b, t, d = x.shape
    tm = min(256, t)
    assert t % tm == 0
    widths = [512, 256, 512, 128, GLA_IN_W, 512, 3072]
    dtypes = [F32, F32, BF16, F32, F32, BF16, BF16]
    out_specs = [pl.BlockSpec((1, tm, w), lambda i, j: (i, j, 0)) for w in widths]
    out_shape = [jax.ShapeDtypeStruct((b, t, w), dt) for w, dt in zip(widths, dtypes)]
    if attn_layout:
        assert tm % Q_BLOCK == 0
        for w in (256, 128):
            out_specs.append(pl.BlockSpec((1, tm, w), lambda i, j: (i, j, 0)))
            out_shape.append(jax.ShapeDtypeStruct((b, t, w), BF16))
        assert tm % FAR_TILE == 0
        for rows, width in ((128, FAR_TILE), (128, Q_BLOCK), (512, Q_BLOCK)):
            out_specs.append(pl.BlockSpec((1, tm // width, rows, width), lambda i, j: (i, j, 0, 0)))
            out_shape.append(jax.ShapeDtypeStruct((b, t // width, rows, width), BF16))
    const = lambda shape: pl.BlockSpec(shape, lambda i, j: (0,) * len(shape))
    return pl.pallas_call(
        _proj_kernel,
        grid=(b, t // tm),
        in_specs=[pl.BlockSpec((1, tm, d), lambda i, j: (i, j, 0)),
                  const((1, d)), const((d, D_IN_PAD)), const((1, 512)), const((1, 128)), const((1, 128)),
                  const((1, 512)), const((512, 512)), const((512, 512))],
        out_specs=out_specs,
        out_shape=out_shape,
        compiler_params=_cparams(2),
    )(x, g_mix, w_pad, gq, gk1, gk2, gxq, p64, p128)


def _memkv_kernel(m_ref, g_ref, w_ref, gk_ref, p128_ref, o_ref):
    h = _rms(m_ref[0], g_ref[...]).astype(BF16)
    kv = _dot(h, w_ref[...])
    o_ref[0, :, 0:512] = _group_rms(kv[:, 0:512], p128_ref[...], X_DIM, gk_ref[...])
    o_ref[0, :, 512:1024] = kv[:, 512:1024]


def _memory_kv(mem, g_mem, w_mem, gxk, p128):
    b, m, d = mem.shape
    const = lambda shape: pl.BlockSpec(shape, lambda i: (0,) * len(shape))
    return pl.pallas_call(
        _memkv_kernel,
        grid=(b,),
        in_specs=[pl.BlockSpec((1, m, d), lambda i: (i, 0, 0)), const((1, d)), const((d, 1024)),
                  const((1, 512)), const((512, 512))],
        out_specs=pl.BlockSpec((1, m, 1024), lambda i: (i, 0, 0)),
        out_shape=jax.ShapeDtypeStruct((b, m, 1024), F32),
        compiler_params=_cparams(1),
    )(mem, g_mem, w_mem, gxk, p128)


def _compress_core(load_j, n_chunks, pe_ref, w1_ref, w2_ref, gk0_ref, p64_ref):
    xs = [load_j(j) for j in range(CMP_STRIDE)]
    halves = [_dot(jnp.concatenate([(x + pe_ref[r, j]).astype(BF16) for j, x in enumerate(xs)], axis=1), w1_ref[r])
              for r in range(2)]
    hid = halves[0] + pltpu.roll(halves[1], n_chunks - 1, 0)
    out = _dot(_gelu(hid).astype(BF16), w2_ref[...])
    row = lax.broadcasted_iota(jnp.int32, (n_chunks, 128), 0)
    live = row < n_chunks - 1
    kc = _group_rms(out[:, 0:128], p64_ref[0:128, 0:128], HEAD_DIM, gk0_ref[...])
    return jnp.where(live, kc, 0.0), jnp.where(live, out[:, 128:256], 0.0)


def _compress_kernel(rk_ref, rv_ref, pe_ref, w1_ref, w2_ref, gk0_ref, p64_ref, kc_ref, vc_ref, *, n_chunks):
    load_j = lambda j: jnp.concatenate([rk_ref[0, pl.ds(j, n_chunks, stride=CMP_STRIDE), :],
                                        rv_ref[0, pl.ds(j, n_chunks, stride=CMP_STRIDE), :]], axis=1)
    kc, vc = _compress_core(load_j, n_chunks, pe_ref, w1_ref, w2_ref, gk0_ref, p64_ref)
    kc_ref[0] = kc.astype(BF16)
    vc_ref[0] = vc.T.astype(BF16)


def _compress(rows, pe, w1, w2, gk0, p64):
    b, t, _ = rows.shape
    n_chunks = t // CMP_STRIDE
    const = lambda shape: pl.BlockSpec(shape, lambda i: (0,) * len(shape))
    return pl.pallas_call(
        functools.partial(_compress_kernel, n_chunks=n_chunks),
        grid=(b,),
        in_specs=[pl.BlockSpec((1, t, 128), lambda i: (i, 0, 0)), pl.BlockSpec((1, t, 128), lambda i: (i, 0, 1)),
                  const((2, CMP_STRIDE, 1, 256)), const((2, CMP_STRIDE * 256, 256)), const((256, 256)),
                  const((1, 128)), const((512, 512))],
        out_specs=[pl.BlockSpec((1, n_chunks, 128), lambda i: (i, 0, 0)),
                   pl.BlockSpec((1, 128, n_chunks), lambda i: (i, 0, 0))],
        out_shape=[jax.ShapeDtypeStruct((b, n_chunks, 128), BF16), jax.ShapeDtypeStruct((b, 128, n_chunks), BF16)],
        compiler_params=_cparams(1),
    )(rows, rows, pe, w1, w2, gk0, p64)


def _masked_softmax(s, axis, exp_fn=jnp.exp):
    valid = s > MASKED_BELOW
    m = jnp.max(s, axis=axis, keepdims=True)
    p = jnp.where(valid, exp_fn(s - m), 0.0)
    return p / jnp.maximum(jnp.sum(p, axis=axis, keepdims=True), TINY)


def _select_blocks(score, top, axis):
    pos = lax.broadcasted_iota(jnp.int32, score.shape, axis).astype(F32)
    sel = jnp.zeros(score.shape, jnp.bool_)
    for _ in range(top):
        mx = jnp.max(score, axis=axis, keepdims=True)
        idx = jnp.min(jnp.where(score == mx, pos, 1e9), axis=axis, keepdims=True)
        hit = pos == idx
        sel = jnp.logical_or(sel, hit)
        score = jnp.where(hit, -3e38, score)
    return sel


def _select_with_forced(imp, forced, top, axis):
    n_forced = 3
    assert top > n_forced
    return jnp.logical_or(forced, _select_blocks(jnp.where(forced, -3e38, imp), top - n_forced, axis))


def _values_times_probs(vt_tiles, p):
    out = None
    start = 0
    for vt in vt_tiles:
        part = _dot(vt, p[start:start + vt.shape[1]])
        start += vt.shape[1]
        out = part if out is None else out + part
    return out


def _softmax_update_t(s_ref, bias, vt_tiles, m_ref, l_ref, acc_ref):
    a_parts, p_parts = [], []
    for g in range(s_ref.shape[1] // Q_BLOCK):
        cs = slice(Q_BLOCK * g, Q_BLOCK * (g + 1))
        sg = s_ref[:, cs]
        if bias is not None:
            sg = sg + bias(cs)
        m_old = m_ref[:, cs]
        m_new = jnp.maximum(m_old, jnp.max(sg, axis=0, keepdims=True))
        alpha = jnp.exp2(m_old - m_new)
        p = jnp.exp2(sg - m_new)
        m_ref[:, cs] = m_new
        l_ref[:, cs] = alpha * l_ref[:, cs] + jnp.sum(p, axis=0, keepdims=True)
        a_parts.append(alpha)
        p_parts.append(p.astype(BF16))
    acc_ref[...] = (jnp.concatenate(a_parts, axis=1) * acc_ref[...]
                    + _values_times_probs(vt_tiles, jnp.concatenate(p_parts, axis=1)))


def _attn_kernel(qt_ref, kc_ref, vct_ref, ka_ref, vst_ref, kw_ref, vwt_ref, gt_ref, tsel_ref, twin_ref, acmp_ref,
                 ovt_ref, o_ref, sa_scr, sb_scr, m_scr, l_scr, acc_scr, *, top):
    k = pl.program_id(1)
    n = pl.program_id(2)
    cols = NSA_GROUP * Q_BLOCK
    vrow = pl.multiple_of(k * HEAD_DIM, HEAD_DIM)
    zero = jnp.zeros((HEAD_DIM, Q_BLOCK), BF16)
    parts = []
    for g in range(NSA_GROUP):
        piece = qt_ref[0, 0, HEAD_DIM * g:HEAD_DIM * (g + 1), :]
        parts.append(jnp.where(k == 0, jnp.concatenate([piece, zero], axis=0),
                               jnp.concatenate([zero, piece], axis=0)))
    qt = jnp.concatenate(parts, axis=1)

    nc = kc_ref.shape[1]
    c32 = lax.broadcasted_iota(jnp.int32, (nc, 32), 0)
    r32 = lax.broadcasted_iota(jnp.int32, (nc, 32), 1)
    onehot = jnp.where((c32 - 8 * n + 9) == (r32 & 15), 1.0, 0.0).astype(BF16)
    band = _dot(onehot, acmp_ref[0])
    cp = lax.broadcasted_iota(jnp.int32, (nc, cols), 0) - 8 * n + 9
    s_c = _dot(kc_ref[0], qt) + jnp.where(cp < 0, 0.0, jnp.where(cp > 15, NEG, band))
    p_c = _masked_softmax(s_c, 0, jnp.exp2).astype(BF16)
    o_c = _dot(vct_ref[0, pl.ds(vrow, HEAD_DIM), :], p_c)
    imp4 = _dot(ovt_ref[...], p_c)
    imp = imp4[:, 0:128] + imp4[:, 128:256] + imp4[:, 256:384] + imp4[:, 384:512]

    jj = lax.broadcasted_iota(jnp.int32, (128, Q_BLOCK), 0)
    tpos = n * Q_BLOCK + lax.broadcasted_iota(jnp.int32, (128, Q_BLOCK), 1)
    tblk = tpos // SEL_BLOCK
    forced = (jj == 0) | (jj == tblk) | (jj == tblk - 1)
    sel_t = _select_with_forced(jnp.where(jj * SEL_BLOCK <= tpos, imp, NEG), forced, top, 0)
    pen_t = jnp.where(sel_t, 0.0, SEL_PENALTY).astype(BF16)
    rhs = jnp.concatenate([qt, jnp.concatenate([pen_t] * NSA_GROUP, axis=1)], axis=0)

    n_far = jnp.maximum(n - 2, 0) // 2
    r_near = n - 2 * n_far

    def value_tiles(ref, first, count):
        return [ref[0, first + d, pl.ds(vrow, HEAD_DIM), :] for d in range(count)]

    def issue_scores(tile, s_ref):
        s_ref[...] = _dot(ka_ref[0, pl.ds(pl.multiple_of(tile * FAR_TILE, FAR_TILE), FAR_TILE), :], rhs)

    def reduce_tile(tile, s_ref, half_idx):
        bias = None
        if half_idx is not None:
            bias = lambda cs: tsel_ref[0, r_near, FAR_TILE * half_idx:FAR_TILE * (half_idx + 1), cs]
        _softmax_update_t(s_ref, bias, value_tiles(vst_ref, tile, 1), m_scr, l_scr, acc_scr)

    m_scr[...] = jnp.full((1, cols), NEG, F32)
    l_scr[...] = jnp.zeros((1, cols), F32)
    acc_scr[...] = jnp.zeros((HEAD_DIM, cols), F32)
    odd = n_far % 2

    @pl.when(odd == 1)
    def _():
        issue_scores(0, sa_scr)
        reduce_tile(0, sa_scr, None)

    issue_scores(odd, sa_scr)

    @pl.loop(0, n_far // 2)
    def _(j):
        t0 = odd + 2 * j
        issue_scores(t0 + 1, sb_scr)
        reduce_tile(t0, sa_scr, None)
        issue_scores(t0 + 2, sa_scr)
        reduce_tile(t0 + 1, sb_scr, None)

    issue_scores(n_far + 1, sb_scr)
    reduce_tile(n_far, sa_scr, 0)
    reduce_tile(n_far + 1, sb_scr, 1)
    o_s = acc_scr[...] / l_scr[...]

    wt = jnp.maximum(n - WINDOW // Q_BLOCK, 0)
    s_w = _dot(kw_ref[0, pl.ds(pl.multiple_of(wt * Q_BLOCK, Q_BLOCK), WIN_TILE), :], qt) + twin_ref[0, 0]
    p_w = jnp.exp2(s_w - jnp.max(s_w, axis=0, keepdims=True))
    o_w = (_values_times_probs(value_tiles(vwt_ref, wt, WIN_TILE // Q_BLOCK), p_w.astype(BF16))
           / jnp.sum(p_w, axis=0, keepdims=True))

    gtt = gt_ref[0].T

    def gate_row(branch):
        rows = [jnp.where(k == 0, gtt[3 * g + branch:3 * g + branch + 1],
                          gtt[12 + 3 * g + branch:12 + 3 * g + branch + 1]) for g in range(NSA_GROUP)]
        return jnp.concatenate(rows, axis=1)

    o_t = gate_row(0) * o_c + gate_row(1) * o_s + gate_row(2) * o_w
    left = jnp.concatenate([o_t[:, 0:128], o_t[:, 128:256]], axis=0).T
    right = jnp.concatenate([o_t[:, 256:384], o_t[:, 384:512]], axis=0).T
    o_ref[0] = jnp.concatenate([left, right], axis=1).astype(BF16)


def _nsa_prompt(qt, kc, vct, ka, kw, vst, vwt, gates, tsel, twin, acmp, ovt):
    b, t, _ = ka.shape
    nqb = t // Q_BLOCK
    nc = kc.shape[1]
    top = min(SEL_TOP, -(-t // SEL_BLOCK))
    assert t >= WIN_TILE and t % FAR_TILE == 0 and t // SEL_BLOCK <= 128
    n_win = WINDOW // Q_BLOCK
    per_b = lambda shape: pl.BlockSpec((1,) + shape, lambda i, k, n: (i,) + (0,) * len(shape))
    return pl.pallas_call(
        functools.partial(_attn_kernel, top=top),
        grid=(b, NSA_KV, nqb),
        in_specs=[pl.BlockSpec((1, 1, NSA_GROUP * HEAD_DIM, Q_BLOCK), lambda i, k, n: (i, n, k, 0)),
                  per_b((nc, 128)), per_b((128, nc)),
                  per_b((t, 256)), per_b((t // FAR_TILE, 128, FAR_TILE)), per_b((t, 128)), per_b((nqb, 128, Q_BLOCK)),
                  pl.BlockSpec((1, Q_BLOCK, 128), lambda i, k, n: (i, n, 0)),
                  pl.BlockSpec((1, 4, NEAR_TILE, 512), lambda i, k, n: (k, 0, 0, 0)),
                  pl.BlockSpec((1, 1, WIN_TILE, 512), lambda i, k, n: (k, jnp.minimum(n, n_win), 0, 0)),
                  pl.BlockSpec((1, 32, 512), lambda i, k, n: (k, 0, 0)),
                  pl.BlockSpec((128, nc), lambda i, k, n: (0, 0))],
        out_specs=pl.BlockSpec((1, Q_BLOCK, 256), lambda i, k, n: (i, n, k)),
        out_shape=jax.ShapeDtypeStruct((b, t, 512), BF16),
        scratch_shapes=[pltpu.VMEM((FAR_TILE, NSA_GROUP * Q_BLOCK), F32), pltpu.VMEM((FAR_TILE, NSA_GROUP * Q_BLOCK), F32),
                        pltpu.VMEM((1, NSA_GROUP * Q_BLOCK), F32), pltpu.VMEM((1, NSA_GROUP * Q_BLOCK), F32),
                        pltpu.VMEM((HEAD_DIM, NSA_GROUP * Q_BLOCK), F32)],
        compiler_params=_cparams(3),
    )(qt, kc, vct, ka, vst, kw, vwt, gates, tsel, twin, acmp, ovt)


def _log_sigmoid(z):
    return jnp.minimum(z, 0.0) - jnp.log1p(jnp.exp(-jnp.abs(z)))


def _gla_kernel(x_ref, s0_ref, wg_ref, bg_ref, ggo_ref, tri_ref, o_ref, st_ref, s_scr, *, n_chunks):
    @pl.when(pl.program_id(1) == 0)
    def _():
        s_scr[...] = s0_ref[0]

    c_len = GLA_CHUNK
    tri = tri_ref[...]
    ti = lax.broadcasted_iota(jnp.int32, (c_len, c_len), 0)
    si = lax.broadcasted_iota(jnp.int32, (c_len, c_len), 1)
    causal = si <= ti
    for c in range(n_chunks):
        rs = slice(c_len * c, c_len * (c + 1))
        q = x_ref[0, rs, 0:256]
        kk = x_ref[0, rs, 256:512]
        v = x_ref[0, rs, 512:1024]
        lr = x_ref[0, rs, 1024:1152]
        r = x_ref[0, rs, 1152:1664]
        la = _log_sigmoid(_dot(lr.astype(BF16), wg_ref[...]) + bg_ref[...]) * (1.0 / GLA_TAU)
        a1 = la.astype(BF16)
        r1 = la - a1.astype(F32)
        a2 = r1.astype(BF16)
        a3 = (r1 - a2.astype(F32)).astype(BF16)
        cb = _dot(tri, a1) + _dot(tri, a2) + _dot(tri, a3)
        last = cb[c_len - 1:c_len, :]
        mid = cb[c_len // 2:c_len // 2 + 1, :]
        qe = (q * jnp.exp(cb)).astype(BF16)
        qa = (q * jnp.exp(jnp.minimum(cb - mid, EXP_CLAMP))).astype(BF16)
        kb = (kk * jnp.exp(jnp.minimum(mid - cb, EXP_CLAMP))).astype(BF16)
        ke = (kk * jnp.exp(last - cb)).astype(BF16)
        dec = jnp.exp(last)
        for h in range(GLA_HEADS):
            ks = slice(GLA_DK * h, GLA_DK * (h + 1))
            vs = slice(GLA_DV * h, GLA_DV * (h + 1))
            att = jnp.where(causal, _dot_nt(qa[:, ks], kb[:, ks]), 0.0)
            vh = v[:, vs].astype(BF16)
            st = s_scr[h]
            o = _dot(att.astype(BF16), vh) + _dot_nt(qe[:, ks], st.astype(BF16))
            s_scr[h] = st * dec[:, ks] + _dot_tn(vh, ke[:, ks])
            on = _rms(o, ggo_ref[...])
            rh = r[:, vs]
            o_ref[0, rs, vs] = (on * (rh * _sigmoid(rh))).astype(BF16)
    st_ref[0] = s_scr[...]


def _gla_prompt(gla_in, s0t, wg, bg, ggo, tri):
    b, t, w = gla_in.shape
    ct = min(256, t)
    assert t % ct == 0 and ct % GLA_CHUNK == 0
    const = lambda shape: pl.BlockSpec(shape, lambda i, j: (0,) * len(shape))
    return pl.pallas_call(
        functools.partial(_gla_kernel, n_chunks=ct // GLA_CHUNK),
        grid=(b, t // ct),
        in_specs=[pl.BlockSpec((1, ct, w), lambda i, j: (i, j, 0)),
                  pl.BlockSpec((1, GLA_HEADS, GLA_DV, GLA_DK), lambda i, j: (i, 0, 0, 0)),
                  const((128, 256)), const((1, 256)), const((1, 128)), const((GLA_CHUNK, GLA_CHUNK))],
        out_specs=[pl.BlockSpec((1, ct, 512), lambda i, j: (i, j, 0)),
                   pl.BlockSpec((1, GLA_HEADS, GLA_DV, GLA_DK), lambda i, j: (i, 0, 0, 0))],
        out_shape=[jax.ShapeDtypeStruct((b, t, 512), BF16),
                   jax.ShapeDtypeStruct((b, GLA_HEADS, GLA_DV, GLA_DK), F32)],
        scratch_shapes=[pltpu.VMEM((GLA_HEADS, GLA_DV, GLA_DK), F32)],
        compiler_params=_cparams(2),
    )(gla_in, s0t, wg, bg, ggo, tri)


def _gla_step_kernel(q_ref, k_ref, lr_ref, v_ref, r_ref, s_ref, wgt_ref, bgt_ref, ggo_ref, o_ref, sn_ref):
    lr = lr_ref[0]
    for h in range(GLA_HEADS):
        z = jnp.sum(wgt_ref[h] * lr, axis=-1, keepdims=True) + bgt_ref[h]
        a = jnp.exp(_log_sigmoid(z) * (1.0 / GLA_TAU))
        s0 = s_ref[0, h]
        kh = k_ref[0, h]
        qh = q_ref[0, h]
        vh = v_ref[0, h]
        sn_ref[0, h] = a * s0 + kh * vh
        o = jnp.sum((qh * a) * s0, axis=0, keepdims=True) + jnp.sum(qh * kh, axis=0, keepdims=True) * vh
        on = _rms(o, ggo_ref[...])
        rh = r_ref[0, h]
        o_ref[0, h] = on * (rh * _sigmoid(rh))


def _gla_step(q_col, k_col, lr, v_row, r_row, s0, wgt, bgt, ggo):
    b = q_col.shape[0]
    const = lambda shape: pl.BlockSpec(shape, lambda i: (0,) * len(shape))
    per_b = lambda shape: pl.BlockSpec((1,) + shape, lambda i: (i,) + (0,) * len(shape))
    return pl.pallas_call(
        _gla_step_kernel,
        grid=(b,),
        in_specs=[per_b((GLA_HEADS, GLA_DK, 1)), per_b((GLA_HEADS, GLA_DK, 1)), per_b((1, 128)),
                  per_b((GLA_HEADS, 1, GLA_DV)), per_b((GLA_HEADS, 1, GLA_DV)), per_b((GLA_HEADS, GLA_DK, GLA_DV)),
                  const((GLA_HEADS, GLA_DK, 128)), const((GLA_HEADS, GLA_DK, 1)), const((1, 128))],
        out_specs=[per_b((GLA_HEADS, 1, GLA_DV)), per_b((GLA_HEADS, GLA_DK, GLA_DV))],
        out_shape=[jax.ShapeDtypeStruct((b, GLA_HEADS, 1, GLA_DV), F32),
                   jax.ShapeDtypeStruct((b, GLA_HEADS, GLA_DK, GLA_DV), F32)],
        compiler_params=_cparams(1),
    )(q_col, k_col, lr, v_row, r_row, s0, wgt, bgt, ggo)


def _xatt_kernel(xq_ref, mem_ref, o_ref):
    for h in range(X_HEADS):
        ls = slice(X_DIM * h, X_DIM * (h + 1))
        kh = mem_ref[0, :, ls].astype(BF16)
        vh = mem_ref[0, :, 512 + X_DIM * h:512 + X_DIM * (h + 1)].astype(BF16)
        s = _dot_nt(xq_ref[0, :, ls], kh)
        p = jnp.exp(s - jnp.max(s, axis=-1, keepdims=True))
        p = p / jnp.sum(p, axis=-1, keepdims=True)
        o_ref[0, :, ls] = _dot(p.astype(BF16), vh).astype(BF16)


def _xatt(xq, memkv):
    b, t, _ = xq.shape
    m = memkv.shape[1]
    tq = min(512, t)
    assert t % tq == 0
    return pl.pallas_call(
        _xatt_kernel,
        grid=(b, t // tq),
        in_specs=[pl.BlockSpec((1, tq, 512), lambda i, j: (i, j, 0)),
                  pl.BlockSpec((1, m, 1024), lambda i, j: (i, 0, 0))],
        out_specs=pl.BlockSpec((1, tq, 512), lambda i, j: (i, j, 0)),
        out_shape=jax.ShapeDtypeStruct((b, t, 512), BF16),
        compiler_params=_cparams(2),
    )(xq, memkv)


def _merge_kernel(on_ref, og_ref, ox_ref, mg_ref, x_ref, wn_ref, wg_ref, wx_ref, wo_ref, x1_ref):
    d = x_ref.shape[-1]
    merged = (mg_ref[:, 0:d].astype(F32) * _dot(on_ref[...], wn_ref[...])
              + mg_ref[:, d:2 * d].astype(F32) * _dot(og_ref[...], wg_ref[...])
              + mg_ref[:, 2 * d:3 * d].astype(F32) * _dot(ox_ref[...], wx_ref[...]))
    x1_ref[...] = x_ref[...] + _dot(merged.astype(BF16), wo_ref[...])


def _merge(o_nsa, o_gla, o_x, mg, x, wn, wg, wx, wo):
    m, d = x.shape
    tm = min(512, m)
    assert m % tm == 0
    row = lambda w: pl.BlockSpec((tm, w), lambda i: (i, 0))
    const = lambda shape: pl.BlockSpec(shape, lambda i: (0,) * len(shape))
    return pl.pallas_call(
        _merge_kernel,
        grid=(m // tm,),
        in_specs=[row(512), row(512), row(512), row(3 * d), row(d),
                  const((512, d)), const((512, d)), const((512, d)), const((d, d))],
        out_specs=row(d),
        out_shape=jax.ShapeDtypeStruct((m, d), F32),
        compiler_params=_cparams(1),
    )(o_nsa, o_gla, o_x, mg, x, wn, wg, wx, wo)


def _ffn_seq_kernel(x_ref, past_ref, g_ref, wup_ref, cw_ref, cb_ref, wdn_ref, y_ref, tail_ref, carry_ref):
    f = cw_ref.shape[-1]
    tm = x_ref.shape[1]

    @pl.when(pl.program_id(1) == 0)
    def _():
        carry_ref[...] = jnp.zeros(carry_ref.shape, F32)
        carry_ref[6:8, :] = past_ref[0]

    x1 = x_ref[0]
    ug = _dot(_rms(x1, g_ref[...]).astype(BF16), wup_ref[...])
    u = ug[:, 0:f]
    g = ug[:, f:2 * f]
    row = lax.broadcasted_iota(jnp.int32, (tm, f), 0)
    p1 = carry_ref[7:8, :]
    p2 = carry_ref[6:7, :]
    gm1 = jnp.where(row == 0, p1, pltpu.roll(g, 1, 0))
    gm2 = jnp.where(row == 0, p2, jnp.where(row == 1, p1, pltpu.roll(g, 2, 0)))
    gc = cb_ref[...] + cw_ref[0:1, :] * gm2 + cw_ref[1:2, :] * gm1 + cw_ref[2:3, :] * g
    y_ref[0] = x1 + _dot((_gelu(gc) * u).astype(BF16), wdn_ref[...])
    carry_ref[...] = g[tm - 8:tm, :]
    tail_ref[0] = g[tm - 8:tm, :]


def _ffn_seq(x1, conv_past, g_ffn, w_up, conv_w, conv_b, w_down):
    b, t, d = x1.shape
    f = conv_w.shape[-1]
    tm = min(256, t)
    assert t % tm == 0 and tm >= 8
    const = lambda shape: pl.BlockSpec(shape, lambda i, j: (0,) * len(shape))
    return pl.pallas_call(
        _ffn_seq_kernel,
        grid=(b, t // tm),
        in_specs=[pl.BlockSpec((1, tm, d), lambda i, j: (i, j, 0)),
                  pl.BlockSpec((1, 2, f), lambda i, j: (i, 0, 0)),
                  const((1, d)), const((d, 2 * f)), const((3, f)), const((1, f)), const((f, d))],
        out_specs=[pl.BlockSpec((1, tm, d), lambda i, j: (i, j, 0)),
                   pl.BlockSpec((1, 8, f), lambda i, j: (i, 0, 0))],
        out_shape=[jax.ShapeDtypeStruct((b, t, d), F32), jax.ShapeDtypeStruct((b, 8, f), F32)],
        scratch_shapes=[pltpu.VMEM((8, f), F32)],
        compiler_params=_cparams(2),
    )(x1, conv_past, g_ffn, w_up, conv_w, conv_b, w_down)


def _ffn_step_kernel(x_ref, p0_ref, p1_ref, g_ref, wup_ref, cw_ref, cb_ref, wdn_ref, y_ref, gnew_ref):
    f = cw_ref.shape[-1]
    x1 = x_ref[...]
    ug = _dot(_rms(x1, g_ref[...]).astype(BF16), wup_ref[...])
    u = ug[:, 0:f]
    g = ug[:, f:2 * f]
    gc = cb_ref[...] + cw_ref[0:1, :] * p0_ref[...] + cw_ref[1:2, :] * p1_ref[...] + cw_ref[2:3, :] * g
    y_ref[...] = x1 + _dot((_gelu(gc) * u).astype(BF16), wdn_ref[...])
    gnew_ref[...] = g


def _ffn_step(x1, p0, p1, g_ffn, w_up, conv_w, conv_b, w_down):
    m, d = x1.shape
    f = conv_w.shape[-1]
    full = lambda shape: pl.BlockSpec(shape, lambda i: (0,) * len(shape))
    return pl.pallas_call(
        _ffn_step_kernel,
        grid=(1,),
        in_specs=[full((m, d)), full((m, f)), full((m, f)), full((1, d)), full((d, 2 * f)), full((3, f)),
                  full((1, f)), full((f, d))],
        out_specs=[full((m, d)), full((m, f))],
        out_shape=[jax.ShapeDtypeStruct((m, d), F32), jax.ShapeDtypeStruct((m, f), F32)],
        compiler_params=_cparams(1),
    )(x1, p0, p1, g_ffn, w_up, conv_w, conv_b, w_down)


def _decode_kernel(pt_ref, *refs, top, n_sel, n_pages):
    del pt_ref
    page_refs = refs[:n_pages]
    (q8_ref, new_ref, neww_ref, cwin_ref, gt_ref, bc_ref, bs_ref, bw_ref, b0_ref, ov_ref, et_ref, pe_ref, w1_ref,
     w2_ref, gk0_ref, p64_ref, o_ref, wout_ref) = refs[n_pages:]
    n_chunks = n_pages * page_refs[0].shape[1]
    length = n_chunks * CMP_STRIDE

    def plane(pair, j):
        lo = 512 * j + 256 * pair
        return jnp.concatenate([pg[0, :, lo:lo + 256] for pg in page_refs], axis=0)

    q8 = q8_ref[0]
    q8f = q8.astype(F32)
    rowk = lax.broadcasted_iota(jnp.int32, (8, 128), 0) // NSA_GROUP
    lane_half = lax.broadcasted_iota(jnp.int32, (8, 128), 1) // HEAD_DIM

    def half_mask(x):
        return jnp.where(rowk == lane_half, x, 0.0)

    kc, vc = _compress_core(functools.partial(plane, 0), n_chunks, pe_ref, w1_ref, w2_ref, gk0_ref, p64_ref)
    kc = kc.astype(BF16)
    vc = vc.astype(BF16)

    p_c = _masked_softmax(_dot_nt(q8, kc) + bc_ref[...], 1).astype(BF16)
    o_c = half_mask(_dot(p_c, vc))
    imp8 = _dot(p_c, ov_ref[...])
    nsp = imp8.shape[1]
    imp = jnp.concatenate([jnp.sum(imp8[0:4], axis=0, keepdims=True),
                           jnp.sum(imp8[4:8], axis=0, keepdims=True)], axis=0)
    jj = lax.broadcasted_iota(jnp.int32, (2, nsp), 1)
    tblk = length // SEL_BLOCK
    forced = (jj == 0) | (jj == tblk) | (jj == tblk - 1)
    sel = _select_with_forced(jnp.where(jj < n_sel, imp, -2e38), forced, top, 1)
    pen = jnp.where(sel, 0.0, SEL_PENALTY)
    pen8 = jnp.concatenate([jnp.broadcast_to(pen[0:1], (4, nsp)), jnp.broadcast_to(pen[1:2], (4, nsp))], axis=0)

    new = new_ref[0]
    neww = neww_ref[0]
    b0 = b0_ref[...]

    def attend(s_parts, s_new, v_parts, v_new):
        m = s_new
        for s in s_parts:
            m = jnp.maximum(m, jnp.max(s, axis=-1, keepdims=True))
        pn = jnp.exp(s_new - m)
        l = pn
        o = pn.astype(BF16).astype(F32) * v_new.astype(BF16).astype(F32)
        for s, v in zip(s_parts, v_parts):
            pp = jnp.exp(s - m)
            l = l + jnp.sum(pp, axis=-1, keepdims=True)
            o = o + _dot(pp.astype(BF16), v)
        return half_mask(o / l)

    def new_score(k_new):
        return jnp.sum(q8f * k_new.astype(BF16).astype(F32), axis=-1, keepdims=True) + b0

    pen_c = _dot_nt(pen8[:, 0:128].astype(BF16), et_ref[...])
    kv_parts = [plane(1, j).astype(BF16) for j in range(CMP_STRIDE)]
    s_parts = [_dot_nt(q8, kv[:, 0:128]) + pen_c + bs_ref[j] for j, kv in enumerate(kv_parts)]
    v_parts = [kv[:, 128:256] for kv in kv_parts]
    lane = lax.broadcasted_iota(jnp.int32, (8, nsp), 1)
    pen_new = jnp.sum(jnp.where(lane == tblk, pen8, 0.0), axis=-1, keepdims=True)
    o_s = attend(s_parts, new_score(new[:, 256:384]) + pen_new, v_parts, new[:, 384:512])

    cw = cwin_ref[0]
    s_w = _dot_nt(q8, cw[:, 0:128].astype(BF16)) + bw_ref[...]
    o_w = attend([s_w], new_score(neww[:, 0:128]), [cw[:, 128:256].astype(BF16)], neww[:, 128:256])

    gt = gt_ref[0]
    o_ref[0] = gt[:, 0:1] * o_c + gt[:, 1:2] * o_s + gt[:, 2:3] * o_w

    wl = cw.shape[0]
    wrow = lax.broadcasted_iota(jnp.int32, cw.shape, 0)
    wout_ref[0] = jnp.where(wrow == wl - 1, neww, pltpu.roll(cw, wl - 1, 0))


def _nsa_decode(page_table, cache_chunks, q8, new_rows, new_win, cache_win, gates8, bc, bs, bw, b0, ov, et,
                pe, w1, w2, gk0, p64):
    db, n_pages = page_table.shape
    cpp = cache_chunks.shape[1]
    n_chunks = n_pages * cpp
    length = n_chunks * CMP_STRIDE
    n_sel = -(-(length + 1) // SEL_BLOCK)
    top = min(SEL_TOP, n_sel)
    wl = cache_win.shape[1]
    nsp = ov.shape[1]
    const = lambda shape: pl.BlockSpec(shape, lambda i, pt: (0,) * len(shape))
    per_b = lambda shape: pl.BlockSpec((1,) + shape, lambda i, pt: (i,) + (0,) * len(shape))
    page_spec = lambda u: pl.BlockSpec((1, cpp, CMP_STRIDE * 512), lambda i, pt: (pt[i, u], 0, 0))
    grid_spec = pltpu.PrefetchScalarGridSpec(
        num_scalar_prefetch=1,
        grid=(db,),
        in_specs=[page_spec(u) for u in range(n_pages)] + [
                  per_b((8, 128)), per_b((1, 512)), per_b((1, 256)), per_b((wl, 256)), per_b((8, 128)),
                  const((8, n_chunks)), const((CMP_STRIDE, 8, n_chunks)), const((8, wl)), const((8, 1)),
                  const((n_chunks, nsp)), const((n_chunks, 128)),
                  const((2, CMP_STRIDE, 1, 256)), const((2, CMP_STRIDE * 256, 256)), const((256, 256)),
                  const((1, 128)), const((512, 512))],
        out_specs=[per_b((8, 128)), per_b((wl, 256))],
    )
    return pl.pallas_call(
        functools.partial(_decode_kernel, top=top, n_sel=n_sel, n_pages=n_pages),
        grid_spec=grid_spec,
        out_shape=[jax.ShapeDtypeStruct((db, 8, 128), F32), jax.ShapeDtypeStruct((db, wl, 256), F32)],
        compiler_params=_cparams(1),
    )(page_table, *([cache_chunks] * n_pages), q8, new_rows, new_win, cache_win, gates8, bc, bs, bw, b0, ov, et,
      pe, w1, w2, gk0, p64)


def _bucket_table():
    n = np.arange(MAX_DISTANCE + 1)
    max_exact = N_BUCKETS // 2
    nf = np.maximum(n, 1).astype(np.float32)
    large = max_exact + (np.log(nf / np.float32(max_exact)) / np.float32(math.log(MAX_DISTANCE / max_exact))
                         * np.float32(N_BUCKETS - max_exact)).astype(np.int32)
    return np.where(n < max_exact, n, np.minimum(large, N_BUCKETS - 1)).astype(np.int32)


def _bias_lookup(tb, rel, valid):
    idx = np.clip(rel, 0, MAX_DISTANCE)
    vals = jnp.moveaxis(tb[idx], -1, 0)
    return jnp.where(jnp.asarray(valid)[None], vals, NEG)


def _overlap(n_cmp_pad, n_cmp, n_sel_pad, n_sel):
    cs = (np.arange(n_cmp_pad) * CMP_STRIDE)[:, None]
    ss = (np.arange(n_sel_pad) * SEL_BLOCK)[None, :]
    ov = (cs < ss + SEL_BLOCK) & (cs + CMP_BLOCK > ss)
    ov &= (np.arange(n_cmp_pad) < n_cmp)[:, None] & (np.arange(n_sel_pad) < n_sel)[None, :]
    return jnp.asarray(ov.astype(np.float32), dtype=BF16)


def _stack_rows(x):
    return x.reshape(NSA_KV, NSA_GROUP * Q_BLOCK, x.shape[-1])


def _toeplitz(tbr, shift, width, max_valid):
    n = width + Q_BLOCK - 1
    u = np.arange(n)
    xs = shift - np.where(u < width, u, u - n)
    fvec = _bias_lookup(tbr, xs, (xs >= 0) & (xs <= max_valid))
    h = fvec.shape[0]
    return jnp.tile(fvec, (1, Q_BLOCK))[:, :Q_BLOCK * (n - 1)].reshape(h, Q_BLOCK, n - 1)[:, :, :width]


def _bias_descending(tb, top):
    far = jnp.broadcast_to(tb[MAX_DISTANCE][:, None], (tb.shape[1], top - MAX_DISTANCE + 1))
    return jnp.concatenate([far, tb[MAX_DISTANCE - 1:0:-1].T], axis=1)


def _prompt_tables(tb):
    tbr = (tb - tb[MAX_DISTANCE][None, :]) * LOG2E
    i = np.arange(Q_BLOCK)[:, None]

    def table(rel, valid):
        return _stack_rows(_bias_lookup(tbr, rel, valid))

    n_r = 4
    m_sel = _stack_rows(_toeplitz(tbr, Q_BLOCK * (n_r - 1), NEAR_TILE + Q_BLOCK * (n_r - 1), 1 << 30))
    tsel = jnp.stack([m_sel[:, :, Q_BLOCK * (n_r - 1 - r):Q_BLOCK * (n_r - 1 - r) + NEAR_TILE] for r in range(n_r)],
                     axis=1)
    n_v = WINDOW // Q_BLOCK + 1
    m_win = _stack_rows(_toeplitz(tbr, Q_BLOCK * (n_v - 1), WIN_TILE + Q_BLOCK * (n_v - 1), WINDOW - 1))
    twin = jnp.stack([m_win[:, :, Q_BLOCK * (n_v - 1 - v):Q_BLOCK * (n_v - 1 - v) + WIN_TILE] for v in range(n_v)],
                     axis=1)
    w = np.arange(16)[None, :] - 9
    rel = i - CMP_STRIDE * w - (CMP_BLOCK - 1)
    a = table(rel, rel >= 0)
    hi = a.astype(BF16)
    lo = (a - hi.astype(F32)).astype(BF16)
    acmp = jnp.concatenate([hi, lo], axis=-1)
    return jnp.swapaxes(tsel, 2, 3), jnp.swapaxes(twin, 2, 3), jnp.swapaxes(acmp, 1, 2)


def kernel(x_prompt, x_sample, cache_kv, cache_win, state_gla, state_conv, cache_mem, page_table, mem_prompt,
           g_mix, w_in, g_nsa_q, g_nsa_k, cmp_k_pe, cmp_k_w1, cmp_k_w2, cmp_v_pe, cmp_v_w1, cmp_v_w2,
           rel_bias, w_gla_gate, b_gla_gate, g_gla_o, g_mem, w_mem_kv, g_x_q, g_x_k,
           w_nsa_out, w_gla_out, w_x_out, w_o, g_ffn, w_up, conv_w, conv_b, w_down):
    bp, t, d = x_prompt.shape
    db = x_sample.shape[0]
    f = conv_w.shape[-1]

    offs = np.cumsum((0,) + IN_SIZES)
    segs = [w_in[:, offs[i]:offs[i + 1]] for i in range(len(IN_SIZES))]
    w_pad = jnp.concatenate([jnp.pad(s, ((0, 0), (0, pw - s.shape[1]))) for s, pw in zip(segs, PAD_SIZES)],
                            axis=1).astype(BF16)
    row = lambda v: v.reshape(1, -1).astype(F32)
    gq = row(jnp.tile(g_nsa_q, NSA_HEADS))
    gk0 = row(jnp.tile(g_nsa_k[0], NSA_KV))
    gk1 = row(jnp.tile(g_nsa_k[1], NSA_KV))
    gk2 = row(jnp.tile(g_nsa_k[2], NSA_KV))
    gxq = row(jnp.tile(g_x_q, X_HEADS))
    gxk = row(jnp.tile(g_x_k, X_HEADS))
    p64 = _block_ones(512, HEAD_DIM)
    p128 = _block_ones(512, X_DIM)
    bd2 = lambda a: jnp.concatenate([jnp.concatenate([a, jnp.zeros_like(a)], -1),
                                     jnp.concatenate([jnp.zeros_like(a), a], -1)], -2)
    bd4 = lambda a, c: jnp.concatenate([jnp.concatenate([bd2(a), jnp.zeros_like(bd2(a))], -1),
                                        jnp.concatenate([jnp.zeros_like(bd2(c)), bd2(c)], -1)], -2)
    pe = jnp.concatenate([jnp.tile(cmp_k_pe, (1, NSA_KV)), jnp.tile(cmp_v_pe, (1, NSA_KV))],
                         axis=-1).reshape(2, CMP_STRIDE, 1, 256)
    w1 = bd4(cmp_k_w1, cmp_v_w1).reshape(2, CMP_STRIDE * 256, 256).astype(BF16)
    w2 = bd4(cmp_k_w2, cmp_v_w2).astype(BF16)
    wg_pad = jnp.pad(w_gla_gate, ((0, 128 - GLA_RANK), (0, 0))).astype(BF16)
    tri = jnp.asarray(np.tril(np.ones((GLA_CHUNK, GLA_CHUNK), np.float32)), dtype=BF16)
    tb = rel_bias.astype(F32)[_bucket_table()]
    wn, wgo, wx, wo = (w.astype(BF16) for w in (w_nsa_out, w_gla_out, w_x_out, w_o))
    wup = w_up.astype(BF16)
    wdn = w_down.astype(BF16)
    ggo = row(g_gla_o)

    rows_p, win_p, _, gates, gla_in, xq, mg, ka, kw, vst, vwt, qt = _proj_in(
        x_prompt, row(g_mix), w_pad, gq, gk1, gk2, gxq, p64, p128, True)
    memkv_p = _memory_kv(mem_prompt, row(g_mem), w_mem_kv.astype(BF16), gxk, p128)
    kc, vct = _compress(rows_p, pe, w1, w2, gk0, p64)
    n_chunks = t // CMP_STRIDE
    n_sel = -(-t // SEL_BLOCK)
    tsel, twin, acmp = _prompt_tables(tb)
    ovt = _overlap(n_chunks, n_chunks - 1, 128, n_sel).T
    o_nsa = _nsa_prompt(qt, kc, vct, ka, kw, vst, vwt, gates, tsel, twin, acmp, ovt)
    s0t = jnp.zeros((bp, GLA_HEADS, GLA_DV, GLA_DK), F32)
    o_gla, st = _gla_prompt(gla_in, s0t, wg_pad, row(b_gla_gate), ggo, tri)
    o_x = _xatt(xq, memkv_p)
    m = bp * t
    x1 = _merge(o_nsa.reshape(m, 512), o_gla.reshape(m, 512), o_x.reshape(m, 512), mg.reshape(m, 3 * d),
                x_prompt.reshape(m, d), wn, wgo, wx, wo)
    y_p, tail = _ffn_seq(x1.reshape(bp, t, d), jnp.zeros((bp, 2, f), F32), row(g_ffn), wup, conv_w, row(conv_b), wdn)
    wl_p = min(WINDOW, t)
    out_rows_p = rows_p.reshape(bp, t, 4, NSA_KV, HEAD_DIM)
    out_win_p = win_p[:, t - wl_p:].reshape(bp, wl_p, 2, NSA_KV, HEAD_DIM)
    out_gla_p = jnp.swapaxes(st, 2, 3)
    out_conv_p = tail[:, 6:8]
    out_mem_p = memkv_p.reshape(bp, -1, 2, X_HEADS, X_DIM)

    n_pages = page_table.shape[1]
    page = cache_kv.shape[1]
    length = n_pages * page
    wl = cache_win.shape[1]
    rows_s, win_s, qn_s, gates_s, gla_s, xq_s, mg_s = (a[0] for a in _proj_in(
        x_sample.reshape(1, db, d), row(g_mix), w_pad, gq, gk1, gk2, gxq, p64, p128, False))

    eye = jnp.eye(NSA_KV, dtype=BF16)
    q8 = (qn_s.reshape(db, NSA_KV, NSA_GROUP, 1, HEAD_DIM) * eye[None, :, None, :, None]).reshape(db, 8, 128)
    gates8 = jnp.pad(gates_s[:, 0:24].reshape(db, 8, 3), ((0, 0), (0, 0), (0, 125)))
    n_chunks_s = length // CMP_STRIDE
    n_sel_s = -(-(length + 1) // SEL_BLOCK)
    nsp = -(-n_sel_s // 128) * 128
    cidx = np.arange(n_chunks_s)
    rel_c = length - (cidx * CMP_STRIDE + CMP_BLOCK - 1)
    bc = _bias_lookup(tb, rel_c, (rel_c >= 0) & (cidx < n_chunks_s - 1))
    bs = jnp.transpose(_bias_descending(tb, length).reshape(8, n_chunks_s, CMP_STRIDE), (2, 0, 1))
    bw =jnp.where(jnp.asarray(np.arange(wl, 0, -1) < WINDOW)[None], _bias_descending(tb, wl), NEG)
    b0 = tb[0].reshape(8, 1)
    ov_s = _overlap(n_chunks_s, n_chunks_s - 1, nsp, n_sel_s)
    chunks_per_block = SEL_BLOCK // CMP_STRIDE
    assert n_chunks_s // chunks_per_block <= 128 and page % CMP_STRIDE == 0
    et = jnp.asarray((cidx[:, None] // chunks_per_block == np.arange(128)[None, :]).astype(np.float32), dtype=BF16)
    cache_chunks = cache_kv.reshape(cache_kv.shape[0], page // CMP_STRIDE, CMP_STRIDE * 512)
    o8, win_new = _nsa_decode(page_table, cache_chunks, q8,
                              rows_s.reshape(db, 1, 512), win_s.reshape(db, 1, 256), cache_win.reshape(db, wl, 256),
                              gates8, bc, bs, bw, b0, ov_s, et, pe, w1, w2, gk0, p64)
    o8 = o8.reshape(db, NSA_KV, NSA_GROUP, NSA_KV, HEAD_DIM)
    o_nsa_s = jnp.stack([o8[:, 0, :, 0], o8[:, 1, :, 1]], axis=1).reshape(db, 512).astype(BF16)

    wgt = jnp.pad(w_gla_gate.T, ((0, 0), (0, 128 - GLA_RANK))).reshape(GLA_HEADS, GLA_DK, 128)
    o_gla_s, gla_state_s = _gla_step(
        gla_s[:, 0:256].reshape(db, GLA_HEADS, GLA_DK, 1), gla_s[:, 256:512].reshape(db, GLA_HEADS, GLA_DK, 1),
        gla_s[:, 1024:1152].reshape(db, 1, 128), gla_s[:, 512:1024].reshape(db, GLA_HEADS, 1, GLA_DV),
        gla_s[:, 1152:1664].reshape(db, GLA_HEADS, 1, GLA_DV), state_gla.astype(F32), wgt,
        b_gla_gate.reshape(GLA_HEADS, GLA_DK, 1), ggo)
    o_gla_s = o_gla_s.reshape(db, 512).astype(BF16)

    xq_pad = jnp.pad(xq_s.reshape(db, 1, 512), ((0, 0), (0, 15), (0, 0)))
    o_x_s = _xatt(xq_pad, cache_mem.reshape(db, -1, 1024))[:, 0]
    x1_s = _merge(o_nsa_s, o_gla_s, o_x_s, mg_s, x_sample.reshape(db, d), wn, wgo, wx, wo)
    y_s, g_new = _ffn_step(x1_s, state_conv[:, 0], state_conv[:, 1], row(g_ffn), wup, conv_w, row(conv_b), wdn)

    out_rows_s = rows_s.reshape(db, 1, 4, NSA_KV, HEAD_DIM)
    out_win_s = win_new.reshape(db, wl, 2, NSA_KV, HEAD_DIM)
    out_conv_s = jnp.stack([state_conv[:, 1], g_new], axis=1)
    return (y_p, y_s.reshape(db, 1, d), out_rows_p, out_win_p, out_gla_p, out_conv_p, out_mem_p,
            out_rows_s, out_win_s, gla_state_s, out_conv_s)
```

```python
import functools
import math

import numpy as np
import jax
import jax.numpy as jnp
from jax import lax
from jax.experimental import pallas as pl
from jax.experimental.pallas import tpu as pltpu

F32 = jnp.float32
BF16 = jnp.bfloat16

NSA_HEADS = 8
NSA_KV = 2
NSA_GROUP = 4
HEAD_DIM = 64
CMP_BLOCK = 32
CMP_STRIDE = 16
SEL_BLOCK = 64
SEL_TOP = 16
WINDOW = 512
Q_BLOCK = 128
GLA_HEADS = 4
GLA_DK = 64
GLA_DV = 128
GLA_RANK = 16
GLA_TAU = 16.0
GLA_CHUNK = 64
X_HEADS = 4
X_DIM = 128
N_BUCKETS = 32
MAX_DISTANCE = 128
EPS = 1e-6
LOG2E = math.log2(math.e)
NEG = -1e30
TINY = 1e-30
SEL_PENALTY = -1e9
MASKED_BELOW = -5e29
EXP_CLAMP = 80.0

IN_SIZES = (512, 768, 24, 256, 256, 512, 16, 512, 512, 3072)
PAD_SIZES = (512, 768, 128, 256, 256, 512, 128, 512, 512, 3072)
PAD_OFFS = tuple(int(v) for v in np.cumsum((0,) + PAD_SIZES))
D_IN_PAD = PAD_OFFS[-1]
GLA_IN_W = 256 + 256 + 512 + 128 + 512

VMEM_LIMIT = 56 * 1024 * 1024
FAR_TILE = 256
NEAR_TILE = 512
WIN_TILE = WINDOW + Q_BLOCK


def _cparams(n_axes):
    return pltpu.CompilerParams(dimension_semantics=("arbitrary",) * n_axes, vmem_limit_bytes=VMEM_LIMIT)


def _dot(a, b):
    return jnp.dot(a, b, preferred_element_type=F32)


def _dot_nt(a, b):
    return lax.dot_general(a, b, (((1,), (1,)), ((), ())), preferred_element_type=F32)


def _dot_tn(a, b):
    return lax.dot_general(a, b, (((0,), (0,)), ((), ())), preferred_element_type=F32)


def _split_dot(x, m):
    hi = x.astype(BF16)
    lo = (x - hi.astype(F32)).astype(BF16)
    return _dot(hi, m) + _dot(lo, m)


def _rms(x, g):
    return x * lax.rsqrt(jnp.mean(x * x, axis=-1, keepdims=True) + EPS) * g


def _group_rms(x, pmat, gsize, g):
    ss = _split_dot(x * x, pmat)
    return x * lax.rsqrt(ss * (1.0 / gsize) + EPS) * g


def _gelu(x):
    return 0.5 * x * (1.0 + jnp.tanh(math.sqrt(2.0 / math.pi) * (x + 0.044715 * (x * x * x))))


def _sigmoid(x):
    return 1.0 / (1.0 + jnp.exp(-x))


def _block_ones(n, gsize):
    i = np.arange(n) // gsize
    return jnp.asarray((i[:, None] == i[None, :]).astype(np.float32), dtype=BF16)


def _proj_kernel(x_ref, gmix_ref, w_ref, gq_ref, gk1_ref, gk2_ref, gxq_ref, p64_ref, p128_ref,
                 rows_ref, win_ref, qn_ref, gates_ref, gla_ref, xq_ref, mg_ref, *attn_refs):
    x = x_ref[0]
    h = _rms(x, gmix_ref[...]).astype(BF16)
    o = PAD_OFFS

    def seg(i):
        return _dot(h, w_ref[:, o[i]:o[i + 1]])

    p64 = p64_ref[...]
    p64s = p64_ref[0:128, 0:128]
    qn = _group_rms(seg(0), p64, HEAD_DIM, gq_ref[...]) * (HEAD_DIM ** -0.5)
    qn_ref[0] = qn.astype(BF16)
    if attn_refs:
        qt_ref = attn_refs[4]
        for u in range(x.shape[0] // Q_BLOCK):
            qt_ref[0, u] = (qn[Q_BLOCK * u:Q_BLOCK * (u + 1)] * LOG2E).T.astype(BF16)

    kv = seg(1)
    k_sel = _group_rms(kv[:, 256:384], p64s, HEAD_DIM, gk1_ref[...])
    k_win = _group_rms(kv[:, 512:640], p64s, HEAD_DIM, gk2_ref[...])
    rows_ref[0, :, 0:256] = kv[:, 0:256]
    rows_ref[0, :, 256:384] = k_sel
    rows_ref[0, :, 384:512] = kv[:, 384:512]
    win_ref[0, :, 0:128] = k_win
    win_ref[0, :, 128:256] = kv[:, 640:768]
    if attn_refs:
        ka_ref, kw_ref, vst_ref, vwt_ref, _ = attn_refs
        tm = x.shape[0]
        tpos = pl.program_id(1) * tm + lax.broadcasted_iota(jnp.int32, (tm, 128), 0)
        blk = lax.broadcasted_iota(jnp.int32, (tm, 128), 1)
        ka_ref[0, :, 0:128] = k_sel.astype(BF16)
        ka_ref[0, :, 128:256] = jnp.where(tpos // SEL_BLOCK == blk, 1.0, 0.0).astype(BF16)
        kw_ref[0] = k_win.astype(BF16)
        for u in range(tm // FAR_TILE):
            vst_ref[0, u] = kv[FAR_TILE * u:FAR_TILE * (u + 1), 384:512].T.astype(BF16)
        for u in range(tm // Q_BLOCK):
            vwt_ref[0, u] = kv[Q_BLOCK * u:Q_BLOCK * (u + 1), 640:768].T.astype(BF16)

    gates_ref[0] = _sigmoid(seg(2))
    gla_ref[0, :, 0:256] = seg(3) * (GLA_DK ** -0.5)
    gla_ref[0, :, 256:512] = seg(4)
    gla_ref[0, :, 512:1024] = seg(5)
    gla_ref[0, :, 1024:1152] = seg(6)
    gla_ref[0, :, 1152:1664] = seg(7)
    xq = _group_rms(seg(8), p128_ref[...], X_DIM, gxq_ref[...]) * (X_DIM ** -0.5)
    xq_ref[0] = xq.astype(BF16)
    mg_ref[0] = _sigmoid(seg(9)).astype(BF16)


def _proj_in(x, g_mix, w_pad, gq, gk1, gk2, gxq, p64, p128, attn_layout):
    b, t, d = x.shape
    tm = min(256, t)
    assert t % tm == 0
    widths = [512, 256, 512, 128, GLA_IN_W, 512, 3072]
    dtypes = [F32, F32, BF16, F32, F32, BF16, BF16]
    out_specs = [pl.BlockSpec((1, tm, w), lambda i, j: (i, j, 0)) for w in widths]
    out_shape = [jax.ShapeDtypeStruct((b, t, w), dt) for w, dt in zip(widths, dtypes)]
    if attn_layout:
        assert tm % Q_BLOCK == 0
        for w in (256, 128):
            out_specs.append(pl.BlockSpec((1, tm, w), lambda i, j: (i, j, 0)))
            out_shape.append(jax.ShapeDtypeStruct((b, t, w), BF16))
        assert tm % FAR_TILE == 0
        for rows, width in ((128, FAR_TILE), (128, Q_BLOCK), (512, Q_BLOCK)):
            out_specs.append(pl.BlockSpec((1, tm // width, rows, width), lambda i, j: (i, j, 0, 0)))
            out_shape.append(jax.ShapeDtypeStruct((b, t // width, rows, width), BF16))
    const = lambda shape: pl.BlockSpec(shape, lambda i, j: (0,) * len(shape))
    return pl.pallas_call(
        _proj_kernel,
        grid=(b, t // tm),
        in_specs=[pl.BlockSpec((1, tm, d), lambda i, j: (i, j, 0)),
                  const((1, d)), const((d, D_IN_PAD)), const((1, 512)), const((1, 128)), const((1, 128)),
                  const((1, 512)), const((512, 512)), const((512, 512))],
        out_specs=out_specs,
        out_shape=out_shape,
        compiler_params=_cparams(2),
    )(x, g_mix, w_pad, gq, gk1, gk2, gxq, p64, p128)


def _memkv_kernel(m_ref, g_ref, w_ref, gk_ref, p128_ref, o_ref):
    h = _rms(m_ref[0], g_ref[...]).astype(BF16)
    kv = _dot(h, w_ref[...])
    o_ref[0, :, 0:512] = _group_rms(kv[:, 0:512], p128_ref[...], X_DIM, gk_ref[...])
    o_ref[0, :, 512:1024] = kv[:, 512:1024]


def _memory_kv(mem, g_mem, w_mem, gxk, p128):
    b, m, d = mem.shape
    const = lambda shape: pl.BlockSpec(shape, lambda i: (0,) * len(shape))
    return pl.pallas_call(
        _memkv_kernel,
        grid=(b,),
        in_specs=[pl.BlockSpec((1, m, d), lambda i: (i, 0, 0)), const((1, d)), const((d, 1024)),
                  const((1, 512)), const((512, 512))],
        out_specs=pl.BlockSpec((1, m, 1024), lambda i: (i, 0, 0)),
        out_shape=jax.ShapeDtypeStruct((b, m, 1024), F32),
        compiler_params=_cparams(1),
    )(mem, g_mem, w_mem, gxk, p128)


def _compress_core(load_j, n_chunks, pe_ref, w1_ref, w2_ref, gk0_ref, p64_ref):
    xs = [load_j(j) for j in range(CMP_STRIDE)]
    halves = [_dot(jnp.concatenate([(x + pe_ref[r, j]).astype(BF16) for j, x in enumerate(xs)], axis=1), w1_ref[r])
              for r in range(2)]
    hid = halves[0] + pltpu.roll(halves[1], n_chunks - 1, 0)
    out = _dot(_gelu(hid).astype(BF16), w2_ref[...])
    row = lax.broadcasted_iota(jnp.int32, (n_chunks, 128), 0)
    live = row < n_chunks - 1
    kc = _group_rms(out[:, 0:128], p64_ref[0:128, 0:128], HEAD_DIM, gk0_ref[...])
    return jnp.where(live, kc, 0.0), jnp.where(live, out[:, 128:256], 0.0)


def _compress_kernel(rk_ref, rv_ref, pe_ref, w1_ref, w2_ref, gk0_ref, p64_ref, kc_ref, vc_ref, *, n_chunks):
    load_j = lambda j: jnp.concatenate([rk_ref[0, pl.ds(j, n_chunks, stride=CMP_STRIDE), :],
                                        rv_ref[0, pl.ds(j, n_chunks, stride=CMP_STRIDE), :]], axis=1)
    kc, vc = _compress_core(load_j, n_chunks, pe_ref, w1_ref, w2_ref, gk0_ref, p64_ref)
    kc_ref[0] = kc.astype(BF16)
    vc_ref[0] = vc.T.astype(BF16)


def _compress(rows, pe, w1, w2, gk0, p64):
    b, t, _ = rows.shape
    n_chunks = t // CMP_STRIDE
    const = lambda shape: pl.BlockSpec(shape, lambda i: (0,) * len(shape))
    return pl.pallas_call(
        functools.partial(_compress_kernel, n_chunks=n_chunks),
        grid=(b,),
        in_specs=[pl.BlockSpec((1, t, 128), lambda i: (i, 0, 0)), pl.BlockSpec((1, t, 128), lambda i: (i, 0, 1)),
                  const((2, CMP_STRIDE, 1, 256)), const((2, CMP_STRIDE * 256, 256)), const((256, 256)),
                  const((1, 128)), const((512, 512))],
        out_specs=[pl.BlockSpec((1, n_chunks, 128), lambda i: (i, 0, 0)),
                   pl.BlockSpec((1, 128, n_chunks), lambda i: (i, 0, 0))],
        out_shape=[jax.ShapeDtypeStruct((b, n_chunks, 128), BF16), jax.ShapeDtypeStruct((b, 128, n_chunks), BF16)],
        compiler_params=_cparams(1),
    )(rows, rows, pe, w1, w2, gk0, p64)


def _masked_softmax(s, axis, exp_fn=jnp.exp):
    valid = s > MASKED_BELOW
    m = jnp.max(s, axis=axis, keepdims=True)
    p = jnp.where(valid, exp_fn(s - m), 0.0)
    return p / jnp.maximum(jnp.sum(p, axis=axis, keepdims=True), TINY)


def _select_blocks(score, top, axis):
    pos = lax.broadcasted_iota(jnp.int32, score.shape, axis).astype(F32)
    sel = jnp.zeros(score.shape, jnp.bool_)
    for _ in range(top):
        mx = jnp.max(score, axis=axis, keepdims=True)
        idx = jnp.min(jnp.where(score == mx, pos, 1e9), axis=axis, keepdims=True)
        hit = pos == idx
        sel = jnp.logical_or(sel, hit)
        score = jnp.where(hit, -3e38, score)
    return sel


def _select_with_forced(imp, forced, top, axis):
    n_forced = 3
    assert top > n_forced
    return jnp.logical_or(forced, _select_blocks(jnp.where(forced, -3e38, imp), top - n_forced, axis))


def _values_times_probs(vt_tiles, p):
    out = None
    start = 0
    for vt in vt_tiles:
        part = _dot(vt, p[start:start + vt.shape[1]])
        start += vt.shape[1]
        out = part if out is None else out + part
    return out


def _softmax_update_t(s_ref, bias, vt_tiles, m_ref, l_ref, acc_ref):
    a_parts, p_parts = [], []
    for g in range(s_ref.shape[1] // Q_BLOCK):
        cs = slice(Q_BLOCK * g, Q_BLOCK * (g + 1))
        sg = s_ref[:, cs]
        if bias is not None:
            sg = sg + bias(cs)
        m_old = m_ref[:, cs]
        m_new = jnp.maximum(m_old, jnp.max(sg, axis=0, keepdims=True))
        alpha = jnp.exp2(m_old - m_new)
        p = jnp.exp2(sg - m_new)
        m_ref[:, cs] = m_new
        l_ref[:, cs] = alpha * l_ref[:, cs] + jnp.sum(p, axis=0, keepdims=True)
        a_parts.append(alpha)
        p_parts.append(p.astype(BF16))
    acc_ref[...] = (jnp.concatenate(a_parts, axis=1) * acc_ref[...]
                    + _values_times_probs(vt_tiles, jnp.concatenate(p_parts, axis=1)))


def _attn_kernel(qt_ref, kc_ref, vct_ref, ka_ref, vst_ref, kw_ref, vwt_ref, gt_ref, tsel_ref, twin_ref, acmp_ref,
                 ovt_ref, o_ref, sa_scr, sb_scr, m_scr, l_scr, acc_scr, *, top):
    k = pl.program_id(1)
    n = pl.program_id(2)
    cols = NSA_GROUP * Q_BLOCK
    vrow = pl.multiple_of(k * HEAD_DIM, HEAD_DIM)
    zero = jnp.zeros((HEAD_DIM, Q_BLOCK), BF16)
    parts = []
    for g in range(NSA_GROUP):
        piece = qt_ref[0, 0, HEAD_DIM * g:HEAD_DIM * (g + 1), :]
        parts.append(jnp.where(k == 0, jnp.concatenate([piece, zero], axis=0),
                               jnp.concatenate([zero, piece], axis=0)))
    qt = jnp.concatenate(parts, axis=1)

    nc = kc_ref.shape[1]
    c32 = lax.broadcasted_iota(jnp.int32, (nc, 32), 0)
    r32 = lax.broadcasted_iota(jnp.int32, (nc, 32), 1)
    onehot = jnp.where((c32 - 8 * n + 9) == (r32 & 15), 1.0, 0.0).astype(BF16)
    band = _dot(onehot, acmp_ref[0])
    cp = lax.broadcasted_iota(jnp.int32, (nc, cols), 0) - 8 * n + 9
    s_c = _dot(kc_ref[0], qt) + jnp.where(cp < 0, 0.0, jnp.where(cp > 15, NEG, band))
    p_c = _masked_softmax(s_c, 0, jnp.exp2).astype(BF16)
    o_c = _dot(vct_ref[0, pl.ds(vrow, HEAD_DIM), :], p_c)
    imp4 = _dot(ovt_ref[...], p_c)
    imp = imp4[:, 0:128] + imp4[:, 128:256] + imp4[:, 256:384] + imp4[:, 384:512]

    jj = lax.broadcasted_iota(jnp.int32, (128, Q_BLOCK), 0)
    tpos = n * Q_BLOCK + lax.broadcasted_iota(jnp.int32, (128, Q_BLOCK), 1)
    tblk = tpos // SEL_BLOCK
    forced = (jj == 0) | (jj == tblk) | (jj == tblk - 1)
    sel_t = _select_with_forced(jnp.where(jj * SEL_BLOCK <= tpos, imp, NEG), forced, top, 0)
    pen_t = jnp.where(sel_t, 0.0, SEL_PENALTY).astype(BF16)
    rhs = jnp.concatenate([qt, jnp.concatenate([pen_t] * NSA_GROUP, axis=1)], axis=0)

    n_far = jnp.maximum(n - 2, 0) // 2
    r_near = n - 2 * n_far

    def value_tiles(ref, first, count):
        return [ref[0, first + d, pl.ds(vrow, HEAD_DIM), :] for d in range(count)]

    def issue_scores(tile, s_ref):
        s_ref[...] = _dot(ka_ref[0, pl.ds(pl.multiple_of(tile * FAR_TILE, FAR_TILE), FAR_TILE), :], rhs)

    def reduce_tile(tile, s_ref, half_idx):
        bias = None
        if half_idx is not None:
            bias = lambda cs: tsel_ref[0, r_near, FAR_TILE * half_idx:FAR_TILE * (half_idx + 1), cs]
        _softmax_update_t(s_ref, bias, value_tiles(vst_ref, tile, 1), m_scr, l_scr, acc_scr)

    m_scr[...] = jnp.full((1, cols), NEG, F32)
    l_scr[...] = jnp.zeros((1, cols), F32)
    acc_scr[...] = jnp.zeros((HEAD_DIM, cols), F32)
    odd = n_far % 2

    @pl.when(odd == 1)
    def _():
        issue_scores(0, sa_scr)
        reduce_tile(0, sa_scr, None)

    issue_scores(odd, sa_scr)

    @pl.loop(0, n_far // 2)
    def _(j):
        t0 = odd + 2 * j
        issue_scores(t0 + 1, sb_scr)
        reduce_tile(t0, sa_scr, None)
        issue_scores(t0 + 2, sa_scr)
        reduce_tile(t0 + 1, sb_scr, None)

    issue_scores(n_far + 1, sb_scr)
    reduce_tile(n_far, sa_scr, 0)
    reduce_tile(n_far + 1, sb_scr, 1)
    o_s = acc_scr[...] / l_scr[...]

    wt = jnp.maximum(n - WINDOW // Q_BLOCK, 0)
    s_w = _dot(kw_ref[0, pl.ds(pl.multiple_of(wt * Q_BLOCK, Q_BLOCK), WIN_TILE), :], qt) + twin_ref[0, 0]
    p_w = jnp.exp2(s_w - jnp.max(s_w, axis=0, keepdims=True))
    o_w = (_values_times_probs(value_tiles(vwt_ref, wt, WIN_TILE // Q_BLOCK), p_w.astype(BF16))
           / jnp.sum(p_w, axis=0, keepdims=True))

    gtt = gt_ref[0].T

    def gate_row(branch):
        rows = [jnp.where(k == 0, gtt[3 * g + branch:3 * g + branch + 1],
                          gtt[12 + 3 * g + branch:12 + 3 * g + branch + 1]) for g in range(NSA_GROUP)]
        return jnp.concatenate(rows, axis=1)

    o_t = gate_row(0) * o_c + gate_row(1) * o_s + gate_row(2) * o_w
    left = jnp.concatenate([o_t[:, 0:128], o_t[:, 128:256]], axis=0).T
    right = jnp.concatenate([o_t[:, 256:384], o_t[:, 384:512]], axis=0).T
    o_ref[0] = jnp.concatenate([left, right], axis=1).astype(BF16)


def _nsa_prompt(qt, kc, vct, ka, kw, vst, vwt, gates, tsel, twin, acmp, ovt):
    b, t, _ = ka.shape
    nqb = t // Q_BLOCK
    nc = kc.shape[1]
    top = min(SEL_TOP, -(-t // SEL_BLOCK))
    assert t >= WIN_TILE and t % FAR_TILE == 0 and t // SEL_BLOCK <= 128
    n_win = WINDOW // Q_BLOCK
    per_b = lambda shape: pl.BlockSpec((1,) + shape, lambda i, k, n: (i,) + (0,) * len(shape))
    return pl.pallas_call(
        functools.partial(_attn_kernel, top=top),
        grid=(b, NSA_KV, nqb),
        in_specs=[pl.BlockSpec((1, 1, NSA_GROUP * HEAD_DIM, Q_BLOCK), lambda i, k, n: (i, n, k, 0)),
                  per_b((nc, 128)), per_b((128, nc)),
                  per_b((t, 256)), per_b((t // FAR_TILE, 128, FAR_TILE)), per_b((t, 128)), per_b((nqb, 128, Q_BLOCK)),
                  pl.BlockSpec((1, Q_BLOCK, 128), lambda i, k, n: (i, n, 0)),
                  pl.BlockSpec((1, 4, NEAR_TILE, 512), lambda i, k, n: (k, 0, 0, 0)),
                  pl.BlockSpec((1, 1, WIN_TILE, 512), lambda i, k, n: (k, jnp.minimum(n, n_win), 0, 0)),
                  pl.BlockSpec((1, 32, 512), lambda i, k, n: (k, 0, 0)),
                  pl.BlockSpec((128, nc), lambda i, k, n: (0, 0))],
        out_specs=pl.BlockSpec((1, Q_BLOCK, 256), lambda i, k, n: (i, n, k)),
        out_shape=jax.ShapeDtypeStruct((b, t, 512), BF16),
        scratch_shapes=[pltpu.VMEM((FAR_TILE, NSA_GROUP * Q_BLOCK), F32), pltpu.VMEM((FAR_TILE, NSA_GROUP * Q_BLOCK), F32),
                        pltpu.VMEM((1, NSA_GROUP * Q_BLOCK), F32), pltpu.VMEM((1, NSA_GROUP * Q_BLOCK), F32),
                        pltpu.VMEM((HEAD_DIM, NSA_GROUP * Q_BLOCK), F32)],
        compiler_params=_cparams(3),
    )(qt, kc, vct, ka, vst, kw, vwt, gates, tsel, twin, acmp, ovt)


def _log_sigmoid(z):
    return jnp.minimum(z, 0.0) - jnp.log1p(jnp.exp(-jnp.abs(z)))


def _gla_kernel(x_ref, s0_ref, wg_ref, bg_ref, ggo_ref, tri_ref, o_ref, st_ref, s_scr, *, n_chunks):
    @pl.when(pl.program_id(0) == 0)
    def _():
        s_scr[...] = s0_ref[...]

    c_len = GLA_CHUNK
    tri = tri_ref[...]
    ti = lax.broadcasted_iota(jnp.int32, (c_len, c_len), 0)
    si = lax.broadcasted_iota(jnp.int32, (c_len, c_len), 1)
    causal = si <= ti
    for c, bi in [(c, bi) for c in range(n_chunks) for bi in range(x_ref.shape[0])]:
        rs = slice(c_len * c, c_len * (c + 1))
        q = x_ref[bi, rs, 0:256]
        kk = x_ref[bi, rs, 256:512]
        v = x_ref[bi, rs, 512:1024]
        lr = x_ref[bi, rs, 1024:1152]
        r = x_ref[bi, rs, 1152:1664]
        la = _log_sigmoid(_dot(lr.astype(BF16), wg_ref[...]) + bg_ref[...]) * (1.0 / GLA_TAU)
        a1 = la.astype(BF16)
        r1 = la - a1.astype(F32)
        a2 = r1.astype(BF16)
        a3 = (r1 - a2.astype(F32)).astype(BF16)
        cb = _dot(tri, a1) + _dot(tri, a2) + _dot(tri, a3)
        last = cb[c_len - 1:c_len, :]
        mid = cb[c_len // 2:c_len // 2 + 1, :]
        qe = (q * jnp.exp(cb)).astype(BF16)
        qa = (q * jnp.exp(jnp.minimum(cb - mid, EXP_CLAMP))).astype(BF16)
        kb = (kk * jnp.exp(jnp.minimum(mid - cb, EXP_CLAMP))).astype(BF16)
        ke = (kk * jnp.exp(last - cb)).astype(BF16)
        dec = jnp.exp(last)
        for h in range(GLA_HEADS):
            ks = slice(GLA_DK * h, GLA_DK * (h + 1))
            vs = slice(GLA_DV * h, GLA_DV * (h + 1))
            att = jnp.where(causal, _dot_nt(qa[:, ks], kb[:, ks]), 0.0)
            vh = v[:, vs].astype(BF16)
            st = s_scr[bi, h]
            o = _dot(att.astype(BF16), vh) + _dot_nt(qe[:, ks], st.astype(BF16))
            s_scr[bi, h] = st * dec[:, ks] + _dot_tn(vh, ke[:, ks])
            on = _rms(o, ggo_ref[...])
            rh = r[:, vs]
            o_ref[bi, rs, vs] = (on * (rh * _sigmoid(rh))).astype(BF16)
    st_ref[...] = s_scr[...]


def _gla_prompt(gla_in, s0t, wg, bg, ggo, tri):
    b, t, w = gla_in.shape
    ct = min(256, t)
    assert t % ct == 0 and ct % GLA_CHUNK == 0
    const = lambda shape: pl.BlockSpec(shape, lambda j: (0,) * len(shape))
    state_shape = (b, GLA_HEADS, GLA_DV, GLA_DK)
    return pl.pallas_call(
        functools.partial(_gla_kernel, n_chunks=ct // GLA_CHUNK),
        grid=(t // ct,),
        in_specs=[pl.BlockSpec((b, ct, w), lambda j: (0, j, 0)), const(state_shape),
                  const((128, 256)), const((1, 256)), const((1, 128)), const((GLA_CHUNK, GLA_CHUNK))],
        out_specs=[pl.BlockSpec((b, ct, 512), lambda j: (0, j, 0)), const(state_shape)],
        out_shape=[jax.ShapeDtypeStruct((b, t, 512), BF16), jax.ShapeDtypeStruct(state_shape, F32)],
        scratch_shapes=[pltpu.VMEM(state_shape, F32)],
        compiler_params=_cparams(1),
    )(gla_in, s0t, wg, bg, ggo, tri)


def _gla_step_kernel(q_ref, k_ref, lr_ref, v_ref, r_ref, s_ref, wgt_ref, bgt_ref, ggo_ref, o_ref, sn_ref):
    lr = lr_ref[0]
    for h in range(GLA_HEADS):
        z = jnp.sum(wgt_ref[h] * lr, axis=-1, keepdims=True) + bgt_ref[h]
        a = jnp.exp(_log_sigmoid(z) * (1.0 / GLA_TAU))
        s0 = s_ref[0, h]
        kh = k_ref[0, h]
        qh = q_ref[0, h]
        vh = v_ref[0, h]
        sn_ref[0, h] = a * s0 + kh * vh
        o = jnp.sum((qh * a) * s0, axis=0, keepdims=True) + jnp.sum(qh * kh, axis=0, keepdims=True) * vh
        on = _rms(o, ggo_ref[...])
        rh = r_ref[0, h]
        o_ref[0, h] = on * (rh * _sigmoid(rh))


def _gla_step(q_col, k_col, lr, v_row, r_row, s0, wgt, bgt, ggo):
    b = q_col.shape[0]
    const = lambda shape: pl.BlockSpec(shape, lambda i: (0,) * len(shape))
    per_b = lambda shape: pl.BlockSpec((1,) + shape, lambda i: (i,) + (0,) * len(shape))
    return pl.pallas_call(
        _gla_step_kernel,
        grid=(b,),
        in_specs=[per_b((GLA_HEADS, GLA_DK, 1)), per_b((GLA_HEADS, GLA_DK, 1)), per_b((1, 128)),
                  per_b((GLA_HEADS, 1, GLA_DV)), per_b((GLA_HEADS, 1, GLA_DV)), per_b((GLA_HEADS, GLA_DK, GLA_DV)),
                  const((GLA_HEADS, GLA_DK, 128)), const((GLA_HEADS, GLA_DK, 1)), const((1, 128))],
        out_specs=[per_b((GLA_HEADS, 1, GLA_DV)), per_b((GLA_HEADS, GLA_DK, GLA_DV))],
        out_shape=[jax.ShapeDtypeStruct((b, GLA_HEADS, 1, GLA_DV), F32),
                   jax.ShapeDtypeStruct((b, GLA_HEADS, GLA_DK, GLA_DV), F32)],
        compiler_params=_cparams(1),
    )(q_col, k_col, lr, v_row, r_row, s0, wgt, bgt, ggo)


def _xatt_kernel(xq_ref, mem_ref, o_ref):
    for h in range(X_HEADS):
        ls = slice(X_DIM * h, X_DIM * (h + 1))
        kh = mem_ref[0, :, ls].astype(BF16)
        vh = mem_ref[0, :, 512 + X_DIM * h:512 + X_DIM * (h + 1)].astype(BF16)
        s = _dot_nt(xq_ref[0, :, ls], kh)
        p = jnp.exp(s - jnp.max(s, axis=-1, keepdims=True))
        p = p / jnp.sum(p, axis=-1, keepdims=True)
        o_ref[0, :, ls] = _dot(p.astype(BF16), vh).astype(BF16)


def _xatt(xq, memkv):
    b, t, _ = xq.shape
    m = memkv.shape[1]
    tq = min(512, t)
    assert t % tq == 0
    return pl.pallas_call(
        _xatt_kernel,
        grid=(b, t // tq),
        in_specs=[pl.BlockSpec((1, tq, 512), lambda i, j: (i, j, 0)),
                  pl.BlockSpec((1, m, 1024), lambda i, j: (i, 0, 0))],
        out_specs=pl.BlockSpec((1, tq, 512), lambda i, j: (i, j, 0)),
        out_shape=jax.ShapeDtypeStruct((b, t, 512), BF16),
        compiler_params=_cparams(2),
    )(xq, memkv)


def _merge_kernel(on_ref, og_ref, ox_ref, mg_ref, x_ref, wn_ref, wg_ref, wx_ref, wo_ref, x1_ref):
    d = x_ref.shape[-1]
    merged = (mg_ref[:, 0:d].astype(F32) * _dot(on_ref[...], wn_ref[...])
              + mg_ref[:, d:2 * d].astype(F32) * _dot(og_ref[...], wg_ref[...])
              + mg_ref[:, 2 * d:3 * d].astype(F32) * _dot(ox_ref[...], wx_ref[...]))
    x1_ref[...] = x_ref[...] + _dot(merged.astype(BF16), wo_ref[...])


def _merge(o_nsa, o_gla, o_x, mg, x, wn, wg, wx, wo):
    m, d = x.shape
    tm = min(512, m)
    assert m % tm == 0
    row = lambda w: pl.BlockSpec((tm, w), lambda i: (i, 0))
    const = lambda shape: pl.BlockSpec(shape, lambda i: (0,) * len(shape))
    return pl.pallas_call(
        _merge_kernel,
        grid=(m // tm,),
        in_specs=[row(512), row(512), row(512), row(3 * d), row(d),
                  const((512, d)), const((512, d)), const((512, d)), const((d, d))],
        out_specs=row(d),
        out_shape=jax.ShapeDtypeStruct((m, d), F32),
        compiler_params=_cparams(1),
    )(o_nsa, o_gla, o_x, mg, x, wn, wg, wx, wo)


def _ffn_seq_kernel(x_ref, past_ref, g_ref, wup_ref, cw_ref, cb_ref, wdn_ref, y_ref, tail_ref, carry_ref):
    f = cw_ref.shape[-1]
    tm = x_ref.shape[1]

    @pl.when(pl.program_id(1) == 0)
    def _():
        carry_ref[...] = jnp.zeros(carry_ref.shape, F32)
        carry_ref[6:8, :] = past_ref[0]

    x1 = x_ref[0]
    ug = _dot(_rms(x1, g_ref[...]).astype(BF16), wup_ref[...])
    u = ug[:, 0:f]
    g = ug[:, f:2 * f]
    row = lax.broadcasted_iota(jnp.int32, (tm, f), 0)
    p1 = carry_ref[7:8, :]
    p2 = carry_ref[6:7, :]
    gm1 = jnp.where(row == 0, p1, pltpu.roll(g, 1, 0))
    gm2 = jnp.where(row == 0, p2, jnp.where(row == 1, p1, pltpu.roll(g, 2, 0)))
    gc = cb_ref[...] + cw_ref[0:1, :] * gm2 + cw_ref[1:2, :] * gm1 + cw_ref[2:3, :] * g
    y_ref[0] = x1 + _dot((_gelu(gc) * u).astype(BF16), wdn_ref[...])
    carry_ref[...] = g[tm - 8:tm, :]
    tail_ref[0] = g[tm - 8:tm, :]


def _ffn_seq(x1, conv_past, g_ffn, w_up, conv_w, conv_b, w_down):
    b, t, d = x1.shape
    f = conv_w.shape[-1]
    tm = min(256, t)
    assert t % tm == 0 and tm >= 8
    const = lambda shape: pl.BlockSpec(shape, lambda i, j: (0,) * len(shape))
    return pl.pallas_call(
        _ffn_seq_kernel,
        grid=(b, t // tm),
        in_specs=[pl.BlockSpec((1, tm, d), lambda i, j: (i, j, 0)),
                  pl.BlockSpec((1, 2, f), lambda i, j: (i, 0, 0)),
                  const((1, d)), const((d, 2 * f)), const((3, f)), const((1, f)), const((f, d))],
        out_specs=[pl.BlockSpec((1, tm, d), lambda i, j: (i, j, 0)),
                   pl.BlockSpec((1, 8, f), lambda i, j: (i, 0, 0))],
        out_shape=[jax.ShapeDtypeStruct((b, t, d), F32), jax.ShapeDtypeStruct((b, 8, f), F32)],
        scratch_shapes=[pltpu.VMEM((8, f), F32)],
        compiler_params=_cparams(2),
    )(x1, conv_past, g_ffn, w_up, conv_w, conv_b, w_down)


def _ffn_step_kernel(x_ref, p0_ref, p1_ref, g_ref, wup_ref, cw_ref, cb_ref, wdn_ref, y_ref, gnew_ref):
    f = cw_ref.shape[-1]
    x1 = x_ref[...]
    ug = _dot(_rms(x1, g_ref[...]).astype(BF16), wup_ref[...])
    u = ug[:, 0:f]
    g = ug[:, f:2 * f]
    gc = cb_ref[...] + cw_ref[0:1, :] * p0_ref[...] + cw_ref[1:2, :] * p1_ref[...] + cw_ref[2:3, :] * g
    y_ref[...] = x1 + _dot((_gelu(gc) * u).astype(BF16), wdn_ref[...])
    gnew_ref[...] = g


def _ffn_step(x1, p0, p1, g_ffn, w_up, conv_w, conv_b, w_down):
    m, d = x1.shape
    f = conv_w.shape[-1]
    full = lambda shape: pl.BlockSpec(shape, lambda i: (0,) * len(shape))
    return pl.pallas_call(
        _ffn_step_kernel,
        grid=(1,),
        in_specs=[full((m, d)), full((m, f)), full((m, f)), full((1, d)), full((d, 2 * f)), full((3, f)),
                  full((1, f)), full((f, d))],
        out_specs=[full((m, d)), full((m, f))],
        out_shape=[jax.ShapeDtypeStruct((m, d), F32), jax.ShapeDtypeStruct((m, f), F32)],
        compiler_params=_cparams(1),
    )(x1, p0, p1, g_ffn, w_up, conv_w, conv_b, w_down)


def _decode_kernel(pt_ref, *refs, top, n_sel, n_pages):
    del pt_ref
    page_refs = refs[:n_pages]
    (q8_ref, new_ref, neww_ref, cwin_ref, gt_ref, bc_ref, bs_ref, bw_ref, b0_ref, ovt_ref, et_ref, pe_ref, w1_ref,
     w2_ref, gk0_ref, p64_ref, o_ref, wout_ref, xs_ref) = refs[n_pages:]
    page = page_refs[0].shape[1]
    length = n_pages * page
    n_chunks = length // CMP_STRIDE
    for u, pg in enumerate(page_refs):
        for c in range(2):
            xs_ref[c, page * u:page * (u + 1), :] = pg[0, :, 128 * c:128 * (c + 1)]

    def load_j(j):
        return jnp.concatenate([xs_ref[c, pl.ds(j, n_chunks, stride=CMP_STRIDE), :] for c in range(2)], axis=1)

    q8 = q8_ref[0]
    q8f = q8.astype(F32)
    rowk = lax.broadcasted_iota(jnp.int32, (8, 128), 0) // NSA_GROUP
    lane_half = lax.broadcasted_iota(jnp.int32, (8, 128), 1) // HEAD_DIM

    def half_mask(x):
        return jnp.where(rowk == lane_half, x, 0.0)

    kc, vc = _compress_core(load_j, n_chunks, pe_ref, w1_ref, w2_ref, gk0_ref, p64_ref)
    kc = kc.astype(BF16)
    vc = vc.astype(BF16)

    p_c = _masked_softmax(_dot_nt(q8, kc) + bc_ref[...], 1).astype(BF16)
    o_c = half_mask(_dot(p_c, vc))
    pf = p_c.astype(F32)
    p2 = jnp.concatenate([jnp.sum(pf[0:4], axis=0, keepdims=True), jnp.sum(pf[4:8], axis=0, keepdims=True),
                          jnp.zeros((126, n_chunks), F32)], axis=0)
    p2_hi = p2.astype(BF16)
    p2_lo = (p2 - p2_hi.astype(F32)).astype(BF16)
    imp_t = _dot_nt(ovt_ref[...], p2_hi) + _dot_nt(ovt_ref[...], p2_lo)
    nsp = imp_t.shape[0]
    jj = lax.broadcasted_iota(jnp.int32, (nsp, 128), 0)
    tblk = length // SEL_BLOCK
    forced = (jj == 0) | (jj == tblk) | (jj == tblk - 1)
    sel_t = _select_with_forced(jnp.where(jj < n_sel, imp_t, -2e38), forced, top, 0)
    pen = jnp.where(sel_t, 0.0, SEL_PENALTY).T
    pen8 = jnp.concatenate([jnp.broadcast_to(pen[0:1], (4, nsp)), jnp.broadcast_to(pen[1:2], (4, nsp))], axis=0)

    new = new_ref[0]
    neww = neww_ref[0]
    b0 = b0_ref[...]

    def attend(s_parts, s_new, v_parts, v_new):
        m = s_new
        for s in s_parts:
            m = jnp.maximum(m, jnp.max(s, axis=-1, keepdims=True))
        pn = jnp.exp(s_new - m)
        l = pn
        o = pn.astype(BF16).astype(F32) * v_new.astype(BF16).astype(F32)
        for s, v in zip(s_parts, v_parts):
            pp = jnp.exp(s - m)
            l = l + jnp.sum(pp, axis=-1, keepdims=True)
            o = o + _dot(pp.astype(BF16), v)
        return half_mask(o / l)

    def new_score(k_new):
        return jnp.sum(q8f * k_new.astype(BF16).astype(F32), axis=-1, keepdims=True) + b0

    k_past = jnp.concatenate([pg[0, :, 256:384] for pg in page_refs], axis=0).astype(BF16)
    v_past = jnp.concatenate([pg[0, :, 384:512] for pg in page_refs], axis=0).astype(BF16)
    s_past = _dot_nt(q8, k_past) + _dot_nt(pen8[:, 0:128].astype(BF16), et_ref[...]) + bs_ref[...]
    lane = lax.broadcasted_iota(jnp.int32, (8, nsp), 1)
    pen_new = jnp.sum(jnp.where(lane == tblk, pen8, 0.0), axis=-1, keepdims=True)
    o_s = attend([s_past], new_score(new[:, 256:384]) + pen_new, [v_past], new[:, 384:512])

    cw = cwin_ref[0]
    s_w = _dot_nt(q8, cw[:, 0:128].astype(BF16)) + bw_ref[...]
    o_w = attend([s_w], new_score(neww[:, 0:128]), [cw[:, 128:256].astype(BF16)], neww[:, 128:256])

    gt = gt_ref[0]
    o_ref[0] = gt[:, 0:1] * o_c + gt[:, 1:2] * o_s + gt[:, 2:3] * o_w

    wl = cw.shape[0]
    wrow = lax.broadcasted_iota(jnp.int32, cw.shape, 0)
    wout_ref[0] = jnp.where(wrow == wl - 1, neww, pltpu.roll(cw, wl - 1, 0))


def _nsa_decode(page_table, cache2d, q8, new_rows, new_win, cache_win, gates8, bc, bs, bw, b0, ov, et,
                pe, w1, w2, gk0, p64):
    db, n_pages = page_table.shape
    page = cache2d.shape[1]
    length = n_pages * page
    n_chunks = length // CMP_STRIDE
    n_sel = -(-(length + 1) // SEL_BLOCK)
    top = min(SEL_TOP, n_sel)
    wl = cache_win.shape[1]
    nsp = ov.shape[0]
    const = lambda shape: pl.BlockSpec(shape, lambda i, pt: (0,) * len(shape))
    per_b = lambda shape: pl.BlockSpec((1,) + shape, lambda i, pt: (i,) + (0,) * len(shape))
    page_spec = lambda u: pl.BlockSpec((1, page, 512), lambda i, pt: (pt[i, u], 0, 0))
    grid_spec = pltpu.PrefetchScalarGridSpec(
        num_scalar_prefetch=1,
        grid=(db,),
        in_specs=[page_spec(u) for u in range(n_pages)] + [
                  per_b((8, 128)), per_b((1, 512)), per_b((1, 256)), per_b((wl, 256)), per_b((8, 128)),
                  const((8, n_chunks)), const((8, length)), const((8, wl)), const((8, 1)),
                  const((nsp, n_chunks)), const((length, 128)),
                  const((2, CMP_STRIDE, 1, 256)), const((2, CMP_STRIDE * 256, 256)), const((256, 256)),
                  const((1, 128)), const((512, 512))],
        out_specs=[per_b((8, 128)), per_b((wl, 256))],
        scratch_shapes=[pltpu.VMEM((2, length, 128), F32)],
    )
    return pl.pallas_call(
        functools.partial(_decode_kernel, top=top, n_sel=n_sel, n_pages=n_pages),
        grid_spec=grid_spec,
        out_shape=[jax.ShapeDtypeStruct((db, 8, 128), F32), jax.ShapeDtypeStruct((db, wl, 256), F32)],
        compiler_params=_cparams(1),
    )(page_table, *([cache2d] * n_pages), q8, new_rows, new_win, cache_win, gates8, bc, bs, bw, b0, ov, et,
      pe, w1, w2, gk0, p64)


def _bucket_table():
    n = np.arange(MAX_DISTANCE + 1)
    max_exact = N_BUCKETS // 2
    nf = np.maximum(n, 1).astype(np.float32)
    large = max_exact + (np.log(nf / np.float32(max_exact)) / np.float32(math.log(MAX_DISTANCE / max_exact))
                         * np.float32(N_BUCKETS - max_exact)).astype(np.int32)
    return np.where(n < max_exact, n, np.minimum(large, N_BUCKETS - 1)).astype(np.int32)


def _bias_lookup(tb, rel, valid):
    idx = np.clip(rel, 0, MAX_DISTANCE)
    vals = jnp.moveaxis(tb[idx], -1, 0)
    return jnp.where(jnp.asarray(valid)[None], vals, NEG)


def _overlap(n_cmp_pad, n_cmp, n_sel_pad, n_sel):
    cs = (np.arange(n_cmp_pad) * CMP_STRIDE)[:, None]
    ss = (np.arange(n_sel_pad) * SEL_BLOCK)[None, :]
    ov = (cs < ss + SEL_BLOCK) & (cs + CMP_BLOCK > ss)
    ov &= (np.arange(n_cmp_pad) < n_cmp)[:, None] & (np.arange(n_sel_pad) < n_sel)[None, :]
    return jnp.asarray(ov.astype(np.float32), dtype=BF16)


def _stack_rows(x):
    return x.reshape(NSA_KV, NSA_GROUP * Q_BLOCK, x.shape[-1])


def _toeplitz(tbr, shift, width, max_valid):
    n = width + Q_BLOCK - 1
    u = np.arange(n)
    xs = shift - np.where(u < width, u, u - n)
    fvec = _bias_lookup(tbr, xs, (xs >= 0) & (xs <= max_valid))
    h = fvec.shape[0]
    return jnp.tile(fvec, (1, Q_BLOCK))[:, :Q_BLOCK * (n - 1)].reshape(h, Q_BLOCK, n - 1)[:, :, :width]


def _bias_descending(tb, top):
    far = jnp.broadcast_to(tb[MAX_DISTANCE][:, None], (tb.shape[1], top - MAX_DISTANCE + 1))
    return jnp.concatenate([far, tb[MAX_DISTANCE - 1:0:-1].T], axis=1)


def _prompt_tables(tb):
    tbr = (tb - tb[MAX_DISTANCE][None, :]) * LOG2E
    i = np.arange(Q_BLOCK)[:, None]

    def table(rel, valid):
        return _stack_rows(_bias_lookup(tbr, rel, valid))

    n_r = 4
    m_sel = _stack_rows(_toeplitz(tbr, Q_BLOCK * (n_r - 1), NEAR_TILE + Q_BLOCK * (n_r - 1), 1 << 30))
    tsel = jnp.stack([m_sel[:, :, Q_BLOCK * (n_r - 1 - r):Q_BLOCK * (n_r - 1 - r) + NEAR_TILE] for r in range(n_r)],
                     axis=1)
    n_v = WINDOW // Q_BLOCK + 1
    m_win = _stack_rows(_toeplitz(tbr, Q_BLOCK * (n_v - 1), WIN_TILE + Q_BLOCK * (n_v - 1), WINDOW - 1))
    twin = jnp.stack([m_win[:, :, Q_BLOCK * (n_v - 1 - v):Q_BLOCK * (n_v - 1 - v) + WIN_TILE] for v in range(n_v)],
                     axis=1)
    w = np.arange(16)[None, :] - 9
    rel = i - CMP_STRIDE * w - (CMP_BLOCK - 1)
    a = table(rel, rel >= 0)
    hi = a.astype(BF16)
    lo = (a - hi.astype(F32)).astype(BF16)
    acmp = jnp.concatenate([hi, lo], axis=-1)
    return jnp.swapaxes(tsel, 2, 3), jnp.swapaxes(twin, 2, 3), jnp.swapaxes(acmp, 1, 2)


def kernel(x_prompt, x_sample, cache_kv, cache_win, state_gla, state_conv, cache_mem, page_table, mem_prompt,
           g_mix, w_in, g_nsa_q, g_nsa_k, cmp_k_pe, cmp_k_w1, cmp_k_w2, cmp_v_pe, cmp_v_w1, cmp_v_w2,
           rel_bias, w_gla_gate, b_gla_gate, g_gla_o, g_mem, w_mem_kv, g_x_q, g_x_k,
           w_nsa_out, w_gla_out, w_x_out, w_o, g_ffn, w_up, conv_w, conv_b, w_down):
    bp, t, d = x_prompt.shape
    db = x_sample.shape[0]
    f = conv_w.shape[-1]

    offs = np.cumsum((0,) + IN_SIZES)
    segs = [w_in[:, offs[i]:offs[i + 1]] for i in range(len(IN_SIZES))]
    w_pad = jnp.concatenate([jnp.pad(s, ((0, 0), (0, pw - s.shape[1]))) for s, pw in zip(segs, PAD_SIZES)],
                            axis=1).astype(BF16)
    row = lambda v: v.reshape(1, -1).astype(F32)
    gq = row(jnp.tile(g_nsa_q, NSA_HEADS))
    gk0 = row(jnp.tile(g_nsa_k[0], NSA_KV))
    gk1 = row(jnp.tile(g_nsa_k[1], NSA_KV))
    gk2 = row(jnp.tile(g_nsa_k[2], NSA_KV))
    gxq = row(jnp.tile(g_x_q, X_HEADS))
    gxk = row(jnp.tile(g_x_k, X_HEADS))
    p64 = _block_ones(512, HEAD_DIM)
    p128 = _block_ones(512, X_DIM)
    bd2 = lambda a: jnp.concatenate([jnp.concatenate([a, jnp.zeros_like(a)], -1),
                                     jnp.concatenate([jnp.zeros_like(a), a], -1)], -2)
    bd4 = lambda a, c: jnp.concatenate([jnp.concatenate([bd2(a), jnp.zeros_like(bd2(a))], -1),
                                        jnp.concatenate([jnp.zeros_like(bd2(c)), bd2(c)], -1)], -2)
    pe = jnp.concatenate([jnp.tile(cmp_k_pe, (1, NSA_KV)), jnp.tile(cmp_v_pe, (1, NSA_KV))],
                         axis=-1).reshape(2, CMP_STRIDE, 1, 256)
    w1 = bd4(cmp_k_w1, cmp_v_w1).reshape(2, CMP_STRIDE * 256, 256).astype(BF16)
    w2 = bd4(cmp_k_w2, cmp_v_w2).astype(BF16)
    wg_pad = jnp.pad(w_gla_gate, ((0, 128 - GLA_RANK), (0, 0))).astype(BF16)
    tri = jnp.asarray(np.tril(np.ones((GLA_CHUNK, GLA_CHUNK), np.float32)), dtype=BF16)
    tb = rel_bias.astype(F32)[_bucket_table()]
    wn, wgo, wx, wo = (w.astype(BF16) for w in (w_nsa_out, w_gla_out, w_x_out, w_o))
    wup = w_up.astype(BF16)
    wdn = w_down.astype(BF16)
    ggo = row(g_gla_o)

    rows_p, win_p, _, gates, gla_in, xq, mg, ka, kw, vst, vwt, qt = _proj_in(
        x_prompt, row(g_mix), w_pad, gq, gk1, gk2, gxq, p64, p128, True)
    memkv_p = _memory_kv(mem_prompt, row(g_mem), w_mem_kv.astype(BF16), gxk, p128)
    kc, vct = _compress(rows_p, pe, w1, w2, gk0, p64)
    n_chunks = t // CMP_STRIDE
    n_sel = -(-t // SEL_BLOCK)
    tsel, twin, acmp = _prompt_tables(tb)
    ovt = _overlap(n_chunks, n_chunks - 1, 128, n_sel).T
    o_nsa = _nsa_prompt(qt, kc, vct, ka, kw, vst, vwt, gates, tsel, twin, acmp, ovt)
    s0t = jnp.zeros((bp, GLA_HEADS, GLA_DV, GLA_DK), F32)
    o_gla, st = _gla_prompt(gla_in, s0t, wg_pad, row(b_gla_gate), ggo, tri)
    o_x = _xatt(xq, memkv_p)
    m = bp * t
    x1 = _merge(o_nsa.reshape(m, 512), o_gla.reshape(m, 512), o_x.reshape(m, 512), mg.reshape(m, 3 * d),
                x_prompt.reshape(m, d), wn, wgo, wx, wo)
    y_p, tail = _ffn_seq(x1.reshape(bp, t, d), jnp.zeros((bp, 2, f), F32), row(g_ffn), wup, conv_w, row(conv_b), wdn)
    wl_p = min(WINDOW, t)
    out_rows_p = rows_p.reshape(bp, t, 4, NSA_KV, HEAD_DIM)
    out_win_p = win_p[:, t - wl_p:].reshape(bp, wl_p, 2, NSA_KV, HEAD_DIM)
    out_gla_p = jnp.swapaxes(st, 2, 3)
    out_conv_p = tail[:, 6:8]
    out_mem_p = memkv_p.reshape(bp, -1, 2, X_HEADS, X_DIM)

    n_pages = page_table.shape[1]
    page = cache_kv.shape[1]
    length = n_pages * page
    wl = cache_win.shape[1]
    rows_s, win_s, qn_s, gates_s, gla_s, xq_s, mg_s = (a[0] for a in _proj_in(
        x_sample.reshape(1, db, d), row(g_mix), w_pad, gq, gk1, gk2, gxq, p64, p128, False))

    eye = jnp.eye(NSA_KV, dtype=BF16)
    q8 = (qn_s.reshape(db, NSA_KV, NSA_GROUP, 1, HEAD_DIM) * eye[None, :, None, :, None]).reshape(db, 8, 128)
    gates8 = jnp.pad(gates_s[:, 0:24].reshape(db, 8, 3), ((0, 0), (0, 0), (0, 125)))
    n_chunks_s = length // CMP_STRIDE
    n_sel_s = -(-(length + 1) // SEL_BLOCK)
    nsp = -(-n_sel_s // 128) * 128
    cidx = np.arange(n_chunks_s)
    rel_c = length - (cidx * CMP_STRIDE + CMP_BLOCK - 1)
    bc = _bias_lookup(tb, rel_c, (rel_c >= 0) & (cidx < n_chunks_s - 1))
    bs = _bias_descending(tb, length)
    bw = jnp.where(jnp.asarray(np.arange(wl, 0, -1) < WINDOW)[None], _bias_descending(tb, wl), NEG)
    b0 = tb[0].reshape(8, 1)
    ov_s = _overlap(n_chunks_s, n_chunks_s - 1, nsp, n_sel_s).T
    assert length // SEL_BLOCK <= 128
    et = jnp.asarray((np.arange(length)[:, None] // SEL_BLOCK == np.arange(128)[None, :]).astype(np.float32),
                     dtype=BF16)
    o8, win_new = _nsa_decode(page_table, cache_kv.reshape(cache_kv.shape[0], page, 512), q8,
                              rows_s.reshape(db, 1, 512), win_s.reshape(db, 1, 256), cache_win.reshape(db, wl, 256),
                              gates8, bc, bs, bw, b0, ov_s, et, pe, w1, w2, gk0, p64)
    o8 = o8.reshape(db, NSA_KV, NSA_GROUP, NSA_KV, HEAD_DIM)
    o_nsa_s = jnp.stack([o8[:, 0, :, 0], o8[:, 1, :, 1]], axis=1).reshape(db, 512).astype(BF16)

    wgt = jnp.pad(w_gla_gate.T, ((0, 0), (0, 128 - GLA_RANK))).reshape(GLA_HEADS, GLA_DK, 128)
    o_gla_s, gla_state_s = _gla_step(
        gla_s[:, 0:256].reshape(db, GLA_HEADS, GLA_DK, 1), gla_s[:, 256:512].reshape(db, GLA_HEADS, GLA_DK, 1),
        gla_s[:, 1024:1152].reshape(db, 1, 128), gla_s[:, 512:1024].reshape(db, GLA_HEADS, 1, GLA_DV),
        gla_s[:, 1152:1664].reshape(db, GLA_HEADS, 1, GLA_DV), state_gla.astype(F32), wgt,
        b_gla_gate.reshape(GLA_HEADS, GLA_DK, 1), ggo)
    o_gla_s = o_gla_s.reshape(db, 512).astype(BF16)

    xq_pad = jnp.pad(xq_s.reshape(db, 1, 512), ((0, 0), (0, 15), (0, 0)))
    o_x_s = _xatt(xq_pad, cache_mem.reshape(db, -1, 1024))[:, 0]
    x1_s = _merge(o_nsa_s, o_gla_s, o_x_s, mg_s, x_sample.reshape(db, d), wn, wgo, wx, wo)
    y_s, g_new = _ffn_step(x1_s, state_conv[:, 0], state_conv[:, 1], row(g_ffn), wup, conv_w, row(conv_b), wdn)

    out_rows_s = rows_s.reshape(db, 1, 4, NSA_KV, HEAD_DIM)
    out_win_s = win_new.reshape(db, wl, 2, NSA_KV, HEAD_DIM)
    out_conv_s = jnp.stack([state_conv[:, 1], g_new], axis=1)
    return (y_p, y_s.reshape(db, 1, d), out_rows_p, out_win_p, out_gla_p, out_conv_p, out_mem_p,
            out_rows_s, out_win_s, gla_state_s, out_conv_s)
```

```python
import functools
import math

import numpy as np
import jax
import jax.numpy as jnp
from jax import lax
from jax.experimental import pallas as pl
from jax.experimental.pallas import tpu as pltpu

F32 = jnp.float32
BF16 = jnp.bfloat16

NSA_HEADS = 8
NSA_KV = 2
NSA_GROUP = 4
HEAD_DIM = 64
CMP_BLOCK = 32
CMP_STRIDE = 16
SEL_BLOCK = 64
SEL_TOP = 16
WINDOW = 512
Q_BLOCK = 128
GLA_HEADS = 4
GLA_DK = 64
GLA_DV = 128
GLA_RANK = 16
GLA_TAU = 16.0
GLA_CHUNK = 64
X_HEADS = 4
X_DIM = 128
N_BUCKETS = 32
MAX_DISTANCE = 128
EPS = 1e-6
LOG2E = math.log2(math.e)
NEG = -1e30
TINY = 1e-30
SEL_PENALTY = -1e9
MASKED_BELOW = -5e29
EXP_CLAMP = 80.0

IN_SIZES = (512, 768, 24, 256, 256, 512, 16, 512, 512, 3072)
PAD_SIZES = (512, 768, 128, 256, 256, 512, 128, 512, 512, 3072)
PAD_OFFS = tuple(int(v) for v in np.cumsum((0,) + PAD_SIZES))
D_IN_PAD = PAD_OFFS[-1]
GLA_IN_W = 256 + 256 + 512 + 128 + 512

VMEM_LIMIT = 56 * 1024 * 1024
FAR_TILE = 256
NEAR_TILE = 512
WIN_TILE = WINDOW + Q_BLOCK


def _cparams(n_axes):
    return pltpu.CompilerParams(dimension_semantics=("arbitrary",) * n_axes, vmem_limit_bytes=VMEM_LIMIT)


def _dot(a, b):
    return jnp.dot(a, b, preferred_element_type=F32)


def _dot_nt(a, b):
    return lax.dot_general(a, b, (((1,), (1,)), ((), ())), preferred_element_type=F32)


def _dot_tn(a, b):
    return lax.dot_general(a, b, (((0,), (0,)), ((), ())), preferred_element_type=F32)


def _split_dot(x, m):
    hi = x.astype(BF16)
    lo = (x - hi.astype(F32)).astype(BF16)
    return _dot(hi, m) + _dot(lo, m)


def _rms(x, g):
    return x * lax.rsqrt(jnp.mean(x * x, axis=-1, keepdims=True) + EPS) * g


def _group_rms(x, pmat, gsize, g):
    ss = _split_dot(x * x, pmat)
    return x * lax.rsqrt(ss * (1.0 / gsize) + EPS) * g


def _gelu(x):
    return 0.5 * x * (1.0 + jnp.tanh(math.sqrt(2.0 / math.pi) * (x + 0.044715 * (x * x * x))))


def _sigmoid(x):
    return 1.0 / (1.0 + jnp.exp(-x))


def _block_ones(n, gsize):
    i = np.arange(n) // gsize
    return jnp.asarray((i[:, None] == i[None, :]).astype(np.float32), dtype=BF16)


def _proj_kernel(x_ref, gmix_ref, w_ref, gq_ref, gk1_ref, gk2_ref, gxq_ref, p64_ref, p128_ref,
                 rows_ref, win_ref, qn_ref, gates_ref, gla_ref, xq_ref, mg_ref, *attn_refs):
    x = x_ref[0]
    h = _rms(x, gmix_ref[...]).astype(BF16)
    o = PAD_OFFS

    def seg(i):
        return _dot(h, w_ref[:, o[i]:o[i + 1]])

    p64 = p64_ref[...]
    p64s = p64_ref[0:128, 0:128]
    qn = _group_rms(seg(0), p64, HEAD_DIM, gq_ref[...]) * (HEAD_DIM ** -0.5)
    qn_ref[0] = qn.astype(BF16)
    if attn_refs:
        qt_ref = attn_refs[4]
        for u in range(x.shape[0] // Q_BLOCK):
            qt_ref[0, u] = (qn[Q_BLOCK * u:Q_BLOCK * (u + 1)] * LOG2E).T.astype(BF16)

    kv = seg(1)
    k_sel = _group_rms(kv[:, 256:384], p64s, HEAD_DIM, gk1_ref[...])
    k_win = _group_rms(kv[:, 512:640], p64s, HEAD_DIM, gk2_ref[...])
    rows_ref[0, :, 0:256] = kv[:, 0:256]
    rows_ref[0, :, 256:384] = k_sel
    rows_ref[0, :, 384:512] = kv[:, 384:512]
    win_ref[0, :, 0:128] = k_win
    win_ref[0, :, 128:256] = kv[:, 640:768]
    if attn_refs:
        ka_ref, kw_ref, vst_ref, vwt_ref, _ = attn_refs
        tm = x.shape[0]
        tpos = pl.program_id(1) * tm + lax.broadcasted_iota(jnp.int32, (tm, 128), 0)
        blk = lax.broadcasted_iota(jnp.int32, (tm, 128), 1)
        ka_ref[0, :, 0:128] = k_sel.astype(BF16)
        ka_ref[0, :, 128:256] = jnp.where(tpos // SEL_BLOCK == blk, 1.0, 0.0).astype(BF16)
        kw_ref[0] = k_win.astype(BF16)
        for u in range(tm // FAR_TILE):
            vst_ref[0, u] = kv[FAR_TILE * u:FAR_TILE * (u + 1), 384:512].T.astype(BF16)
        for u in range(tm // Q_BLOCK):
            vwt_ref[0, u] = kv[Q_BLOCK * u:Q_BLOCK * (u + 1), 640:768].T.astype(BF16)

    gates_ref[0] = _sigmoid(seg(2))
    gla_ref[0, :, 0:256] = seg(3) * (GLA_DK ** -0.5)
    gla_ref[0, :, 256:512] = seg(4)
    gla_ref[0, :, 512:1024] = seg(5)
    gla_ref[0, :, 1024:1152] = seg(6)
    gla_ref[0, :, 1152:1664] = seg(7)
    xq = _group_rms(seg(8), p128_ref[...], X_DIM, gxq_ref[...]) * (X_DIM ** -0.5)
    xq_ref[0] = xq.astype(BF16)
    mg_ref[0] = _sigmoid(seg(9)).astype(BF16)


def _proj_in(x, g_mix, w_pad, gq, gk1, gk2, gxq, p64, p128, attn_layout):
    b, t, d = x.shape
    tm = min(256, t)
    assert t % tm == 0
    widths = [512, 256, 512, 128, GLA_IN_W, 512, 3072]
    dtypes = [F32, F32, BF16, F32, F32, BF16, BF16]
    out_specs = [pl.BlockSpec((1, tm, w), lambda i, j: (i, j, 0)) for w in widths]
    out_shape = [jax.ShapeDtypeStruct((b, t, w), dt) for w, dt in zip(widths, dtypes)]
    if attn_layout:
        assert tm % Q_BLOCK == 0
        for w in (256, 128):
            out_specs.append(pl.BlockSpec((1, tm, w), lambda i, j: (i, j, 0)))
            out_shape.append(jax.ShapeDtypeStruct((b, t, w), BF16))
        assert tm % FAR_TILE == 0
        for rows, width in ((128, FAR_TILE), (128, Q_BLOCK), (512, Q_BLOCK)):
            out_specs.append(pl.BlockSpec((1, tm // width, rows, width), lambda i, j: (i, j, 0, 0)))
            out_shape.append(jax.ShapeDtypeStruct((b, t // width, rows, width), BF16))
    const = lambda shape: pl.BlockSpec(shape, lambda i, j: (0,) * len(shape))
    return pl.pallas_call(
        _proj_kernel,
        grid=(b, t // tm),
        in_specs=[pl.BlockSpec((1, tm, d), lambda i, j: (i, j, 0)),
                  const((1, d)), const((d, D_IN_PAD)), const((1, 512)), const((1, 128)), const((1, 128)),
                  const((1, 512)), const((512, 512)), const((512, 512))],
        out_specs=out_specs,
        out_shape=out_shape,
        compiler_params=_cparams(2),
    )(x, g_mix, w_pad, gq, gk1, gk2, gxq, p64, p128)


def _memkv_kernel(m_ref, g_ref, w_ref, gk_ref, p128_ref, o_ref):
    h = _rms(m_ref[0], g_ref[...]).astype(BF16)
    kv = _dot(h, w_ref[...])
    o_ref[0, :, 0:512] = _group_rms(kv[:, 0:512], p128_ref[...], X_DIM, gk_ref[...])
    o_ref[0, :, 512:1024] = kv[:, 512:1024]


def _memory_kv(mem, g_mem, w_mem, gxk, p128):
    b, m, d = mem.shape
    const = lambda shape: pl.BlockSpec(shape, lambda i: (0,) * len(shape))
    return pl.pallas_call(
        _memkv_kernel,
        grid=(b,),
        in_specs=[pl.BlockSpec((1, m, d), lambda i: (i, 0, 0)), const((1, d)), const((d, 1024)),
                  const((1, 512)), const((512, 512))],
        out_specs=pl.BlockSpec((1, m, 1024), lambda i: (i, 0, 0)),
        out_shape=jax.ShapeDtypeStruct((b, m, 1024), F32),
        compiler_params=_cparams(1),
    )(mem, g_mem, w_mem, gxk, p128)


def _compress_core(load_j, n_chunks, pe_ref, w1_ref, w2_ref, gk0_ref, p64_ref):
    xs = [load_j(j) for j in range(CMP_STRIDE)]
    halves = [_dot(jnp.concatenate([(x + pe_ref[r, j]).astype(BF16) for j, x in enumerate(xs)], axis=1), w1_ref[r])
              for r in range(2)]
    hid = halves[0] + pltpu.roll(halves[1], n_chunks - 1, 0)
    out = _dot(_gelu(hid).astype(BF16), w2_ref[...])
    row = lax.broadcasted_iota(jnp.int32, (n_chunks, 128), 0)
    live = row < n_chunks - 1
    kc = _group_rms(out[:, 0:128], p64_ref[0:128, 0:128], HEAD_DIM, gk0_ref[...])
    return jnp.where(live, kc, 0.0), jnp.where(live, out[:, 128:256], 0.0)


def _compress_kernel(rk_ref, rv_ref, pe_ref, w1_ref, w2_ref, gk0_ref, p64_ref, kc_ref, vc_ref, *, n_chunks):
    load_j = lambda j: jnp.concatenate([rk_ref[0, pl.ds(j, n_chunks, stride=CMP_STRIDE), :],
                                        rv_ref[0, pl.ds(j, n_chunks, stride=CMP_STRIDE), :]], axis=1)
    kc, vc = _compress_core(load_j, n_chunks, pe_ref, w1_ref, w2_ref, gk0_ref, p64_ref)
    kc_ref[0] = kc.astype(BF16)
    vc_ref[0] = vc.T.astype(BF16)


def _compress(rows, pe, w1, w2, gk0, p64):
    b, t, _ = rows.shape
    n_chunks = t // CMP_STRIDE
    const = lambda shape: pl.BlockSpec(shape, lambda i: (0,) * len(shape))
    return pl.pallas_call(
        functools.partial(_compress_kernel, n_chunks=n_chunks),
        grid=(b,),
        in_specs=[pl.BlockSpec((1, t, 128), lambda i: (i, 0, 0)), pl.BlockSpec((1, t, 128), lambda i: (i, 0, 1)),
                  const((2, CMP_STRIDE, 1, 256)), const((2, CMP_STRIDE * 256, 256)), const((256, 256)),
                  const((1, 128)), const((512, 512))],
        out_specs=[pl.BlockSpec((1, n_chunks, 128), lambda i: (i, 0, 0)),
                   pl.BlockSpec((1, 128, n_chunks), lambda i: (i, 0, 0))],
        out_shape=[jax.ShapeDtypeStruct((b, n_chunks, 128), BF16), jax.ShapeDtypeStruct((b, 128, n_chunks), BF16)],
        compiler_params=_cparams(1),
    )(rows, rows, pe, w1, w2, gk0, p64)


def _masked_softmax(s, axis, exp_fn=jnp.exp):
    m = jnp.maximum(jnp.max(s, axis=axis, keepdims=True), MASKED_BELOW)
    p = exp_fn(s - m)
    return p / jnp.maximum(jnp.sum(p, axis=axis, keepdims=True), TINY)


def _select_blocks(score, top, axis):
    pos = lax.broadcasted_iota(jnp.int32, score.shape, axis).astype(F32)
    sel = jnp.zeros(score.shape, jnp.bool_)
    for _ in range(top):
        mx = jnp.max(score, axis=axis, keepdims=True)
        idx = jnp.min(jnp.where(score == mx, pos, 1e9), axis=axis, keepdims=True)
        hit = pos == idx
        sel = jnp.logical_or(sel, hit)
        score = jnp.where(hit, -3e38, score)
    return sel


def _select_with_forced(imp, forced, top, axis):
    n_forced = 3
    assert top > n_forced
    return jnp.logical_or(forced, _select_blocks(jnp.where(forced, -3e38, imp), top - n_forced, axis))


def _values_times_probs(vt_tiles, p):
    out = None
    start = 0
    for vt in vt_tiles:
        part = _dot(vt, p[start:start + vt.shape[1]])
        start += vt.shape[1]
        out = part if out is None else out + part
    return out


def _softmax_stats_t(s_ref, bias, m_ref, l_ref):
    a_parts, p_parts = [], []
    for g in range(s_ref.shape[1] // Q_BLOCK):
        cs = slice(Q_BLOCK * g, Q_BLOCK * (g + 1))
        sg = s_ref[:, cs]
        if bias is not None:
            sg = sg + bias(cs)
        m_old = m_ref[:, cs]
        m_new = jnp.maximum(m_old, jnp.max(sg, axis=0, keepdims=True))
        alpha = jnp.exp2(m_old - m_new)
        p = jnp.exp2(sg - m_new)
        m_ref[:, cs] = m_new
        l_ref[:, cs] = alpha * l_ref[:, cs] + jnp.sum(p, axis=0, keepdims=True)
        a_parts.append(alpha)
        p_parts.append(p.astype(BF16))
    return jnp.concatenate(a_parts, axis=1), jnp.concatenate(p_parts, axis=1)


def _acc_update_t(acc_ref, alpha, vt_tiles, p):
    acc_ref[...] = alpha * acc_ref[...] + _values_times_probs(vt_tiles, p)


def _attn_kernel(qt_ref, kc_ref, vct_ref, ka_ref, vst_ref, kw_ref, vwt_ref, gt_ref, tsel_ref, twin_ref, acmp_ref,
                 ovt_ref, o_ref, sa_scr, sb_scr, m_scr, l_scr, acc_scr, pend_a, pend_p, *, top):
    k = pl.program_id(1)
    n = pl.program_id(2)
    cols = NSA_GROUP * Q_BLOCK
    vrow = pl.multiple_of(k * HEAD_DIM, HEAD_DIM)
    zero = jnp.zeros((HEAD_DIM, Q_BLOCK), BF16)
    parts = []
    for g in range(NSA_GROUP):
        piece = qt_ref[0, 0, HEAD_DIM * g:HEAD_DIM * (g + 1), :]
        parts.append(jnp.where(k == 0, jnp.concatenate([piece, zero], axis=0),
                               jnp.concatenate([zero, piece], axis=0)))
    qt = jnp.concatenate(parts, axis=1)

    nc = kc_ref.shape[1]
    c32 = lax.broadcasted_iota(jnp.int32, (nc, 32), 0)
    r32 = lax.broadcasted_iota(jnp.int32, (nc, 32), 1)
    onehot = jnp.where((c32 - 8 * n + 9) == (r32 & 15), 1.0, 0.0).astype(BF16)
    band = _dot(onehot, acmp_ref[0])
    cp = lax.broadcasted_iota(jnp.int32, (nc, cols), 0) - 8 * n + 9
    s_c = _dot(kc_ref[0], qt) + jnp.where(cp < 0, 0.0, jnp.where(cp > 15, NEG, band))
    p_c = _masked_softmax(s_c, 0, jnp.exp2).astype(BF16)
    o_c = _dot(vct_ref[0, pl.ds(vrow, HEAD_DIM), :], p_c)
    imp4 = _dot(ovt_ref[...], p_c)
    imp = imp4[:, 0:128] + imp4[:, 128:256] + imp4[:, 256:384] + imp4[:, 384:512]

    def value_tiles(ref, first, count):
        return [ref[0, first + d, pl.ds(vrow, HEAD_DIM), :] for d in range(count)]

    wt = jnp.maximum(n - WINDOW // Q_BLOCK, 0)
    s_w = _dot(kw_ref[0, pl.ds(pl.multiple_of(wt * Q_BLOCK, Q_BLOCK), WIN_TILE), :], qt) + twin_ref[0, 0]
    p_w = jnp.exp2(s_w - jnp.max(s_w, axis=0, keepdims=True))
    o_w = (_values_times_probs(value_tiles(vwt_ref, wt, WIN_TILE // Q_BLOCK), p_w.astype(BF16))
           / jnp.sum(p_w, axis=0, keepdims=True))

    jj = lax.broadcasted_iota(jnp.int32, (128, Q_BLOCK), 0)
    tpos = n * Q_BLOCK + lax.broadcasted_iota(jnp.int32, (128, Q_BLOCK), 1)
    tblk = tpos // SEL_BLOCK
    forced = (jj == 0) | (jj == tblk) | (jj == tblk - 1)
    sel_t = _select_with_forced(jnp.where(jj * SEL_BLOCK <= tpos, imp, NEG), forced, top, 0)
    pen_t = jnp.where(sel_t, 0.0, SEL_PENALTY).astype(BF16)
    rhs = jnp.concatenate([qt, jnp.concatenate([pen_t] * NSA_GROUP, axis=1)], axis=0)

    n_far = jnp.maximum(n - 2, 0) // 2
    r_near = n - 2 * n_far

    def issue_scores(tile, s_ref):
        s_ref[...] = _dot(ka_ref[0, pl.ds(pl.multiple_of(tile * FAR_TILE, FAR_TILE), FAR_TILE), :], rhs)

    def stats(s_ref, half_idx=None):
        bias = None
        if half_idx is not None:
            bias = lambda cs: tsel_ref[0, r_near, FAR_TILE * half_idx:FAR_TILE * (half_idx + 1), cs]
        return _softmax_stats_t(s_ref, bias, m_scr, l_scr)

    def accumulate(tile, alpha, p):
        _acc_update_t(acc_scr, alpha, value_tiles(vst_ref, tile, 1), p)

    m_scr[...] = jnp.full((1, cols), NEG, F32)
    l_scr[...] = jnp.zeros((1, cols), F32)
    acc_scr[...] = jnp.zeros((HEAD_DIM, cols), F32)
    pend_a[...] = jnp.ones((1, cols), F32)
    pend_p[...] = jnp.zeros((FAR_TILE, cols), BF16)
    odd = n_far % 2

    @pl.when(odd == 1)
    def _():
        issue_scores(0, sa_scr)
        accumulate(0, *stats(sa_scr))

    issue_scores(odd, sa_scr)

    @pl.loop(0, n_far // 2)
    def _(j):
        t0 = odd + 2 * j
        issue_scores(t0 + 1, sb_scr)
        alpha_a, p_a = stats(sa_scr)
        accumulate(jnp.maximum(t0 - 1, 0), pend_a[...], pend_p[...])
        issue_scores(t0 + 2, sa_scr)
        alpha_b, p_b = stats(sb_scr)
        accumulate(t0, alpha_a, p_a)
        pend_a[...] = alpha_b
        pend_p[...] = p_b

    accumulate(jnp.maximum(n_far - 1, 0), pend_a[...], pend_p[...])
    issue_scores(n_far + 1, sb_scr)
    accumulate(n_far, *stats(sa_scr, 0))
    accumulate(n_far + 1, *stats(sb_scr, 1))
    o_s = acc_scr[...] / l_scr[...]

    gtt = gt_ref[0].T

    def gate_row(branch):
        rows = [jnp.where(k == 0, gtt[3 * g + branch:3 * g + branch + 1],
                          gtt[12 + 3 * g + branch:12 + 3 * g + branch + 1]) for g in range(NSA_GROUP)]
        return jnp.concatenate(rows, axis=1)

    o_t = gate_row(0) * o_c + gate_row(1) * o_s + gate_row(2) * o_w
    left = jnp.concatenate([o_t[:, 0:128], o_t[:, 128:256]], axis=0).T
    right = jnp.concatenate([o_t[:, 256:384], o_t[:, 384:512]], axis=0).T
    o_ref[0] = jnp.concatenate([left, right], axis=1).astype(BF16)


def _nsa_prompt(qt, kc, vct, ka, kw, vst, vwt, gates, tsel, twin, acmp, ovt):
    b, t, _ = ka.shape
    nqb = t // Q_BLOCK
    nc = kc.shape[1]
    top = min(SEL_TOP, -(-t // SEL_BLOCK))
    assert t >= WIN_TILE and t % FAR_TILE == 0 and t // SEL_BLOCK <= 128
    n_win = WINDOW // Q_BLOCK
    per_b = lambda shape: pl.BlockSpec((1,) + shape, lambda i, k, n: (i,) + (0,) * len(shape))
    return pl.pallas_call(
        functools.partial(_attn_kernel, top=top),
        grid=(b, NSA_KV, nqb),
        in_specs=[pl.BlockSpec((1, 1, NSA_GROUP * HEAD_DIM, Q_BLOCK), lambda i, k, n: (i, n, k, 0)),
                  per_b((nc, 128)), per_b((128, nc)),
                  per_b((t, 256)), per_b((t // FAR_TILE, 128, FAR_TILE)), per_b((t, 128)), per_b((nqb, 128, Q_BLOCK)),
                  pl.BlockSpec((1, Q_BLOCK, 128), lambda i, k, n: (i, n, 0)),
                  pl.BlockSpec((1, 4, NEAR_TILE, 512), lambda i, k, n: (k, 0, 0, 0)),
                  pl.BlockSpec((1, 1, WIN_TILE, 512), lambda i, k, n: (k, jnp.minimum(n, n_win), 0, 0)),
                  pl.BlockSpec((1, 32, 512), lambda i, k, n: (k, 0, 0)),
                  pl.BlockSpec((128, nc), lambda i, k, n: (0, 0))],
        out_specs=pl.BlockSpec((1, Q_BLOCK, 256), lambda i, k, n: (i, n, k)),
        out_shape=jax.ShapeDtypeStruct((b, t, 512), BF16),
        scratch_shapes=[pltpu.VMEM((FAR_TILE, NSA_GROUP * Q_BLOCK), F32), pltpu.VMEM((FAR_TILE, NSA_GROUP * Q_BLOCK), F32),
                        pltpu.VMEM((1, NSA_GROUP * Q_BLOCK), F32), pltpu.VMEM((1, NSA_GROUP * Q_BLOCK), F32),
                        pltpu.VMEM((HEAD_DIM, NSA_GROUP * Q_BLOCK), F32),
                        pltpu.VMEM((1, NSA_GROUP * Q_BLOCK), F32), pltpu.VMEM((FAR_TILE, NSA_GROUP * Q_BLOCK), BF16)],
        compiler_params=_cparams(3),
    )(qt, kc, vct, ka, vst, kw, vwt, gates, tsel, twin, acmp, ovt)


def _log_sigmoid(z):
    return jnp.minimum(z, 0.0) - jnp.log1p(jnp.exp(-jnp.abs(z)))


def _gla_kernel(x_ref, s0_ref, wg_ref, bg_ref, ggo_ref, tri_ref, o_ref, st_ref, s_scr, *, n_chunks):
    @pl.when(pl.program_id(0) == 0)
    def _():
        s_scr[...] = s0_ref[...]

    c_len = GLA_CHUNK
    tri = tri_ref[...]
    ti = lax.broadcasted_iota(jnp.int32, (c_len, c_len), 0)
    si = lax.broadcasted_iota(jnp.int32, (c_len, c_len), 1)
    causal = si <= ti
    for c, bi in [(c, bi) for c in range(n_chunks) for bi in range(x_ref.shape[0])]:
        rs = slice(c_len * c, c_len * (c + 1))
        q = x_ref[bi, rs, 0:256]
        kk = x_ref[bi, rs, 256:512]
        v = x_ref[bi, rs, 512:1024]
        lr = x_ref[bi, rs, 1024:1152]
        r = x_ref[bi, rs, 1152:1664]
        la = _log_sigmoid(_dot(lr.astype(BF16), wg_ref[...]) + bg_ref[...]) * (1.0 / GLA_TAU)
        a1 = la.astype(BF16)
        r1 = la - a1.astype(F32)
        a2 = r1.astype(BF16)
        a3 = (r1 - a2.astype(F32)).astype(BF16)
        cb = _dot(tri, a1) + _dot(tri, a2) + _dot(tri, a3)
        last = cb[c_len - 1:c_len, :]
        mid = cb[c_len // 2:c_len // 2 + 1, :]
        qe = (q * jnp.exp(cb)).astype(BF16)
        qa = (q * jnp.exp(jnp.minimum(cb - mid, EXP_CLAMP))).astype(BF16)
        kb = (kk * jnp.exp(jnp.minimum(mid - cb, EXP_CLAMP))).astype(BF16)
        ke = (kk * jnp.exp(last - cb)).astype(BF16)
        dec = jnp.exp(last)
        for h in range(GLA_HEADS):
            ks = slice(GLA_DK * h, GLA_DK * (h + 1))
            vs = slice(GLA_DV * h, GLA_DV * (h + 1))
            att = jnp.where(causal, _dot_nt(qa[:, ks], kb[:, ks]), 0.0)
            vh = v[:, vs].astype(BF16)
            st = s_scr[bi, h]
            o = _dot(att.astype(BF16), vh) + _dot_nt(qe[:, ks], st.astype(BF16))
            s_scr[bi, h] = st * dec[:, ks] + _dot_tn(vh, ke[:, ks])
            on = _rms(o, ggo_ref[...])
            rh = r[:, vs]
            o_ref[bi, rs, vs] = (on * (rh * _sigmoid(rh))).astype(BF16)
    st_ref[...] = s_scr[...]


def _gla_prompt(gla_in, s0t, wg, bg, ggo, tri):
    b, t, w = gla_in.shape
    ct = min(256, t)
    assert t % ct == 0 and ct % GLA_CHUNK == 0
    const = lambda shape: pl.BlockSpec(shape, lambda j: (0,) * len(shape))
    state_shape = (b, GLA_HEADS, GLA_DV, GLA_DK)
    return pl.pallas_call(
        functools.partial(_gla_kernel, n_chunks=ct // GLA_CHUNK),
        grid=(t // ct,),
        in_specs=[pl.BlockSpec((b, ct, w), lambda j: (0, j, 0)), const(state_shape),
                  const((128, 256)), const((1, 256)), const((1, 128)), const((GLA_CHUNK, GLA_CHUNK))],
        out_specs=[pl.BlockSpec((b, ct, 512), lambda j: (0, j, 0)), const(state_shape)],
        out_shape=[jax.ShapeDtypeStruct((b, t, 512), BF16), jax.ShapeDtypeStruct(state_shape, F32)],
        scratch_shapes=[pltpu.VMEM(state_shape, F32)],
        compiler_params=_cparams(1),
    )(gla_in, s0t, wg, bg, ggo, tri)


def _gla_step_kernel(q_ref, k_ref, lr_ref, v_ref, r_ref, s_ref, wgt_ref, bgt_ref, ggo_ref, o_ref, sn_ref):
    lr = lr_ref[0]
    for h in range(GLA_HEADS):
        z = jnp.sum(wgt_ref[h] * lr, axis=-1, keepdims=True) + bgt_ref[h]
        a = jnp.exp(_log_sigmoid(z) * (1.0 / GLA_TAU))
        s0 = s_ref[0, h]
        kh = k_ref[0, h]
        qh = q_ref[0, h]
        vh = v_ref[0, h]
        sn_ref[0, h] = a * s0 + kh * vh
        o = jnp.sum((qh * a) * s0, axis=0, keepdims=True) + jnp.sum(qh * kh, axis=0, keepdims=True) * vh
        on = _rms(o, ggo_ref[...])
        rh = r_ref[0, h]
        o_ref[0, h] = on * (rh * _sigmoid(rh))


def _gla_step(q_col, k_col, lr, v_row, r_row, s0, wgt, bgt, ggo):
    b = q_col.shape[0]
    const = lambda shape: pl.BlockSpec(shape, lambda i: (0,) * len(shape))
    per_b = lambda shape: pl.BlockSpec((1,) + shape, lambda i: (i,) + (0,) * len(shape))
    return pl.pallas_call(
        _gla_step_kernel,
        grid=(b,),
        in_specs=[per_b((GLA_HEADS, GLA_DK, 1)), per_b((GLA_HEADS, GLA_DK, 1)), per_b((1, 128)),
                  per_b((GLA_HEADS, 1, GLA_DV)), per_b((GLA_HEADS, 1, GLA_DV)), per_b((GLA_HEADS, GLA_DK, GLA_DV)),
                  const((GLA_HEADS, GLA_DK, 128)), const((GLA_HEADS, GLA_DK, 1)), const((1, 128))],
        out_specs=[per_b((GLA_HEADS, 1, GLA_DV)), per_b((GLA_HEADS, GLA_DK, GLA_DV))],
        out_shape=[jax.ShapeDtypeStruct((b, GLA_HEADS, 1, GLA_DV), F32),
                   jax.ShapeDtypeStruct((b, GLA_HEADS, GLA_DK, GLA_DV), F32)],
        compiler_params=_cparams(1),
    )(q_col, k_col, lr, v_row, r_row, s0, wgt, bgt, ggo)


def _xatt_kernel(xq_ref, mem_ref, o_ref):
    for h in range(X_HEADS):
        ls = slice(X_DIM * h, X_DIM * (h + 1))
        kh = mem_ref[0, :, ls].astype(BF16)
        vh = mem_ref[0, :, 512 + X_DIM * h:512 + X_DIM * (h + 1)].astype(BF16)
        s = _dot_nt(xq_ref[0, :, ls], kh)
        p = jnp.exp(s - jnp.max(s, axis=-1, keepdims=True))
        p = p / jnp.sum(p, axis=-1, keepdims=True)
        o_ref[0, :, ls] = _dot(p.astype(BF16), vh).astype(BF16)


def _xatt(xq, memkv):
    b, t, _ = xq.shape
    m = memkv.shape[1]
    tq = min(512, t)
    assert t % tq == 0
    return pl.pallas_call(
        _xatt_kernel,
        grid=(b, t // tq),
        in_specs=[pl.BlockSpec((1, tq, 512), lambda i, j: (i, j, 0)),
                  pl.BlockSpec((1, m, 1024), lambda i, j: (i, 0, 0))],
        out_specs=pl.BlockSpec((1, tq, 512), lambda i, j: (i, j, 0)),
        out_shape=jax.ShapeDtypeStruct((b, t, 512), BF16),
        compiler_params=_cparams(2),
    )(xq, memkv)


def _merge_kernel(on_ref, og_ref, ox_ref, mg_ref, x_ref, wn_ref, wg_ref, wx_ref, wo_ref, x1_ref):
    d = x_ref.shape[-1]
    merged = (mg_ref[:, 0:d].astype(F32) * _dot(on_ref[...], wn_ref[...])
              + mg_ref[:, d:2 * d].astype(F32) * _dot(og_ref[...], wg_ref[...])
              + mg_ref[:, 2 * d:3 * d].astype(F32) * _dot(ox_ref[...], wx_ref[...]))
    x1_ref[...] = x_ref[...] + _dot(merged.astype(BF16), wo_ref[...])


def _merge(o_nsa, o_gla, o_x, mg, x, wn, wg, wx, wo):
    m, d = x.shape
    tm = min(512, m)
    assert m % tm == 0
    row = lambda w: pl.BlockSpec((tm, w), lambda i: (i, 0))
    const = lambda shape: pl.BlockSpec(shape, lambda i: (0,) * len(shape))
    return pl.pallas_call(
        _merge_kernel,
        grid=(m // tm,),
        in_specs=[row(512), row(512), row(512), row(3 * d), row(d),
                  const((512, d)), const((512, d)), const((512, d)), const((d, d))],
        out_specs=row(d),
        out_shape=jax.ShapeDtypeStruct((m, d), F32),
        compiler_params=_cparams(1),
    )(o_nsa, o_gla, o_x, mg, x, wn, wg, wx, wo)


def _ffn_seq_kernel(x_ref, past_ref, g_ref, wup_ref, cw_ref, cb_ref, wdn_ref, y_ref, tail_ref, carry_ref):
    f = cw_ref.shape[-1]
    tm = x_ref.shape[1]

    @pl.when(pl.program_id(1) == 0)
    def _():
        carry_ref[...] = jnp.zeros(carry_ref.shape, F32)
        carry_ref[6:8, :] = past_ref[0]

    x1 = x_ref[0]
    ug = _dot(_rms(x1, g_ref[...]).astype(BF16), wup_ref[...])
    u = ug[:, 0:f]
    g = ug[:, f:2 * f]
    row = lax.broadcasted_iota(jnp.int32, (tm, f), 0)
    p1 = carry_ref[7:8, :]
    p2 = carry_ref[6:7, :]
    gm1 = jnp.where(row == 0, p1, pltpu.roll(g, 1, 0))
    gm2 = jnp.where(row == 0, p2, jnp.where(row == 1, p1, pltpu.roll(g, 2, 0)))
    gc = cb_ref[...] + cw_ref[0:1, :] * gm2 + cw_ref[1:2, :] * gm1 + cw_ref[2:3, :] * g
    y_ref[0] = x1 + _dot((_gelu(gc) * u).astype(BF16), wdn_ref[...])
    carry_ref[...] = g[tm - 8:tm, :]
    tail_ref[0] = g[tm - 8:tm, :]


def _ffn_seq(x1, conv_past, g_ffn, w_up, conv_w, conv_b, w_down):
    b, t, d = x1.shape
    f = conv_w.shape[-1]
    tm = min(256, t)
    assert t % tm == 0 and tm >= 8
    const = lambda shape: pl.BlockSpec(shape, lambda i, j: (0,) * len(shape))
    return pl.pallas_call(
        _ffn_seq_kernel,
        grid=(b, t // tm),
        in_specs=[pl.BlockSpec((1, tm, d), lambda i, j: (i, j, 0)),
                  pl.BlockSpec((1, 2, f), lambda i, j: (i, 0, 0)),
                  const((1, d)), const((d, 2 * f)), const((3, f)), const((1, f)), const((f, d))],
        out_specs=[pl.BlockSpec((1, tm, d), lambda i, j: (i, j, 0)),
                   pl.BlockSpec((1, 8, f), lambda i, j: (i, 0, 0))],
        out_shape=[jax.ShapeDtypeStruct((b, t, d), F32), jax.ShapeDtypeStruct((b, 8, f), F32)],
        scratch_shapes=[pltpu.VMEM((8, f), F32)],
        compiler_params=_cparams(2),
    )(x1, conv_past, g_ffn, w_up, conv_w, conv_b, w_down)


def _ffn_step_kernel(x_ref, p0_ref, p1_ref, g_ref, wup_ref, cw_ref, cb_ref, wdn_ref, y_ref, gnew_ref):
    f = cw_ref.shape[-1]
    x1 = x_ref[...]
    ug = _dot(_rms(x1, g_ref[...]).astype(BF16), wup_ref[...])
    u = ug[:, 0:f]
    g = ug[:, f:2 * f]
    gc = cb_ref[...] + cw_ref[0:1, :] * p0_ref[...] + cw_ref[1:2, :] * p1_ref[...] + cw_ref[2:3, :] * g
    y_ref[...] = x1 + _dot((_gelu(gc) * u).astype(BF16), wdn_ref[...])
    gnew_ref[...] = g


def _ffn_step(x1, p0, p1, g_ffn, w_up, conv_w, conv_b, w_down):
    m, d = x1.shape
    f = conv_w.shape[-1]
    full = lambda shape: pl.BlockSpec(shape, lambda i: (0,) * len(shape))
    return pl.pallas_call(
        _ffn_step_kernel,
        grid=(1,),
        in_specs=[full((m, d)), full((m, f)), full((m, f)), full((1, d)), full((d, 2 * f)), full((3, f)),
                  full((1, f)), full((f, d))],
        out_specs=[full((m, d)), full((m, f))],
        out_shape=[jax.ShapeDtypeStruct((m, d), F32), jax.ShapeDtypeStruct((m, f), F32)],
        compiler_params=_cparams(1),
    )(x1, p0, p1, g_ffn, w_up, conv_w, conv_b, w_down)


def _decode_kernel(pt_ref, *refs, top, n_sel, n_pages):
    del pt_ref
    page_refs = refs[:n_pages]
    (q8_ref, new_ref, neww_ref, cwin_ref, gt_ref, bc_ref, bs_ref, bw_ref, b0_ref, ovt_ref, et_ref, pe_ref, w1_ref,
     w2_ref, gk0_ref, p64_ref, o_ref, wout_ref, xs_ref) = refs[n_pages:]
    page = page_refs[0].shape[1]
    length = n_pages * page
    n_chunks = length // CMP_STRIDE
    for u, pg in enumerate(page_refs):
        for c in range(2):
            xs_ref[c, page * u:page * (u + 1), :] = pg[0, :, 128 * c:128 * (c + 1)]

    def load_j(j):
        return jnp.concatenate([xs_ref[c, pl.ds(j, n_chunks, stride=CMP_STRIDE), :] for c in range(2)], axis=1)

    q8 = q8_ref[0]
    q8f = q8.astype(F32)
    rowk = lax.broadcasted_iota(jnp.int32, (8, 128), 0) // NSA_GROUP
    lane_half = lax.broadcasted_iota(jnp.int32, (8, 128), 1) // HEAD_DIM

    def half_mask(x):
        return jnp.where(rowk == lane_half, x, 0.0)

    kc, vc = _compress_core(load_j, n_chunks, pe_ref, w1_ref, w2_ref, gk0_ref, p64_ref)
    kc = kc.astype(BF16)
    vc = vc.astype(BF16)

    p_c = _masked_softmax(_dot_nt(q8, kc) + bc_ref[...], 1).astype(BF16)
    o_c = half_mask(_dot(p_c, vc))
    pf = p_c.astype(F32)
    p2 = jnp.concatenate([jnp.sum(pf[0:4], axis=0, keepdims=True), jnp.sum(pf[4:8], axis=0, keepdims=True),
                          jnp.zeros((126, n_chunks), F32)], axis=0)
    p2_hi = p2.astype(BF16)
    p2_lo = (p2 - p2_hi.astype(F32)).astype(BF16)
    imp_t = _dot_nt(ovt_ref[...], p2_hi) + _dot_nt(ovt_ref[...], p2_lo)
    nsp = imp_t.shape[0]
    jj = lax.broadcasted_iota(jnp.int32, (nsp, 128), 0)
    tblk = length // SEL_BLOCK
    forced = (jj == 0) | (jj == tblk) | (jj == tblk - 1)
    sel_t = _select_with_forced(jnp.where(jj < n_sel, imp_t, -2e38), forced, top, 0)
    pen = jnp.where(sel_t, 0.0, SEL_PENALTY).T
    pen8 = jnp.concatenate([jnp.broadcast_to(pen[0:1], (4, nsp)), jnp.broadcast_to(pen[1:2], (4, nsp))], axis=0)

    new = new_ref[0]
    neww = neww_ref[0]
    b0 = b0_ref[...]

    def attend(s_parts, s_new, v_parts, v_new):
        m = s_new
        for s in s_parts:
            m = jnp.maximum(m, jnp.max(s, axis=-1, keepdims=True))
        pn = jnp.exp(s_new - m)
        l = pn
        o = pn.astype(BF16).astype(F32) * v_new.astype(BF16).astype(F32)
        for s, v in zip(s_parts, v_parts):
            pp = jnp.exp(s - m)
            l = l + jnp.sum(pp, axis=-1, keepdims=True)
            o = o + _dot(pp.astype(BF16), v)
        return half_mask(o / l)

    def new_score(k_new):
        return jnp.sum(q8f * k_new.astype(BF16).astype(F32), axis=-1, keepdims=True) + b0

    k_past = jnp.concatenate([pg[0, :, 256:384] for pg in page_refs], axis=0).astype(BF16)
    v_past = jnp.concatenate([pg[0, :, 384:512] for pg in page_refs], axis=0).astype(BF16)
    s_past = _dot_nt(q8, k_past) + _dot_nt(pen8[:, 0:128].astype(BF16), et_ref[...]) + bs_ref[...]
    lane = lax.broadcasted_iota(jnp.int32, (8, nsp), 1)
    pen_new = jnp.sum(jnp.where(lane == tblk, pen8, 0.0), axis=-1, keepdims=True)
    o_s = attend([s_past], new_score(new[:, 256:384]) + pen_new, [v_past], new[:, 384:512])

    cw = cwin_ref[0]
    s_w = _dot_nt(q8, cw[:, 0:128].astype(BF16)) + bw_ref[...]
    o_w = attend([s_w], new_score(neww[:, 0:128]), [cw[:, 128:256].astype(BF16)], neww[:, 128:256])

    gt = gt_ref[0]
    o_ref[0] = gt[:, 0:1] * o_c + gt[:, 1:2] * o_s + gt[:, 2:3] * o_w

    wl = cw.shape[0]
    wrow = lax.broadcasted_iota(jnp.int32, cw.shape, 0)
    wout_ref[0] = jnp.where(wrow == wl - 1, neww, pltpu.roll(cw, wl - 1, 0))


def _nsa_decode(page_table, cache2d, q8, new_rows, new_win, cache_win, gates8, bc, bs, bw, b0, ov, et,
                pe, w1, w2, gk0, p64):
    db, n_pages = page_table.shape
    page = cache2d.shape[1]
    length = n_pages * page
    n_chunks = length // CMP_STRIDE
    n_sel = -(-(length + 1) // SEL_BLOCK)
    top = min(SEL_TOP, n_sel)
    wl = cache_win.shape[1]
    nsp = ov.shape[0]
    const = lambda shape: pl.BlockSpec(shape, lambda i, pt: (0,) * len(shape))
    per_b = lambda shape: pl.BlockSpec((1,) + shape, lambda i, pt: (i,) + (0,) * len(shape))
    page_spec = lambda u: pl.BlockSpec((1, page, 512), lambda i, pt: (pt[i, u], 0, 0))
    grid_spec = pltpu.PrefetchScalarGridSpec(
        num_scalar_prefetch=1,
        grid=(db,),
        in_specs=[page_spec(u) for u in range(n_pages)] + [
                  per_b((8, 128)), per_b((1, 512)), per_b((1, 256)), per_b((wl, 256)), per_b((8, 128)),
                  const((8, n_chunks)), const((8, length)), const((8, wl)), const((8, 1)),
                  const((nsp, n_chunks)), const((length, 128)),
                  const((2, CMP_STRIDE, 1, 256)), const((2, CMP_STRIDE * 256, 256)), const((256, 256)),
                  const((1, 128)), const((512, 512))],
        out_specs=[per_b((8, 128)), per_b((wl, 256))],
        scratch_shapes=[pltpu.VMEM((2, length, 128), F32)],
    )
    return pl.pallas_call(
        functools.partial(_decode_kernel, top=top, n_sel=n_sel, n_pages=n_pages),
        grid_spec=grid_spec,
        out_shape=[jax.ShapeDtypeStruct((db, 8, 128), F32), jax.ShapeDtypeStruct((db, wl, 256), F32)],
        compiler_params=_cparams(1),
    )(page_table, *([cache2d] * n_pages), q8, new_rows, new_win, cache_win, gates8, bc, bs, bw, b0, ov, et,
      pe, w1, w2, gk0, p64)


def _bucket_table():
    n = np.arange(MAX_DISTANCE + 1)
    max_exact = N_BUCKETS // 2
    nf = np.maximum(n, 1).astype(np.float32)
    large = max_exact + (np.log(nf / np.float32(max_exact)) / np.float32(math.log(MAX_DISTANCE / max_exact))
                         * np.float32(N_BUCKETS - max_exact)).astype(np.int32)
    return np.where(n < max_exact, n, np.minimum(large, N_BUCKETS - 1)).astype(np.int32)


def _bias_lookup(tb, rel, valid):
    idx = np.clip(rel, 0, MAX_DISTANCE)
    vals = jnp.moveaxis(tb[idx], -1, 0)
    return jnp.where(jnp.asarray(valid)[None], vals, NEG)


def _overlap(n_cmp_pad, n_cmp, n_sel_pad, n_sel):
    cs = (np.arange(n_cmp_pad) * CMP_STRIDE)[:, None]
    ss = (np.arange(n_sel_pad) * SEL_BLOCK)[None, :]
    ov = (cs < ss + SEL_BLOCK) & (cs + CMP_BLOCK > ss)
    ov &= (np.arange(n_cmp_pad) < n_cmp)[:, None] & (np.arange(n_sel_pad) < n_sel)[None, :]
    return jnp.asarray(ov.astype(np.float32), dtype=BF16)


def _stack_rows(x):
    return x.reshape(NSA_KV, NSA_GROUP * Q_BLOCK, x.shape[-1])


def _toeplitz(tbr, shift, width, max_valid):
    n = width + Q_BLOCK - 1
    u = np.arange(n)
    xs = shift - np.where(u < width, u, u - n)
    fvec = _bias_lookup(tbr, xs, (xs >= 0) & (xs <= max_valid))
    h = fvec.shape[0]
    return jnp.tile(fvec, (1, Q_BLOCK))[:, :Q_BLOCK * (n - 1)].reshape(h, Q_BLOCK, n - 1)[:, :, :width]


def _bias_descending(tb, top):
    far = jnp.broadcast_to(tb[MAX_DISTANCE][:, None], (tb.shape[1], top - MAX_DISTANCE + 1))
    return jnp.concatenate([far, tb[MAX_DISTANCE - 1:0:-1].T], axis=1)


def _prompt_tables(tb):
    tbr = (tb - tb[MAX_DISTANCE][None, :]) * LOG2E
    i = np.arange(Q_BLOCK)[:, None]

    def table(rel, valid):
        return _stack_rows(_bias_lookup(tbr, rel, valid))

    n_r = 4
    m_sel = _stack_rows(_toeplitz(tbr, Q_BLOCK * (n_r - 1), NEAR_TILE + Q_BLOCK * (n_r - 1), 1 << 30))
    tsel = jnp.stack([m_sel[:, :, Q_BLOCK * (n_r - 1 - r):Q_BLOCK * (n_r - 1 - r) + NEAR_TILE] for r in range(n_r)],
                     axis=1)
    n_v = WINDOW // Q_BLOCK + 1
    m_win = _stack_rows(_toeplitz(tbr, Q_BLOCK * (n_v - 1), WIN_TILE + Q_BLOCK * (n_v - 1), WINDOW - 1))
    twin = jnp.stack([m_win[:, :, Q_BLOCK * (n_v - 1 - v):Q_BLOCK * (n_v - 1 - v) + WIN_TILE] for v in range(n_v)],
                     axis=1)
    w = np.arange(16)[None, :] - 9
    rel = i - CMP_STRIDE * w - (CMP_BLOCK - 1)
    a = table(rel, rel >= 0)
    hi = a.astype(BF16)
    lo = (a - hi.astype(F32)).astype(BF16)
    acmp = jnp.concatenate([hi, lo], axis=-1)
    return jnp.swapaxes(tsel, 2, 3), jnp.swapaxes(twin, 2, 3), jnp.swapaxes(acmp, 1, 2)


def kernel(x_prompt, x_sample, cache_kv, cache_win, state_gla, state_conv, cache_mem, page_table, mem_prompt,
           g_mix, w_in, g_nsa_q, g_nsa_k, cmp_k_pe, cmp_k_w1, cmp_k_w2, cmp_v_pe, cmp_v_w1, cmp_v_w2,
           rel_bias, w_gla_gate, b_gla_gate, g_gla_o, g_mem, w_mem_kv, g_x_q, g_x_k,
           w_nsa_out, w_gla_out, w_x_out, w_o, g_ffn, w_up, conv_w, conv_b, w_down):
    bp, t, d = x_prompt.shape
    db = x_sample.shape[0]
    f = conv_w.shape[-1]

    offs = np.cumsum((0,) + IN_SIZES)
    segs = [w_in[:, offs[i]:offs[i + 1]] for i in range(len(IN_SIZES))]
    w_pad = jnp.concatenate([jnp.pad(s, ((0, 0), (0, pw - s.shape[1]))) for s, pw in zip(segs, PAD_SIZES)],
                            axis=1).astype(BF16)
    row = lambda v: v.reshape(1, -1).astype(F32)
    gq = row(jnp.tile(g_nsa_q, NSA_HEADS))
    gk0 = row(jnp.tile(g_nsa_k[0], NSA_KV))
    gk1 = row(jnp.tile(g_nsa_k[1], NSA_KV))
    gk2 = row(jnp.tile(g_nsa_k[2], NSA_KV))
    gxq = row(jnp.tile(g_x_q, X_HEADS))
    gxk = row(jnp.tile(g_x_k, X_HEADS))
    p64 = _block_ones(512, HEAD_DIM)
    p128 = _block_ones(512, X_DIM)
    bd2 = lambda a: jnp.concatenate([jnp.concatenate([a, jnp.zeros_like(a)], -1),
                                     jnp.concatenate([jnp.zeros_like(a), a], -1)], -2)
    bd4 = lambda a, c: jnp.concatenate([jnp.concatenate([bd2(a), jnp.zeros_like(bd2(a))], -1),
                                        jnp.concatenate([jnp.zeros_like(bd2(c)), bd2(c)], -1)], -2)
    pe = jnp.concatenate([jnp.tile(cmp_k_pe, (1, NSA_KV)), jnp.tile(cmp_v_pe, (1, NSA_KV))],
                         axis=-1).reshape(2, CMP_STRIDE, 1, 256)
    w1 = bd4(cmp_k_w1, cmp_v_w1).reshape(2, CMP_STRIDE * 256, 256).astype(BF16)
    w2 = bd4(cmp_k_w2, cmp_v_w2).astype(BF16)
    wg_pad = jnp.pad(w_gla_gate, ((0, 128 - GLA_RANK), (0, 0))).astype(BF16)
    tri = jnp.asarray(np.tril(np.ones((GLA_CHUNK, GLA_CHUNK), np.float32)), dtype=BF16)
    tb = rel_bias.astype(F32)[_bucket_table()]
    wn, wgo, wx, wo = (w.astype(BF16) for w in (w_nsa_out, w_gla_out, w_x_out, w_o))
    wup = w_up.astype(BF16)
    wdn = w_down.astype(BF16)
    ggo = row(g_gla_o)

    rows_p, win_p, _, gates, gla_in, xq, mg, ka, kw, vst, vwt, qt = _proj_in(
        x_prompt, row(g_mix), w_pad, gq, gk1, gk2, gxq, p64, p128, True)
    memkv_p = _memory_kv(mem_prompt, row(g_mem), w_mem_kv.astype(BF16), gxk, p128)
    kc, vct = _compress(rows_p, pe, w1, w2, gk0, p64)
    n_chunks = t // CMP_STRIDE
    n_sel = -(-t // SEL_BLOCK)
    tsel, twin, acmp = _prompt_tables(tb)
    ovt = _overlap(n_chunks, n_chunks - 1, 128, n_sel).T
    o_nsa = _nsa_prompt(qt, kc, vct, ka, kw, vst, vwt, gates, tsel, twin, acmp, ovt)
    s0t = jnp.zeros((bp, GLA_HEADS, GLA_DV, GLA_DK), F32)
    o_gla, st = _gla_prompt(gla_in, s0t, wg_pad, row(b_gla_gate), ggo, tri)
    o_x = _xatt(xq, memkv_p)
    m = bp * t
    x1 = _merge(o_nsa.reshape(m, 512), o_gla.reshape(m, 512), o_x.reshape(m, 512), mg.reshape(m, 3 * d),
                x_prompt.reshape(m, d), wn, wgo, wx, wo)
    y_p, tail = _ffn_seq(x1.reshape(bp, t, d), jnp.zeros((bp, 2, f), F32), row(g_ffn), wup, conv_w, row(conv_b), wdn)
    wl_p = min(WINDOW, t)
    out_rows_p = rows_p.reshape(bp, t, 4, NSA_KV, HEAD_DIM)
    out_win_p = win_p[:, t - wl_p:].reshape(bp, wl_p, 2, NSA_KV, HEAD_DIM)
    out_gla_p = jnp.swapaxes(st, 2, 3)
    out_conv_p = tail[:, 6:8]
    out_mem_p = memkv_p.reshape(bp, -1, 2, X_HEADS, X_DIM)

    n_pages = page_table.shape[1]
    page = cache_kv.shape[1]
    length = n_pages * page
    wl = cache_win.shape[1]
    rows_s, win_s, qn_s, gates_s, gla_s, xq_s, mg_s = (a[0] for a in _proj_in(
        x_sample.reshape(1, db, d), row(g_mix), w_pad, gq, gk1, gk2, gxq, p64, p128, False))

    eye = jnp.eye(NSA_KV, dtype=BF16)
    q8 = (qn_s.reshape(db, NSA_KV, NSA_GROUP, 1, HEAD_DIM) * eye[None, :, None, :, None]).reshape(db, 8, 128)
    gates8 = jnp.pad(gates_s[:, 0:24].reshape(db, 8, 3), ((0, 0), (0, 0), (0, 125)))
    n_chunks_s = length // CMP_STRIDE
    n_sel_s = -(-(length + 1) // SEL_BLOCK)
    nsp = -(-n_sel_s // 128) * 128
    cidx = np.arange(n_chunks_s)
    rel_c = length - (cidx * CMP_STRIDE + CMP_BLOCK - 1)
    bc = _bias_lookup(tb, rel_c, (rel_c >= 0) & (cidx < n_chunks_s - 1))
    bs = _bias_descending(tb, length)
    bw = jnp.where(jnp.asarray(np.arange(wl, 0, -1) < WINDOW)[None], _bias_descending(tb, wl), NEG)
    b0 = tb[0].reshape(8, 1)
    ov_s = _overlap(n_chunks_s, n_chunks_s - 1, nsp, n_sel_s).T
    assert length // SEL_BLOCK <= 128
    et = jnp.asarray((np.arange(length)[:, None] // SEL_BLOCK == np.arange(128)[None, :]).astype(np.float32),
                     dtype=BF16)
    o8, win_new = _nsa_decode(page_table, cache_kv.reshape(cache_kv.shape[0], page, 512), q8,
                              rows_s.reshape(db, 1, 512), win_s.reshape(db, 1, 256), cache_win.reshape(db, wl, 256),
                              gates8, bc, bs, bw, b0, ov_s, et, pe, w1, w2, gk0, p64)
    o8 = o8.reshape(db, NSA_KV, NSA_GROUP, NSA_KV, HEAD_DIM)
    o_nsa_s = jnp.stack([o8[:, 0, :, 0], o8[:, 1, :, 1]], axis=1).reshape(db, 512).astype(BF16)

    wgt = jnp.pad(w_gla_gate.T, ((0, 0), (0, 128 - GLA_RANK))).reshape(GLA_HEADS, GLA_DK, 128)
    o_gla_s, gla_state_s = _gla_step(
        gla_s[:, 0:256].reshape(db, GLA_HEADS, GLA_DK, 1), gla_s[:, 256:512].reshape(db, GLA_HEADS, GLA_DK, 1),
        gla_s[:, 1024:1152].reshape(db, 1, 128), gla_s[:, 512:1024].reshape(db, GLA_HEADS, 1, GLA_DV),
        gla_s[:, 1152:1664].reshape(db, GLA_HEADS, 1, GLA_DV), state_gla.astype(F32), wgt,
        b_gla_gate.reshape(GLA_HEADS, GLA_DK, 1), ggo)
    o_gla_s = o_gla_s.reshape(db, 512).astype(BF16)

    xq_pad = jnp.pad(xq_s.reshape(db, 1, 512), ((0, 0), (0, 15), (0, 0)))
    o_x_s = _xatt(xq_pad, cache_mem.reshape(db, -1, 1024))[:, 0]
    x1_s = _merge(o_nsa_s, o_gla_s, o_x_s, mg_s, x_sample.reshape(db, d), wn, wgo, wx, wo)
    y_s, g_new = _ffn_step(x1_s, state_conv[:, 0], state_conv[:, 1], row(g_ffn), wup, conv_w, row(conv_b), wdn)

    out_rows_s = rows_s.reshape(db, 1, 4, NSA_KV, HEAD_DIM)
    out_win_s = win_new.reshape(db, wl, 2, NSA_KV, HEAD_DIM)
    out_conv_s = jnp.stack([state_conv[:, 1], g_new], axis=1)
    return (y_p, y_s.reshape(db, 1, d), out_rows_p, out_win_p, out_gla_p, out_conv_p, out_mem_p,
            out_rows_s, out_win_s, gla_state_s, out_conv_s)
```

```python
import functools
import math

import numpy as np
import jax
import jax.numpy as jnp
from jax import lax
from jax.experimental import pallas as pl
from jax.experimental.pallas import tpu as pltpu

F32 = jnp.float32
BF16 = jnp.bfloat16

NSA_HEADS = 8
NSA_KV = 2
NSA_GROUP = 4
HEAD_DIM = 64
CMP_BLOCK = 32
CMP_STRIDE = 16
SEL_BLOCK = 64
SEL_TOP = 16
WINDOW = 512
Q_BLOCK = 128
GLA_HEADS = 4
GLA_DK = 64
GLA_DV = 128
GLA_RANK = 16
GLA_TAU = 16.0
GLA_CHUNK = 64
X_HEADS = 4
X_DIM = 128
N_BUCKETS = 32
MAX_DISTANCE = 128
EPS = 1e-6
LOG2E = math.log2(math.e)
NEG = -1e30
TINY = 1e-30
SEL_PENALTY = -1e9
MASKED_BELOW = -5e29
EXP_CLAMP = 80.0

IN_SIZES = (512, 768, 24, 256, 256, 512, 16, 512, 512, 3072)
PAD_SIZES = (512, 768, 128, 256, 256, 512, 128, 512, 512, 3072)
PAD_OFFS = tuple(int(v) for v in np.cumsum((0,) + PAD_SIZES))
D_IN_PAD = PAD_OFFS[-1]
GLA_IN_W = 256 + 256 + 512 + 128 + 512

VMEM_LIMIT = 56 * 1024 * 1024
FAR_TILE = 256
NEAR_TILE = 512
WIN_TILE = WINDOW + Q_BLOCK


def _cparams(n_axes):
    return pltpu.CompilerParams(dimension_semantics=("arbitrary",) * n_axes, vmem_limit_bytes=VMEM_LIMIT)


def _dot(a, b):
    return jnp.dot(a, b, preferred_element_type=F32)


def _dot_nt(a, b):
    return lax.dot_general(a, b, (((1,), (1,)), ((), ())), preferred_element_type=F32)


def _dot_tn(a, b):
    return lax.dot_general(a, b, (((0,), (0,)), ((), ())), preferred_element_type=F32)


def _split_dot(x, m):
    hi = x.astype(BF16)
    lo = (x - hi.astype(F32)).astype(BF16)
    return _dot(hi, m) + _dot(lo, m)


def _rms(x, g):
    return x * lax.rsqrt(jnp.mean(x * x, axis=-1, keepdims=True) + EPS) * g


def _group_rms(x, pmat, gsize, g):
    ss = _split_dot(x * x, pmat)
    return x * lax.rsqrt(ss * (1.0 / gsize) + EPS) * g


def _gelu(x):
    return 0.5 * x * (1.0 + jnp.tanh(math.sqrt(2.0 / math.pi) * (x + 0.044715 * (x * x * x))))


def _sigmoid(x):
    return 1.0 / (1.0 + jnp.exp(-x))


def _block_ones(n, gsize):
    i = np.arange(n) // gsize
    return jnp.asarray((i[:, None] == i[None, :]).astype(np.float32), dtype=BF16)


def _proj_kernel(x_ref, gmix_ref, w_ref, gq_ref, gk1_ref, gk2_ref, gxq_ref, p64_ref, p128_ref,
                 rows_ref, win_ref, qn_ref, gates_ref, gla_ref, xq_ref, mg_ref, *attn_refs):
    x = x_ref[0]
    h = _rms(x, gmix_ref[...]).astype(BF16)
    o = PAD_OFFS

    def seg(i):
        return _dot(h, w_ref[:, o[i]:o[i + 1]])

    p64 = p64_ref[...]
    p64s = p64_ref[0:128, 0:128]
    qn = _group_rms(seg(0), p64, HEAD_DIM, gq_ref[...]) * (HEAD_DIM ** -0.5)
    qn_ref[0] = qn.astype(BF16)
    if attn_refs:
        qt_ref = attn_refs[4]
        for u in range(x.shape[0] // Q_BLOCK):
            qt_ref[0, u] = (qn[Q_BLOCK * u:Q_BLOCK * (u + 1)] * LOG2E).T.astype(BF16)

    kv = seg(1)
    k_sel = _group_rms(kv[:, 256:384], p64s, HEAD_DIM, gk1_ref[...])
    k_win = _group_rms(kv[:, 512:640], p64s, HEAD_DIM, gk2_ref[...])
    rows_ref[0, :, 0:256] = kv[:, 0:256]
    rows_ref[0, :, 256:384] = k_sel
    rows_ref[0, :, 384:512] = kv[:, 384:512]
    win_ref[0, :, 0:128] = k_win
    win_ref[0, :, 128:256] = kv[:, 640:768]
    if attn_refs:
        ka_ref, kw_ref, vst_ref, vwt_ref, _ = attn_refs
        tm = x.shape[0]
        tpos = pl.program_id(1) * tm + lax.broadcasted_iota(jnp.int32, (tm, 128), 0)
        blk = lax.broadcasted_iota(jnp.int32, (tm, 128), 1)
        ka_ref[0, :, 0:128] = k_sel.astype(BF16)
        ka_ref[0, :, 128:256] = jnp.where(tpos // SEL_BLOCK == blk, 1.0, 0.0).astype(BF16)
        kw_ref[0] = k_win.astype(BF16)
        for u in range(tm // FAR_TILE):
            vst_ref[0, u] = kv[FAR_TILE * u:FAR_TILE * (u + 1), 384:512].T.astype(BF16)
        for u in range(tm // Q_BLOCK):
            vwt_ref[0, u] = kv[Q_BLOCK * u:Q_BLOCK * (u + 1), 640:768].T.astype(BF16)

    gates_ref[0] = _sigmoid(seg(2))
    gla_ref[0, :, 0:256] = seg(3) * (GLA_DK ** -0.5)
    gla_ref[0, :, 256:512] = seg(4)
    gla_ref[0, :, 512:1024] = seg(5)
    gla_ref[0, :, 1024:1152] = seg(6)
    gla_ref[0, :, 1152:1664] = seg(7)
    xq = _group_rms(seg(8), p128_ref[...], X_DIM, gxq_ref[...]) * (X_DIM ** -0.5)
    xq_ref[0] = xq.astype(BF16)
    mg_ref[0] = _sigmoid(seg(9)).astype(BF16)


def _proj_in(x, g_mix, w_pad, gq, gk1, gk2, gxq, p64, p128, attn_layout):
    b, t, d = x.shape
    tm = min(256, t)
    assert t % tm == 0
    widths = [512, 256, 512, 128, GLA_IN_W, 512, 3072]
    dtypes = [F32, F32, BF16, F32, F32, BF16, BF16]
    out_specs = [pl.BlockSpec((1, tm, w), lambda i, j: (i, j, 0)) for w in widths]
    out_shape = [jax.ShapeDtypeStruct((b, t, w), dt) for w, dt in zip(widths, dtypes)]
    if attn_layout:
        assert tm % Q_BLOCK == 0
        for w in (256, 128):
            out_specs.append(pl.BlockSpec((1, tm, w), lambda i, j: (i, j, 0)))
            out_shape.append(jax.ShapeDtypeStruct((b, t, w), BF16))
        assert tm % FAR_TILE == 0
        for rows, width in ((128, FAR_TILE), (128, Q_BLOCK), (512, Q_BLOCK)):
            out_specs.append(pl.BlockSpec((1, tm // width, rows, width), lambda i, j: (i, j, 0, 0)))
            out_shape.append(jax.ShapeDtypeStruct((b, t // width, rows, width), BF16))
    const = lambda shape: pl.BlockSpec(shape, lambda i, j: (0,) * len(shape))
    return pl.pallas_call(
        _proj_kernel,
        grid=(b, t // tm),
        in_specs=[pl.BlockSpec((1, tm, d), lambda i, j: (i, j, 0)),
                  const((1, d)), const((d, D_IN_PAD)), const((1, 512)), const((1, 128)), const((1, 128)),
                  const((1, 512)), const((512, 512)), const((512, 512))],
        out_specs=out_specs,
        out_shape=out_shape,
        compiler_params=_cparams(2),
    )(x, g_mix, w_pad, gq, gk1, gk2, gxq, p64, p128)


def _memkv_kernel(m_ref, g_ref, w_ref, gk_ref, p128_ref, o_ref):
    h = _rms(m_ref[0], g_ref[...]).astype(BF16)
    kv = _dot(h, w_ref[...])
    o_ref[0, :, 0:512] = _group_rms(kv[:, 0:512], p128_ref[...], X_DIM, gk_ref[...])
    o_ref[0, :, 512:1024] = kv[:, 512:1024]


def _memory_kv(mem, g_mem, w_mem, gxk, p128):
    b, m, d = mem.shape
    const = lambda shape: pl.BlockSpec(shape, lambda i: (0,) * len(shape))
    return pl.pallas_call(
        _memkv_kernel,
        grid=(b,),
        in_specs=[pl.BlockSpec((1, m, d), lambda i: (i, 0, 0)), const((1, d)), const((d, 1024)),
                  const((1, 512)), const((512, 512))],
        out_specs=pl.BlockSpec((1, m, 1024), lambda i: (i, 0, 0)),
        out_shape=jax.ShapeDtypeStruct((b, m, 1024), F32),
        compiler_params=_cparams(1),
    )(mem, g_mem, w_mem, gxk, p128)


def _compress_core(load_j, n_chunks, pe_ref, w1_ref, w2_ref, gk0_ref, p64_ref):
    xs = [load_j(j) for j in range(CMP_STRIDE)]
    halves = [_dot(jnp.concatenate([(x + pe_ref[r, j]).astype(BF16) for j, x in enumerate(xs)], axis=1), w1_ref[r])
              for r in range(2)]
    hid = halves[0] + pltpu.roll(halves[1], n_chunks - 1, 0)
    out = _dot(_gelu(hid).astype(BF16), w2_ref[...])
    row = lax.broadcasted_iota(jnp.int32, (n_chunks, 128), 0)
    live = row < n_chunks - 1
    kc = _group_rms(out[:, 0:128], p64_ref[0:128, 0:128], HEAD_DIM, gk0_ref[...])
    return jnp.where(live, kc, 0.0), jnp.where(live, out[:, 128:256], 0.0)


def _compress_kernel(rk_ref, rv_ref, pe_ref, w1_ref, w2_ref, gk0_ref, p64_ref, kc_ref, vc_ref, *, n_chunks):
    load_j = lambda j: jnp.concatenate([rk_ref[0, pl.ds(j, n_chunks, stride=CMP_STRIDE), :],
                                        rv_ref[0, pl.ds(j, n_chunks, stride=CMP_STRIDE), :]], axis=1)
    kc, vc = _compress_core(load_j, n_chunks, pe_ref, w1_ref, w2_ref, gk0_ref, p64_ref)
    kc_ref[0] = kc.astype(BF16)
    vc_ref[0] = vc.T.astype(BF16)


def _compress(rows, pe, w1, w2, gk0, p64):
    b, t, _ = rows.shape
    n_chunks = t // CMP_STRIDE
    const = lambda shape: pl.BlockSpec(shape, lambda i: (0,) * len(shape))
    return pl.pallas_call(
        functools.partial(_compress_kernel, n_chunks=n_chunks),
        grid=(b,),
        in_specs=[pl.BlockSpec((1, t, 128), lambda i: (i, 0, 0)), pl.BlockSpec((1, t, 128), lambda i: (i, 0, 1)),
                  const((2, CMP_STRIDE, 1, 256)), const((2, CMP_STRIDE * 256, 256)), const((256, 256)),
                  const((1, 128)), const((512, 512))],
        out_specs=[pl.BlockSpec((1, n_chunks, 128), lambda i: (i, 0, 0)),
                   pl.BlockSpec((1, 128, n_chunks), lambda i: (i, 0, 0))],
        out_shape=[jax.ShapeDtypeStruct((b, n_chunks, 128), BF16), jax.ShapeDtypeStruct((b, 128, n_chunks), BF16)],
        compiler_params=_cparams(1),
    )(rows, rows, pe, w1, w2, gk0, p64)


def _masked_softmax(s, axis, exp_fn=jnp.exp):
    m = jnp.maximum(jnp.max(s, axis=axis, keepdims=True), MASKED_BELOW)
    p = exp_fn(s - m)
    return p / jnp.maximum(jnp.sum(p, axis=axis, keepdims=True), TINY)


def _select_blocks(score, top, axis):
    pos = lax.broadcasted_iota(jnp.int32, score.shape, axis).astype(F32)
    sel = jnp.zeros(score.shape, jnp.bool_)
    for _ in range(top):
        mx = jnp.max(score, axis=axis, keepdims=True)
        idx = jnp.min(jnp.where(score == mx, pos, 1e9), axis=axis, keepdims=True)
        hit = pos == idx
        sel = jnp.logical_or(sel, hit)
        score = jnp.where(hit, -3e38, score)
    return sel


def _select_with_forced(imp, forced, top, axis):
    n_forced = 3
    assert top > n_forced
    return jnp.logical_or(forced, _select_blocks(jnp.where(forced, -3e38, imp), top - n_forced, axis))


def _values_times_probs(vt_tiles, p):
    out = None
    start = 0
    for vt in vt_tiles:
        part = _dot(vt, p[start:start + vt.shape[1]])
        start += vt.shape[1]
        out = part if out is None else out + part
    return out


def _softmax_stats_t(s_ref, bias, m_ref, l_ref):
    a_parts, p_parts = [], []
    for g in range(s_ref.shape[1] // Q_BLOCK):
        cs = slice(Q_BLOCK * g, Q_BLOCK * (g + 1))
        sg = s_ref[:, cs]
        if bias is not None:
            sg = sg + bias(cs)
        m_old = m_ref[:, cs]
        m_new = jnp.maximum(m_old, jnp.max(sg, axis=0, keepdims=True))
        alpha = jnp.exp2(m_old - m_new)
        p = jnp.exp2(sg - m_new)
        m_ref[:, cs] = m_new
        l_ref[:, cs] = alpha * l_ref[:, cs] + jnp.sum(p, axis=0, keepdims=True)
        a_parts.append(alpha)
        p_parts.append(p.astype(BF16))
    return jnp.concatenate(a_parts, axis=1), jnp.concatenate(p_parts, axis=1)


def _acc_update_t(acc_ref, alpha, vt_tiles, p):
    acc_ref[...] = alpha * acc_ref[...] + _values_times_probs(vt_tiles, p)


def _attn_kernel(qt_ref, kc_ref, vct_ref, ka_ref, vst_ref, kw_ref, vwt_ref, gt_ref, tsel_ref, twin_ref, acmp_ref,
                 ovt_ref, o_ref, sa_scr, sb_scr, m_scr, l_scr, acc_scr, pend_a, pend_p, *, top):
    k = pl.program_id(1)
    n = pl.program_id(2)
    cols = NSA_GROUP * Q_BLOCK
    vrow = pl.multiple_of(k * HEAD_DIM, HEAD_DIM)
    zero = jnp.zeros((HEAD_DIM, Q_BLOCK), BF16)
    parts = []
    for g in range(NSA_GROUP):
        piece = qt_ref[0, 0, HEAD_DIM * g:HEAD_DIM * (g + 1), :]
        parts.append(jnp.where(k == 0, jnp.concatenate([piece, zero], axis=0),
                               jnp.concatenate([zero, piece], axis=0)))
    qt = jnp.concatenate(parts, axis=1)

    nc = kc_ref.shape[1]
    blocks_per_q = Q_BLOCK // CMP_STRIDE

    def cmp_branch(rows):
        def run():
            c32 = lax.broadcasted_iota(jnp.int32, (rows, 32), 0)
            r32 = lax.broadcasted_iota(jnp.int32, (rows, 32), 1)
            onehot = jnp.where((c32 - blocks_per_q * n + 9) == (r32 & 15), 1.0, 0.0).astype(BF16)
            band = _dot(onehot, acmp_ref[0])
            cp = lax.broadcasted_iota(jnp.int32, (rows, cols), 0) - blocks_per_q * n + 9
            s_c = _dot(kc_ref[0, 0:rows, :], qt) + jnp.where(cp < 0, 0.0, jnp.where(cp > 15, NEG, band))
            p_c = _masked_softmax(s_c, 0, jnp.exp2).astype(BF16)
            o_cmp = _dot(vct_ref[0, pl.ds(vrow, HEAD_DIM), 0:rows], p_c)
            imp4 = _dot(ovt_ref[:, 0:rows], p_c)
            return o_cmp, imp4[:, 0:128] + imp4[:, 128:256] + imp4[:, 256:384] + imp4[:, 384:512]
        return run

    if nc % Q_BLOCK == 0 and nc > Q_BLOCK:
        need = jnp.minimum((blocks_per_q * n + 6) // Q_BLOCK, nc // Q_BLOCK - 1)
        o_c, imp = lax.switch(need, [cmp_branch(Q_BLOCK * (i + 1)) for i in range(nc // Q_BLOCK)])
    else:
        o_c, imp = cmp_branch(nc)()

    def value_tiles(ref, first, count):
        return [ref[0, first + d, pl.ds(vrow, HEAD_DIM), :] for d in range(count)]

    wt = jnp.maximum(n - WINDOW // Q_BLOCK, 0)
    s_w = _dot(kw_ref[0, pl.ds(pl.multiple_of(wt * Q_BLOCK, Q_BLOCK), WIN_TILE), :], qt) + twin_ref[0, 0]
    p_w = jnp.exp2(s_w - jnp.max(s_w, axis=0, keepdims=True))
    o_w = (_values_times_probs(value_tiles(vwt_ref, wt, WIN_TILE // Q_BLOCK), p_w.astype(BF16))
           / jnp.sum(p_w, axis=0, keepdims=True))

    jj = lax.broadcasted_iota(jnp.int32, (128, Q_BLOCK), 0)
    tpos = n * Q_BLOCK + lax.broadcasted_iota(jnp.int32, (128, Q_BLOCK), 1)
    tblk = tpos // SEL_BLOCK
    forced = (jj == 0) | (jj == tblk) | (jj == tblk - 1)
    sel_t = _select_with_forced(jnp.where(jj * SEL_BLOCK <= tpos, imp, NEG), forced, top, 0)
    pen_t = jnp.where(sel_t, 0.0, SEL_PENALTY).astype(BF16)
    rhs = jnp.concatenate([qt, jnp.concatenate([pen_t] * NSA_GROUP, axis=1)], axis=0)

    n_far = jnp.maximum(n - 2, 0) // 2
    r_near = n - 2 * n_far

    def issue_scores(tile, s_ref):
        s_ref[...] = _dot(ka_ref[0, pl.ds(pl.multiple_of(tile * FAR_TILE, FAR_TILE), FAR_TILE), :], rhs)

    def stats(s_ref, half_idx=None):
        bias = None
        if half_idx is not None:
            bias = lambda cs: tsel_ref[0, r_near, FAR_TILE * half_idx:FAR_TILE * (half_idx + 1), cs]
        return _softmax_stats_t(s_ref, bias, m_scr, l_scr)

    def accumulate(tile, alpha, p):
        _acc_update_t(acc_scr, alpha, value_tiles(vst_ref, tile, 1), p)

    m_scr[...] = jnp.full((1, cols), NEG, F32)
    l_scr[...] = jnp.zeros((1, cols), F32)
    acc_scr[...] = jnp.zeros((HEAD_DIM, cols), F32)
    pend_a[...] = jnp.ones((1, cols), F32)
    pend_p[...] = jnp.zeros((FAR_TILE, cols), BF16)
    n_pairs = n_far // 2
    issue_scores(0, sa_scr)

    @pl.loop(0, n_pairs)
    def _(j):
        t0 = 2 * j
        issue_scores(t0 + 1, sb_scr)
        alpha_a, p_a = stats(sa_scr)
        accumulate(jnp.maximum(t0 - 1, 0), pend_a[...], pend_p[...])
        issue_scores(t0 + 2, sa_scr)
        alpha_b, p_b = stats(sb_scr)
        accumulate(t0, alpha_a, p_a)
        pend_a[...] = alpha_b
        pend_p[...] = p_b

    t1 = 2 * n_pairs
    accumulate(jnp.maximum(t1 - 1, 0), pend_a[...], pend_p[...])
    issue_scores(t1 + 1, sb_scr)

    @pl.when(n_far % 2 == 1)
    def _():
        accumulate(t1, *stats(sa_scr))
        issue_scores(t1 + 2, sa_scr)
        accumulate(t1 + 1, *stats(sb_scr, 0))
        accumulate(t1 + 2, *stats(sa_scr, 1))

    @pl.when(n_far % 2 == 0)
    def _():
        accumulate(t1, *stats(sa_scr, 0))
        accumulate(t1 + 1, *stats(sb_scr, 1))

    o_s = acc_scr[...] / l_scr[...]

    gtt = gt_ref[0].T

    def gate_row(branch):
        rows = [jnp.where(k == 0, gtt[3 * g + branch:3 * g + branch + 1],
                          gtt[12 + 3 * g + branch:12 + 3 * g + branch + 1]) for g in range(NSA_GROUP)]
        return jnp.concatenate(rows, axis=1)

    o_t = gate_row(0) * o_c + gate_row(1) * o_s + gate_row(2) * o_w
    left = jnp.concatenate([o_t[:, 0:128], o_t[:, 128:256]], axis=0).T
    right = jnp.concatenate([o_t[:, 256:384], o_t[:, 384:512]], axis=0).T
    o_ref[0] = jnp.concatenate([left, right], axis=1).astype(BF16)


def _nsa_prompt(qt, kc, vct, ka, kw, vst, vwt, gates, tsel, twin, acmp, ovt):
    b, t, _ = ka.shape
    nqb = t // Q_BLOCK
    nc = kc.shape[1]
    top = min(SEL_TOP, -(-t // SEL_BLOCK))
    assert t >= WIN_TILE and t % FAR_TILE == 0 and t // SEL_BLOCK <= 128
    n_win = WINDOW // Q_BLOCK
    per_b = lambda shape: pl.BlockSpec((1,) + shape, lambda i, k, n: (i,) + (0,) * len(shape))
    return pl.pallas_call(
        functools.partial(_attn_kernel, top=top),
        grid=(b, NSA_KV, nqb),
        in_specs=[pl.BlockSpec((1, 1, NSA_GROUP * HEAD_DIM, Q_BLOCK), lambda i, k, n: (i, n, k, 0)),
                  per_b((nc, 128)), per_b((128, nc)),
                  per_b((t, 256)), per_b((t // FAR_TILE, 128, FAR_TILE)), per_b((t, 128)), per_b((nqb, 128, Q_BLOCK)),
                  pl.BlockSpec((1, Q_BLOCK, 128), lambda i, k, n: (i, n, 0)),
                  pl.BlockSpec((1, 4, NEAR_TILE, 512), lambda i, k, n: (k, 0, 0, 0)),
                  pl.BlockSpec((1, 1, WIN_TILE, 512), lambda i, k, n: (k, jnp.minimum(n, n_win), 0, 0)),
                  pl.BlockSpec((1, 32, 512), lambda i, k, n: (k, 0, 0)),
                  pl.BlockSpec((128, nc), lambda i, k, n: (0, 0))],
        out_specs=pl.BlockSpec((1, Q_BLOCK, 256), lambda i, k, n: (i, n, k)),
        out_shape=jax.ShapeDtypeStruct((b, t, 512), BF16),
        scratch_shapes=[pltpu.VMEM((FAR_TILE, NSA_GROUP * Q_BLOCK), F32), pltpu.VMEM((FAR_TILE, NSA_GROUP * Q_BLOCK), F32),
                        pltpu.VMEM((1, NSA_GROUP * Q_BLOCK), F32), pltpu.VMEM((1, NSA_GROUP * Q_BLOCK), F32),
                        pltpu.VMEM((HEAD_DIM, NSA_GROUP * Q_BLOCK), F32),
                        pltpu.VMEM((1, NSA_GROUP * Q_BLOCK), F32), pltpu.VMEM((FAR_TILE, NSA_GROUP * Q_BLOCK), BF16)],
        compiler_params=_cparams(3),
    )(qt, kc, vct, ka, vst, kw, vwt, gates, tsel, twin, acmp, ovt)


def _log_sigmoid(z):
    return jnp.minimum(z, 0.0) - jnp.log1p(jnp.exp(-jnp.abs(z)))


def _gla_kernel(x_ref, s0_ref, wg_ref, bg_ref, ggo_ref, tri_ref, o_ref, st_ref, s_scr, *, n_chunks):
    @pl.when(pl.program_id(0) == 0)
    def _():
        s_scr[...] = s0_ref[...]

    c_len = GLA_CHUNK
    tri = tri_ref[...]
    ti = lax.broadcasted_iota(jnp.int32, (c_len, c_len), 0)
    si = lax.broadcasted_iota(jnp.int32, (c_len, c_len), 1)
    causal = si <= ti
    for c, bi in [(c, bi) for c in range(n_chunks) for bi in range(x_ref.shape[0])]:
        rs = slice(c_len * c, c_len * (c + 1))
        q = x_ref[bi, rs, 0:256]
        kk = x_ref[bi, rs, 256:512]
        v = x_ref[bi, rs, 512:1024]
        lr = x_ref[bi, rs, 1024:1152]
        r = x_ref[bi, rs, 1152:1664]
        la = _log_sigmoid(_dot(lr.astype(BF16), wg_ref[...]) + bg_ref[...]) * (1.0 / GLA_TAU)
        a1 = la.astype(BF16)
        r1 = la - a1.astype(F32)
        a2 = r1.astype(BF16)
        a3 = (r1 - a2.astype(F32)).astype(BF16)
        cb = _dot(tri, a1) + _dot(tri, a2) + _dot(tri, a3)
        last = cb[c_len - 1:c_len, :]
        mid = cb[c_len // 2:c_len // 2 + 1, :]
        qe = (q * jnp.exp(cb)).astype(BF16)
        qa = (q * jnp.exp(jnp.minimum(cb - mid, EXP_CLAMP))).astype(BF16)
        kb = (kk * jnp.exp(jnp.minimum(mid - cb, EXP_CLAMP))).astype(BF16)
        ke = (kk * jnp.exp(last - cb)).astype(BF16)
        dec = jnp.exp(last)
        for h in range(GLA_HEADS):
            ks = slice(GLA_DK * h, GLA_DK * (h + 1))
            vs = slice(GLA_DV * h, GLA_DV * (h + 1))
            att = jnp.where(causal, _dot_nt(qa[:, ks], kb[:, ks]), 0.0)
            vh = v[:, vs].astype(BF16)
            st = s_scr[bi, h]
            o = _dot(att.astype(BF16), vh) + _dot_nt(qe[:, ks], st.astype(BF16))
            s_scr[bi, h] = st * dec[:, ks] + _dot_tn(vh, ke[:, ks])
            on = _rms(o, ggo_ref[...])
            rh = r[:, vs]
            o_ref[bi, rs, vs] = (on * (rh * _sigmoid(rh))).astype(BF16)
    st_ref[...] = s_scr[...]


def _gla_prompt(gla_in, s0t, wg, bg, ggo, tri):
    b, t, w = gla_in.shape
    ct = min(256, t)
    assert t % ct == 0 and ct % GLA_CHUNK == 0
    const = lambda shape: pl.BlockSpec(shape, lambda j: (0,) * len(shape))
    state_shape = (b, GLA_HEADS, GLA_DV, GLA_DK)
    return pl.pallas_call(
        functools.partial(_gla_kernel, n_chunks=ct // GLA_CHUNK),
        grid=(t // ct,),
        in_specs=[pl.BlockSpec((b, ct, w), lambda j: (0, j, 0)), const(state_shape),
                  const((128, 256)), const((1, 256)), const((1, 128)), const((GLA_CHUNK, GLA_CHUNK))],
        out_specs=[pl.BlockSpec((b, ct, 512), lambda j: (0, j, 0)), const(state_shape)],
        out_shape=[jax.ShapeDtypeStruct((b, t, 512), BF16), jax.ShapeDtypeStruct(state_shape, F32)],
        scratch_shapes=[pltpu.VMEM(state_shape, F32)],
        compiler_params=_cparams(1),
    )(gla_in, s0t, wg, bg, ggo, tri)


def _gla_step_kernel(q_ref, k_ref, lr_ref, v_ref, r_ref, s_ref, wgt_ref, bgt_ref, ggo_ref, o_ref, sn_ref):
    lr = lr_ref[0]
    for h in range(GLA_HEADS):
        z = jnp.sum(wgt_ref[h] * lr, axis=-1, keepdims=True) + bgt_ref[h]
        a = jnp.exp(_log_sigmoid(z) * (1.0 / GLA_TAU))
        s0 = s_ref[0, h]
        kh = k_ref[0, h]
        qh = q_ref[0, h]
        vh = v_ref[0, h]
        sn_ref[0, h] = a * s0 + kh * vh
        o = jnp.sum((qh * a) * s0, axis=0, keepdims=True) + jnp.sum(qh * kh, axis=0, keepdims=True) * vh
        on = _rms(o, ggo_ref[...])
        rh = r_ref[0, h]
        o_ref[0, h] = on * (rh * _sigmoid(rh))


def _gla_step(q_col, k_col, lr, v_row, r_row, s0, wgt, bgt, ggo):
    b = q_col.shape[0]
    const = lambda shape: pl.BlockSpec(shape, lambda i: (0,) * len(shape))
    per_b = lambda shape: pl.BlockSpec((1,) + shape, lambda i: (i,) + (0,) * len(shape))
    return pl.pallas_call(
        _gla_step_kernel,
        grid=(b,),
        in_specs=[per_b((GLA_HEADS, GLA_DK, 1)), per_b((GLA_HEADS, GLA_DK, 1)), per_b((1, 128)),
                  per_b((GLA_HEADS, 1, GLA_DV)), per_b((GLA_HEADS, 1, GLA_DV)), per_b((GLA_HEADS, GLA_DK, GLA_DV)),
                  const((GLA_HEADS, GLA_DK, 128)), const((GLA_HEADS, GLA_DK, 1)), const((1, 128))],
        out_specs=[per_b((GLA_HEADS, 1, GLA_DV)), per_b((GLA_HEADS, GLA_DK, GLA_DV))],
        out_shape=[jax.ShapeDtypeStruct((b, GLA_HEADS, 1, GLA_DV), F32),
                   jax.ShapeDtypeStruct((b, GLA_HEADS, GLA_DK, GLA_DV), F32)],
        compiler_params=_cparams(1),
    )(q_col, k_col, lr, v_row, r_row, s0, wgt, bgt, ggo)


def _xatt_kernel(xq_ref, mem_ref, o_ref):
    for bi in range(xq_ref.shape[0]):
        for h in range(X_HEADS):
            ls = slice(X_DIM * h, X_DIM * (h + 1))
            kh = mem_ref[bi, :, ls].astype(BF16)
            vh = mem_ref[bi, :, 512 + X_DIM * h:512 + X_DIM * (h + 1)].astype(BF16)
            s = _dot_nt(xq_ref[bi, :, ls], kh)
            p = jnp.exp(s - jnp.max(s, axis=-1, keepdims=True))
            p = p / jnp.sum(p, axis=-1, keepdims=True)
            o_ref[bi, :, ls] = _dot(p.astype(BF16), vh).astype(BF16)


def _xatt(xq, memkv):
    b, t, _ = xq.shape
    m = memkv.shape[1]
    tq = min(512, t)
    assert t % tq == 0
    bb = math.gcd(b, max(1, 128 // tq))
    return pl.pallas_call(
        _xatt_kernel,
        grid=(b // bb, t // tq),
        in_specs=[pl.BlockSpec((bb, tq, 512), lambda i, j: (i, j, 0)),
                  pl.BlockSpec((bb, m, 1024), lambda i, j: (i, 0, 0))],
        out_specs=pl.BlockSpec((bb, tq, 512), lambda i, j: (i, j, 0)),
        out_shape=jax.ShapeDtypeStruct((b, t, 512), BF16),
        compiler_params=_cparams(2),
    )(xq, memkv)


def _merge_kernel(on_ref, og_ref, ox_ref, mg_ref, x_ref, wn_ref, wg_ref, wx_ref, wo_ref, x1_ref):
    d = x_ref.shape[-1]
    merged = (mg_ref[:, 0:d].astype(F32) * _dot(on_ref[...], wn_ref[...])
              + mg_ref[:, d:2 * d].astype(F32) * _dot(og_ref[...], wg_ref[...])
              + mg_ref[:, 2 * d:3 * d].astype(F32) * _dot(ox_ref[...], wx_ref[...]))
    x1_ref[...] = x_ref[...] + _dot(merged.astype(BF16), wo_ref[...])


def _merge(o_nsa, o_gla, o_x, mg, x, wn, wg, wx, wo):
    m, d = x.shape
    tm = min(512, m)
    assert m % tm == 0
    row = lambda w: pl.BlockSpec((tm, w), lambda i: (i, 0))
    const = lambda shape: pl.BlockSpec(shape, lambda i: (0,) * len(shape))
    return pl.pallas_call(
        _merge_kernel,
        grid=(m // tm,),
        in_specs=[row(512), row(512), row(512), row(3 * d), row(d),
                  const((512, d)), const((512, d)), const((512, d)), const((d, d))],
        out_specs=row(d),
        out_shape=jax.ShapeDtypeStruct((m, d), F32),
        compiler_params=_cparams(1),
    )(o_nsa, o_gla, o_x, mg, x, wn, wg, wx, wo)


def _ffn_seq_kernel(x_ref, past_ref, g_ref, wup_ref, cw_ref, cb_ref, wdn_ref, y_ref, tail_ref, carry_ref):
    f = cw_ref.shape[-1]
    tm = x_ref.shape[1]

    @pl.when(pl.program_id(1) == 0)
    def _():
        carry_ref[...] = jnp.zeros(carry_ref.shape, F32)
        carry_ref[6:8, :] = past_ref[0]

    x1 = x_ref[0]
    ug = _dot(_rms(x1, g_ref[...]).astype(BF16), wup_ref[...])
    u = ug[:, 0:f]
    g = ug[:, f:2 * f]
    row = lax.broadcasted_iota(jnp.int32, (tm, f), 0)
    p1 = carry_ref[7:8, :]
    p2 = carry_ref[6:7, :]
    gm1 = jnp.where(row == 0, p1, pltpu.roll(g, 1, 0))
    gm2 = jnp.where(row == 0, p2, jnp.where(row == 1, p1, pltpu.roll(g, 2, 0)))
    gc = cb_ref[...] + cw_ref[0:1, :] * gm2 + cw_ref[1:2, :] * gm1 + cw_ref[2:3, :] * g
    y_ref[0] = x1 + _dot((_gelu(gc) * u).astype(BF16), wdn_ref[...])
    carry_ref[...] = g[tm - 8:tm, :]
    tail_ref[0] = g[tm - 8:tm, :]


def _ffn_seq(x1, conv_past, g_ffn, w_up, conv_w, conv_b, w_down):
    b, t, d = x1.shape
    f = conv_w.shape[-1]
    tm = min(256, t)
    assert t % tm == 0 and tm >= 8
    const = lambda shape: pl.BlockSpec(shape, lambda i, j: (0,) * len(shape))
    return pl.pallas_call(
        _ffn_seq_kernel,
        grid=(b, t // tm),
        in_specs=[pl.BlockSpec((1, tm, d), lambda i, j: (i, j, 0)),
                  pl.BlockSpec((1, 2, f), lambda i, j: (i, 0, 0)),
                  const((1, d)), const((d, 2 * f)), const((3, f)), const((1, f)), const((f, d))],
        out_specs=[pl.BlockSpec((1, tm, d), lambda i, j: (i, j, 0)),
                   pl.BlockSpec((1, 8, f), lambda i, j: (i, 0, 0))],
        out_shape=[jax.ShapeDtypeStruct((b, t, d), F32), jax.ShapeDtypeStruct((b, 8, f), F32)],
        scratch_shapes=[pltpu.VMEM((8, f), F32)],
        compiler_params=_cparams(2),
    )(x1, conv_past, g_ffn, w_up, conv_w, conv_b, w_down)


def _ffn_step_kernel(x_ref, p0_ref, p1_ref, g_ref, wup_ref, cw_ref, cb_ref, wdn_ref, y_ref, gnew_ref):
    f = cw_ref.shape[-1]
    x1 = x_ref[...]
    ug = _dot(_rms(x1, g_ref[...]).astype(BF16), wup_ref[...])
    u = ug[:, 0:f]
    g = ug[:, f:2 * f]
    gc = cb_ref[...] + cw_ref[0:1, :] * p0_ref[...] + cw_ref[1:2, :] * p1_ref[...] + cw_ref[2:3, :] * g
    y_ref[...] = x1 + _dot((_gelu(gc) * u).astype(BF16), wdn_ref[...])
    gnew_ref[...] = g


def _ffn_step(x1, p0, p1, g_ffn, w_up, conv_w, conv_b, w_down):
    m, d = x1.shape
    f = conv_w.shape[-1]
    full = lambda shape: pl.BlockSpec(shape, lambda i: (0,) * len(shape))
    return pl.pallas_call(
        _ffn_step_kernel,
        grid=(1,),
        in_specs=[full((m, d)), full((m, f)), full((m, f)), full((1, d)), full((d, 2 * f)), full((3, f)),
                  full((1, f)), full((f, d))],
        out_specs=[full((m, d)), full((m, f))],
        out_shape=[jax.ShapeDtypeStruct((m, d), F32), jax.ShapeDtypeStruct((m, f), F32)],
        compiler_params=_cparams(1),
    )(x1, p0, p1, g_ffn, w_up, conv_w, conv_b, w_down)


def _decode_kernel(pt_ref, *refs, top, n_sel, n_pages):
    del pt_ref
    page_refs = refs[:n_pages]
    (q8_ref, new_ref, neww_ref, cwin_ref, gt_ref, bc_ref, bs_ref, bw_ref, b0_ref, ovt_ref, et_ref, pe_ref, w1_ref,
     w2_ref, gk0_ref, p64_ref, o_ref, wout_ref, xs_ref) = refs[n_pages:]
    page = page_refs[0].shape[1]
    length = n_pages * page
    n_chunks = length // CMP_STRIDE
    for u, pg in enumerate(page_refs):
        for c in range(2):
            xs_ref[c, page * u:page * (u + 1), :] = pg[0, :, 128 * c:128 * (c + 1)]

    def load_j(j):
        return jnp.concatenate([xs_ref[c, pl.ds(j, n_chunks, stride=CMP_STRIDE), :] for c in range(2)], axis=1)

    q8 = q8_ref[0]
    q8f = q8.astype(F32)
    rowk = lax.broadcasted_iota(jnp.int32, (8, 128), 0) // NSA_GROUP
    lane_half = lax.broadcasted_iota(jnp.int32, (8, 128), 1) // HEAD_DIM

    def half_mask(x):
        return jnp.where(rowk == lane_half, x, 0.0)

    kc, vc = _compress_core(load_j, n_chunks, pe_ref, w1_ref, w2_ref, gk0_ref, p64_ref)
    kc = kc.astype(BF16)
    vc = vc.astype(BF16)

    p_c = _masked_softmax(_dot_nt(q8, kc) + bc_ref[...], 1).astype(BF16)
    o_c = half_mask(_dot(p_c, vc))
    pf = p_c.astype(F32)
    p2 = jnp.concatenate([jnp.sum(pf[0:4], axis=0, keepdims=True), jnp.sum(pf[4:8], axis=0, keepdims=True),
                          jnp.zeros((126, n_chunks), F32)], axis=0)
    p2_hi = p2.astype(BF16)
    p2_lo = (p2 - p2_hi.astype(F32)).astype(BF16)
    imp_t = _dot_nt(ovt_ref[...], p2_hi) + _dot_nt(ovt_ref[...], p2_lo)
    nsp = imp_t.shape[0]
    jj = lax.broadcasted_iota(jnp.int32, (nsp, 128), 0)
    tblk = length // SEL_BLOCK
    forced = (jj == 0) | (jj == tblk) | (jj == tblk - 1)
    sel_t = _select_with_forced(jnp.where(jj < n_sel, imp_t, -2e38), forced, top, 0)
    pen = jnp.where(sel_t, 0.0, SEL_PENALTY).T
    pen8 = jnp.concatenate([jnp.broadcast_to(pen[0:1], (4, nsp)), jnp.broadcast_to(pen[1:2], (4, nsp))], axis=0)

    new = new_ref[0]
    neww = neww_ref[0]
    b0 = b0_ref[...]

    def attend(s_parts, s_new, v_parts, v_new):
        m = s_new
        for s in s_parts:
            m = jnp.maximum(m, jnp.max(s, axis=-1, keepdims=True))
        pn = jnp.exp(s_new - m)
        l = pn
        o = pn.astype(BF16).astype(F32) * v_new.astype(BF16).astype(F32)
        for s, v in zip(s_parts, v_parts):
            pp = jnp.exp(s - m)
            l = l + jnp.sum(pp, axis=-1, keepdims=True)
            o = o + _dot(pp.astype(BF16), v)
        return half_mask(o / l)

    def new_score(k_new):
        return jnp.sum(q8f * k_new.astype(BF16).astype(F32), axis=-1, keepdims=True) + b0

    k_past = jnp.concatenate([pg[0, :, 256:384] for pg in page_refs], axis=0).astype(BF16)
    v_past = jnp.concatenate([pg[0, :, 384:512] for pg in page_refs], axis=0).astype(BF16)
    s_past = _dot_nt(q8, k_past) + _dot_nt(pen8[:, 0:128].astype(BF16), et_ref[...]) + bs_ref[...]
    lane = lax.broadcasted_iota(jnp.int32, (8, nsp), 1)
    pen_new = jnp.sum(jnp.where(lane == tblk, pen8, 0.0), axis=-1, keepdims=True)
    o_s = attend([s_past], new_score(new[:, 256:384]) + pen_new, [v_past], new[:, 384:512])

    cw = cwin_ref[0]
    s_w = _dot_nt(q8, cw[:, 0:128].astype(BF16)) + bw_ref[...]
    o_w = attend([s_w], new_score(neww[:, 0:128]), [cw[:, 128:256].astype(BF16)], neww[:, 128:256])

    gt = gt_ref[0]
    o_ref[0] = gt[:, 0:1] * o_c + gt[:, 1:2] * o_s + gt[:, 2:3] * o_w

    wl = cw.shape[0]
    wrow = lax.broadcasted_iota(jnp.int32, cw.shape, 0)
    wout_ref[0] = jnp.where(wrow == wl - 1, neww, pltpu.roll(cw, wl - 1, 0))


def _nsa_decode(page_table, cache2d, q8, new_rows, new_win, cache_win, gates8, bc, bs, bw, b0, ov, et,
                pe, w1, w2, gk0, p64):
    db, n_pages = page_table.shape
    page = cache2d.shape[1]
    length = n_pages * page
    n_chunks = length // CMP_STRIDE
    n_sel = -(-(length + 1) // SEL_BLOCK)
    top = min(SEL_TOP, n_sel)
    wl = cache_win.shape[1]
    nsp = ov.shape[0]
    const = lambda shape: pl.BlockSpec(shape, lambda i, pt: (0,) * len(shape))
    per_b = lambda shape: pl.BlockSpec((1,) + shape, lambda i, pt: (i,) + (0,) * len(shape))
    page_spec = lambda u: pl.BlockSpec((1, page, 512), lambda i, pt: (pt[i, u], 0, 0))
    grid_spec = pltpu.PrefetchScalarGridSpec(
        num_scalar_prefetch=1,
        grid=(db,),
        in_specs=[page_spec(u) for u in range(n_pages)] + [
                  per_b((8, 128)), per_b((1, 512)), per_b((1, 256)), per_b((wl, 256)), per_b((8, 128)),
                  const((8, n_chunks)), const((8, length)), const((8, wl)), const((8, 1)),
                  const((nsp, n_chunks)), const((length, 128)),
                  const((2, CMP_STRIDE, 1, 256)), const((2, CMP_STRIDE * 256, 256)), const((256, 256)),
                  const((1, 128)), const((512, 512))],
        out_specs=[per_b((8, 128)), per_b((wl, 256))],
        scratch_shapes=[pltpu.VMEM((2, length, 128), F32)],
    )
    return pl.pallas_call(
        functools.partial(_decode_kernel, top=top, n_sel=n_sel, n_pages=n_pages),
        grid_spec=grid_spec,
        out_shape=[jax.ShapeDtypeStruct((db, 8, 128), F32), jax.ShapeDtypeStruct((db, wl, 256), F32)],
        compiler_params=_cparams(1),
    )(page_table, *([cache2d] * n_pages), q8, new_rows, new_win, cache_win, gates8, bc, bs, bw, b0, ov, et,
      pe, w1, w2, gk0, p64)


def _bucket_table():
    n = np.arange(MAX_DISTANCE + 1)
    max_exact = N_BUCKETS // 2
    nf = np.maximum(n, 1).astype(np.float32)
    large = max_exact + (np.log(nf / np.float32(max_exact)) / np.float32(math.log(MAX_DISTANCE / max_exact))
                         * np.float32(N_BUCKETS - max_exact)).astype(np.int32)
    return np.where(n < max_exact, n, np.minimum(large, N_BUCKETS - 1)).astype(np.int32)


def _bias_lookup(tb, rel, valid):
    idx = np.clip(rel, 0, MAX_DISTANCE)
    vals = jnp.moveaxis(tb[idx], -1, 0)
    return jnp.where(jnp.asarray(valid)[None], vals, NEG)


def _overlap(n_cmp_pad, n_cmp, n_sel_pad, n_sel):
    cs = (np.arange(n_cmp_pad) * CMP_STRIDE)[:, None]
    ss = (np.arange(n_sel_pad) * SEL_BLOCK)[None, :]
    ov = (cs < ss + SEL_BLOCK) & (cs + CMP_BLOCK > ss)
    ov &= (np.arange(n_cmp_pad) < n_cmp)[:, None] & (np.arange(n_sel_pad) < n_sel)[None, :]
    return jnp.asarray(ov.astype(np.float32), dtype=BF16)


def _stack_rows(x):
    return x.reshape(NSA_KV, NSA_GROUP * Q_BLOCK, x.shape[-1])


def _toeplitz(tbr, shift, width, max_valid):
    n = width + Q_BLOCK - 1
    u = np.arange(n)
    xs = shift - np.where(u < width, u, u - n)
    fvec = _bias_lookup(tbr, xs, (xs >= 0) & (xs <= max_valid))
    h = fvec.shape[0]
    return jnp.tile(fvec, (1, Q_BLOCK))[:, :Q_BLOCK * (n - 1)].reshape(h, Q_BLOCK, n - 1)[:, :, :width]


def _bias_descending(tb, top):
    far = jnp.broadcast_to(tb[MAX_DISTANCE][:, None], (tb.shape[1], top - MAX_DISTANCE + 1))
    return jnp.concatenate([far, tb[MAX_DISTANCE - 1:0:-1].T], axis=1)


def _prompt_tables(tb):
    tbr = (tb - tb[MAX_DISTANCE][None, :]) * LOG2E
    i = np.arange(Q_BLOCK)[:, None]

    def table(rel, valid):
        return _stack_rows(_bias_lookup(tbr, rel, valid))

    n_r = 4
    m_sel = _stack_rows(_toeplitz(tbr, Q_BLOCK * (n_r - 1), NEAR_TILE + Q_BLOCK * (n_r - 1), 1 << 30))
    tsel = jnp.stack([m_sel[:, :, Q_BLOCK * (n_r - 1 - r):Q_BLOCK * (n_r - 1 - r) + NEAR_TILE] for r in range(n_r)],
                     axis=1)
    n_v = WINDOW // Q_BLOCK + 1
    m_win = _stack_rows(_toeplitz(tbr, Q_BLOCK * (n_v - 1), WIN_TILE + Q_BLOCK * (n_v - 1), WINDOW - 1))
    twin = jnp.stack([m_win[:, :, Q_BLOCK * (n_v - 1 - v):Q_BLOCK * (n_v - 1 - v) + WIN_TILE] for v in range(n_v)],
                     axis=1)
    w = np.arange(16)[None, :] - 9
    rel = i - CMP_STRIDE * w - (CMP_BLOCK - 1)
    a = table(rel, rel >= 0)
    hi = a.astype(BF16)
    lo = (a - hi.astype(F32)).astype(BF16)
    acmp = jnp.concatenate([hi, lo], axis=-1)
    return jnp.swapaxes(tsel, 2, 3), jnp.swapaxes(twin, 2, 3), jnp.swapaxes(acmp, 1, 2)


def kernel(x_prompt, x_sample, cache_kv, cache_win, state_gla, state_conv, cache_mem, page_table, mem_prompt,
           g_mix, w_in, g_nsa_q, g_nsa_k, cmp_k_pe, cmp_k_w1, cmp_k_w2, cmp_v_pe, cmp_v_w1, cmp_v_w2,
           rel_bias, w_gla_gate, b_gla_gate, g_gla_o, g_mem, w_mem_kv, g_x_q, g_x_k,
           w_nsa_out, w_gla_out, w_x_out, w_o, g_ffn, w_up, conv_w, conv_b, w_down):
    bp, t, d = x_prompt.shape
    db = x_sample.shape[0]
    f = conv_w.shape[-1]

    offs = np.cumsum((0,) + IN_SIZES)
    segs = [w_in[:, offs[i]:offs[i + 1]] for i in range(len(IN_SIZES))]
    w_pad = jnp.concatenate([jnp.pad(s, ((0, 0), (0, pw - s.shape[1]))) for s, pw in zip(segs, PAD_SIZES)],
                            axis=1).astype(BF16)
    row = lambda v: v.reshape(1, -1).astype(F32)
    gq = row(jnp.tile(g_nsa_q, NSA_HEADS))
    gk0 = row(jnp.tile(g_nsa_k[0], NSA_KV))
    gk1 = row(jnp.tile(g_nsa_k[1], NSA_KV))
    gk2 = row(jnp.tile(g_nsa_k[2], NSA_KV))
    gxq = row(jnp.tile(g_x_q, X_HEADS))
    gxk = row(jnp.tile(g_x_k, X_HEADS))
    p64 = _block_ones(512, HEAD_DIM)
    p128 = _block_ones(512, X_DIM)
    bd2 = lambda a: jnp.concatenate([jnp.concatenate([a, jnp.zeros_like(a)], -1),
                                     jnp.concatenate([jnp.zeros_like(a), a], -1)], -2)
    bd4 = lambda a, c: jnp.concatenate([jnp.concatenate([bd2(a), jnp.zeros_like(bd2(a))], -1),
                                        jnp.concatenate([jnp.zeros_like(bd2(c)), bd2(c)], -1)], -2)
    pe = jnp.concatenate([jnp.tile(cmp_k_pe, (1, NSA_KV)), jnp.tile(cmp_v_pe, (1, NSA_KV))],
                         axis=-1).reshape(2, CMP_STRIDE, 1, 256)
    w1 = bd4(cmp_k_w1, cmp_v_w1).reshape(2, CMP_STRIDE * 256, 256).astype(BF16)
    w2 = bd4(cmp_k_w2, cmp_v_w2).astype(BF16)
    wg_pad = jnp.pad(w_gla_gate, ((0, 128 - GLA_RANK), (0, 0))).astype(BF16)
    tri = jnp.asarray(np.tril(np.ones((GLA_CHUNK, GLA_CHUNK), np.float32)), dtype=BF16)
    tb = rel_bias.astype(F32)[_bucket_table()]
    wn, wgo, wx, wo = (w.astype(BF16) for w in (w_nsa_out, w_gla_out, w_x_out, w_o))
    wup = w_up.astype(BF16)
    wdn = w_down.astype(BF16)
    ggo = row(g_gla_o)

    rows_p, win_p, _, gates, gla_in, xq, mg, ka, kw, vst, vwt, qt = _proj_in(
        x_prompt, row(g_mix), w_pad, gq, gk1, gk2, gxq, p64, p128, True)
    memkv_p = _memory_kv(mem_prompt, row(g_mem), w_mem_kv.astype(BF16), gxk, p128)
    kc, vct = _compress(rows_p, pe, w1, w2, gk0, p64)
    n_chunks = t // CMP_STRIDE
    n_sel = -(-t // SEL_BLOCK)
    tsel, twin, acmp = _prompt_tables(tb)
    ovt = _overlap(n_chunks, n_chunks - 1, 128, n_sel).T
    o_nsa = _nsa_prompt(qt, kc, vct, ka, kw, vst, vwt, gates, tsel, twin, acmp, ovt)
    s0t = jnp.zeros((bp, GLA_HEADS, GLA_DV, GLA_DK), F32)
    o_gla, st = _gla_prompt(gla_in, s0t, wg_pad, row(b_gla_gate), ggo, tri)
    o_x = _xatt(xq, memkv_p)
    m = bp * t
    x1 = _merge(o_nsa.reshape(m, 512), o_gla.reshape(m, 512), o_x.reshape(m, 512), mg.reshape(m, 3 * d),
                x_prompt.reshape(m, d), wn, wgo, wx, wo)
    y_p, tail = _ffn_seq(x1.reshape(bp, t, d), jnp.zeros((bp, 2, f), F32), row(g_ffn), wup, conv_w, row(conv_b), wdn)
    wl_p = min(WINDOW, t)
    out_rows_p = rows_p.reshape(bp, t, 4, NSA_KV, HEAD_DIM)
    out_win_p = win_p[:, t - wl_p:].reshape(bp, wl_p, 2, NSA_KV, HEAD_DIM)
    out_gla_p = jnp.swapaxes(st, 2, 3)
    out_conv_p = tail[:, 6:8]
    out_mem_p = memkv_p.reshape(bp, -1, 2, X_HEADS, X_DIM)

    n_pages = page_table.shape[1]
    page = cache_kv.shape[1]
    length = n_pages * page
    wl = cache_win.shape[1]
    rows_s, win_s, qn_s, gates_s, gla_s, xq_s, mg_s = (a[0] for a in _proj_in(
        x_sample.reshape(1, db, d), row(g_mix), w_pad, gq, gk1, gk2, gxq, p64, p128, False))

    eye = jnp.eye(NSA_KV, dtype=BF16)
    q8 = (qn_s.reshape(db, NSA_KV, NSA_GROUP, 1, HEAD_DIM) * eye[None, :, None, :, None]).reshape(db, 8, 128)
    gates8 = jnp.pad(gates_s[:, 0:24].reshape(db, 8, 3), ((0, 0), (0, 0), (0, 125)))
    n_chunks_s = length // CMP_STRIDE
    n_sel_s = -(-(length + 1) // SEL_BLOCK)
    nsp = -(-n_sel_s // 128) * 128
    cidx = np.arange(n_chunks_s)
    rel_c = length - (cidx * CMP_STRIDE + CMP_BLOCK - 1)
    bc = _bias_lookup(tb, rel_c, (rel_c >= 0) & (cidx < n_chunks_s - 1))
    bs = _bias_descending(tb, length)
    bw = jnp.where(jnp.asarray(np.arange(wl, 0, -1) < WINDOW)[None], _bias_descending(tb, wl), NEG)
    b0 = tb[0].reshape(8, 1)
    ov_s = _overlap(n_chunks_s, n_chunks_s - 1, nsp, n_sel_s).T
    assert length // SEL_BLOCK <= 128
    et = jnp.asarray((np.arange(length)[:, None] // SEL_BLOCK == np.arange(128)[None, :]).astype(np.float32),
                     dtype=BF16)
    o8, win_new = _nsa_decode(page_table, cache_kv.reshape(cache_kv.shape[0], page, 512), q8,
                              rows_s.reshape(db, 1, 512), win_s.reshape(db, 1, 256), cache_win.reshape(db, wl, 256),
                              gates8, bc, bs, bw, b0, ov_s, et, pe, w1, w2, gk0, p64)
    o8 = o8.reshape(db, NSA_KV, NSA_GROUP, NSA_KV, HEAD_DIM)
    o_nsa_s = jnp.stack([o8[:, 0, :, 0], o8[:, 1, :, 1]], axis=1).reshape(db, 512).astype(BF16)

    wgt = jnp.pad(w_gla_gate.T, ((0, 0), (0, 128 - GLA_RANK))).reshape(GLA_HEADS, GLA_DK, 128)
    o_gla_s, gla_state_s = _gla_step(
        gla_s[:, 0:256].reshape(db, GLA_HEADS, GLA_DK, 1), gla_s[:, 256:512].reshape(db, GLA_HEADS, GLA_DK, 1),
        gla_s[:, 1024:1152].reshape(db, 1, 128), gla_s[:, 512:1024].reshape(db, GLA_HEADS, 1, GLA_DV),
        gla_s[:, 1152:1664].reshape(db, GLA_HEADS, 1, GLA_DV), state_gla.astype(F32), wgt,
        b_gla_gate.reshape(GLA_HEADS, GLA_DK, 1), ggo)
    o_gla_s = o_gla_s.reshape(db, 512).astype(BF16)

    xq_pad = jnp.pad(xq_s.reshape(db, 1, 512), ((0, 0), (0, 15), (0, 0)))
    o_x_s = _xatt(xq_pad, cache_mem.reshape(db, -1, 1024))[:, 0]
    x1_s = _merge(o_nsa_s, o_gla_s, o_x_s, mg_s, x_sample.reshape(db, d), wn, wgo, wx, wo)
    y_s, g_new = _ffn_step(x1_s, state_conv[:, 0], state_conv[:, 1], row(g_ffn), wup, conv_w, row(conv_b), wdn)

    out_rows_s = rows_s.reshape(db, 1, 4, NSA_KV, HEAD_DIM)
    out_win_s = win_new.reshape(db, wl, 2, NSA_KV, HEAD_DIM)
    out_conv_s = jnp.stack([state_conv[:, 1], g_new], axis=1)
    return (y_p, y_s.reshape(db, 1, d), out_rows_p, out_win_p, out_gla_p, out_conv_p, out_mem_p,
            out_rows_s, out_win_s, gla_state_s, out_conv_s)
```

```python
import functools
import math

import numpy as np
import jax
import jax.numpy as jnp
from jax import lax
from jax.experimental import pallas as pl
from jax.experimental.pallas import tpu as pltpu

F32 = jnp.float32
BF16 = jnp.bfloat16

NSA_HEADS = 8
NSA_KV = 2
NSA_GROUP = 4
HEAD_DIM = 64
CMP_BLOCK = 32
CMP_STRIDE = 16
SEL_BLOCK = 64
SEL_TOP = 16
WINDOW = 512
Q_BLOCK = 128
GLA_HEADS = 4
GLA_DK = 64
GLA_DV = 128
GLA_RANK = 16
GLA_TAU = 16.0
GLA_CHUNK = 64
X_HEADS = 4
X_DIM = 128
N_BUCKETS = 32
MAX_DISTANCE = 128
EPS = 1e-6
LOG2E = math.log2(math.e)
NEG = -1e30
TINY = 1e-30
SEL_PENALTY = -1e9
MASKED_BELOW = -5e29
EXP_CLAMP = 80.0

IN_SIZES = (512, 768, 24, 256, 256, 512, 16, 512, 512, 3072)
PAD_SIZES = (512, 768, 128, 256, 256, 512, 128, 512, 512, 3072)
PAD_OFFS = tuple(int(v) for v in np.cumsum((0,) + PAD_SIZES))
D_IN_PAD = PAD_OFFS[-1]
GLA_IN_W = 256 + 256 + 512 + 128 + 512

VMEM_LIMIT = 56 * 1024 * 1024
FAR_TILE = 256
NEAR_TILE = 512
WIN_TILE = WINDOW + Q_BLOCK


def _cparams(n_axes):
    return pltpu.CompilerParams(dimension_semantics=("arbitrary",) * n_axes, vmem_limit_bytes=VMEM_LIMIT)


def _dot(a, b):
    return jnp.dot(a, b, preferred_element_type=F32)


def _dot_nt(a, b):
    return lax.dot_general(a, b, (((1,), (1,)), ((), ())), preferred_element_type=F32)


def _dot_tn(a, b):
    return lax.dot_general(a, b, (((0,), (0,)), ((), ())), preferred_element_type=F32)


def _split_dot(x, m):
    hi = x.astype(BF16)
    lo = (x - hi.astype(F32)).astype(BF16)
    return _dot(hi, m) + _dot(lo, m)


def _rms(x, g):
    return x * lax.rsqrt(jnp.mean(x * x, axis=-1, keepdims=True) + EPS) * g


def _group_rms(x, pmat, gsize, g):
    ss = _split_dot(x * x, pmat)
    return x * lax.rsqrt(ss * (1.0 / gsize) + EPS) * g


def _gelu(x):
    return 0.5 * x * (1.0 + jnp.tanh(math.sqrt(2.0 / math.pi) * (x + 0.044715 * (x * x * x))))


def _sigmoid(x):
    return 1.0 / (1.0 + jnp.exp(-x))


def _block_ones(n, gsize):
    i = np.arange(n) // gsize
    return jnp.asarray((i[:, None] == i[None, :]).astype(np.float32), dtype=BF16)


def _proj_kernel(x_ref, gmix_ref, w_ref, gq_ref, gk1_ref, gk2_ref, gxq_ref, p64_ref, p128_ref,
                 rows_ref, win_ref, qn_ref, gates_ref, gla_ref, xq_ref, mg_ref, *attn_refs):
    x = x_ref[0]
    h = _rms(x, gmix_ref[...]).astype(BF16)
    o = PAD_OFFS

    def seg(i):
        return _dot(h, w_ref[:, o[i]:o[i + 1]])

    p64 = p64_ref[...]
    p64s = p64_ref[0:128, 0:128]
    qn = _group_rms(seg(0), p64, HEAD_DIM, gq_ref[...]) * (HEAD_DIM ** -0.5)
    qn_ref[0] = qn.astype(BF16)
    if attn_refs:
        qt_ref = attn_refs[4]
        for u in range(x.shape[0] // Q_BLOCK):
            qt_ref[0, u] = (qn[Q_BLOCK * u:Q_BLOCK * (u + 1)] * LOG2E).T.astype(BF16)

    kv = seg(1)
    k_sel = _group_rms(kv[:, 256:384], p64s, HEAD_DIM, gk1_ref[...])
    k_win = _group_rms(kv[:, 512:640], p64s, HEAD_DIM, gk2_ref[...])
    rows_ref[0, :, 0:256] = kv[:, 0:256]
    rows_ref[0, :, 256:384] = k_sel
    rows_ref[0, :, 384:512] = kv[:, 384:512]
    win_ref[0, :, 0:128] = k_win
    win_ref[0, :, 128:256] = kv[:, 640:768]
    if attn_refs:
        ka_ref, kw_ref, vst_ref, vwt_ref, _ = attn_refs
        tm = x.shape[0]
        tpos = pl.program_id(1) * tm + lax.broadcasted_iota(jnp.int32, (tm, 128), 0)
        blk = lax.broadcasted_iota(jnp.int32, (tm, 128), 1)
        ka_ref[0, :, 0:128] = k_sel.astype(BF16)
        ka_ref[0, :, 128:256] = jnp.where(tpos // SEL_BLOCK == blk, 1.0, 0.0).astype(BF16)
        kw_ref[0] = k_win.astype(BF16)
        for u in range(tm // FAR_TILE):
            vst_ref[0, u] = kv[FAR_TILE * u:FAR_TILE * (u + 1), 384:512].T.astype(BF16)
        for u in range(tm // Q_BLOCK):
            vwt_ref[0, u] = kv[Q_BLOCK * u:Q_BLOCK * (u + 1), 640:768].T.astype(BF16)

    gates_ref[0] = _sigmoid(seg(2))
    gla_ref[0, :, 0:256] = seg(3) * (GLA_DK ** -0.5)
    gla_ref[0, :, 256:512] = seg(4)
    gla_ref[0, :, 512:1024] = seg(5)
    gla_ref[0, :, 1024:1152] = seg(6)
    gla_ref[0, :, 1152:1664] = seg(7)
    xq = _group_rms(seg(8), p128_ref[...], X_DIM, gxq_ref[...]) * (X_DIM ** -0.5)
    xq_ref[0] = xq.astype(BF16)
    mg_ref[0] = _sigmoid(seg(9)).astype(BF16)


def _proj_in(x, g_mix, w_pad, gq, gk1, gk2, gxq, p64, p128, attn_layout):
    b, t, d = x.shape
    tm = min(512, t)
    assert t % tm == 0
    widths = [512, 256, 512, 128, GLA_IN_W, 512, 3072]
    dtypes = [F32, F32, BF16, F32, F32, BF16, BF16]
    out_specs = [pl.BlockSpec((1, tm, w), lambda i, j: (i, j, 0)) for w in widths]
    out_shape = [jax.ShapeDtypeStruct((b, t, w), dt) for w, dt in zip(widths, dtypes)]
    if attn_layout:
        assert tm % Q_BLOCK == 0
        for w in (256, 128):
            out_specs.append(pl.BlockSpec((1, tm, w), lambda i, j: (i, j, 0)))
            out_shape.append(jax.ShapeDtypeStruct((b, t, w), BF16))
        assert tm % FAR_TILE == 0
        for rows, width in ((128, FAR_TILE), (128, Q_BLOCK), (512, Q_BLOCK)):
            out_specs.append(pl.BlockSpec((1, tm // width, rows, width), lambda i, j: (i, j, 0, 0)))
            out_shape.append(jax.ShapeDtypeStruct((b, t // width, rows, width), BF16))
    const = lambda shape: pl.BlockSpec(shape, lambda i, j: (0,) * len(shape), pipeline_mode=pl.Buffered(1))
    return pl.pallas_call(
        _proj_kernel,
        grid=(b, t // tm),
        in_specs=[pl.BlockSpec((1, tm, d), lambda i, j: (i, j, 0)),
                  const((1, d)), const((d, D_IN_PAD)), const((1, 512)), const((1, 128)), const((1, 128)),
                  const((1, 512)), const((512, 512)), const((512, 512))],
        out_specs=out_specs,
        out_shape=out_shape,
        compiler_params=_cparams(2),
    )(x, g_mix, w_pad, gq, gk1, gk2, gxq, p64, p128)


def _memkv_kernel(m_ref, g_ref, w_ref, gk_ref, p128_ref, o_ref):
    h = _rms(m_ref[0], g_ref[...]).astype(BF16)
    kv = _dot(h, w_ref[...])
    o_ref[0, :, 0:512] = _group_rms(kv[:, 0:512], p128_ref[...], X_DIM, gk_ref[...])
    o_ref[0, :, 512:1024] = kv[:, 512:1024]


def _memory_kv(mem, g_mem, w_mem, gxk, p128):
    b, m, d = mem.shape
    const = lambda shape: pl.BlockSpec(shape, lambda i: (0,) * len(shape))
    return pl.pallas_call(
        _memkv_kernel,
        grid=(b,),
        in_specs=[pl.BlockSpec((1, m, d), lambda i: (i, 0, 0)), const((1, d)), const((d, 1024)),
                  const((1, 512)), const((512, 512))],
        out_specs=pl.BlockSpec((1, m, 1024), lambda i: (i, 0, 0)),
        out_shape=jax.ShapeDtypeStruct((b, m, 1024), F32),
        compiler_params=_cparams(1),
    )(mem, g_mem, w_mem, gxk, p128)


def _compress_core(load_j, n_chunks, pe_ref, w1_ref, w2_ref, gk0_ref, p64_ref):
    xs = [load_j(j) for j in range(CMP_STRIDE)]
    halves = [_dot(jnp.concatenate([(x + pe_ref[r, j]).astype(BF16) for j, x in enumerate(xs)], axis=1), w1_ref[r])
              for r in range(2)]
    hid = halves[0] + pltpu.roll(halves[1], n_chunks - 1, 0)
    out = _dot(_gelu(hid).astype(BF16), w2_ref[...])
    row = lax.broadcasted_iota(jnp.int32, (n_chunks, 128), 0)
    live = row < n_chunks - 1
    kc = _group_rms(out[:, 0:128], p64_ref[0:128, 0:128], HEAD_DIM, gk0_ref[...])
    return jnp.where(live, kc, 0.0), jnp.where(live, out[:, 128:256], 0.0)


def _compress_kernel(rk_ref, rv_ref, pe_ref, w1_ref, w2_ref, gk0_ref, p64_ref, kc_ref, vc_ref, *, n_chunks):
    load_j = lambda j: jnp.concatenate([rk_ref[0, pl.ds(j, n_chunks, stride=CMP_STRIDE), :],
                                        rv_ref[0, pl.ds(j, n_chunks, stride=CMP_STRIDE), :]], axis=1)
    kc, vc = _compress_core(load_j, n_chunks, pe_ref, w1_ref, w2_ref, gk0_ref, p64_ref)
    kc_ref[0] = kc.astype(BF16)
    vc_ref[0] = vc.T.astype(BF16)


def _compress(rows, pe, w1, w2, gk0, p64):
    b, t, _ = rows.shape
    n_chunks = t // CMP_STRIDE
    const = lambda shape: pl.BlockSpec(shape, lambda i: (0,) * len(shape))
    return pl.pallas_call(
        functools.partial(_compress_kernel, n_chunks=n_chunks),
        grid=(b,),
        in_specs=[pl.BlockSpec((1, t, 128), lambda i: (i, 0, 0)), pl.BlockSpec((1, t, 128), lambda i: (i, 0, 1)),
                  const((2, CMP_STRIDE, 1, 256)), const((2, CMP_STRIDE * 256, 256)), const((256, 256)),
                  const((1, 128)), const((512, 512))],
        out_specs=[pl.BlockSpec((1, n_chunks, 128), lambda i: (i, 0, 0)),
                   pl.BlockSpec((1, 128, n_chunks), lambda i: (i, 0, 0))],
        out_shape=[jax.ShapeDtypeStruct((b, n_chunks, 128), BF16), jax.ShapeDtypeStruct((b, 128, n_chunks), BF16)],
        compiler_params=_cparams(1),
    )(rows, rows, pe, w1, w2, gk0, p64)


def _masked_softmax(s, axis, exp_fn=jnp.exp):
    m = jnp.maximum(jnp.max(s, axis=axis, keepdims=True), MASKED_BELOW)
    p = exp_fn(s - m)
    return p / jnp.maximum(jnp.sum(p, axis=axis, keepdims=True), TINY)


def _select_blocks(score, top, axis):
    pos = lax.broadcasted_iota(jnp.int32, score.shape, axis).astype(F32)
    sel = jnp.zeros(score.shape, jnp.bool_)
    for _ in range(top):
        mx = jnp.max(score, axis=axis, keepdims=True)
        idx = jnp.min(jnp.where(score == mx, pos, 1e9), axis=axis, keepdims=True)
        hit = pos == idx
        sel = jnp.logical_or(sel, hit)
        score = jnp.where(hit, -3e38, score)
    return sel


def _select_with_forced(imp, forced, top, axis):
    n_forced = 3
    assert top > n_forced
    return jnp.logical_or(forced, _select_blocks(jnp.where(forced, -3e38, imp), top - n_forced, axis))


def _values_times_probs(vt_tiles, p):
    out = None
    start = 0
    for vt in vt_tiles:
        part = _dot(vt, p[start:start + vt.shape[1]])
        start += vt.shape[1]
        out = part if out is None else out + part
    return out


def _softmax_stats_t(s_ref, bias, m_ref, l_ref):
    a_parts, p_parts = [], []
    for g in range(s_ref.shape[1] // Q_BLOCK):
        cs = slice(Q_BLOCK * g, Q_BLOCK * (g + 1))
        sg = s_ref[:, cs]
        if bias is not None:
            sg = sg + bias(cs)
        m_old = m_ref[:, cs]
        m_new = jnp.maximum(m_old, jnp.max(sg, axis=0, keepdims=True))
        alpha = jnp.exp2(m_old - m_new)
        p = jnp.exp2(sg - m_new)
        m_ref[:, cs] = m_new
        l_ref[:, cs] = alpha * l_ref[:, cs] + jnp.sum(p, axis=0, keepdims=True)
        a_parts.append(alpha)
        p_parts.append(p.astype(BF16))
    return jnp.concatenate(a_parts, axis=1), jnp.concatenate(p_parts, axis=1)


def _acc_update_t(acc_ref, alpha, vt_tiles, p):
    acc_ref[...] = alpha * acc_ref[...] + _values_times_probs(vt_tiles, p)


def _attn_kernel(qt_ref, kc_ref, vct_ref, ka_ref, vst_ref, kw_ref, vwt_ref, gt_ref, tsel_ref, twin_ref, acmp_ref,
                 ovt_ref, o_ref, sa_scr, sb_scr, m_scr, l_scr, acc_scr, pend_a, pend_p, *, top):
    k = pl.program_id(1)
    n = pl.program_id(2)
    cols = NSA_GROUP * Q_BLOCK
    vrow = pl.multiple_of(k * HEAD_DIM, HEAD_DIM)
    zero = jnp.zeros((HEAD_DIM, Q_BLOCK), BF16)
    parts = []
    for g in range(NSA_GROUP):
        piece = qt_ref[0, 0, HEAD_DIM * g:HEAD_DIM * (g + 1), :]
        parts.append(jnp.where(k == 0, jnp.concatenate([piece, zero], axis=0),
                               jnp.concatenate([zero, piece], axis=0)))
    qt = jnp.concatenate(parts, axis=1)

    nc = kc_ref.shape[1]
    blocks_per_q = Q_BLOCK // CMP_STRIDE
    c32 = lax.broadcasted_iota(jnp.int32, (nc, 32), 0)
    r32 = lax.broadcasted_iota(jnp.int32, (nc, 32), 1)
    onehot = jnp.where((c32 - blocks_per_q * n + 9) == (r32 & 15), 1.0, 0.0).astype(BF16)
    band = _dot(onehot, acmp_ref[0])
    cp = lax.broadcasted_iota(jnp.int32, (nc, cols), 0) - blocks_per_q * n + 9
    s_c = _dot(kc_ref[0], qt) + jnp.where(cp < 0, 0.0, jnp.where(cp > 15, NEG, band))
    p_c = _masked_softmax(s_c, 0, jnp.exp2).astype(BF16)
    o_c = _dot(vct_ref[0, pl.ds(vrow, HEAD_DIM), :], p_c)
    imp4 = _dot(ovt_ref[...], p_c)
    imp = imp4[:, 0:128] + imp4[:, 128:256] + imp4[:, 256:384] + imp4[:, 384:512]

    def value_tiles(ref, first, count):
        return [ref[0, first + d, pl.ds(vrow, HEAD_DIM), :] for d in range(count)]

    wt = jnp.maximum(n - WINDOW // Q_BLOCK, 0)
    s_w = _dot(kw_ref[0, pl.ds(pl.multiple_of(wt * Q_BLOCK, Q_BLOCK), WIN_TILE), :], qt) + twin_ref[0, 0]
    p_w = jnp.exp2(s_w - jnp.max(s_w, axis=0, keepdims=True))
    o_w = (_values_times_probs(value_tiles(vwt_ref, wt, WIN_TILE // Q_BLOCK), p_w.astype(BF16))
           / jnp.sum(p_w, axis=0, keepdims=True))

    jj = lax.broadcasted_iota(jnp.int32, (128, Q_BLOCK), 0)
    tpos = n * Q_BLOCK + lax.broadcasted_iota(jnp.int32, (128, Q_BLOCK), 1)
    tblk = tpos // SEL_BLOCK
    forced = (jj == 0) | (jj == tblk) | (jj == tblk - 1)
    sel_t = _select_with_forced(jnp.where(jj * SEL_BLOCK <= tpos, imp, NEG), forced, top, 0)
    pen_t = jnp.where(sel_t, 0.0, SEL_PENALTY).astype(BF16)
    rhs = jnp.concatenate([qt, jnp.concatenate([pen_t] * NSA_GROUP, axis=1)], axis=0)

    n_far = jnp.maximum(n - 2, 0) // 2
    r_near = n - 2 * n_far

    def issue_scores(tile, s_ref):
        s_ref[...] = _dot(ka_ref[0, pl.ds(pl.multiple_of(tile * FAR_TILE, FAR_TILE), FAR_TILE), :], rhs)

    def stats(s_ref, half_idx=None):
        bias = None
        if half_idx is not None:
            bias = lambda cs: tsel_ref[0, r_near, FAR_TILE * half_idx:FAR_TILE * (half_idx + 1), cs]
        return _softmax_stats_t(s_ref, bias, m_scr, l_scr)

    def accumulate(tile, alpha, p):
        _acc_update_t(acc_scr, alpha, value_tiles(vst_ref, tile, 1), p)

    m_scr[...] = jnp.full((1, cols), NEG, F32)
    l_scr[...] = jnp.zeros((1, cols), F32)
    acc_scr[...] = jnp.zeros((HEAD_DIM, cols), F32)
    pend_a[...] = jnp.ones((1, cols), F32)
    pend_p[...] = jnp.zeros((FAR_TILE, cols), BF16)
    odd = n_far % 2

    @pl.when(odd == 1)
    def _():
        issue_scores(0, sa_scr)
        accumulate(0, *stats(sa_scr))

    issue_scores(odd, sa_scr)

    @pl.loop(0, n_far // 2)
    def _(j):
        t0 = odd + 2 * j
        issue_scores(t0 + 1, sb_scr)
        alpha_a, p_a = stats(sa_scr)
        accumulate(jnp.maximum(t0 - 1, 0), pend_a[...], pend_p[...])
        issue_scores(t0 + 2, sa_scr)
        alpha_b, p_b = stats(sb_scr)
        accumulate(t0, alpha_a, p_a)
        pend_a[...] = alpha_b
        pend_p[...] = p_b

    accumulate(jnp.maximum(n_far - 1, 0), pend_a[...], pend_p[...])
    issue_scores(n_far + 1, sb_scr)
    accumulate(n_far, *stats(sa_scr, 0))
    accumulate(n_far + 1, *stats(sb_scr, 1))
    o_s = acc_scr[...] / l_scr[...]

    gtt = gt_ref[0].T

    def gate_row(branch):
        rows = [jnp.where(k == 0, gtt[3 * g + branch:3 * g + branch + 1],
                          gtt[12 + 3 * g + branch:12 + 3 * g + branch + 1]) for g in range(NSA_GROUP)]
        return jnp.concatenate(rows, axis=1)

    o_t = gate_row(0) * o_c + gate_row(1) * o_s + gate_row(2) * o_w
    left = jnp.concatenate([o_t[:, 0:128], o_t[:, 128:256]], axis=0).T
    right = jnp.concatenate([o_t[:, 256:384], o_t[:, 384:512]], axis=0).T
    o_ref[0] = jnp.concatenate([left, right], axis=1).astype(BF16)


def _nsa_prompt(qt, kc, vct, ka, kw, vst, vwt, gates, tsel, twin, acmp, ovt):
    b, t, _ = ka.shape
    nqb = t // Q_BLOCK
    nc = kc.shape[1]
    top = min(SEL_TOP, -(-t // SEL_BLOCK))
    assert t >= WIN_TILE and t % FAR_TILE == 0 and t // SEL_BLOCK <= 128
    n_win = WINDOW // Q_BLOCK
    per_b = lambda shape: pl.BlockSpec((1,) + shape, lambda i, k, n: (i,) + (0,) * len(shape))
    return pl.pallas_call(
        functools.partial(_attn_kernel, top=top),
        grid=(b, NSA_KV, nqb),
        in_specs=[pl.BlockSpec((1, 1, NSA_GROUP * HEAD_DIM, Q_BLOCK), lambda i, k, n: (i, n, k, 0)),
                  per_b((nc, 128)), per_b((128, nc)),
                  per_b((t, 256)), per_b((t // FAR_TILE, 128, FAR_TILE)), per_b((t, 128)), per_b((nqb, 128, Q_BLOCK)),
                  pl.BlockSpec((1, Q_BLOCK, 128), lambda i, k, n: (i, n, 0)),
                  pl.BlockSpec((1, 4, NEAR_TILE, 512), lambda i, k, n: (k, 0, 0, 0)),
                  pl.BlockSpec((1, 1, WIN_TILE, 512), lambda i, k, n: (k, jnp.minimum(n, n_win), 0, 0)),
                  pl.BlockSpec((1, 32, 512), lambda i, k, n: (k, 0, 0)),
                  pl.BlockSpec((128, nc), lambda i, k, n: (0, 0))],
        out_specs=pl.BlockSpec((1, Q_BLOCK, 256), lambda i, k, n: (i, n, k)),
        out_shape=jax.ShapeDtypeStruct((b, t, 512), BF16),
        scratch_shapes=[pltpu.VMEM((FAR_TILE, NSA_GROUP * Q_BLOCK), F32), pltpu.VMEM((FAR_TILE, NSA_GROUP * Q_BLOCK), F32),
                        pltpu.VMEM((1, NSA_GROUP * Q_BLOCK), F32), pltpu.VMEM((1, NSA_GROUP * Q_BLOCK), F32),
                        pltpu.VMEM((HEAD_DIM, NSA_GROUP * Q_BLOCK), F32),
                        pltpu.VMEM((1, NSA_GROUP * Q_BLOCK), F32), pltpu.VMEM((FAR_TILE, NSA_GROUP * Q_BLOCK), BF16)],
        compiler_params=_cparams(3),
    )(qt, kc, vct, ka, vst, kw, vwt, gates, tsel, twin, acmp, ovt)


def _log_sigmoid(z):
    return jnp.minimum(z, 0.0) - jnp.log1p(jnp.exp(-jnp.abs(z)))


def _gla_kernel(x_ref, s0_ref, wg_ref, bg_ref, ggo_ref, tri_ref, o_ref, st_ref, s_scr, *, n_chunks):
    @pl.when(pl.program_id(0) == 0)
    def _():
        s_scr[...] = s0_ref[...]

    c_len = GLA_CHUNK
    ct = n_chunks * c_len
    tri = tri_ref[...]
    ti = lax.broadcasted_iota(jnp.int32, (ct, ct), 0)
    si = lax.broadcasted_iota(jnp.int32, (ct, ct), 1)
    causal = (si <= ti) & (si // c_len == ti // c_len)

    def per_chunk(x, row):
        return jnp.concatenate([jnp.broadcast_to(x[c_len * c + row:c_len * c + row + 1], (c_len, x.shape[1]))
                                for c in range(n_chunks)], axis=0)

    for bi in range(x_ref.shape[0]):
        q = x_ref[bi, :, 0:256]
        kk = x_ref[bi, :, 256:512]
        v = x_ref[bi, :, 512:1024]
        lr = x_ref[bi, :, 1024:1152]
        r = x_ref[bi, :, 1152:1664]
        la = _log_sigmoid(_dot(lr.astype(BF16), wg_ref[...]) + bg_ref[...]) * (1.0 / GLA_TAU)
        a1 = la.astype(BF16)
        r1 = la - a1.astype(F32)
        a2 = r1.astype(BF16)
        a3 = (r1 - a2.astype(F32)).astype(BF16)
        cb = _dot(tri, a1) + _dot(tri, a2) + _dot(tri, a3)
        last = per_chunk(cb, c_len - 1)
        mid = per_chunk(cb, c_len // 2)
        qe = (q * jnp.exp(cb)).astype(BF16)
        qa = (q * jnp.exp(jnp.minimum(cb - mid, EXP_CLAMP))).astype(BF16)
        kb = (kk * jnp.exp(jnp.minimum(mid - cb, EXP_CLAMP))).astype(BF16)
        ke = (kk * jnp.exp(last - cb)).astype(BF16)
        for h in range(GLA_HEADS):
            ks = slice(GLA_DK * h, GLA_DK * (h + 1))
            vs = slice(GLA_DV * h, GLA_DV * (h + 1))
            att = jnp.where(causal, _dot_nt(qa[:, ks], kb[:, ks]), 0.0)
            vh = v[:, vs].astype(BF16)
            o_intra = _dot(att.astype(BF16), vh)
            st = s_scr[bi, h]
            o_inter = []
            for c in range(n_chunks):
                rs = slice(c_len * c, c_len * (c + 1))
                o_inter.append(_dot_nt(qe[rs, ks], st.astype(BF16)))
                dec = jnp.exp(cb[c_len * (c + 1) - 1:c_len * (c + 1), ks])
                st = st * dec + _dot_tn(vh[rs], ke[rs, ks])
            s_scr[bi, h] = st
            on = _rms(o_intra + jnp.concatenate(o_inter, axis=0), ggo_ref[...])
            rh = r[:, vs]
            o_ref[bi, :, vs] = (on * (rh * _sigmoid(rh))).astype(BF16)
    st_ref[...] = s_scr[...]


def _gla_prompt(gla_in, s0t, wg, bg, ggo):
    b, t, w = gla_in.shape
    ct = min(256, t)
    assert t % ct == 0 and ct % GLA_CHUNK == 0
    pos = np.arange(ct)
    tri = jnp.asarray(((pos[None, :] <= pos[:, None])
                       & (pos[None, :] // GLA_CHUNK == pos[:, None] // GLA_CHUNK)).astype(np.float32), dtype=BF16)
    const = lambda shape: pl.BlockSpec(shape, lambda j: (0,) * len(shape))
    state_shape = (b, GLA_HEADS, GLA_DV, GLA_DK)
    return pl.pallas_call(
        functools.partial(_gla_kernel, n_chunks=ct // GLA_CHUNK),
        grid=(t // ct,),
        in_specs=[pl.BlockSpec((b, ct, w), lambda j: (0, j, 0)), const(state_shape),
                  const((128, 256)), const((1, 256)), const((1, 128)), const((ct, ct))],
        out_specs=[pl.BlockSpec((b, ct, 512), lambda j: (0, j, 0)), const(state_shape)],
        out_shape=[jax.ShapeDtypeStruct((b, t, 512), BF16), jax.ShapeDtypeStruct(state_shape, F32)],
        scratch_shapes=[pltpu.VMEM(state_shape, F32)],
        compiler_params=_cparams(1),
    )(gla_in, s0t, wg, bg, ggo, tri)


def _gla_step_kernel(q_ref, k_ref, lr_ref, v_ref, r_ref, s_ref, wgt_ref, bgt_ref, ggo_ref, o_ref, sn_ref):
    lr = lr_ref[0]
    for h in range(GLA_HEADS):
        z = jnp.sum(wgt_ref[h] * lr, axis=-1, keepdims=True) + bgt_ref[h]
        a = jnp.exp(_log_sigmoid(z) * (1.0 / GLA_TAU))
        s0 = s_ref[0, h]
        kh = k_ref[0, h]
        qh = q_ref[0, h]
        vh = v_ref[0, h]
        sn_ref[0, h] = a * s0 + kh * vh
        o = jnp.sum((qh * a) * s0, axis=0, keepdims=True) + jnp.sum(qh * kh, axis=0, keepdims=True) * vh
        on = _rms(o, ggo_ref[...])
        rh = r_ref[0, h]
        o_ref[0, h] = on * (rh * _sigmoid(rh))


def _gla_step(q_col, k_col, lr, v_row, r_row, s0, wgt, bgt, ggo):
    b = q_col.shape[0]
    const = lambda shape: pl.BlockSpec(shape, lambda i: (0,) * len(shape))
    per_b = lambda shape: pl.BlockSpec((1,) + shape, lambda i: (i,) + (0,) * len(shape))
    return pl.pallas_call(
        _gla_step_kernel,
        grid=(b,),
        in_specs=[per_b((GLA_HEADS, GLA_DK, 1)), per_b((GLA_HEADS, GLA_DK, 1)), per_b((1, 128)),
                  per_b((GLA_HEADS, 1, GLA_DV)), per_b((GLA_HEADS, 1, GLA_DV)), per_b((GLA_HEADS, GLA_DK, GLA_DV)),
                  const((GLA_HEADS, GLA_DK, 128)), const((GLA_HEADS, GLA_DK, 1)), const((1, 128))],
        out_specs=[per_b((GLA_HEADS, 1, GLA_DV)), per_b((GLA_HEADS, GLA_DK, GLA_DV))],
        out_shape=[jax.ShapeDtypeStruct((b, GLA_HEADS, 1, GLA_DV), F32),
                   jax.ShapeDtypeStruct((b, GLA_HEADS, GLA_DK, GLA_DV), F32)],
        compiler_params=_cparams(1),
    )(q_col, k_col, lr, v_row, r_row, s0, wgt, bgt, ggo)


def _xatt_kernel(xq_ref, mem_ref, o_ref):
    for bi in range(xq_ref.shape[0]):
        for h in range(X_HEADS):
            ls = slice(X_DIM * h, X_DIM * (h + 1))
            kh = mem_ref[bi, :, ls].astype(BF16)
            vh = mem_ref[bi, :, 512 + X_DIM * h:512 + X_DIM * (h + 1)].astype(BF16)
            s = _dot_nt(xq_ref[bi, :, ls], kh)
            p = jnp.exp(s - jnp.max(s, axis=-1, keepdims=True))
            p = p / jnp.sum(p, axis=-1, keepdims=True)
            o_ref[bi, :, ls] = _dot(p.astype(BF16), vh).astype(BF16)


def _xatt(xq, memkv):
    b, t, _ = xq.shape
    m = memkv.shape[1]
    tq = min(512, t)
    assert t % tq == 0
    bb = math.gcd(b, max(1, 128 // tq))
    return pl.pallas_call(
        _xatt_kernel,
        grid=(b // bb, t // tq),
        in_specs=[pl.BlockSpec((bb, tq, 512), lambda i, j: (i, j, 0)),
                  pl.BlockSpec((bb, m, 1024), lambda i, j: (i, 0, 0))],
        out_specs=pl.BlockSpec((bb, tq, 512), lambda i, j: (i, j, 0)),
        out_shape=jax.ShapeDtypeStruct((b, t, 512), BF16),
        compiler_params=_cparams(2),
    )(xq, memkv)


def _merge_kernel(on_ref, og_ref, ox_ref, mg_ref, x_ref, wn_ref, wg_ref, wx_ref, wo_ref, x1_ref):
    d = x_ref.shape[-1]
    merged = (mg_ref[:, 0:d].astype(F32) * _dot(on_ref[...], wn_ref[...])
              + mg_ref[:, d:2 * d].astype(F32) * _dot(og_ref[...], wg_ref[...])
              + mg_ref[:, 2 * d:3 * d].astype(F32) * _dot(ox_ref[...], wx_ref[...]))
    x1_ref[...] = x_ref[...] + _dot(merged.astype(BF16), wo_ref[...])


def _merge(o_nsa, o_gla, o_x, mg, x, wn, wg, wx, wo):
    m, d = x.shape
    tm = min(512, m)
    assert m % tm == 0
    row = lambda w: pl.BlockSpec((tm, w), lambda i: (i, 0))
    const = lambda shape: pl.BlockSpec(shape, lambda i: (0,) * len(shape))
    return pl.pallas_call(
        _merge_kernel,
        grid=(m // tm,),
        in_specs=[row(512), row(512), row(512), row(3 * d), row(d),
                  const((512, d)), const((512, d)), const((512, d)), const((d, d))],
        out_specs=row(d),
        out_shape=jax.ShapeDtypeStruct((m, d), F32),
        compiler_params=_cparams(1),
    )(o_nsa, o_gla, o_x, mg, x, wn, wg, wx, wo)


def _ffn_seq_kernel(x_ref, past_ref, g_ref, wup_ref, cw_ref, cb_ref, wdn_ref, y_ref, tail_ref, carry_ref):
    f = cw_ref.shape[-1]
    tm = x_ref.shape[1]

    @pl.when(pl.program_id(1) == 0)
    def _():
        carry_ref[...] = jnp.zeros(carry_ref.shape, F32)
        carry_ref[6:8, :] = past_ref[0]

    x1 = x_ref[0]
    ug = _dot(_rms(x1, g_ref[...]).astype(BF16), wup_ref[...])
    u = ug[:, 0:f]
    g = ug[:, f:2 * f]
    row = lax.broadcasted_iota(jnp.int32, (tm, f), 0)
    p1 = carry_ref[7:8, :]
    p2 = carry_ref[6:7, :]
    gm1 = jnp.where(row == 0, p1, pltpu.roll(g, 1, 0))
    gm2 = jnp.where(row == 0, p2, jnp.where(row == 1, p1, pltpu.roll(g, 2, 0)))
    gc = cb_ref[...] + cw_ref[0:1, :] * gm2 + cw_ref[1:2, :] * gm1 + cw_ref[2:3, :] * g
    y_ref[0] = x1 + _dot((_gelu(gc) * u).astype(BF16), wdn_ref[...])
    carry_ref[...] = g[tm - 8:tm, :]
    tail_ref[0] = g[tm - 8:tm, :]


def _ffn_seq(x1, conv_past, g_ffn, w_up, conv_w, conv_b, w_down):
    b, t, d = x1.shape
    f = conv_w.shape[-1]
    tm = min(512, t)
    assert t % tm == 0 and tm >= 8
    const = lambda shape: pl.BlockSpec(shape, lambda i, j: (0,) * len(shape), pipeline_mode=pl.Buffered(1))
    return pl.pallas_call(
        _ffn_seq_kernel,
        grid=(b, t // tm),
        in_specs=[pl.BlockSpec((1, tm, d), lambda i, j: (i, j, 0)),
                  pl.BlockSpec((1, 2, f), lambda i, j: (i, 0, 0)),
                  const((1, d)), const((d, 2 * f)), const((3, f)), const((1, f)), const((f, d))],
        out_specs=[pl.BlockSpec((1, tm, d), lambda i, j: (i, j, 0)),
                   pl.BlockSpec((1, 8, f), lambda i, j: (i, 0, 0))],
        out_shape=[jax.ShapeDtypeStruct((b, t, d), F32), jax.ShapeDtypeStruct((b, 8, f), F32)],
        scratch_shapes=[pltpu.VMEM((8, f), F32)],
        compiler_params=_cparams(2),
    )(x1, conv_past, g_ffn, w_up, conv_w, conv_b, w_down)


def _ffn_step_kernel(x_ref, p0_ref, p1_ref, g_ref, wup_ref, cw_ref, cb_ref, wdn_ref, y_ref, gnew_ref):
    f = cw_ref.shape[-1]
    x1 = x_ref[...]
    ug = _dot(_rms(x1, g_ref[...]).astype(BF16), wup_ref[...])
    u = ug[:, 0:f]
    g = ug[:, f:2 * f]
    gc = cb_ref[...] + cw_ref[0:1, :] * p0_ref[...] + cw_ref[1:2, :] * p1_ref[...] + cw_ref[2:3, :] * g
    y_ref[...] = x1 + _dot((_gelu(gc) * u).astype(BF16), wdn_ref[...])
    gnew_ref[...] = g


def _ffn_step(x1, p0, p1, g_ffn, w_up, conv_w, conv_b, w_down):
    m, d = x1.shape
    f = conv_w.shape[-1]
    full = lambda shape: pl.BlockSpec(shape, lambda i: (0,) * len(shape))
    return pl.pallas_call(
        _ffn_step_kernel,
        grid=(1,),
        in_specs=[full((m, d)), full((m, f)), full((m, f)), full((1, d)), full((d, 2 * f)), full((3, f)),
                  full((1, f)), full((f, d))],
        out_specs=[full((m, d)), full((m, f))],
        out_shape=[jax.ShapeDtypeStruct((m, d), F32), jax.ShapeDtypeStruct((m, f), F32)],
        compiler_params=_cparams(1),
    )(x1, p0, p1, g_ffn, w_up, conv_w, conv_b, w_down)


def _decode_kernel(pt_ref, *refs, top, n_sel, n_pages):
    del pt_ref
    page_refs = refs[:n_pages]
    (q8_ref, new_ref, neww_ref, cwin_ref, gt_ref, bc_ref, bs_ref, bw_ref, b0_ref, ovt_ref, et_ref, pe_ref, w1_ref,
     w2_ref, gk0_ref, p64_ref, o_ref, wout_ref, xs_ref) = refs[n_pages:]
    page = page_refs[0].shape[1]
    length = n_pages * page
    n_chunks = length // CMP_STRIDE
    for u, pg in enumerate(page_refs):
        for c in range(2):
            xs_ref[c, page * u:page * (u + 1), :] = pg[0, :, 128 * c:128 * (c + 1)]

    def load_j(j):
        return jnp.concatenate([xs_ref[c, pl.ds(j, n_chunks, stride=CMP_STRIDE), :] for c in range(2)], axis=1)

    q8 = q8_ref[0]
    q8f = q8.astype(F32)
    rowk = lax.broadcasted_iota(jnp.int32, (8, 128), 0) // NSA_GROUP
    lane_half = lax.broadcasted_iota(jnp.int32, (8, 128), 1) // HEAD_DIM

    def half_mask(x):
        return jnp.where(rowk == lane_half, x, 0.0)

    kc, vc = _compress_core(load_j, n_chunks, pe_ref, w1_ref, w2_ref, gk0_ref, p64_ref)
    kc = kc.astype(BF16)
    vc = vc.astype(BF16)

    p_c = _masked_softmax(_dot_nt(q8, kc) + bc_ref[...], 1).astype(BF16)
    o_c = half_mask(_dot(p_c, vc))
    pf = p_c.astype(F32)
    p2 = jnp.concatenate([jnp.sum(pf[0:4], axis=0, keepdims=True), jnp.sum(pf[4:8], axis=0, keepdims=True),
                          jnp.zeros((126, n_chunks), F32)], axis=0)
    p2_hi = p2.astype(BF16)
    p2_lo = (p2 - p2_hi.astype(F32)).astype(BF16)
    imp_t = _dot_nt(ovt_ref[...], p2_hi) + _dot_nt(ovt_ref[...], p2_lo)
    nsp = imp_t.shape[0]
    jj = lax.broadcasted_iota(jnp.int32, (nsp, 128), 0)
    tblk = length // SEL_BLOCK
    forced = (jj == 0) | (jj == tblk) | (jj == tblk - 1)
    sel_t = _select_with_forced(jnp.where(jj < n_sel, imp_t, -2e38), forced, top, 0)
    pen = jnp.where(sel_t, 0.0, SEL_PENALTY).T
    pen8 = jnp.concatenate([jnp.broadcast_to(pen[0:1], (4, nsp)), jnp.broadcast_to(pen[1:2], (4, nsp))], axis=0)

    new = new_ref[0]
    neww = neww_ref[0]
    b0 = b0_ref[...]

    def attend(s_parts, s_new, v_parts, v_new):
        m = s_new
        for s in s_parts:
            m = jnp.maximum(m, jnp.max(s, axis=-1, keepdims=True))
        pn = jnp.exp(s_new - m)
        l = pn
        o = pn.astype(BF16).astype(F32) * v_new.astype(BF16).astype(F32)
        for s, v in zip(s_parts, v_parts):
            pp = jnp.exp(s - m)
            l = l + jnp.sum(pp, axis=-1, keepdims=True)
            o = o + _dot(pp.astype(BF16), v)
        return half_mask(o / l)

    def new_score(k_new):
        return jnp.sum(q8f * k_new.astype(BF16).astype(F32), axis=-1, keepdims=True) + b0

    k_past = jnp.concatenate([pg[0, :, 256:384] for pg in page_refs], axis=0).astype(BF16)
    v_past = jnp.concatenate([pg[0, :, 384:512] for pg in page_refs], axis=0).astype(BF16)
    s_past = _dot_nt(q8, k_past) + _dot_nt(pen8[:, 0:128].astype(BF16), et_ref[...]) + bs_ref[...]
    lane = lax.broadcasted_iota(jnp.int32, (8, nsp), 1)
    pen_new = jnp.sum(jnp.where(lane == tblk, pen8, 0.0), axis=-1, keepdims=True)
    o_s = attend([s_past], new_score(new[:, 256:384]) + pen_new, [v_past], new[:, 384:512])

    cw = cwin_ref[0]
    s_w = _dot_nt(q8, cw[:, 0:128].astype(BF16)) + bw_ref[...]
    o_w = attend([s_w], new_score(neww[:, 0:128]), [cw[:, 128:256].astype(BF16)], neww[:, 128:256])

    gt = gt_ref[0]
    o_ref[0] = gt[:, 0:1] * o_c + gt[:, 1:2] * o_s + gt[:, 2:3] * o_w

    wl = cw.shape[0]
    wrow = lax.broadcasted_iota(jnp.int32, cw.shape, 0)
    wout_ref[0] = jnp.where(wrow == wl - 1, neww, pltpu.roll(cw, wl - 1, 0))


def _nsa_decode(page_table, cache2d, q8, new_rows, new_win, cache_win, gates8, bc, bs, bw, b0, ov, et,
                pe, w1, w2, gk0, p64):
    db, n_pages = page_table.shape
    page = cache2d.shape[1]
    length = n_pages * page
    n_chunks = length // CMP_STRIDE
    n_sel = -(-(length + 1) // SEL_BLOCK)
    top = min(SEL_TOP, n_sel)
    wl = cache_win.shape[1]
    nsp = ov.shape[0]
    const = lambda shape: pl.BlockSpec(shape, lambda i, pt: (0,) * len(shape))
    per_b = lambda shape: pl.BlockSpec((1,) + shape, lambda i, pt: (i,) + (0,) * len(shape))
    page_spec = lambda u: pl.BlockSpec((1, page, 512), lambda i, pt: (pt[i, u], 0, 0))
    grid_spec = pltpu.PrefetchScalarGridSpec(
        num_scalar_prefetch=1,
        grid=(db,),
        in_specs=[page_spec(u) for u in range(n_pages)] + [
                  per_b((8, 128)), per_b((1, 512)), per_b((1, 256)), per_b((wl, 256)), per_b((8, 128)),
                  const((8, n_chunks)), const((8, length)), const((8, wl)), const((8, 1)),
                  const((nsp, n_chunks)), const((length, 128)),
                  const((2, CMP_STRIDE, 1, 256)), const((2, CMP_STRIDE * 256, 256)), const((256, 256)),
                  const((1, 128)), const((512, 512))],
        out_specs=[per_b((8, 128)), per_b((wl, 256))],
        scratch_shapes=[pltpu.VMEM((2, length, 128), F32)],
    )
    return pl.pallas_call(
        functools.partial(_decode_kernel, top=top, n_sel=n_sel, n_pages=n_pages),
        grid_spec=grid_spec,
        out_shape=[jax.ShapeDtypeStruct((db, 8, 128), F32), jax.ShapeDtypeStruct((db, wl, 256), F32)],
        compiler_params=_cparams(1),
    )(page_table, *([cache2d] * n_pages), q8, new_rows, new_win, cache_win, gates8, bc, bs, bw, b0, ov, et,
      pe, w1, w2, gk0, p64)


def _bucket_table():
    n = np.arange(MAX_DISTANCE + 1)
    max_exact = N_BUCKETS // 2
    nf = np.maximum(n, 1).astype(np.float32)
    large = max_exact + (np.log(nf / np.float32(max_exact)) / np.float32(math.log(MAX_DISTANCE / max_exact))
                         * np.float32(N_BUCKETS - max_exact)).astype(np.int32)
    return np.where(n < max_exact, n, np.minimum(large, N_BUCKETS - 1)).astype(np.int32)


def _bias_lookup(tb, rel, valid):
    idx = np.clip(rel, 0, MAX_DISTANCE)
    vals = jnp.moveaxis(tb[idx], -1, 0)
    return jnp.where(jnp.asarray(valid)[None], vals, NEG)


def _overlap(n_cmp_pad, n_cmp, n_sel_pad, n_sel):
    cs = (np.arange(n_cmp_pad) * CMP_STRIDE)[:, None]
    ss = (np.arange(n_sel_pad) * SEL_BLOCK)[None, :]
    ov = (cs < ss + SEL_BLOCK) & (cs + CMP_BLOCK > ss)
    ov &= (np.arange(n_cmp_pad) < n_cmp)[:, None] & (np.arange(n_sel_pad) < n_sel)[None, :]
    return jnp.asarray(ov.astype(np.float32), dtype=BF16)


def _stack_rows(x):
    return x.reshape(NSA_KV, NSA_GROUP * Q_BLOCK, x.shape[-1])


def _toeplitz(tbr, shift, width, max_valid):
    n = width + Q_BLOCK - 1
    u = np.arange(n)
    xs = shift - np.where(u < width, u, u - n)
    fvec = _bias_lookup(tbr, xs, (xs >= 0) & (xs <= max_valid))
    h = fvec.shape[0]
    return jnp.tile(fvec, (1, Q_BLOCK))[:, :Q_BLOCK * (n - 1)].reshape(h, Q_BLOCK, n - 1)[:, :, :width]


def _bias_descending(tb, top):
    far = jnp.broadcast_to(tb[MAX_DISTANCE][:, None], (tb.shape[1], top - MAX_DISTANCE + 1))
    return jnp.concatenate([far, tb[MAX_DISTANCE - 1:0:-1].T], axis=1)


def _prompt_tables(tb):
    tbr = (tb - tb[MAX_DISTANCE][None, :]) * LOG2E
    i = np.arange(Q_BLOCK)[:, None]

    def table(rel, valid):
        return _stack_rows(_bias_lookup(tbr, rel, valid))

    n_r = 4
    m_sel = _stack_rows(_toeplitz(tbr, Q_BLOCK * (n_r - 1), NEAR_TILE + Q_BLOCK * (n_r - 1), 1 << 30))
    tsel = jnp.stack([m_sel[:, :, Q_BLOCK * (n_r - 1 - r):Q_BLOCK * (n_r - 1 - r) + NEAR_TILE] for r in range(n_r)],
                     axis=1)
    n_v = WINDOW // Q_BLOCK + 1
    m_win = _stack_rows(_toeplitz(tbr, Q_BLOCK * (n_v - 1), WIN_TILE + Q_BLOCK * (n_v - 1), WINDOW - 1))
    twin = jnp.stack([m_win[:, :, Q_BLOCK * (n_v - 1 - v):Q_BLOCK * (n_v - 1 - v) + WIN_TILE] for v in range(n_v)],
                     axis=1)
    w = np.arange(16)[None, :] - 9
    rel = i - CMP_STRIDE * w - (CMP_BLOCK - 1)
    a = table(rel, rel >= 0)
    hi = a.astype(BF16)
    lo = (a - hi.astype(F32)).astype(BF16)
    acmp = jnp.concatenate([hi, lo], axis=-1)
    return jnp.swapaxes(tsel, 2, 3), jnp.swapaxes(twin, 2, 3), jnp.swapaxes(acmp, 1, 2)


def kernel(x_prompt, x_sample, cache_kv, cache_win, state_gla, state_conv, cache_mem, page_table, mem_prompt,
           g_mix, w_in, g_nsa_q, g_nsa_k, cmp_k_pe, cmp_k_w1, cmp_k_w2, cmp_v_pe, cmp_v_w1, cmp_v_w2,
           rel_bias, w_gla_gate, b_gla_gate, g_gla_o, g_mem, w_mem_kv, g_x_q, g_x_k,
           w_nsa_out, w_gla_out, w_x_out, w_o, g_ffn, w_up, conv_w, conv_b, w_down):
    bp, t, d = x_prompt.shape
    db = x_sample.shape[0]
    f = conv_w.shape[-1]

    offs = np.cumsum((0,) + IN_SIZES)
    segs = [w_in[:, offs[i]:offs[i + 1]] for i in range(len(IN_SIZES))]
    w_pad = jnp.concatenate([jnp.pad(s, ((0, 0), (0, pw - s.shape[1]))) for s, pw in zip(segs, PAD_SIZES)],
                            axis=1).astype(BF16)
    row = lambda v: v.reshape(1, -1).astype(F32)
    gq = row(jnp.tile(g_nsa_q, NSA_HEADS))
    gk0 = row(jnp.tile(g_nsa_k[0], NSA_KV))
    gk1 = row(jnp.tile(g_nsa_k[1], NSA_KV))
    gk2 = row(jnp.tile(g_nsa_k[2], NSA_KV))
    gxq = row(jnp.tile(g_x_q, X_HEADS))
    gxk = row(jnp.tile(g_x_k, X_HEADS))
    p64 = _block_ones(512, HEAD_DIM)
    p128 = _block_ones(512, X_DIM)
    bd2 = lambda a: jnp.concatenate([jnp.concatenate([a, jnp.zeros_like(a)], -1),
                                     jnp.concatenate([jnp.zeros_like(a), a], -1)], -2)
    bd4 = lambda a, c: jnp.concatenate([jnp.concatenate([bd2(a), jnp.zeros_like(bd2(a))], -1),
                                        jnp.concatenate([jnp.zeros_like(bd2(c)), bd2(c)], -1)], -2)
    pe = jnp.concatenate([jnp.tile(cmp_k_pe, (1, NSA_KV)), jnp.tile(cmp_v_pe, (1, NSA_KV))],
                         axis=-1).reshape(2, CMP_STRIDE, 1, 256)
    w1 = bd4(cmp_k_w1, cmp_v_w1).reshape(2, CMP_STRIDE * 256, 256).astype(BF16)
    w2 = bd4(cmp_k_w2, cmp_v_w2).astype(BF16)
    wg_pad = jnp.pad(w_gla_gate, ((0, 128 - GLA_RANK), (0, 0))).astype(BF16)
    tb = rel_bias.astype(F32)[_bucket_table()]
    wn, wgo, wx, wo = (w.astype(BF16) for w in (w_nsa_out, w_gla_out, w_x_out, w_o))
    wup = w_up.astype(BF16)
    wdn = w_down.astype(BF16)
    ggo = row(g_gla_o)

    rows_p, win_p, _, gates, gla_in, xq, mg, ka, kw, vst, vwt, qt = _proj_in(
        x_prompt, row(g_mix), w_pad, gq, gk1, gk2, gxq, p64, p128, True)
    memkv_p = _memory_kv(mem_prompt, row(g_mem), w_mem_kv.astype(BF16), gxk, p128)
    kc, vct = _compress(rows_p, pe, w1, w2, gk0, p64)
    n_chunks = t // CMP_STRIDE
    n_sel = -(-t // SEL_BLOCK)
    tsel, twin, acmp = _prompt_tables(tb)
    ovt = _overlap(n_chunks, n_chunks - 1, 128, n_sel).T
    o_nsa = _nsa_prompt(qt, kc, vct, ka, kw, vst, vwt, gates, tsel, twin, acmp, ovt)
    s0t = jnp.zeros((bp, GLA_HEADS, GLA_DV, GLA_DK), F32)
    o_gla, st = _gla_prompt(gla_in, s0t, wg_pad, row(b_gla_gate), ggo)
    o_x = _xatt(xq, memkv_p)
    m = bp * t
    x1 = _merge(o_nsa.reshape(m, 512), o_gla.reshape(m, 512), o_x.reshape(m, 512), mg.reshape(m, 3 * d),
                x_prompt.reshape(m, d), wn, wgo, wx, wo)
    y_p, tail = _ffn_seq(x1.reshape(bp, t, d), jnp.zeros((bp, 2, f), F32), row(g_ffn), wup, conv_w, row(conv_b), wdn)
    wl_p = min(WINDOW, t)
    out_rows_p = rows_p.reshape(bp, t, 4, NSA_KV, HEAD_DIM)
    out_win_p = win_p[:, t - wl_p:].reshape(bp, wl_p, 2, NSA_KV, HEAD_DIM)
    out_gla_p = jnp.swapaxes(st, 2, 3)
    out_conv_p = tail[:, 6:8]
    out_mem_p = memkv_p.reshape(bp, -1, 2, X_HEADS, X_DIM)

    n_pages = page_table.shape[1]
    page = cache_kv.shape[1]
    length = n_pages * page
    wl = cache_win.shape[1]
    rows_s, win_s, qn_s, gates_s, gla_s, xq_s, mg_s = (a[0] for a in _proj_in(
        x_sample.reshape(1, db, d), row(g_mix), w_pad, gq, gk1, gk2, gxq, p64, p128, False))

    eye = jnp.eye(NSA_KV, dtype=BF16)
    q8 = (qn_s.reshape(db, NSA_KV, NSA_GROUP, 1, HEAD_DIM) * eye[None, :, None, :, None]).reshape(db, 8, 128)
    gates8 = jnp.pad(gates_s[:, 0:24].reshape(db, 8, 3), ((0, 0), (0, 0), (0, 125)))
    n_chunks_s = length // CMP_STRIDE
    n_sel_s = -(-(length + 1) // SEL_BLOCK)
    nsp = -(-n_sel_s // 128) * 128
    cidx = np.arange(n_chunks_s)
    rel_c = length - (cidx * CMP_STRIDE + CMP_BLOCK - 1)
    bc = _bias_lookup(tb, rel_c, (rel_c >= 0) & (cidx < n_chunks_s - 1))
    bs = _bias_descending(tb, length)
    bw = jnp.where(jnp.asarray(np.arange(wl, 0, -1) < WINDOW)[None], _bias_descending(tb, wl), NEG)
    b0 = tb[0].reshape(8, 1)
    ov_s = _overlap(n_chunks_s, n_chunks_s - 1, nsp, n_sel_s).T
    assert length // SEL_BLOCK <= 128
    et = jnp.asarray((np.arange(length)[:, None] // SEL_BLOCK == np.arange(128)[None, :]).astype(np.float32),
                     dtype=BF16)
    o8, win_new = _nsa_decode(page_table, cache_kv.reshape(cache_kv.shape[0], page, 512), q8,
                              rows_s.reshape(db, 1, 512), win_s.reshape(db, 1, 256), cache_win.reshape(db, wl, 256),
                              gates8, bc, bs, bw, b0, ov_s, et, pe, w1, w2, gk0, p64)
    o8 = o8.reshape(db, NSA_KV, NSA_GROUP, NSA_KV, HEAD_DIM)
    o_nsa_s = jnp.stack([o8[:, 0, :, 0], o8[:, 1, :, 1]], axis=1).reshape(db, 512).astype(BF16)

    wgt = jnp.pad(w_gla_gate.T, ((0, 0), (0, 128 - GLA_RANK))).reshape(GLA_HEADS, GLA_DK, 128)
    o_gla_s, gla_state_s = _gla_step(
        gla_s[:, 0:256].reshape(db, GLA_HEADS, GLA_DK, 1), gla_s[:, 256:512].reshape(db, GLA_HEADS, GLA_DK, 1),
        gla_s[:, 1024:1152].reshape(db, 1, 128), gla_s[:, 512:1024].reshape(db, GLA_HEADS, 1, GLA_DV),
        gla_s[:, 1152:1664].reshape(db, GLA_HEADS, 1, GLA_DV), state_gla.astype(F32), wgt,
        b_gla_gate.reshape(GLA_HEADS, GLA_DK, 1), ggo)
    o_gla_s = o_gla_s.reshape(db, 512).astype(BF16)

    xq_pad = jnp.pad(xq_s.reshape(db, 1, 512), ((0, 0), (0, 15), (0, 0)))
    o_x_s = _xatt(xq_pad, cache_mem.reshape(db, -1, 1024))[:, 0]
    x1_s = _merge(o_nsa_s, o_gla_s, o_x_s, mg_s, x_sample.reshape(db, d), wn, wgo, wx, wo)
    y_s, g_new = _ffn_step(x1_s, state_conv[:, 0], state_conv[:, 1], row(g_ffn), wup, conv_w, row(conv_b), wdn)

    out_rows_s = rows_s.reshape(db, 1, 4, NSA_KV, HEAD_DIM)
    out_win_s = win_new.reshape(db, wl, 2, NSA_KV, HEAD_DIM)
    out_conv_s = jnp.stack([state_conv[:, 1], g_new], axis=1)
    return (y_p, y_s.reshape(db, 1, d), out_rows_p, out_win_p, out_gla_p, out_conv_p, out_mem_p,
            out_rows_s, out_win_s, gla_state_s, out_conv_s)
```

```python
import functools
import math

import numpy as np
import jax
import jax.numpy as jnp
from jax import lax
from jax.experimental import pallas as pl
from jax.experimental.pallas import tpu as pltpu

F32 = jnp.float32
BF16 = jnp.bfloat16

NSA_HEADS = 8
NSA_KV = 2
NSA_GROUP = 4
HEAD_DIM = 64
CMP_BLOCK = 32
CMP_STRIDE = 16
SEL_BLOCK = 64
SEL_TOP = 16
WINDOW = 512
Q_BLOCK = 128
GLA_HEADS = 4
GLA_DK = 64
GLA_DV = 128
GLA_RANK = 16
GLA_TAU = 16.0
GLA_CHUNK = 64
X_HEADS = 4
X_DIM = 128
N_BUCKETS = 32
MAX_DISTANCE = 128
EPS = 1e-6
LOG2E = math.log2(math.e)
NEG = -1e30
TINY = 1e-30
SEL_PENALTY = -1e9
MASKED_BELOW = -5e29
EXP_CLAMP = 80.0

IN_SIZES = (512, 768, 24, 256, 256, 512, 16, 512, 512, 3072)
PAD_SIZES = (512, 768, 128, 256, 256, 512, 128, 512, 512, 3072)
PAD_OFFS = tuple(int(v) for v in np.cumsum((0,) + PAD_SIZES))
D_IN_PAD = PAD_OFFS[-1]
GLA_IN_W = 256 + 256 + 512 + 128 + 512

VMEM_LIMIT = 56 * 1024 * 1024
FAR_TILE = 256
NEAR_TILE = 512
WIN_TILE = WINDOW + Q_BLOCK


def _cparams(n_axes):
    return pltpu.CompilerParams(dimension_semantics=("arbitrary",) * n_axes, vmem_limit_bytes=VMEM_LIMIT)


def _dot(a, b):
    return jnp.dot(a, b, preferred_element_type=F32)


def _dot_nt(a, b):
    return lax.dot_general(a, b, (((1,), (1,)), ((), ())), preferred_element_type=F32)


def _dot_tn(a, b):
    return lax.dot_general(a, b, (((0,), (0,)), ((), ())), preferred_element_type=F32)


def _split_dot(x, m):
    hi = x.astype(BF16)
    lo = (x - hi.astype(F32)).astype(BF16)
    return _dot(hi, m) + _dot(lo, m)


def _rms(x, g):
    return x * lax.rsqrt(jnp.mean(x * x, axis=-1, keepdims=True) + EPS) * g


def _group_rms(x, pmat, gsize, g):
    ss = _split_dot(x * x, pmat)
    return x * lax.rsqrt(ss * (1.0 / gsize) + EPS) * g


def _gelu(x):
    return 0.5 * x * (1.0 + jnp.tanh(math.sqrt(2.0 / math.pi) * (x + 0.044715 * (x * x * x))))


def _sigmoid(x):
    return 1.0 / (1.0 + jnp.exp(-x))


def _block_ones(n, gsize):
    i = np.arange(n) // gsize
    return jnp.asarray((i[:, None] == i[None, :]).astype(np.float32), dtype=BF16)


def _proj_kernel(x_ref, gmix_ref, w_ref, gq_ref, gk1_ref, gk2_ref, gxq_ref, p64_ref, p128_ref,
                 rows_ref, win_ref, qn_ref, gates_ref, gla_ref, xq_ref, mg_ref, *attn_refs):
    x = x_ref[0]
    h = _rms(x, gmix_ref[...]).astype(BF16)
    o = PAD_OFFS

    def seg(i):
        return _dot(h, w_ref[:, o[i]:o[i + 1]])

    p64 = p64_ref[...]
    p64s = p64_ref[0:128, 0:128]
    qn = _group_rms(seg(0), p64, HEAD_DIM, gq_ref[...]) * (HEAD_DIM ** -0.5)
    qn_ref[0] = qn.astype(BF16)
    if attn_refs:
        qt_ref = attn_refs[4]
        for u in range(x.shape[0] // Q_BLOCK):
            qt_ref[0, u] = (qn[Q_BLOCK * u:Q_BLOCK * (u + 1)] * LOG2E).T.astype(BF16)

    kv = seg(1)
    k_sel = _group_rms(kv[:, 256:384], p64s, HEAD_DIM, gk1_ref[...])
    k_win = _group_rms(kv[:, 512:640], p64s, HEAD_DIM, gk2_ref[...])
    rows_ref[0, :, 0:256] = kv[:, 0:256]
    rows_ref[0, :, 256:384] = k_sel
    rows_ref[0, :, 384:512] = kv[:, 384:512]
    win_ref[0, :, 0:128] = k_win
    win_ref[0, :, 128:256] = kv[:, 640:768]
    if attn_refs:
        ka_ref, kw_ref, vst_ref, vwt_ref, _ = attn_refs
        tm = x.shape[0]
        tpos = pl.program_id(1) * tm + lax.broadcasted_iota(jnp.int32, (tm, 128), 0)
        blk = lax.broadcasted_iota(jnp.int32, (tm, 128), 1)
        ka_ref[0, :, 0:128] = k_sel.astype(BF16)
        ka_ref[0, :, 128:256] = jnp.where(tpos // SEL_BLOCK == blk, 1.0, 0.0).astype(BF16)
        kw_ref[0] = k_win.astype(BF16)
        for u in range(tm // FAR_TILE):
            vst_ref[0, u] = kv[FAR_TILE * u:FAR_TILE * (u + 1), 384:512].T.astype(BF16)
        for u in range(tm // Q_BLOCK):
            vwt_ref[0, u] = kv[Q_BLOCK * u:Q_BLOCK * (u + 1), 640:768].T.astype(BF16)

    gates_ref[0] = _sigmoid(seg(2))
    gla_ref[0, :, 0:256] = seg(3) * (GLA_DK ** -0.5)
    gla_ref[0, :, 256:512] = seg(4)
    gla_ref[0, :, 512:1024] = seg(5)
    gla_ref[0, :, 1024:1152] = seg(6)
    gla_ref[0, :, 1152:1664] = seg(7)
    xq = _group_rms(seg(8), p128_ref[...], X_DIM, gxq_ref[...]) * (X_DIM ** -0.5)
    xq_ref[0] = xq.astype(BF16)
    mg_ref[0] = _sigmoid(seg(9)).astype(BF16)


def _proj_in(x, g_mix, w_pad, gq, gk1, gk2, gxq, p64, p128, attn_layout):
    b, t, d = x.shape
    tm = min(512, t)
    assert t % tm == 0
    widths = [512, 256, 512, 128, GLA_IN_W, 512, 3072]
    dtypes = [F32, F32, BF16, F32, F32, BF16, BF16]
    out_specs = [pl.BlockSpec((1, tm, w), lambda i, j: (i, j, 0)) for w in widths]
    out_shape = [jax.ShapeDtypeStruct((b, t, w), dt) for w, dt in zip(widths, dtypes)]
    if attn_layout:
        assert tm % Q_BLOCK == 0
        for w in (256, 128):
            out_specs.append(pl.BlockSpec((1, tm, w), lambda i, j: (i, j, 0)))
            out_shape.append(jax.ShapeDtypeStruct((b, t, w), BF16))
        assert tm % FAR_TILE == 0
        for rows, width in ((128, FAR_TILE), (128, Q_BLOCK), (512, Q_BLOCK)):
            out_specs.append(pl.BlockSpec((1, tm // width, rows, width), lambda i, j: (i, j, 0, 0)))
            out_shape.append(jax.ShapeDtypeStruct((b, t // width, rows, width), BF16))
    const = lambda shape: pl.BlockSpec(shape, lambda i, j: (0,) * len(shape), pipeline_mode=pl.Buffered(1))
    return pl.pallas_call(
        _proj_kernel,
        grid=(b, t // tm),
        in_specs=[pl.BlockSpec((1, tm, d), lambda i, j: (i, j, 0)),
                  const((1, d)), const((d, D_IN_PAD)), const((1, 512)), const((1, 128)), const((1, 128)),
                  const((1, 512)), const((512, 512)), const((512, 512))],
        out_specs=out_specs,
        out_shape=out_shape,
        compiler_params=_cparams(2),
    )(x, g_mix, w_pad, gq, gk1, gk2, gxq, p64, p128)


def _memkv_kernel(m_ref, g_ref, w_ref, gk_ref, p128_ref, o_ref):
    h = _rms(m_ref[0], g_ref[...]).astype(BF16)
    kv = _dot(h, w_ref[...])
    o_ref[0, :, 0:512] = _group_rms(kv[:, 0:512], p128_ref[...], X_DIM, gk_ref[...])
    o_ref[0, :, 512:1024] = kv[:, 512:1024]


def _memory_kv(mem, g_mem, w_mem, gxk, p128):
    b, m, d = mem.shape
    const = lambda shape: pl.BlockSpec(shape, lambda i: (0,) * len(shape))
    return pl.pallas_call(
        _memkv_kernel,
        grid=(b,),
        in_specs=[pl.BlockSpec((1, m, d), lambda i: (i, 0, 0)), const((1, d)), const((d, 1024)),
                  const((1, 512)), const((512, 512))],
        out_specs=pl.BlockSpec((1, m, 1024), lambda i: (i, 0, 0)),
        out_shape=jax.ShapeDtypeStruct((b, m, 1024), F32),
        compiler_params=_cparams(1),
    )(mem, g_mem, w_mem, gxk, p128)


def _compress_core(load_j, n_chunks, pe_ref, w1_ref, w2_ref, gk0_ref, p64_ref):
    xs = [load_j(j) for j in range(CMP_STRIDE)]
    halves = [_dot(jnp.concatenate([(x + pe_ref[r, j]).astype(BF16) for j, x in enumerate(xs)], axis=1), w1_ref[r])
              for r in range(2)]
    hid = halves[0] + pltpu.roll(halves[1], n_chunks - 1, 0)
    out = _dot(_gelu(hid).astype(BF16), w2_ref[...])
    row = lax.broadcasted_iota(jnp.int32, (n_chunks, 128), 0)
    live = row < n_chunks - 1
    kc = _group_rms(out[:, 0:128], p64_ref[0:128, 0:128], HEAD_DIM, gk0_ref[...])
    return jnp.where(live, kc, 0.0), jnp.where(live, out[:, 128:256], 0.0)


def _compress_kernel(rk_ref, rv_ref, pe_ref, w1_ref, w2_ref, gk0_ref, p64_ref, kc_ref, vc_ref, *, n_chunks):
    load_j = lambda j: jnp.concatenate([rk_ref[0, pl.ds(j, n_chunks, stride=CMP_STRIDE), :],
                                        rv_ref[0, pl.ds(j, n_chunks, stride=CMP_STRIDE), :]], axis=1)
    kc, vc = _compress_core(load_j, n_chunks, pe_ref, w1_ref, w2_ref, gk0_ref, p64_ref)
    kc_ref[0] = kc.astype(BF16)
    vc_ref[0] = vc.T.astype(BF16)


def _compress(rows, pe, w1, w2, gk0, p64):
    b, t, _ = rows.shape
    n_chunks = t // CMP_STRIDE
    const = lambda shape: pl.BlockSpec(shape, lambda i: (0,) * len(shape))
    return pl.pallas_call(
        functools.partial(_compress_kernel, n_chunks=n_chunks),
        grid=(b,),
        in_specs=[pl.BlockSpec((1, t, 128), lambda i: (i, 0, 0)), pl.BlockSpec((1, t, 128), lambda i: (i, 0, 1)),
                  const((2, CMP_STRIDE, 1, 256)), const((2, CMP_STRIDE * 256, 256)), const((256, 256)),
                  const((1, 128)), const((512, 512))],
        out_specs=[pl.BlockSpec((1, n_chunks, 128), lambda i: (i, 0, 0)),
                   pl.BlockSpec((1, 128, n_chunks), lambda i: (i, 0, 0))],
        out_shape=[jax.ShapeDtypeStruct((b, n_chunks, 128), BF16), jax.ShapeDtypeStruct((b, 128, n_chunks), BF16)],
        compiler_params=_cparams(1),
    )(rows, rows, pe, w1, w2, gk0, p64)


def _masked_softmax(s, axis, exp_fn=jnp.exp):
    m = jnp.maximum(jnp.max(s, axis=axis, keepdims=True), MASKED_BELOW)
    p = exp_fn(s - m)
    return p / jnp.maximum(jnp.sum(p, axis=axis, keepdims=True), TINY)


def _select_blocks(score, top, axis):
    pos = lax.broadcasted_iota(jnp.int32, score.shape, axis).astype(F32)
    sel = jnp.zeros(score.shape, jnp.bool_)
    for _ in range(top):
        mx = jnp.max(score, axis=axis, keepdims=True)
        idx = jnp.min(jnp.where(score == mx, pos, 1e9), axis=axis, keepdims=True)
        hit = pos == idx
        sel = jnp.logical_or(sel, hit)
        score = jnp.where(hit, -3e38, score)
    return sel


def _select_with_forced(imp, forced, top, axis):
    n_forced = 3
    assert top > n_forced
    return jnp.logical_or(forced, _select_blocks(jnp.where(forced, -3e38, imp), top - n_forced, axis))


def _values_times_probs(vt_tiles, p):
    out = None
    start = 0
    for vt in vt_tiles:
        part = _dot(vt, p[start:start + vt.shape[1]])
        start += vt.shape[1]
        out = part if out is None else out + part
    return out


def _softmax_stats_t(s_ref, bias, m_ref, l_ref):
    a_parts, p_parts = [], []
    for g in range(s_ref.shape[1] // Q_BLOCK):
        cs = slice(Q_BLOCK * g, Q_BLOCK * (g + 1))
        sg = s_ref[:, cs]
        if bias is not None:
            sg = sg + bias(cs)
        m_old = m_ref[:, cs]
        m_new = jnp.maximum(m_old, jnp.max(sg, axis=0, keepdims=True))
        alpha = jnp.exp2(m_old - m_new)
        p = jnp.exp2(sg - m_new)
        m_ref[:, cs] = m_new
        l_ref[:, cs] = alpha * l_ref[:, cs] + jnp.sum(p, axis=0, keepdims=True)
        a_parts.append(alpha)
        p_parts.append(p.astype(BF16))
    return jnp.concatenate(a_parts, axis=1), jnp.concatenate(p_parts, axis=1)


def _acc_update_t(acc_ref, alpha, vt_tiles, p):
    acc_ref[...] = alpha * acc_ref[...] + _values_times_probs(vt_tiles, p)


def _attn_kernel(qt_ref, kc_ref, vct_ref, ka_ref, vst_ref, kw_ref, vwt_ref, gt_ref, tsel_ref, twin_ref, acmp_ref,
                 ovt_ref, o_ref, sa_scr, sb_scr, m_scr, l_scr, acc_scr, pend_a, pend_p, *, top):
    k = pl.program_id(1)
    n = pl.program_id(2)
    cols = NSA_GROUP * Q_BLOCK
    vrow = pl.multiple_of(k * HEAD_DIM, HEAD_DIM)
    zero = jnp.zeros((HEAD_DIM, Q_BLOCK), BF16)
    parts = []
    for g in range(NSA_GROUP):
        piece = qt_ref[0, 0, HEAD_DIM * g:HEAD_DIM * (g + 1), :]
        parts.append(jnp.where(k == 0, jnp.concatenate([piece, zero], axis=0),
                               jnp.concatenate([zero, piece], axis=0)))
    qt = jnp.concatenate(parts, axis=1)

    nc = kc_ref.shape[1]
    blocks_per_q = Q_BLOCK // CMP_STRIDE
    cp = lax.broadcasted_iota(jnp.int32, (nc, 128), 0) - blocks_per_q * n + 9
    fr = lax.broadcasted_iota(jnp.int32, (nc, 128), 1)
    feat = ((fr < 32) & (cp == (fr & 15))) | ((fr == 32) & (cp > 15))
    kc_aug = jnp.concatenate([kc_ref[0], jnp.where(feat, 1.0, 0.0).astype(BF16)], axis=1)
    s_c = _dot(kc_aug, jnp.concatenate([qt, acmp_ref[0]], axis=0))
    p_c = _masked_softmax(s_c, 0, jnp.exp2).astype(BF16)
    o_c = _dot(vct_ref[0, pl.ds(vrow, HEAD_DIM), :], p_c)
    imp4 = _dot(ovt_ref[...], p_c)
    imp = imp4[:, 0:128] + imp4[:, 128:256] + imp4[:, 256:384] + imp4[:, 384:512]

    def value_tiles(ref, first, count):
        return [ref[0, first + d, pl.ds(vrow, HEAD_DIM), :] for d in range(count)]

    wt = jnp.maximum(n - WINDOW // Q_BLOCK, 0)
    s_w = _dot(kw_ref[0, pl.ds(pl.multiple_of(wt * Q_BLOCK, Q_BLOCK), WIN_TILE), :], qt) + twin_ref[0, 0]
    p_w = jnp.exp2(s_w - jnp.max(s_w, axis=0, keepdims=True))
    o_w = (_values_times_probs(value_tiles(vwt_ref, wt, WIN_TILE // Q_BLOCK), p_w.astype(BF16))
           / jnp.sum(p_w, axis=0, keepdims=True))

    jj = lax.broadcasted_iota(jnp.int32, (128, Q_BLOCK), 0)
    tpos = n * Q_BLOCK + lax.broadcasted_iota(jnp.int32, (128, Q_BLOCK), 1)
    tblk = tpos // SEL_BLOCK
    forced = (jj == 0) | (jj == tblk) | (jj == tblk - 1)
    sel_t = _select_with_forced(jnp.where(jj * SEL_BLOCK <= tpos, imp, NEG), forced, top, 0)
    pen_t = jnp.where(sel_t, 0.0, SEL_PENALTY).astype(BF16)
    rhs = jnp.concatenate([qt, jnp.concatenate([pen_t] * NSA_GROUP, axis=1)], axis=0)

    n_far = jnp.maximum(n - 2, 0) // 2
    r_near = n - 2 * n_far

    def issue_scores(tile, s_ref):
        s_ref[...] = _dot(ka_ref[0, pl.ds(pl.multiple_of(tile * FAR_TILE, FAR_TILE), FAR_TILE), :], rhs)

    def stats(s_ref, half_idx=None):
        bias = None
        if half_idx is not None:
            bias = lambda cs: tsel_ref[0, r_near, FAR_TILE * half_idx:FAR_TILE * (half_idx + 1), cs]
        return _softmax_stats_t(s_ref, bias, m_scr, l_scr)

    def accumulate(tile, alpha, p):
        _acc_update_t(acc_scr, alpha, value_tiles(vst_ref, tile, 1), p)

    m_scr[...] = jnp.full((1, cols), NEG, F32)
    l_scr[...] = jnp.zeros((1, cols), F32)
    acc_scr[...] = jnp.zeros((HEAD_DIM, cols), F32)
    pend_a[...] = jnp.ones((1, cols), F32)
    pend_p[...] = jnp.zeros((FAR_TILE, cols), BF16)
    odd = n_far % 2

    @pl.when(odd == 1)
    def _():
        issue_scores(0, sa_scr)
        accumulate(0, *stats(sa_scr))

    issue_scores(odd, sa_scr)

    @pl.loop(0, n_far // 2)
    def _(j):
        t0 = odd + 2 * j
        issue_scores(t0 + 1, sb_scr)
        alpha_a, p_a = stats(sa_scr)
        accumulate(jnp.maximum(t0 - 1, 0), pend_a[...], pend_p[...])
        issue_scores(t0 + 2, sa_scr)
        alpha_b, p_b = stats(sb_scr)
        accumulate(t0, alpha_a, p_a)
        pend_a[...] = alpha_b
        pend_p[...] = p_b

    accumulate(jnp.maximum(n_far - 1, 0), pend_a[...], pend_p[...])
    issue_scores(n_far + 1, sb_scr)
    accumulate(n_far, *stats(sa_scr, 0))
    accumulate(n_far + 1, *stats(sb_scr, 1))
    o_s = acc_scr[...] / l_scr[...]

    gtt = gt_ref[0].T

    def gate_row(branch):
        rows = [jnp.where(k == 0, gtt[3 * g + branch:3 * g + branch + 1],
                          gtt[12 + 3 * g + branch:12 + 3 * g + branch + 1]) for g in range(NSA_GROUP)]
        return jnp.concatenate(rows, axis=1)

    o_t = gate_row(0) * o_c + gate_row(1) * o_s + gate_row(2) * o_w
    left = jnp.concatenate([o_t[:, 0:128], o_t[:, 128:256]], axis=0).T
    right = jnp.concatenate([o_t[:, 256:384], o_t[:, 384:512]], axis=0).T
    o_ref[0] = jnp.concatenate([left, right], axis=1).astype(BF16)


def _nsa_prompt(qt, kc, vct, ka, kw, vst, vwt, gates, tsel, twin, acmp, ovt):
    b, t, _ = ka.shape
    nqb = t // Q_BLOCK
    nc = kc.shape[1]
    top = min(SEL_TOP, -(-t // SEL_BLOCK))
    assert t >= WIN_TILE and t % FAR_TILE == 0 and t // SEL_BLOCK <= 128
    n_win = WINDOW // Q_BLOCK
    per_b = lambda shape: pl.BlockSpec((1,) + shape, lambda i, k, n: (i,) + (0,) * len(shape))
    return pl.pallas_call(
        functools.partial(_attn_kernel, top=top),
        grid=(b, NSA_KV, nqb),
        in_specs=[pl.BlockSpec((1, 1, NSA_GROUP * HEAD_DIM, Q_BLOCK), lambda i, k, n: (i, n, k, 0)),
                  per_b((nc, 128)), per_b((128, nc)),
                  per_b((t, 256)), per_b((t // FAR_TILE, 128, FAR_TILE)), per_b((t, 128)), per_b((nqb, 128, Q_BLOCK)),
                  pl.BlockSpec((1, Q_BLOCK, 128), lambda i, k, n: (i, n, 0)),
                  pl.BlockSpec((1, 4, NEAR_TILE, 512), lambda i, k, n: (k, 0, 0, 0)),
                  pl.BlockSpec((1, 1, WIN_TILE, 512), lambda i, k, n: (k, jnp.minimum(n, n_win), 0, 0)),
                  pl.BlockSpec((1, 128, 512), lambda i, k, n: (k, 0, 0)),
                  pl.BlockSpec((128, nc), lambda i, k, n: (0, 0))],
        out_specs=pl.BlockSpec((1, Q_BLOCK, 256), lambda i, k, n: (i, n, k)),
        out_shape=jax.ShapeDtypeStruct((b, t, 512), BF16),
        scratch_shapes=[pltpu.VMEM((FAR_TILE, NSA_GROUP * Q_BLOCK), F32), pltpu.VMEM((FAR_TILE, NSA_GROUP * Q_BLOCK), F32),
                        pltpu.VMEM((1, NSA_GROUP * Q_BLOCK), F32), pltpu.VMEM((1, NSA_GROUP * Q_BLOCK), F32),
                        pltpu.VMEM((HEAD_DIM, NSA_GROUP * Q_BLOCK), F32),
                        pltpu.VMEM((1, NSA_GROUP * Q_BLOCK), F32), pltpu.VMEM((FAR_TILE, NSA_GROUP * Q_BLOCK), BF16)],
        compiler_params=_cparams(3),
    )(qt, kc, vct, ka, vst, kw, vwt, gates, tsel, twin, acmp, ovt)


def _log_sigmoid(z):
    return jnp.minimum(z, 0.0) - jnp.log1p(jnp.exp(-jnp.abs(z)))


def _gla_kernel(x_ref, s0_ref, wg_ref, bg_ref, ggo_ref, tri_ref, o_ref, st_ref, s_scr, *, n_chunks):
    @pl.when(pl.program_id(0) == 0)
    def _():
        s_scr[...] = s0_ref[...]

    c_len = GLA_CHUNK
    ct = n_chunks * c_len
    tri = tri_ref[...]
    ti = lax.broadcasted_iota(jnp.int32, (ct, ct), 0)
    si = lax.broadcasted_iota(jnp.int32, (ct, ct), 1)
    causal = (si <= ti) & (si // c_len == ti // c_len)

    def per_chunk(x, row):
        return jnp.concatenate([jnp.broadcast_to(x[c_len * c + row:c_len * c + row + 1], (c_len, x.shape[1]))
                                for c in range(n_chunks)], axis=0)

    for bi in range(x_ref.shape[0]):
        q = x_ref[bi, :, 0:256]
        kk = x_ref[bi, :, 256:512]
        v = x_ref[bi, :, 512:1024]
        lr = x_ref[bi, :, 1024:1152]
        r = x_ref[bi, :, 1152:1664]
        la = _log_sigmoid(_dot(lr.astype(BF16), wg_ref[...]) + bg_ref[...]) * (1.0 / GLA_TAU)
        a1 = la.astype(BF16)
        r1 = la - a1.astype(F32)
        a2 = r1.astype(BF16)
        a3 = (r1 - a2.astype(F32)).astype(BF16)
        cb = _dot(tri, a1) + _dot(tri, a2) + _dot(tri, a3)
        last = per_chunk(cb, c_len - 1)
        mid = per_chunk(cb, c_len // 2)
        qe = (q * jnp.exp(cb)).astype(BF16)
        qa = (q * jnp.exp(jnp.minimum(cb - mid, EXP_CLAMP))).astype(BF16)
        kb = (kk * jnp.exp(jnp.minimum(mid - cb, EXP_CLAMP))).astype(BF16)
        ke = (kk * jnp.exp(last - cb)).astype(BF16)
        for h in range(GLA_HEADS):
            ks = slice(GLA_DK * h, GLA_DK * (h + 1))
            vs = slice(GLA_DV * h, GLA_DV * (h + 1))
            att = jnp.where(causal, _dot_nt(qa[:, ks], kb[:, ks]), 0.0)
            vh = v[:, vs].astype(BF16)
            o_intra = _dot(att.astype(BF16), vh)
            st = s_scr[bi, h]
            o_inter = []
            for c in range(n_chunks):
                rs = slice(c_len * c, c_len * (c + 1))
                o_inter.append(_dot_nt(qe[rs, ks], st.astype(BF16)))
                dec = jnp.exp(cb[c_len * (c + 1) - 1:c_len * (c + 1), ks])
                st = st * dec + _dot_tn(vh[rs], ke[rs, ks])
            s_scr[bi, h] = st
            on = _rms(o_intra + jnp.concatenate(o_inter, axis=0), ggo_ref[...])
            rh = r[:, vs]
            o_ref[bi, :, vs] = (on * (rh * _sigmoid(rh))).astype(BF16)
    st_ref[...] = s_scr[...]


def _gla_prompt(gla_in, s0t, wg, bg, ggo):
    b, t, w = gla_in.shape
    ct = min(256, t)
    assert t % ct == 0 and ct % GLA_CHUNK == 0
    pos = np.arange(ct)
    tri = jnp.asarray(((pos[None, :] <= pos[:, None])
                       & (pos[None, :] // GLA_CHUNK == pos[:, None] // GLA_CHUNK)).astype(np.float32), dtype=BF16)
    const = lambda shape: pl.BlockSpec(shape, lambda j: (0,) * len(shape))
    state_shape = (b, GLA_HEADS, GLA_DV, GLA_DK)
    return pl.pallas_call(
        functools.partial(_gla_kernel, n_chunks=ct // GLA_CHUNK),
        grid=(t // ct,),
        in_specs=[pl.BlockSpec((b, ct, w), lambda j: (0, j, 0)), const(state_shape),
                  const((128, 256)), const((1, 256)), const((1, 128)), const((ct, ct))],
        out_specs=[pl.BlockSpec((b, ct, 512), lambda j: (0, j, 0)), const(state_shape)],
        out_shape=[jax.ShapeDtypeStruct((b, t, 512), BF16), jax.ShapeDtypeStruct(state_shape, F32)],
        scratch_shapes=[pltpu.VMEM(state_shape, F32)],
        compiler_params=_cparams(1),
    )(gla_in, s0t, wg, bg, ggo, tri)


def _gla_step_kernel(q_ref, k_ref, lr_ref, v_ref, r_ref, s_ref, wgt_ref, bgt_ref, ggo_ref, o_ref, sn_ref):
    lr = lr_ref[0]
    for h in range(GLA_HEADS):
        z = jnp.sum(wgt_ref[h] * lr, axis=-1, keepdims=True) + bgt_ref[h]
        a = jnp.exp(_log_sigmoid(z) * (1.0 / GLA_TAU))
        s0 = s_ref[0, h]
        kh = k_ref[0, h]
        qh = q_ref[0, h]
        vh = v_ref[0, h]
        sn_ref[0, h] = a * s0 + kh * vh
        o = jnp.sum((qh * a) * s0, axis=0, keepdims=True) + jnp.sum(qh * kh, axis=0, keepdims=True) * vh
        on = _rms(o, ggo_ref[...])
        rh = r_ref[0, h]
        o_ref[0, h] = on * (rh * _sigmoid(rh))


def _gla_step(q_col, k_col, lr, v_row, r_row, s0, wgt, bgt, ggo):
    b = q_col.shape[0]
    const = lambda shape: pl.BlockSpec(shape, lambda i: (0,) * len(shape))
    per_b = lambda shape: pl.BlockSpec((1,) + shape, lambda i: (i,) + (0,) * len(shape))
    return pl.pallas_call(
        _gla_step_kernel,
        grid=(b,),
        in_specs=[per_b((GLA_HEADS, GLA_DK, 1)), per_b((GLA_HEADS, GLA_DK, 1)), per_b((1, 128)),
                  per_b((GLA_HEADS, 1, GLA_DV)), per_b((GLA_HEADS, 1, GLA_DV)), per_b((GLA_HEADS, GLA_DK, GLA_DV)),
                  const((GLA_HEADS, GLA_DK, 128)), const((GLA_HEADS, GLA_DK, 1)), const((1, 128))],
        out_specs=[per_b((GLA_HEADS, 1, GLA_DV)), per_b((GLA_HEADS, GLA_DK, GLA_DV))],
        out_shape=[jax.ShapeDtypeStruct((b, GLA_HEADS, 1, GLA_DV), F32),
                   jax.ShapeDtypeStruct((b, GLA_HEADS, GLA_DK, GLA_DV), F32)],
        compiler_params=_cparams(1),
    )(q_col, k_col, lr, v_row, r_row, s0, wgt, bgt, ggo)


def _xatt_kernel(xq_ref, mem_ref, o_ref):
    for bi in range(xq_ref.shape[0]):
        for h in range(X_HEADS):
            ls = slice(X_DIM * h, X_DIM * (h + 1))
            kh = mem_ref[bi, :, ls].astype(BF16)
            vh = mem_ref[bi, :, 512 + X_DIM * h:512 + X_DIM * (h + 1)].astype(BF16)
            s = _dot_nt(xq_ref[bi, :, ls], kh)
            p = jnp.exp(s - jnp.max(s, axis=-1, keepdims=True))
            p = p / jnp.sum(p, axis=-1, keepdims=True)
            o_ref[bi, :, ls] = _dot(p.astype(BF16), vh).astype(BF16)


def _xatt(xq, memkv):
    b, t, _ = xq.shape
    m = memkv.shape[1]
    tq = min(512, t)
    assert t % tq == 0
    bb = math.gcd(b, max(1, 128 // tq))
    return pl.pallas_call(
        _xatt_kernel,
        grid=(b // bb, t // tq),
        in_specs=[pl.BlockSpec((bb, tq, 512), lambda i, j: (i, j, 0)),
                  pl.BlockSpec((bb, m, 1024), lambda i, j: (i, 0, 0))],
        out_specs=pl.BlockSpec((bb, tq, 512), lambda i, j: (i, j, 0)),
        out_shape=jax.ShapeDtypeStruct((b, t, 512), BF16),
        compiler_params=_cparams(2),
    )(xq, memkv)


def _merge_kernel(on_ref, og_ref, ox_ref, mg_ref, x_ref, wn_ref, wg_ref, wx_ref, wo_ref, x1_ref):
    d = x_ref.shape[-1]
    merged = (mg_ref[:, 0:d].astype(F32) * _dot(on_ref[...], wn_ref[...])
              + mg_ref[:, d:2 * d].astype(F32) * _dot(og_ref[...], wg_ref[...])
              + mg_ref[:, 2 * d:3 * d].astype(F32) * _dot(ox_ref[...], wx_ref[...]))
    x1_ref[...] = x_ref[...] + _dot(merged.astype(BF16), wo_ref[...])


def _merge(o_nsa, o_gla, o_x, mg, x, wn, wg, wx, wo):
    m, d = x.shape
    tm = min(512, m)
    assert m % tm == 0
    row = lambda w: pl.BlockSpec((tm, w), lambda i: (i, 0))
    const = lambda shape: pl.BlockSpec(shape, lambda i: (0,) * len(shape))
    return pl.pallas_call(
        _merge_kernel,
        grid=(m // tm,),
        in_specs=[row(512), row(512), row(512), row(3 * d), row(d),
                  const((512, d)), const((512, d)), const((512, d)), const((d, d))],
        out_specs=row(d),
        out_shape=jax.ShapeDtypeStruct((m, d), F32),
        compiler_params=_cparams(1),
    )(o_nsa, o_gla, o_x, mg, x, wn, wg, wx, wo)


def _ffn_seq_kernel(x_ref, past_ref, g_ref, wup_ref, cw_ref, cb_ref, wdn_ref, y_ref, tail_ref, carry_ref):
    f = cw_ref.shape[-1]
    tm = x_ref.shape[1]

    @pl.when(pl.program_id(1) == 0)
    def _():
        carry_ref[...] = jnp.zeros(carry_ref.shape, F32)
        carry_ref[6:8, :] = past_ref[0]

    x1 = x_ref[0]
    ug = _dot(_rms(x1, g_ref[...]).astype(BF16), wup_ref[...])
    u = ug[:, 0:f]
    g = ug[:, f:2 * f]
    row = lax.broadcasted_iota(jnp.int32, (tm, f), 0)
    p1 = carry_ref[7:8, :]
    p2 = carry_ref[6:7, :]
    gm1 = jnp.where(row == 0, p1, pltpu.roll(g, 1, 0))
    gm2 = jnp.where(row == 0, p2, jnp.where(row == 1, p1, pltpu.roll(g, 2, 0)))
    gc = cb_ref[...] + cw_ref[0:1, :] * gm2 + cw_ref[1:2, :] * gm1 + cw_ref[2:3, :] * g
    y_ref[0] = x1 + _dot((_gelu(gc) * u).astype(BF16), wdn_ref[...])
    carry_ref[...] = g[tm - 8:tm, :]
    tail_ref[0] = g[tm - 8:tm, :]


def _ffn_seq(x1, conv_past, g_ffn, w_up, conv_w, conv_b, w_down):
    b, t, d = x1.shape
    f = conv_w.shape[-1]
    tm = min(512, t)
    assert t % tm == 0 and tm >= 8
    const = lambda shape: pl.BlockSpec(shape, lambda i, j: (0,) * len(shape), pipeline_mode=pl.Buffered(1))
    return pl.pallas_call(
        _ffn_seq_kernel,
        grid=(b, t // tm),
        in_specs=[pl.BlockSpec((1, tm, d), lambda i, j: (i, j, 0)),
                  pl.BlockSpec((1, 2, f), lambda i, j: (i, 0, 0)),
                  const((1, d)), const((d, 2 * f)), const((3, f)), const((1, f)), const((f, d))],
        out_specs=[pl.BlockSpec((1, tm, d), lambda i, j: (i, j, 0)),
                   pl.BlockSpec((1, 8, f), lambda i, j: (i, 0, 0))],
        out_shape=[jax.ShapeDtypeStruct((b, t, d), F32), jax.ShapeDtypeStruct((b, 8, f), F32)],
        scratch_shapes=[pltpu.VMEM((8, f), F32)],
        compiler_params=_cparams(2),
    )(x1, conv_past, g_ffn, w_up, conv_w, conv_b, w_down)


def _ffn_step_kernel(x_ref, p0_ref, p1_ref, g_ref, wup_ref, cw_ref, cb_ref, wdn_ref, y_ref, gnew_ref):
    f = cw_ref.shape[-1]
    x1 = x_ref[...]
    ug = _dot(_rms(x1, g_ref[...]).astype(BF16), wup_ref[...])
    u = ug[:, 0:f]
    g = ug[:, f:2 * f]
    gc = cb_ref[...] + cw_ref[0:1, :] * p0_ref[...] + cw_ref[1:2, :] * p1_ref[...] + cw_ref[2:3, :] * g
    y_ref[...] = x1 + _dot((_gelu(gc) * u).astype(BF16), wdn_ref[...])
    gnew_ref[...] = g


def _ffn_step(x1, p0, p1, g_ffn, w_up, conv_w, conv_b, w_down):
    m, d = x1.shape
    f = conv_w.shape[-1]
    full = lambda shape: pl.BlockSpec(shape, lambda i: (0,) * len(shape))
    return pl.pallas_call(
        _ffn_step_kernel,
        grid=(1,),
        in_specs=[full((m, d)), full((m, f)), full((m, f)), full((1, d)), full((d, 2 * f)), full((3, f)),
                  full((1, f)), full((f, d))],
        out_specs=[full((m, d)), full((m, f))],
        out_shape=[jax.ShapeDtypeStruct((m, d), F32), jax.ShapeDtypeStruct((m, f), F32)],
        compiler_params=_cparams(1),
    )(x1, p0, p1, g_ffn, w_up, conv_w, conv_b, w_down)


def _decode_kernel(pt_ref, *refs, top, n_sel, n_pages):
    del pt_ref
    page_refs = refs[:n_pages]
    (q8_ref, new_ref, neww_ref, cwin_ref, gt_ref, bc_ref, bs_ref, bw_ref, b0_ref, ovt_ref, et_ref, pe_ref, w1_ref,
     w2_ref, gk0_ref, p64_ref, o_ref, wout_ref) = refs[n_pages:]
    cpp = page_refs[0].shape[1]
    page = cpp * CMP_STRIDE
    length = n_pages * page
    n_chunks = length // CMP_STRIDE

    def load_j(j):
        return jnp.concatenate([pg[0, :, j, 0:256] for pg in page_refs], axis=0)

    def token_rows(lo):
        return jnp.concatenate([pg[0, :, :, lo:lo + 128].reshape(page, 128) for pg in page_refs], axis=0)

    q8 = q8_ref[0]
    q8f = q8.astype(F32)
    rowk = lax.broadcasted_iota(jnp.int32, (8, 128), 0) // NSA_GROUP
    lane_half = lax.broadcasted_iota(jnp.int32, (8, 128), 1) // HEAD_DIM

    def half_mask(x):
        return jnp.where(rowk == lane_half, x, 0.0)

    kc, vc = _compress_core(load_j, n_chunks, pe_ref, w1_ref, w2_ref, gk0_ref, p64_ref)
    kc = kc.astype(BF16)
    vc = vc.astype(BF16)

    p_c = _masked_softmax(_dot_nt(q8, kc) + bc_ref[...], 1).astype(BF16)
    o_c = half_mask(_dot(p_c, vc))
    pf = p_c.astype(F32)
    p2 = jnp.concatenate([jnp.sum(pf[0:4], axis=0, keepdims=True), jnp.sum(pf[4:8], axis=0, keepdims=True),
                          jnp.zeros((126, n_chunks), F32)], axis=0)
    p2_hi = p2.astype(BF16)
    p2_lo = (p2 - p2_hi.astype(F32)).astype(BF16)
    imp_t = _dot_nt(ovt_ref[...], p2_hi) + _dot_nt(ovt_ref[...], p2_lo)
    nsp = imp_t.shape[0]
    jj = lax.broadcasted_iota(jnp.int32, (nsp, 128), 0)
    tblk = length // SEL_BLOCK
    forced = (jj == 0) | (jj == tblk) | (jj == tblk - 1)
    sel_t = _select_with_forced(jnp.where(jj < n_sel, imp_t, -2e38), forced, top, 0)
    pen = jnp.where(sel_t, 0.0, SEL_PENALTY).T
    pen8 = jnp.concatenate([jnp.broadcast_to(pen[0:1], (4, nsp)), jnp.broadcast_to(pen[1:2], (4, nsp))], axis=0)

    new = new_ref[0]
    neww = neww_ref[0]
    b0 = b0_ref[...]

    def attend(s_parts, s_new, v_parts, v_new):
        m = s_new
        for s in s_parts:
            m = jnp.maximum(m, jnp.max(s, axis=-1, keepdims=True))
        pn = jnp.exp(s_new - m)
        l = pn
        o = pn.astype(BF16).astype(F32) * v_new.astype(BF16).astype(F32)
        for s, v in zip(s_parts, v_parts):
            pp = jnp.exp(s - m)
            l = l + jnp.sum(pp, axis=-1, keepdims=True)
            o = o + _dot(pp.astype(BF16), v)
        return half_mask(o / l)

    def new_score(k_new):
        return jnp.sum(q8f * k_new.astype(BF16).astype(F32), axis=-1, keepdims=True) + b0

    k_past = token_rows(256).astype(BF16)
    v_past = token_rows(384).astype(BF16)
    s_past = _dot_nt(q8, k_past) + _dot_nt(pen8[:, 0:128].astype(BF16), et_ref[...]) + bs_ref[...]
    lane = lax.broadcasted_iota(jnp.int32, (8, nsp), 1)
    pen_new = jnp.sum(jnp.where(lane == tblk, pen8, 0.0), axis=-1, keepdims=True)
    o_s = attend([s_past], new_score(new[:, 256:384]) + pen_new, [v_past], new[:, 384:512])

    cw = cwin_ref[0]
    s_w = _dot_nt(q8, cw[:, 0:128].astype(BF16)) + bw_ref[...]
    o_w = attend([s_w], new_score(neww[:, 0:128]), [cw[:, 128:256].astype(BF16)], neww[:, 128:256])

    gt = gt_ref[0]
    o_ref[0] = gt[:, 0:1] * o_c + gt[:, 1:2] * o_s + gt[:, 2:3] * o_w

    wl = cw.shape[0]
    wrow = lax.broadcasted_iota(jnp.int32, cw.shape, 0)
    wout_ref[0] = jnp.where(wrow == wl - 1, neww, pltpu.roll(cw, wl - 1, 0))


def _nsa_decode(page_table, cache2d, q8, new_rows, new_win, cache_win, gates8, bc, bs, bw, b0, ov, et,
                pe, w1, w2, gk0, p64):
    db, n_pages = page_table.shape
    page = cache2d.shape[1]
    length = n_pages * page
    n_chunks = length // CMP_STRIDE
    cpp = page // CMP_STRIDE
    cache4d = cache2d.reshape(cache2d.shape[0], cpp, CMP_STRIDE, 512)
    n_sel = -(-(length + 1) // SEL_BLOCK)
    top = min(SEL_TOP, n_sel)
    wl = cache_win.shape[1]
    nsp = ov.shape[0]
    const = lambda shape: pl.BlockSpec(shape, lambda i, pt: (0,) * len(shape))
    per_b = lambda shape: pl.BlockSpec((1,) + shape, lambda i, pt: (i,) + (0,) * len(shape))
    page_spec = lambda u: pl.BlockSpec((1, cpp, CMP_STRIDE, 512), lambda i, pt: (pt[i, u], 0, 0, 0))
    grid_spec = pltpu.PrefetchScalarGridSpec(
        num_scalar_prefetch=1,
        grid=(db,),
        in_specs=[page_spec(u) for u in range(n_pages)] + [
                  per_b((8, 128)), per_b((1, 512)), per_b((1, 256)), per_b((wl, 256)), per_b((8, 128)),
                  const((8, n_chunks)), const((8, length)), const((8, wl)), const((8, 1)),
                  const((nsp, n_chunks)), const((length, 128)),
                  const((2, CMP_STRIDE, 1, 256)), const((2, CMP_STRIDE * 256, 256)), const((256, 256)),
                  const((1, 128)), const((512, 512))],
        out_specs=[per_b((8, 128)), per_b((wl, 256))],
    )
    return pl.pallas_call(
        functools.partial(_decode_kernel, top=top, n_sel=n_sel, n_pages=n_pages),
        grid_spec=grid_spec,
        out_shape=[jax.ShapeDtypeStruct((db, 8, 128), F32), jax.ShapeDtypeStruct((db, wl, 256), F32)],
        compiler_params=_cparams(1),
    )(page_table, *([cache4d] * n_pages), q8, new_rows, new_win, cache_win, gates8, bc, bs, bw, b0, ov, et,
      pe, w1, w2, gk0, p64)


def _bucket_table():
    n = np.arange(MAX_DISTANCE + 1)
    max_exact = N_BUCKETS // 2
    nf = np.maximum(n, 1).astype(np.float32)
    large = max_exact + (np.log(nf / np.float32(max_exact)) / np.float32(math.log(MAX_DISTANCE / max_exact))
                         * np.float32(N_BUCKETS - max_exact)).astype(np.int32)
    return np.where(n < max_exact, n, np.minimum(large, N_BUCKETS - 1)).astype(np.int32)


def _bias_lookup(tb, rel, valid):
    idx = np.clip(rel, 0, MAX_DISTANCE)
    vals = jnp.moveaxis(tb[idx], -1, 0)
    return jnp.where(jnp.asarray(valid)[None], vals, NEG)


def _overlap(n_cmp_pad, n_cmp, n_sel_pad, n_sel):
    cs = (np.arange(n_cmp_pad) * CMP_STRIDE)[:, None]
    ss = (np.arange(n_sel_pad) * SEL_BLOCK)[None, :]
    ov = (cs < ss + SEL_BLOCK) & (cs + CMP_BLOCK > ss)
    ov &= (np.arange(n_cmp_pad) < n_cmp)[:, None] & (np.arange(n_sel_pad) < n_sel)[None, :]
    return jnp.asarray(ov.astype(np.float32), dtype=BF16)


def _stack_rows(x):
    return x.reshape(NSA_KV, NSA_GROUP * Q_BLOCK, x.shape[-1])


def _toeplitz(tbr, shift, width, max_valid):
    n = width + Q_BLOCK - 1
    u = np.arange(n)
    xs = shift - np.where(u < width, u, u - n)
    fvec = _bias_lookup(tbr, xs, (xs >= 0) & (xs <= max_valid))
    h = fvec.shape[0]
    return jnp.tile(fvec, (1, Q_BLOCK))[:, :Q_BLOCK * (n - 1)].reshape(h, Q_BLOCK, n - 1)[:, :, :width]


def _bias_descending(tb, top):
    far = jnp.broadcast_to(tb[MAX_DISTANCE][:, None], (tb.shape[1], top - MAX_DISTANCE + 1))
    return jnp.concatenate([far, tb[MAX_DISTANCE - 1:0:-1].T], axis=1)


def _prompt_tables(tb):
    tbr = (tb - tb[MAX_DISTANCE][None, :]) * LOG2E
    i = np.arange(Q_BLOCK)[:, None]

    def table(rel, valid):
        return _stack_rows(_bias_lookup(tbr, rel, valid))

    n_r = 4
    m_sel = _stack_rows(_toeplitz(tbr, Q_BLOCK * (n_r - 1), NEAR_TILE + Q_BLOCK * (n_r - 1), 1 << 30))
    tsel = jnp.stack([m_sel[:, :, Q_BLOCK * (n_r - 1 - r):Q_BLOCK * (n_r - 1 - r) + NEAR_TILE] for r in range(n_r)],
                     axis=1)
    n_v = WINDOW // Q_BLOCK + 1
    m_win = _stack_rows(_toeplitz(tbr, Q_BLOCK * (n_v - 1), WIN_TILE + Q_BLOCK * (n_v - 1), WINDOW - 1))
    twin = jnp.stack([m_win[:, :, Q_BLOCK * (n_v - 1 - v):Q_BLOCK * (n_v - 1 - v) + WIN_TILE] for v in range(n_v)],
                     axis=1)
    w = np.arange(16)[None, :] - 9
    rel = i - CMP_STRIDE * w - (CMP_BLOCK - 1)
    a = table(rel, rel >= 0)
    hi = a.astype(BF16)
    lo = (a - hi.astype(F32)).astype(BF16)
    acmp = jnp.concatenate([hi, lo, jnp.full(hi.shape[:2] + (1,), NEG, BF16),
                            jnp.zeros(hi.shape[:2] + (128 - 33,), BF16)], axis=-1)
    return jnp.swapaxes(tsel, 2, 3), jnp.swapaxes(twin, 2, 3), jnp.swapaxes(acmp, 1, 2)


def kernel(x_prompt, x_sample, cache_kv, cache_win, state_gla, state_conv, cache_mem, page_table, mem_prompt,
           g_mix, w_in, g_nsa_q, g_nsa_k, cmp_k_pe, cmp_k_w1, cmp_k_w2, cmp_v_pe, cmp_v_w1, cmp_v_w2,
           rel_bias, w_gla_gate, b_gla_gate, g_gla_o, g_mem, w_mem_kv, g_x_q, g_x_k,
           w_nsa_out, w_gla_out, w_x_out, w_o, g_ffn, w_up, conv_w, conv_b, w_down):
    bp, t, d = x_prompt.shape
    db = x_sample.shape[0]
    f = conv_w.shape[-1]

    offs = np.cumsum((0,) + IN_SIZES)
    segs = [w_in[:, offs[i]:offs[i + 1]] for i in range(len(IN_SIZES))]
    w_pad = jnp.concatenate([jnp.pad(s, ((0, 0), (0, pw - s.shape[1]))) for s, pw in zip(segs, PAD_SIZES)],
                            axis=1).astype(BF16)
    row = lambda v: v.reshape(1, -1).astype(F32)
    gq = row(jnp.tile(g_nsa_q, NSA_HEADS))
    gk0 = row(jnp.tile(g_nsa_k[0], NSA_KV))
    gk1 = row(jnp.tile(g_nsa_k[1], NSA_KV))
    gk2 = row(jnp.tile(g_nsa_k[2], NSA_KV))
    gxq = row(jnp.tile(g_x_q, X_HEADS))
    gxk = row(jnp.tile(g_x_k, X_HEADS))
    p64 = _block_ones(512, HEAD_DIM)
    p128 = _block_ones(512, X_DIM)
    bd2 = lambda a: jnp.concatenate([jnp.concatenate([a, jnp.zeros_like(a)], -1),
                                     jnp.concatenate([jnp.zeros_like(a), a], -1)], -2)
    bd4 = lambda a, c: jnp.concatenate([jnp.concatenate([bd2(a), jnp.zeros_like(bd2(a))], -1),
                                        jnp.concatenate([jnp.zeros_like(bd2(c)), bd2(c)], -1)], -2)
    pe = jnp.concatenate([jnp.tile(cmp_k_pe, (1, NSA_KV)), jnp.tile(cmp_v_pe, (1, NSA_KV))],
                         axis=-1).reshape(2, CMP_STRIDE, 1, 256)
    w1 = bd4(cmp_k_w1, cmp_v_w1).reshape(2, CMP_STRIDE * 256, 256).astype(BF16)
    w2 = bd4(cmp_k_w2, cmp_v_w2).astype(BF16)
    wg_pad = jnp.pad(w_gla_gate, ((0, 128 - GLA_RANK), (0, 0))).astype(BF16)
    tb = rel_bias.astype(F32)[_bucket_table()]
    wn, wgo, wx, wo = (w.astype(BF16) for w in (w_nsa_out, w_gla_out, w_x_out, w_o))
    wup = w_up.astype(BF16)
    wdn = w_down.astype(BF16)
    ggo = row(g_gla_o)

    rows_p, win_p, _, gates, gla_in, xq, mg, ka, kw, vst, vwt, qt = _proj_in(
        x_prompt, row(g_mix), w_pad, gq, gk1, gk2, gxq, p64, p128, True)
    memkv_p = _memory_kv(mem_prompt, row(g_mem), w_mem_kv.astype(BF16), gxk, p128)
    kc, vct = _compress(rows_p, pe, w1, w2, gk0, p64)
    n_chunks = t // CMP_STRIDE
    n_sel = -(-t // SEL_BLOCK)
    tsel, twin, acmp = _prompt_tables(tb)
    ovt = _overlap(n_chunks, n_chunks - 1, 128, n_sel).T
    o_nsa = _nsa_prompt(qt, kc, vct, ka, kw, vst, vwt, gates, tsel, twin, acmp, ovt)
    s0t = jnp.zeros((bp, GLA_HEADS, GLA_DV, GLA_DK), F32)
    o_gla, st = _gla_prompt(gla_in, s0t, wg_pad, row(b_gla_gate), ggo)
    o_x = _xatt(xq, memkv_p)
    m = bp * t
    x1 = _merge(o_nsa.reshape(m, 512), o_gla.reshape(m, 512), o_x.reshape(m, 512), mg.reshape(m, 3 * d),
                x_prompt.reshape(m, d), wn, wgo, wx, wo)
    y_p, tail = _ffn_seq(x1.reshape(bp, t, d), jnp.zeros((bp, 2, f), F32), row(g_ffn), wup, conv_w, row(conv_b), wdn)
    wl_p = min(WINDOW, t)
    out_rows_p = rows_p.reshape(bp, t, 4, NSA_KV, HEAD_DIM)
    out_win_p = win_p[:, t - wl_p:].reshape(bp, wl_p, 2, NSA_KV, HEAD_DIM)
    out_gla_p = jnp.swapaxes(st, 2, 3)
    out_conv_p = tail[:, 6:8]
    out_mem_p = memkv_p.reshape(bp, -1, 2, X_HEADS, X_DIM)

    n_pages = page_table.shape[1]
    page = cache_kv.shape[1]
    length = n_pages * page
    wl = cache_win.shape[1]
    rows_s, win_s, qn_s, gates_s, gla_s, xq_s, mg_s = (a[0] for a in _proj_in(
        x_sample.reshape(1, db, d), row(g_mix), w_pad, gq, gk1, gk2, gxq, p64, p128, False))

    eye = jnp.eye(NSA_KV, dtype=BF16)
    q8 = (qn_s.reshape(db, NSA_KV, NSA_GROUP, 1, HEAD_DIM) * eye[None, :, None, :, None]).reshape(db, 8, 128)
    gates8 = jnp.pad(gates_s[:, 0:24].reshape(db, 8, 3), ((0, 0), (0, 0), (0, 125)))
    n_chunks_s = length // CMP_STRIDE
    n_sel_s = -(-(length + 1) // SEL_BLOCK)
    nsp = -(-n_sel_s // 128) * 128
    cidx = np.arange(n_chunks_s)
    rel_c = length - (cidx * CMP_STRIDE + CMP_BLOCK - 1)
    bc = _bias_lookup(tb, rel_c, (rel_c >= 0) & (cidx < n_chunks_s - 1))
    bs = _bias_descending(tb, length)
    bw = jnp.where(jnp.asarray(np.arange(wl, 0, -1) < WINDOW)[None], _bias_descending(tb, wl), NEG)
    b0 = tb[0].reshape(8, 1)
    ov_s = _overlap(n_chunks_s, n_chunks_s - 1, nsp, n_sel_s).T
    assert length // SEL_BLOCK <= 128
    et = jnp.asarray((np.arange(length)[:, None] // SEL_BLOCK == np.arange(128)[None, :]).astype(np.float32),
                     dtype=BF16)
    o8, win_new = _nsa_decode(page_table, cache_kv.reshape(cache_kv.shape[0], page, 512), q8,
                              rows_s.reshape(db, 1, 512), win_s.reshape(db, 1, 256), cache_win.reshape(db, wl, 256),
                              gates8, bc, bs, bw, b0, ov_s, et, pe, w1, w2, gk0, p64)
    o8 = o8.reshape(db, NSA_KV, NSA_GROUP, NSA_KV, HEAD_DIM)
    o_nsa_s = jnp.stack([o8[:, 0, :, 0], o8[:, 1, :, 1]], axis=1).reshape(db, 512).astype(BF16)

    wgt = jnp.pad(w_gla_gate.T, ((0, 0), (0, 128 - GLA_RANK))).reshape(GLA_HEADS, GLA_DK, 128)
    o_gla_s, gla_state_s = _gla_step(
        gla_s[:, 0:256].reshape(db, GLA_HEADS, GLA_DK, 1), gla_s[:, 256:512].reshape(db, GLA_HEADS, GLA_DK, 1),
        gla_s[:, 1024:1152].reshape(db, 1, 128), gla_s[:, 512:1024].reshape(db, GLA_HEADS, 1, GLA_DV),
        gla_s[:, 1152:1664].reshape(db, GLA_HEADS, 1, GLA_DV), state_gla.astype(F32), wgt,
        b_gla_gate.reshape(GLA_HEADS, GLA_DK, 1), ggo)
    o_gla_s = o_gla_s.reshape(db, 512).astype(BF16)

    xq_pad = jnp.pad(xq_s.reshape(db, 1, 512), ((0, 0), (0, 15), (0, 0)))
    o_x_s = _xatt(xq_pad, cache_mem.reshape(db, -1, 1024))[:, 0]
    x1_s = _merge(o_nsa_s, o_gla_s, o_x_s, mg_s, x_sample.reshape(db, d), wn, wgo, wx, wo)
    y_s, g_new = _ffn_step(x1_s, state_conv[:, 0], state_conv[:, 1], row(g_ffn), wup, conv_w, row(conv_b), wdn)

    out_rows_s = rows_s.reshape(db, 1, 4, NSA_KV, HEAD_DIM)
    out_win_s = win_new.reshape(db, wl, 2, NSA_KV, HEAD_DIM)
    out_conv_s = jnp.stack([state_conv[:, 1], g_new], axis=1)
    return (y_p, y_s.reshape(db, 1, d), out_rows_p, out_win_p, out_gla_p, out_conv_p, out_mem_p,
            out_rows_s, out_win_s, gla_state_s, out_conv_s)
```

```python
import functools
import math

import numpy as np
import jax
import jax.numpy as jnp
from jax import lax
from jax.experimental import pallas as pl
from jax.experimental.pallas import tpu as pltpu

F32 = jnp.float32
BF16 = jnp.bfloat16

NSA_HEADS = 8
NSA_KV = 2
NSA_GROUP = 4
HEAD_DIM = 64
CMP_BLOCK = 32
CMP_STRIDE = 16
SEL_BLOCK = 64
SEL_TOP = 16
WINDOW = 512
Q_BLOCK = 128
GLA_HEADS = 4
GLA_DK = 64
GLA_DV = 128
GLA_RANK = 16
GLA_TAU = 16.0
GLA_CHUNK = 64
X_HEADS = 4
X_DIM = 128
N_BUCKETS = 32
MAX_DISTANCE = 128
EPS = 1e-6
LOG2E = math.log2(math.e)
NEG = -1e30
TINY = 1e-30
SEL_PENALTY = -1e9
MASKED_BELOW = -5e29
EXP_CLAMP = 80.0

IN_SIZES = (512, 768, 24, 256, 256, 512, 16, 512, 512, 3072)
PAD_SIZES = (512, 768, 128, 256, 256, 512, 128, 512, 512, 3072)
PAD_OFFS = tuple(int(v) for v in np.cumsum((0,) + PAD_SIZES))
D_IN_PAD = PAD_OFFS[-1]
GLA_IN_W = 256 + 256 + 512 + 128 + 512

VMEM_LIMIT = 56 * 1024 * 1024
FAR_TILE = 256
NEAR_TILE = 512
WIN_TILE = WINDOW + Q_BLOCK


def _cparams(n_axes):
    return pltpu.CompilerParams(dimension_semantics=("arbitrary",) * n_axes, vmem_limit_bytes=VMEM_LIMIT)


def _dot(a, b):
    return jnp.dot(a, b, preferred_element_type=F32)


def _dot_nt(a, b):
    return lax.dot_general(a, b, (((1,), (1,)), ((), ())), preferred_element_type=F32)


def _dot_tn(a, b):
    return lax.dot_general(a, b, (((0,), (0,)), ((), ())), preferred_element_type=F32)


def _split_dot(x, m):
    hi = x.astype(BF16)
    lo = (x - hi.astype(F32)).astype(BF16)
    return _dot(hi, m) + _dot(lo, m)


def _rms(x, g):
    return x * lax.rsqrt(jnp.mean(x * x, axis=-1, keepdims=True) + EPS) * g


def _group_rms(x, pmat, gsize, g):
    ss = _split_dot(x * x, pmat)
    return x * lax.rsqrt(ss * (1.0 / gsize) + EPS) * g


def _gelu(x):
    return 0.5 * x * (1.0 + jnp.tanh(math.sqrt(2.0 / math.pi) * (x + 0.044715 * (x * x * x))))


def _sigmoid(x):
    return 1.0 / (1.0 + jnp.exp(-x))


def _block_ones(n, gsize):
    i = np.arange(n) // gsize
    return jnp.asarray((i[:, None] == i[None, :]).astype(np.float32), dtype=BF16)


def _proj_kernel(x_ref, gmix_ref, w_ref, gq_ref, gk1_ref, gk2_ref, gxq_ref, p64_ref, p128_ref,
                 rows_ref, win_ref, qn_ref, gates_ref, gla_ref, xq_ref, mg_ref, *attn_refs):
    x = x_ref[0]
    h = _rms(x, gmix_ref[...]).astype(BF16)
    o = PAD_OFFS

    def seg(i):
        return _dot(h, w_ref[:, o[i]:o[i + 1]])

    p64 = p64_ref[...]
    p64s = p64_ref[0:128, 0:128]
    qn = _group_rms(seg(0), p64, HEAD_DIM, gq_ref[...]) * (HEAD_DIM ** -0.5)
    qn_ref[0] = qn.astype(BF16)
    if attn_refs:
        qt_ref = attn_refs[4]
        for u in range(x.shape[0] // Q_BLOCK):
            qt_ref[0, u] = (qn[Q_BLOCK * u:Q_BLOCK * (u + 1)] * LOG2E).T.astype(BF16)

    kv = seg(1)
    k_sel = _group_rms(kv[:, 256:384], p64s, HEAD_DIM, gk1_ref[...])
    k_win = _group_rms(kv[:, 512:640], p64s, HEAD_DIM, gk2_ref[...])
    rows_ref[0, :, 0:256] = kv[:, 0:256]
    rows_ref[0, :, 256:384] = k_sel
    rows_ref[0, :, 384:512] = kv[:, 384:512]
    win_ref[0, :, 0:128] = k_win
    win_ref[0, :, 128:256] = kv[:, 640:768]
    if attn_refs:
        ka_ref, kw_ref, vst_ref, vwt_ref, _ = attn_refs
        tm = x.shape[0]
        tpos = pl.program_id(1) * tm + lax.broadcasted_iota(jnp.int32, (tm, 128), 0)
        blk = lax.broadcasted_iota(jnp.int32, (tm, 128), 1)
        ka_ref[0, :, 0:128] = k_sel.astype(BF16)
        ka_ref[0, :, 128:256] = jnp.where(tpos // SEL_BLOCK == blk, 1.0, 0.0).astype(BF16)
        kw_ref[0] = k_win.astype(BF16)
        for u in range(tm // FAR_TILE):
            vst_ref[0, u] = kv[FAR_TILE * u:FAR_TILE * (u + 1), 384:512].T.astype(BF16)
        for u in range(tm // Q_BLOCK):
            vwt_ref[0, u] = kv[Q_BLOCK * u:Q_BLOCK * (u + 1), 640:768].T.astype(BF16)

    gates_ref[0] = _sigmoid(seg(2))
    gla_ref[0, :, 0:256] = seg(3) * (GLA_DK ** -0.5)
    gla_ref[0, :, 256:512] = seg(4)
    gla_ref[0, :, 512:1024] = seg(5)
    gla_ref[0, :, 1024:1152] = seg(6)
    gla_ref[0, :, 1152:1664] = seg(7)
    xq = _group_rms(seg(8), p128_ref[...], X_DIM, gxq_ref[...]) * (X_DIM ** -0.5)
    xq_ref[0] = xq.astype(BF16)
    mg_ref[0] = _sigmoid(seg(9)).astype(BF16)


def _proj_in(x, g_mix, w_pad, gq, gk1, gk2, gxq, p64, p128, attn_layout):
    b, t, d = x.shape
    tm = min(512, t)
    assert t % tm == 0
    widths = [512, 256, 512, 128, GLA_IN_W, 512, 3072]
    dtypes = [F32, F32, BF16, F32, F32, BF16, BF16]
    out_specs = [pl.BlockSpec((1, tm, w), lambda i, j: (i, j, 0)) for w in widths]
    out_shape = [jax.ShapeDtypeStruct((b, t, w), dt) for w, dt in zip(widths, dtypes)]
    if attn_layout:
        assert tm % Q_BLOCK == 0
        for w in (256, 128):
            out_specs.append(pl.BlockSpec((1, tm, w), lambda i, j: (i, j, 0)))
            out_shape.append(jax.ShapeDtypeStruct((b, t, w), BF16))
        assert tm % FAR_TILE == 0
        for rows, width in ((128, FAR_TILE), (128, Q_BLOCK), (512, Q_BLOCK)):
            out_specs.append(pl.BlockSpec((1, tm // width, rows, width), lambda i, j: (i, j, 0, 0)))
            out_shape.append(jax.ShapeDtypeStruct((b, t // width, rows, width), BF16))
    const = lambda shape: pl.BlockSpec(shape, lambda i, j: (0,) * len(shape), pipeline_mode=pl.Buffered(1))
    return pl.pallas_call(
        _proj_kernel,
        grid=(b, t // tm),
        in_specs=[pl.BlockSpec((1, tm, d), lambda i, j: (i, j, 0)),
                  const((1, d)), const((d, D_IN_PAD)), const((1, 512)), const((1, 128)), const((1, 128)),
                  const((1, 512)), const((512, 512)), const((512, 512))],
        out_specs=out_specs,
        out_shape=out_shape,
        compiler_params=_cparams(2),
    )(x, g_mix, w_pad, gq, gk1, gk2, gxq, p64, p128)


def _memkv_kernel(m_ref, g_ref, w_ref, gk_ref, p128_ref, o_ref):
    h = _rms(m_ref[0], g_ref[...]).astype(BF16)
    kv = _dot(h, w_ref[...])
    o_ref[0, :, 0:512] = _group_rms(kv[:, 0:512], p128_ref[...], X_DIM, gk_ref[...])
    o_ref[0, :, 512:1024] = kv[:, 512:1024]


def _memory_kv(mem, g_mem, w_mem, gxk, p128):
    b, m, d = mem.shape
    const = lambda shape: pl.BlockSpec(shape, lambda i: (0,) * len(shape))
    return pl.pallas_call(
        _memkv_kernel,
        grid=(b,),
        in_specs=[pl.BlockSpec((1, m, d), lambda i: (i, 0, 0)), const((1, d)), const((d, 1024)),
                  const((1, 512)), const((512, 512))],
        out_specs=pl.BlockSpec((1, m, 1024), lambda i: (i, 0, 0)),
        out_shape=jax.ShapeDtypeStruct((b, m, 1024), F32),
        compiler_params=_cparams(1),
    )(mem, g_mem, w_mem, gxk, p128)


def _compress_core(load_j, n_chunks, pe_ref, w1_ref, w2_ref, gk0_ref, p64_ref):
    xs = [load_j(j) for j in range(CMP_STRIDE)]
    halves = [_dot(jnp.concatenate([(x + pe_ref[r, j]).astype(BF16) for j, x in enumerate(xs)], axis=1), w1_ref[r])
              for r in range(2)]
    hid = halves[0] + pltpu.roll(halves[1], n_chunks - 1, 0)
    out = _dot(_gelu(hid).astype(BF16), w2_ref[...])
    row = lax.broadcasted_iota(jnp.int32, (n_chunks, 128), 0)
    live = row < n_chunks - 1
    kc = _group_rms(out[:, 0:128], p64_ref[0:128, 0:128], HEAD_DIM, gk0_ref[...])
    return jnp.where(live, kc, 0.0), jnp.where(live, out[:, 128:256], 0.0)


def _compress_kernel(rk_ref, rv_ref, pe_ref, w1_ref, w2_ref, gk0_ref, p64_ref, kc_ref, vc_ref, *, n_chunks):
    load_j = lambda j: jnp.concatenate([rk_ref[0, pl.ds(j, n_chunks, stride=CMP_STRIDE), :],
                                        rv_ref[0, pl.ds(j, n_chunks, stride=CMP_STRIDE), :]], axis=1)
    kc, vc = _compress_core(load_j, n_chunks, pe_ref, w1_ref, w2_ref, gk0_ref, p64_ref)
    kc_ref[0] = kc.astype(BF16)
    vc_ref[0] = vc.T.astype(BF16)


def _compress(rows, pe, w1, w2, gk0, p64):
    b, t, _ = rows.shape
    n_chunks = t // CMP_STRIDE
    const = lambda shape: pl.BlockSpec(shape, lambda i: (0,) * len(shape))
    return pl.pallas_call(
        functools.partial(_compress_kernel, n_chunks=n_chunks),
        grid=(b,),
        in_specs=[pl.BlockSpec((1, t, 128), lambda i: (i, 0, 0)), pl.BlockSpec((1, t, 128), lambda i: (i, 0, 1)),
                  const((2, CMP_STRIDE, 1, 256)), const((2, CMP_STRIDE * 256, 256)), const((256, 256)),
                  const((1, 128)), const((512, 512))],
        out_specs=[pl.BlockSpec((1, n_chunks, 128), lambda i: (i, 0, 0)),
                   pl.BlockSpec((1, 128, n_chunks), lambda i: (i, 0, 0))],
        out_shape=[jax.ShapeDtypeStruct((b, n_chunks, 128), BF16), jax.ShapeDtypeStruct((b, 128, n_chunks), BF16)],
        compiler_params=_cparams(1),
    )(rows, rows, pe, w1, w2, gk0, p64)


def _masked_softmax(s, axis, exp_fn=jnp.exp):
    m = jnp.maximum(jnp.max(s, axis=axis, keepdims=True), MASKED_BELOW)
    p = exp_fn(s - m)
    return p / jnp.maximum(jnp.sum(p, axis=axis, keepdims=True), TINY)


def _select_blocks(score, top, axis):
    pos = lax.broadcasted_iota(jnp.int32, score.shape, axis).astype(F32)
    sel = jnp.zeros(score.shape, jnp.bool_)
    for _ in range(top):
        mx = jnp.max(score, axis=axis, keepdims=True)
        idx = jnp.min(jnp.where(score == mx, pos, 1e9), axis=axis, keepdims=True)
        hit = pos == idx
        sel = jnp.logical_or(sel, hit)
        score = jnp.where(hit, -3e38, score)
    return sel


def _select_with_forced(imp, forced, top, axis):
    n_forced = 3
    assert top > n_forced
    return jnp.logical_or(forced, _select_blocks(jnp.where(forced, -3e38, imp), top - n_forced, axis))


def _values_times_probs(vt_tiles, p):
    out = None
    start = 0
    for vt in vt_tiles:
        part = _dot(vt, p[start:start + vt.shape[1]])
        start += vt.shape[1]
        out = part if out is None else out + part
    return out


def _softmax_stats_t(s_ref, bias, m_ref, l_ref):
    a_parts, p_parts = [], []
    for g in range(s_ref.shape[1] // Q_BLOCK):
        cs = slice(Q_BLOCK * g, Q_BLOCK * (g + 1))
        sg = s_ref[:, cs]
        if bias is not None:
            sg = sg + bias(cs)
        m_old = m_ref[:, cs]
        m_new = jnp.maximum(m_old, jnp.max(sg, axis=0, keepdims=True))
        alpha = jnp.exp2(m_old - m_new)
        p = jnp.exp2(sg - m_new)
        m_ref[:, cs] = m_new
        l_ref[:, cs] = alpha * l_ref[:, cs] + jnp.sum(p, axis=0, keepdims=True)
        a_parts.append(alpha)
        p_parts.append(p.astype(BF16))
    return jnp.concatenate(a_parts, axis=1), jnp.concatenate(p_parts, axis=1)


def _acc_update_t(acc_ref, alpha, vt_tiles, p):
    acc_ref[...] = alpha * acc_ref[...] + _values_times_probs(vt_tiles, p)


def _attn_kernel(qt_ref, kc_ref, vct_ref, ka_ref, vst_ref, kw_ref, vwt_ref, gt_ref, tsel_ref, twin_ref, acmp_ref,
                 ovt_ref, o_ref, sa_scr, sb_scr, m_scr, l_scr, acc_scr, pend_a, pend_p, *, top):
    k = pl.program_id(1)
    n = pl.program_id(2)
    cols = NSA_GROUP * Q_BLOCK
    vrow = pl.multiple_of(k * HEAD_DIM, HEAD_DIM)
    zero = jnp.zeros((HEAD_DIM, Q_BLOCK), BF16)
    parts = []
    for g in range(NSA_GROUP):
        piece = qt_ref[0, 0, HEAD_DIM * g:HEAD_DIM * (g + 1), :]
        parts.append(jnp.where(k == 0, jnp.concatenate([piece, zero], axis=0),
                               jnp.concatenate([zero, piece], axis=0)))
    qt = jnp.concatenate(parts, axis=1)

    nc = kc_ref.shape[1]
    blocks_per_q = Q_BLOCK // CMP_STRIDE
    cp = lax.broadcasted_iota(jnp.int32, (nc, 128), 0) - blocks_per_q * n + 9
    fr = lax.broadcasted_iota(jnp.int32, (nc, 128), 1)
    feat = ((fr < 32) & (cp == (fr & 15))) | ((fr == 32) & (cp > 15))
    kc_aug = jnp.concatenate([kc_ref[0], jnp.where(feat, 1.0, 0.0).astype(BF16)], axis=1)
    s_c = _dot(kc_aug, jnp.concatenate([qt, acmp_ref[0]], axis=0))
    p_c = _masked_softmax(s_c, 0, jnp.exp2).astype(BF16)
    o_c = _dot(vct_ref[0, pl.ds(vrow, HEAD_DIM), :], p_c)
    imp4 = _dot(ovt_ref[...], p_c)
    imp = imp4[:, 0:128] + imp4[:, 128:256] + imp4[:, 256:384] + imp4[:, 384:512]

    def value_tiles(ref, first, count):
        return [ref[0, first + d, pl.ds(vrow, HEAD_DIM), :] for d in range(count)]

    wt = jnp.maximum(n - WINDOW // Q_BLOCK, 0)
    s_w = _dot(kw_ref[0, pl.ds(pl.multiple_of(wt * Q_BLOCK, Q_BLOCK), WIN_TILE), :], qt) + twin_ref[0, 0]
    p_w = jnp.exp2(s_w - jnp.max(s_w, axis=0, keepdims=True))
    o_w = (_values_times_probs(value_tiles(vwt_ref, wt, WIN_TILE // Q_BLOCK), p_w.astype(BF16))
           / jnp.sum(p_w, axis=0, keepdims=True))

    jj = lax.broadcasted_iota(jnp.int32, (128, Q_BLOCK), 0)
    tpos = n * Q_BLOCK + lax.broadcasted_iota(jnp.int32, (128, Q_BLOCK), 1)
    tblk = tpos // SEL_BLOCK
    forced = (jj == 0) | (jj == tblk) | (jj == tblk - 1)
    sel_t = _select_with_forced(jnp.where(jj * SEL_BLOCK <= tpos, imp, NEG), forced, top, 0)
    pen_t = jnp.where(sel_t, 0.0, SEL_PENALTY).astype(BF16)
    rhs = jnp.concatenate([qt, jnp.concatenate([pen_t] * NSA_GROUP, axis=1)], axis=0)

    n_far = jnp.maximum(n - 2, 0) // 2
    r_near = n - 2 * n_far

    def issue_scores(tile, s_ref):
        s_ref[...] = _dot(ka_ref[0, pl.ds(pl.multiple_of(tile * FAR_TILE, FAR_TILE), FAR_TILE), :], rhs)

    def stats(s_ref, half_idx=None):
        bias = None
        if half_idx is not None:
            bias = lambda cs: tsel_ref[0, r_near, FAR_TILE * half_idx:FAR_TILE * (half_idx + 1), cs]
        return _softmax_stats_t(s_ref, bias, m_scr, l_scr)

    def accumulate(tile, alpha, p):
        _acc_update_t(acc_scr, alpha, value_tiles(vst_ref, tile, 1), p)

    m_scr[...] = jnp.full((1, cols), NEG, F32)
    l_scr[...] = jnp.zeros((1, cols), F32)
    acc_scr[...] = jnp.zeros((HEAD_DIM, cols), F32)
    pend_a[...] = jnp.ones((1, cols), F32)
    pend_p[...] = jnp.zeros((FAR_TILE, cols), BF16)
    odd = n_far % 2

    @pl.when(odd == 1)
    def _():
        issue_scores(0, sa_scr)
        accumulate(0, *stats(sa_scr))

    issue_scores(odd, sa_scr)

    @pl.loop(0, n_far // 2)
    def _(j):
        t0 = odd + 2 * j
        issue_scores(t0 + 1, sb_scr)
        alpha_a, p_a = stats(sa_scr)
        accumulate(jnp.maximum(t0 - 1, 0), pend_a[...], pend_p[...])
        issue_scores(t0 + 2, sa_scr)
        alpha_b, p_b = stats(sb_scr)
        accumulate(t0, alpha_a, p_a)
        pend_a[...] = alpha_b
        pend_p[...] = p_b

    accumulate(jnp.maximum(n_far - 1, 0), pend_a[...], pend_p[...])
    issue_scores(n_far + 1, sb_scr)
    accumulate(n_far, *stats(sa_scr, 0))
    accumulate(n_far + 1, *stats(sb_scr, 1))
    o_s = acc_scr[...] / l_scr[...]

    gtt = gt_ref[0].T

    def gate_row(branch):
        rows = [jnp.where(k == 0, gtt[3 * g + branch:3 * g + branch + 1],
                          gtt[12 + 3 * g + branch:12 + 3 * g + branch + 1]) for g in range(NSA_GROUP)]
        return jnp.concatenate(rows, axis=1)

    o_t = gate_row(0) * o_c + gate_row(1) * o_s + gate_row(2) * o_w
    left = jnp.concatenate([o_t[:, 0:128], o_t[:, 128:256]], axis=0).T
    right = jnp.concatenate([o_t[:, 256:384], o_t[:, 384:512]], axis=0).T
    o_ref[0] = jnp.concatenate([left, right], axis=1).astype(BF16)


def _nsa_prompt(qt, kc, vct, ka, kw, vst, vwt, gates, tsel, twin, acmp, ovt):
    b, t, _ = ka.shape
    nqb = t // Q_BLOCK
    nc = kc.shape[1]
    top = min(SEL_TOP, -(-t // SEL_BLOCK))
    assert t >= WIN_TILE and t % FAR_TILE == 0 and t // SEL_BLOCK <= 128
    n_win = WINDOW // Q_BLOCK
    per_b = lambda shape: pl.BlockSpec((1,) + shape, lambda i, k, n: (i,) + (0,) * len(shape))
    return pl.pallas_call(
        functools.partial(_attn_kernel, top=top),
        grid=(b, NSA_KV, nqb),
        in_specs=[pl.BlockSpec((1, 1, NSA_GROUP * HEAD_DIM, Q_BLOCK), lambda i, k, n: (i, n, k, 0)),
                  per_b((nc, 128)), per_b((128, nc)),
                  per_b((t, 256)), per_b((t // FAR_TILE, 128, FAR_TILE)), per_b((t, 128)), per_b((nqb, 128, Q_BLOCK)),
                  pl.BlockSpec((1, Q_BLOCK, 128), lambda i, k, n: (i, n, 0)),
                  pl.BlockSpec((1, 4, NEAR_TILE, 512), lambda i, k, n: (k, 0, 0, 0)),
                  pl.BlockSpec((1, 1, WIN_TILE, 512), lambda i, k, n: (k, jnp.minimum(n, n_win), 0, 0)),
                  pl.BlockSpec((1, 128, 512), lambda i, k, n: (k, 0, 0)),
                  pl.BlockSpec((128, nc), lambda i, k, n: (0, 0))],
        out_specs=pl.BlockSpec((1, Q_BLOCK, 256), lambda i, k, n: (i, n, k)),
        out_shape=jax.ShapeDtypeStruct((b, t, 512), BF16),
        scratch_shapes=[pltpu.VMEM((FAR_TILE, NSA_GROUP * Q_BLOCK), F32), pltpu.VMEM((FAR_TILE, NSA_GROUP * Q_BLOCK), F32),
                        pltpu.VMEM((1, NSA_GROUP * Q_BLOCK), F32), pltpu.VMEM((1, NSA_GROUP * Q_BLOCK), F32),
                        pltpu.VMEM((HEAD_DIM, NSA_GROUP * Q_BLOCK), F32),
                        pltpu.VMEM((1, NSA_GROUP * Q_BLOCK), F32), pltpu.VMEM((FAR_TILE, NSA_GROUP * Q_BLOCK), BF16)],
        compiler_params=_cparams(3),
    )(qt, kc, vct, ka, vst, kw, vwt, gates, tsel, twin, acmp, ovt)


def _log_sigmoid(z):
    return jnp.minimum(z, 0.0) - jnp.log1p(jnp.exp(-jnp.abs(z)))


def _gla_kernel(x_ref, s0_ref, wg_ref, bg_ref, ggo_ref, tri_ref, o_ref, st_ref, s_scr, *, n_chunks):
    @pl.when(pl.program_id(0) == 0)
    def _():
        s_scr[...] = s0_ref[...]

    c_len = GLA_CHUNK
    ct = n_chunks * c_len
    tri = tri_ref[...]
    ti = lax.broadcasted_iota(jnp.int32, (ct, ct), 0)
    si = lax.broadcasted_iota(jnp.int32, (ct, ct), 1)
    causal = (si <= ti) & (si // c_len == ti // c_len)

    def per_chunk(x, row):
        return jnp.concatenate([jnp.broadcast_to(x[c_len * c + row:c_len * c + row + 1], (c_len, x.shape[1]))
                                for c in range(n_chunks)], axis=0)

    for bi in range(x_ref.shape[0]):
        q = x_ref[bi, :, 0:256]
        kk = x_ref[bi, :, 256:512]
        v = x_ref[bi, :, 512:1024]
        lr = x_ref[bi, :, 1024:1152]
        r = x_ref[bi, :, 1152:1664]
        la = _log_sigmoid(_dot(lr.astype(BF16), wg_ref[...]) + bg_ref[...]) * (1.0 / GLA_TAU)
        a1 = la.astype(BF16)
        r1 = la - a1.astype(F32)
        a2 = r1.astype(BF16)
        a3 = (r1 - a2.astype(F32)).astype(BF16)
        cb = _dot(tri, a1) + _dot(tri, a2) + _dot(tri, a3)
        last = per_chunk(cb, c_len - 1)
        mid = per_chunk(cb, c_len // 2)
        qe = (q * jnp.exp(cb)).astype(BF16)
        qa = (q * jnp.exp(jnp.minimum(cb - mid, EXP_CLAMP))).astype(BF16)
        kb = (kk * jnp.exp(jnp.minimum(mid - cb, EXP_CLAMP))).astype(BF16)
        ke = (kk * jnp.exp(last - cb)).astype(BF16)
        for h in range(GLA_HEADS):
            ks = slice(GLA_DK * h, GLA_DK * (h + 1))
            vs = slice(GLA_DV * h, GLA_DV * (h + 1))
            att = jnp.where(causal, _dot_nt(qa[:, ks], kb[:, ks]), 0.0)
            vh = v[:, vs].astype(BF16)
            o_intra = _dot(att.astype(BF16), vh)
            st = s_scr[bi, h]
            o_inter = []
            for c in range(n_chunks):
                rs = slice(c_len * c, c_len * (c + 1))
                o_inter.append(_dot_nt(qe[rs, ks], st.astype(BF16)))
                dec = jnp.exp(cb[c_len * (c + 1) - 1:c_len * (c + 1), ks])
                st = st * dec + _dot_tn(vh[rs], ke[rs, ks])
            s_scr[bi, h] = st
            on = _rms(o_intra + jnp.concatenate(o_inter, axis=0), ggo_ref[...])
            rh = r[:, vs]
            o_ref[bi, :, vs] = (on * (rh * _sigmoid(rh))).astype(BF16)
    st_ref[...] = s_scr[...]


def _gla_prompt(gla_in, s0t, wg, bg, ggo):
    b, t, w = gla_in.shape
    ct = min(256, t)
    assert t % ct == 0 and ct % GLA_CHUNK == 0
    pos = np.arange(ct)
    tri = jnp.asarray(((pos[None, :] <= pos[:, None])
                       & (pos[None, :] // GLA_CHUNK == pos[:, None] // GLA_CHUNK)).astype(np.float32), dtype=BF16)
    const = lambda shape: pl.BlockSpec(shape, lambda j: (0,) * len(shape))
    state_shape = (b, GLA_HEADS, GLA_DV, GLA_DK)
    return pl.pallas_call(
        functools.partial(_gla_kernel, n_chunks=ct // GLA_CHUNK),
        grid=(t // ct,),
        in_specs=[pl.BlockSpec((b, ct, w), lambda j: (0, j, 0)), const(state_shape),
                  const((128, 256)), const((1, 256)), const((1, 128)), const((ct, ct))],
        out_specs=[pl.BlockSpec((b, ct, 512), lambda j: (0, j, 0)), const(state_shape)],
        out_shape=[jax.ShapeDtypeStruct((b, t, 512), BF16), jax.ShapeDtypeStruct(state_shape, F32)],
        scratch_shapes=[pltpu.VMEM(state_shape, F32)],
        compiler_params=_cparams(1),
    )(gla_in, s0t, wg, bg, ggo, tri)


def _gla_step_kernel(q_ref, k_ref, lr_ref, v_ref, r_ref, s_ref, wgt_ref, bgt_ref, ggo_ref, o_ref, sn_ref):
    lr = lr_ref[0]
    for h in range(GLA_HEADS):
        z = jnp.sum(wgt_ref[h] * lr, axis=-1, keepdims=True) + bgt_ref[h]
        a = jnp.exp(_log_sigmoid(z) * (1.0 / GLA_TAU))
        s0 = s_ref[0, h]
        kh = k_ref[0, h]
        qh = q_ref[0, h]
        vh = v_ref[0, h]
        sn_ref[0, h] = a * s0 + kh * vh
        o = jnp.sum((qh * a) * s0, axis=0, keepdims=True) + jnp.sum(qh * kh, axis=0, keepdims=True) * vh
        on = _rms(o, ggo_ref[...])
        rh = r_ref[0, h]
        o_ref[0, h] = on * (rh * _sigmoid(rh))


def _gla_step(q_col, k_col, lr, v_row, r_row, s0, wgt, bgt, ggo):
    b = q_col.shape[0]
    const = lambda shape: pl.BlockSpec(shape, lambda i: (0,) * len(shape))
    per_b = lambda shape: pl.BlockSpec((1,) + shape, lambda i: (i,) + (0,) * len(shape))
    return pl.pallas_call(
        _gla_step_kernel,
        grid=(b,),
        in_specs=[per_b((GLA_HEADS, GLA_DK, 1)), per_b((GLA_HEADS, GLA_DK, 1)), per_b((1, 128)),
                  per_b((GLA_HEADS, 1, GLA_DV)), per_b((GLA_HEADS, 1, GLA_DV)), per_b((GLA_HEADS, GLA_DK, GLA_DV)),
                  const((GLA_HEADS, GLA_DK, 128)), const((GLA_HEADS, GLA_DK, 1)), const((1, 128))],
        out_specs=[per_b((GLA_HEADS, 1, GLA_DV)), per_b((GLA_HEADS, GLA_DK, GLA_DV))],
        out_shape=[jax.ShapeDtypeStruct((b, GLA_HEADS, 1, GLA_DV), F32),
                   jax.ShapeDtypeStruct((b, GLA_HEADS, GLA_DK, GLA_DV), F32)],
        compiler_params=_cparams(1),
    )(q_col, k_col, lr, v_row, r_row, s0, wgt, bgt, ggo)


def _xatt_kernel(xq_ref, mem_ref, o_ref):
    for bi in range(xq_ref.shape[0]):
        for h in range(X_HEADS):
            ls = slice(X_DIM * h, X_DIM * (h + 1))
            kh = mem_ref[bi, :, ls].astype(BF16)
            vh = mem_ref[bi, :, 512 + X_DIM * h:512 + X_DIM * (h + 1)].astype(BF16)
            s = _dot_nt(xq_ref[bi, :, ls], kh)
            p = jnp.exp(s - jnp.max(s, axis=-1, keepdims=True))
            p = p / jnp.sum(p, axis=-1, keepdims=True)
            o_ref[bi, :, ls] = _dot(p.astype(BF16), vh).astype(BF16)


def _xatt(xq, memkv):
    b, t, _ = xq.shape
    m = memkv.shape[1]
    tq = min(512, t)
    assert t % tq == 0
    bb = math.gcd(b, max(1, 128 // tq))
    return pl.pallas_call(
        _xatt_kernel,
        grid=(b // bb, t // tq),
        in_specs=[pl.BlockSpec((bb, tq, 512), lambda i, j: (i, j, 0)),
                  pl.BlockSpec((bb, m, 1024), lambda i, j: (i, 0, 0))],
        out_specs=pl.BlockSpec((bb, tq, 512), lambda i, j: (i, j, 0)),
        out_shape=jax.ShapeDtypeStruct((b, t, 512), BF16),
        compiler_params=_cparams(2),
    )(xq, memkv)


def _merge_kernel(on_ref, og_ref, ox_ref, mg_ref, x_ref, wn_ref, wg_ref, wx_ref, wo_ref, x1_ref):
    d = x_ref.shape[-1]
    merged = (mg_ref[:, 0:d].astype(F32) * _dot(on_ref[...], wn_ref[...])
              + mg_ref[:, d:2 * d].astype(F32) * _dot(og_ref[...], wg_ref[...])
              + mg_ref[:, 2 * d:3 * d].astype(F32) * _dot(ox_ref[...], wx_ref[...]))
    x1_ref[...] = x_ref[...] + _dot(merged.astype(BF16), wo_ref[...])


def _merge(o_nsa, o_gla, o_x, mg, x, wn, wg, wx, wo):
    m, d = x.shape
    tm = min(512, m)
    assert m % tm == 0
    row = lambda w: pl.BlockSpec((tm, w), lambda i: (i, 0))
    const = lambda shape: pl.BlockSpec(shape, lambda i: (0,) * len(shape))
    return pl.pallas_call(
        _merge_kernel,
        grid=(m // tm,),
        in_specs=[row(512), row(512), row(512), row(3 * d), row(d),
                  const((512, d)), const((512, d)), const((512, d)), const((d, d))],
        out_specs=row(d),
        out_shape=jax.ShapeDtypeStruct((m, d), F32),
        compiler_params=_cparams(1),
    )(o_nsa, o_gla, o_x, mg, x, wn, wg, wx, wo)


def _ffn_seq_kernel(x_ref, past_ref, g_ref, wup_ref, cw_ref, cb_ref, wdn_ref, y_ref, tail_ref, carry_ref):
    f = cw_ref.shape[-1]
    tm = x_ref.shape[1]

    @pl.when(pl.program_id(1) == 0)
    def _():
        carry_ref[...] = jnp.zeros(carry_ref.shape, F32)
        carry_ref[6:8, :] = past_ref[0]

    x1 = x_ref[0]
    ug = _dot(_rms(x1, g_ref[...]).astype(BF16), wup_ref[...])
    u = ug[:, 0:f]
    g = ug[:, f:2 * f]
    row = lax.broadcasted_iota(jnp.int32, (tm, f), 0)
    p1 = carry_ref[7:8, :]
    p2 = carry_ref[6:7, :]
    gm1 = jnp.where(row == 0, p1, pltpu.roll(g, 1, 0))
    gm2 = jnp.where(row == 0, p2, jnp.where(row == 1, p1, pltpu.roll(g, 2, 0)))
    gc = cb_ref[...] + cw_ref[0:1, :] * gm2 + cw_ref[1:2, :] * gm1 + cw_ref[2:3, :] * g
    y_ref[0] = x1 + _dot((_gelu(gc) * u).astype(BF16), wdn_ref[...])
    carry_ref[...] = g[tm - 8:tm, :]
    tail_ref[0] = g[tm - 8:tm, :]


def _ffn_seq(x1, conv_past, g_ffn, w_up, conv_w, conv_b, w_down):
    b, t, d = x1.shape
    f = conv_w.shape[-1]
    tm = min(512, t)
    assert t % tm == 0 and tm >= 8
    const = lambda shape: pl.BlockSpec(shape, lambda i, j: (0,) * len(shape), pipeline_mode=pl.Buffered(1))
    return pl.pallas_call(
        _ffn_seq_kernel,
        grid=(b, t // tm),
        in_specs=[pl.BlockSpec((1, tm, d), lambda i, j: (i, j, 0)),
                  pl.BlockSpec((1, 2, f), lambda i, j: (i, 0, 0)),
                  const((1, d)), const((d, 2 * f)), const((3, f)), const((1, f)), const((f, d))],
        out_specs=[pl.BlockSpec((1, tm, d), lambda i, j: (i, j, 0)),
                   pl.BlockSpec((1, 8, f), lambda i, j: (i, 0, 0))],
        out_shape=[jax.ShapeDtypeStruct((b, t, d), F32), jax.ShapeDtypeStruct((b, 8, f), F32)],
        scratch_shapes=[pltpu.VMEM((8, f), F32)],
        compiler_params=_cparams(2),
    )(x1, conv_past, g_ffn, w_up, conv_w, conv_b, w_down)


def _ffn_step_kernel(x_ref, p0_ref, p1_ref, g_ref, wup_ref, cw_ref, cb_ref, wdn_ref, y_ref, gnew_ref):
    f = cw_ref.shape[-1]
    x1 = x_ref[...]
    ug = _dot(_rms(x1, g_ref[...]).astype(BF16), wup_ref[...])
    u = ug[:, 0:f]
    g = ug[:, f:2 * f]
    gc = cb_ref[...] + cw_ref[0:1, :] * p0_ref[...] + cw_ref[1:2, :] * p1_ref[...] + cw_ref[2:3, :] * g
    y_ref[...] = x1 + _dot((_gelu(gc) * u).astype(BF16), wdn_ref[...])
    gnew_ref[...] = g


def _ffn_step(x1, p0, p1, g_ffn, w_up, conv_w, conv_b, w_down):
    m, d = x1.shape
    f = conv_w.shape[-1]
    full = lambda shape: pl.BlockSpec(shape, lambda i: (0,) * len(shape))
    return pl.pallas_call(
        _ffn_step_kernel,
        grid=(1,),
        in_specs=[full((m, d)), full((m, f)), full((m, f)), full((1, d)), full((d, 2 * f)), full((3, f)),
                  full((1, f)), full((f, d))],
        out_specs=[full((m, d)), full((m, f))],
        out_shape=[jax.ShapeDtypeStruct((m, d), F32), jax.ShapeDtypeStruct((m, f), F32)],
        compiler_params=_cparams(1),
    )(x1, p0, p1, g_ffn, w_up, conv_w, conv_b, w_down)


def _decode_kernel(pt_ref, *refs, top, n_sel, n_pages):
    del pt_ref
    page_refs = refs[:n_pages]
    (q8_ref, new_ref, neww_ref, cwin_ref, gt_ref, bc_ref, bs_ref, bw_ref, b0_ref, ovt_ref, et_ref, pe_ref, w1_ref,
     w2_ref, gk0_ref, p64_ref, o_ref, wout_ref, xs_ref) = refs[n_pages:]
    page = page_refs[0].shape[1]
    length = n_pages * page
    n_chunks = length // CMP_STRIDE
    for u, pg in enumerate(page_refs):
        for c in range(2):
            xs_ref[c, page * u:page * (u + 1), :] = pg[0, :, 128 * c:128 * (c + 1)]

    def load_j(j):
        return jnp.concatenate([xs_ref[c, pl.ds(j, n_chunks, stride=CMP_STRIDE), :] for c in range(2)], axis=1)

    def token_rows(lo):
        return jnp.concatenate([pg[0, :, lo:lo + 128] for pg in page_refs], axis=0)

    q8 = q8_ref[0]
    q8f = q8.astype(F32)
    rowk = lax.broadcasted_iota(jnp.int32, (8, 128), 0) // NSA_GROUP
    lane_half = lax.broadcasted_iota(jnp.int32, (8, 128), 1) // HEAD_DIM

    def half_mask(x):
        return jnp.where(rowk == lane_half, x, 0.0)

    kc, vc = _compress_core(load_j, n_chunks, pe_ref, w1_ref, w2_ref, gk0_ref, p64_ref)
    kc = kc.astype(BF16)
    vc = vc.astype(BF16)

    p_c = _masked_softmax(_dot_nt(q8, kc) + bc_ref[...], 1).astype(BF16)
    o_c = half_mask(_dot(p_c, vc))
    pf = p_c.astype(F32)
    p2 = jnp.concatenate([jnp.sum(pf[0:4], axis=0, keepdims=True), jnp.sum(pf[4:8], axis=0, keepdims=True),
                          jnp.zeros((126, n_chunks), F32)], axis=0)
    p2_hi = p2.astype(BF16)
    p2_lo = (p2 - p2_hi.astype(F32)).astype(BF16)
    imp_t = _dot_nt(ovt_ref[...], p2_hi) + _dot_nt(ovt_ref[...], p2_lo)
    nsp = imp_t.shape[0]
    jj = lax.broadcasted_iota(jnp.int32, (nsp, 128), 0)
    tblk = length // SEL_BLOCK
    forced = (jj == 0) | (jj == tblk) | (jj == tblk - 1)
    sel_t = _select_with_forced(jnp.where(jj < n_sel, imp_t, -2e38), forced, top, 0)
    pen = jnp.where(sel_t, 0.0, SEL_PENALTY).T
    pen8 = jnp.concatenate([jnp.broadcast_to(pen[0:1], (4, nsp)), jnp.broadcast_to(pen[1:2], (4, nsp))], axis=0)

    new = new_ref[0]
    neww = neww_ref[0]
    b0 = b0_ref[...]

    def attend(s_parts, s_new, v_parts, v_new):
        m = s_new
        for s in s_parts:
            m = jnp.maximum(m, jnp.max(s, axis=-1, keepdims=True))
        pn = jnp.exp(s_new - m)
        l = pn
        o = pn.astype(BF16).astype(F32) * v_new.astype(BF16).astype(F32)
        for s, v in zip(s_parts, v_parts):
            pp = jnp.exp(s - m)
            l = l + jnp.sum(pp, axis=-1, keepdims=True)
            o = o + _dot(pp.astype(BF16), v)
        return half_mask(o / l)

    def new_score(k_new):
        return jnp.sum(q8f * k_new.astype(BF16).astype(F32), axis=-1, keepdims=True) + b0

    k_past = token_rows(256).astype(BF16)
    v_past = token_rows(384).astype(BF16)
    s_past = _dot_nt(q8, k_past) + _dot_nt(pen8[:, 0:128].astype(BF16), et_ref[...]) + bs_ref[...]
    lane = lax.broadcasted_iota(jnp.int32, (8, nsp), 1)
    pen_new = jnp.sum(jnp.where(lane == tblk, pen8, 0.0), axis=-1, keepdims=True)
    o_s = attend([s_past], new_score(new[:, 256:384]) + pen_new, [v_past], new[:, 384:512])

    cw = cwin_ref[0]
    s_w = _dot_nt(q8, cw[:, 0:128].astype(BF16)) + bw_ref[...]
    o_w = attend([s_w], new_score(neww[:, 0:128]), [cw[:, 128:256].astype(BF16)], neww[:, 128:256])

    gt = gt_ref[0]
    o_ref[0] = gt[:, 0:1] * o_c + gt[:, 1:2] * o_s + gt[:, 2:3] * o_w

    wl = cw.shape[0]
    wrow = lax.broadcasted_iota(jnp.int32, cw.shape, 0)
    wout_ref[0] = jnp.where(wrow == wl - 1, neww, pltpu.roll(cw, wl - 1, 0))


def _nsa_decode(page_table, cache2d, q8, new_rows, new_win, cache_win, gates8, bc, bs, bw, b0, ov, et,
                pe, w1, w2, gk0, p64):
    db, n_pages = page_table.shape
    page = cache2d.shape[1]
    length = n_pages * page
    n_chunks = length // CMP_STRIDE
    n_sel = -(-(length + 1) // SEL_BLOCK)
    top = min(SEL_TOP, n_sel)
    wl = cache_win.shape[1]
    nsp = ov.shape[0]
    const = lambda shape: pl.BlockSpec(shape, lambda i, pt: (0,) * len(shape))
    per_b = lambda shape: pl.BlockSpec((1,) + shape, lambda i, pt: (i,) + (0,) * len(shape))
    page_spec = lambda u: pl.BlockSpec((1, page, 512), lambda i, pt: (pt[i, u], 0, 0))
    grid_spec = pltpu.PrefetchScalarGridSpec(
        num_scalar_prefetch=1,
        grid=(db,),
        in_specs=[page_spec(u) for u in range(n_pages)] + [
                  per_b((8, 128)), per_b((1, 512)), per_b((1, 256)), per_b((wl, 256)), per_b((8, 128)),
                  const((8, n_chunks)), const((8, length)), const((8, wl)), const((8, 1)),
                  const((nsp, n_chunks)), const((length, 128)),
                  const((2, CMP_STRIDE, 1, 256)), const((2, CMP_STRIDE * 256, 256)), const((256, 256)),
                  const((1, 128)), const((512, 512))],
        out_specs=[per_b((8, 128)), per_b((wl, 256))],
        scratch_shapes=[pltpu.VMEM((2, length, 128), F32)],
    )
    return pl.pallas_call(
        functools.partial(_decode_kernel, top=top, n_sel=n_sel, n_pages=n_pages),
        grid_spec=grid_spec,
        out_shape=[jax.ShapeDtypeStruct((db, 8, 128), F32), jax.ShapeDtypeStruct((db, wl, 256), F32)],
        compiler_params=_cparams(1),
    )(page_table, *([cache2d] * n_pages), q8, new_rows, new_win, cache_win, gates8, bc, bs, bw, b0, ov, et,
      pe, w1, w2, gk0, p64)


def _bucket_table():
    n = np.arange(MAX_DISTANCE + 1)
    max_exact = N_BUCKETS // 2
    nf = np.maximum(n, 1).astype(np.float32)
    large = max_exact + (np.log(nf / np.float32(max_exact)) / np.float32(math.log(MAX_DISTANCE / max_exact))
                         * np.float32(N_BUCKETS - max_exact)).astype(np.int32)
    return np.where(n < max_exact, n, np.minimum(large, N_BUCKETS - 1)).astype(np.int32)


def _bias_lookup(tb, rel, valid):
    idx = np.clip(rel, 0, MAX_DISTANCE)
    vals = jnp.moveaxis(tb[idx], -1, 0)
    return jnp.where(jnp.asarray(valid)[None], vals, NEG)


def _overlap(n_cmp_pad, n_cmp, n_sel_pad, n_sel):
    cs = (np.arange(n_cmp_pad) * CMP_STRIDE)[:, None]
    ss = (np.arange(n_sel_pad) * SEL_BLOCK)[None, :]
    ov = (cs < ss + SEL_BLOCK) & (cs + CMP_BLOCK > ss)
    ov &= (np.arange(n_cmp_pad) < n_cmp)[:, None] & (np.arange(n_sel_pad) < n_sel)[None, :]
    return jnp.asarray(ov.astype(np.float32), dtype=BF16)


def _stack_rows(x):
    return x.reshape(NSA_KV, NSA_GROUP * Q_BLOCK, x.shape[-1])


def _toeplitz(tbr, shift, width, max_valid):
    n = width + Q_BLOCK - 1
    u = np.arange(n)
    xs = shift - np.where(u < width, u, u - n)
    fvec = _bias_lookup(tbr, xs, (xs >= 0) & (xs <= max_valid))
    h = fvec.shape[0]
    return jnp.tile(fvec, (1, Q_BLOCK))[:, :Q_BLOCK * (n - 1)].reshape(h, Q_BLOCK, n - 1)[:, :, :width]


def _bias_descending(tb, top):
    far = jnp.broadcast_to(tb[MAX_DISTANCE][:, None], (tb.shape[1], top - MAX_DISTANCE + 1))
    return jnp.concatenate([far, tb[MAX_DISTANCE - 1:0:-1].T], axis=1)


def _prompt_tables(tb):
    tbr = (tb - tb[MAX_DISTANCE][None, :]) * LOG2E
    i = np.arange(Q_BLOCK)[:, None]

    def table(rel, valid):
        return _stack_rows(_bias_lookup(tbr, rel, valid))

    n_r = 4
    m_sel = _stack_rows(_toeplitz(tbr, Q_BLOCK * (n_r - 1), NEAR_TILE + Q_BLOCK * (n_r - 1), 1 << 30))
    tsel = jnp.stack([m_sel[:, :, Q_BLOCK * (n_r - 1 - r):Q_BLOCK * (n_r - 1 - r) + NEAR_TILE] for r in range(n_r)],
                     axis=1)
    n_v = WINDOW // Q_BLOCK + 1
    m_win = _stack_rows(_toeplitz(tbr, Q_BLOCK * (n_v - 1), WIN_TILE + Q_BLOCK * (n_v - 1), WINDOW - 1))
    twin = jnp.stack([m_win[:, :, Q_BLOCK * (n_v - 1 - v):Q_BLOCK * (n_v - 1 - v) + WIN_TILE] for v in range(n_v)],
                     axis=1)
    w = np.arange(16)[None, :] - 9
    rel = i - CMP_STRIDE * w - (CMP_BLOCK - 1)
    a = table(rel, rel >= 0)
    hi = a.astype(BF16)
    lo = (a - hi.astype(F32)).astype(BF16)
    acmp = jnp.concatenate([hi, lo, jnp.full(hi.shape[:2] + (1,), NEG, BF16),
                            jnp.zeros(hi.shape[:2] + (128 - 33,), BF16)], axis=-1)
    return jnp.swapaxes(tsel, 2, 3), jnp.swapaxes(twin, 2, 3), jnp.swapaxes(acmp, 1, 2)


def kernel(x_prompt, x_sample, cache_kv, cache_win, state_gla, state_conv, cache_mem, page_table, mem_prompt,
           g_mix, w_in, g_nsa_q, g_nsa_k, cmp_k_pe, cmp_k_w1, cmp_k_w2, cmp_v_pe, cmp_v_w1, cmp_v_w2,
           rel_bias, w_gla_gate, b_gla_gate, g_gla_o, g_mem, w_mem_kv, g_x_q, g_x_k,
           w_nsa_out, w_gla_out, w_x_out, w_o, g_ffn, w_up, conv_w, conv_b, w_down):
    bp, t, d = x_prompt.shape
    db = x_sample.shape[0]
    f = conv_w.shape[-1]

    offs = np.cumsum((0,) + IN_SIZES)
    segs = [w_in[:, offs[i]:offs[i + 1]] for i in range(len(IN_SIZES))]
    w_pad = jnp.concatenate([jnp.pad(s, ((0, 0), (0, pw - s.shape[1]))) for s, pw in zip(segs, PAD_SIZES)],
                            axis=1).astype(BF16)
    row = lambda v: v.reshape(1, -1).astype(F32)
    gq = row(jnp.tile(g_nsa_q, NSA_HEADS))
    gk0 = row(jnp.tile(g_nsa_k[0], NSA_KV))
    gk1 = row(jnp.tile(g_nsa_k[1], NSA_KV))
    gk2 = row(jnp.tile(g_nsa_k[2], NSA_KV))
    gxq = row(jnp.tile(g_x_q, X_HEADS))
    gxk = row(jnp.tile(g_x_k, X_HEADS))
    p64 = _block_ones(512, HEAD_DIM)
    p128 = _block_ones(512, X_DIM)
    bd2 = lambda a: jnp.concatenate([jnp.concatenate([a, jnp.zeros_like(a)], -1),
                                     jnp.concatenate([jnp.zeros_like(a), a], -1)], -2)
    bd4 = lambda a, c: jnp.concatenate([jnp.concatenate([bd2(a), jnp.zeros_like(bd2(a))], -1),
                                        jnp.concatenate([jnp.zeros_like(bd2(c)), bd2(c)], -1)], -2)
    pe = jnp.concatenate([jnp.tile(cmp_k_pe, (1, NSA_KV)), jnp.tile(cmp_v_pe, (1, NSA_KV))],
                         axis=-1).reshape(2, CMP_STRIDE, 1, 256)
    w1 = bd4(cmp_k_w1, cmp_v_w1).reshape(2, CMP_STRIDE * 256, 256).astype(BF16)
    w2 = bd4(cmp_k_w2, cmp_v_w2).astype(BF16)
    wg_pad = jnp.pad(w_gla_gate, ((0, 128 - GLA_RANK), (0, 0))).astype(BF16)
    tb = rel_bias.astype(F32)[_bucket_table()]
    wn, wgo, wx, wo = (w.astype(BF16) for w in (w_nsa_out, w_gla_out, w_x_out, w_o))
    wup = w_up.astype(BF16)
    wdn = w_down.astype(BF16)
    ggo = row(g_gla_o)

    rows_p, win_p, _, gates, gla_in, xq, mg, ka, kw, vst, vwt, qt = _proj_in(
        x_prompt, row(g_mix), w_pad, gq, gk1, gk2, gxq, p64, p128, True)
    memkv_p = _memory_kv(mem_prompt, row(g_mem), w_mem_kv.astype(BF16), gxk, p128)
    kc, vct = _compress(rows_p, pe, w1, w2, gk0, p64)
    n_chunks = t // CMP_STRIDE
    n_sel = -(-t // SEL_BLOCK)
    tsel, twin, acmp = _prompt_tables(tb)
    ovt = _overlap(n_chunks, n_chunks - 1, 128, n_sel).T
    o_nsa = _nsa_prompt(qt, kc, vct, ka, kw, vst, vwt, gates, tsel, twin, acmp, ovt)
    s0t = jnp.zeros((bp, GLA_HEADS, GLA_DV, GLA_DK), F32)
    o_gla, st = _gla_prompt(gla_in, s0t, wg_pad, row(b_gla_gate), ggo)
    o_x = _xatt(xq, memkv_p)
    m = bp * t
    x1 = _merge(o_nsa.reshape(m, 512), o_gla.reshape(m, 512), o_x.reshape(m, 512), mg.reshape(m, 3 * d),
                x_prompt.reshape(m, d), wn, wgo, wx, wo)
    y_p, tail = _ffn_seq(x1.reshape(bp, t, d), jnp.zeros((bp, 2, f), F32), row(g_ffn), wup, conv_w, row(conv_b), wdn)
    wl_p = min(WINDOW, t)
    out_rows_p = rows_p.reshape(bp, t, 4, NSA_KV, HEAD_DIM)
    out_win_p = win_p[:, t - wl_p:].reshape(bp, wl_p, 2, NSA_KV, HEAD_DIM)
    out_gla_p = jnp.swapaxes(st, 2, 3)
    out_conv_p = tail[:, 6:8]
    out_mem_p = memkv_p.reshape(bp, -1, 2, X_HEADS, X_DIM)

    n_pages = page_table.shape[1]
    page = cache_kv.shape[1]
    length = n_pages * page
    wl = cache_win.shape[1]
    rows_s, win_s, qn_s, gates_s, gla_s, xq_s, mg_s = (a[0] for a in _proj_in(
        x_sample.reshape(1, db, d), row(g_mix), w_pad, gq, gk1, gk2, gxq, p64, p128, False))

    eye = jnp.eye(NSA_KV, dtype=BF16)
    q8 = (qn_s.reshape(db, NSA_KV, NSA_GROUP, 1, HEAD_DIM) * eye[None, :, None, :, None]).reshape(db, 8, 128)
    gates8 = jnp.pad(gates_s[:, 0:24].reshape(db, 8, 3), ((0, 0), (0, 0), (0, 125)))
    n_chunks_s = length // CMP_STRIDE
    n_sel_s = -(-(length + 1) // SEL_BLOCK)
    nsp = -(-n_sel_s // 128) * 128
    cidx = np.arange(n_chunks_s)
    rel_c = length - (cidx * CMP_STRIDE + CMP_BLOCK - 1)
    bc = _bias_lookup(tb, rel_c, (rel_c >= 0) & (cidx < n_chunks_s - 1))
    bs = _bias_descending(tb, length)
    bw = jnp.where(jnp.asarray(np.arange(wl, 0, -1) < WINDOW)[None], _bias_descending(tb, wl), NEG)
    b0 = tb[0].reshape(8, 1)
    ov_s = _overlap(n_chunks_s, n_chunks_s - 1, nsp, n_sel_s).T
    assert length // SEL_BLOCK <= 128
    et = jnp.asarray((np.arange(length)[:, None] // SEL_BLOCK == np.arange(128)[None, :]).astype(np.float32),
                     dtype=BF16)
    o8, win_new = _nsa_decode(page_table, cache_kv.reshape(cache_kv.shape[0], page, 512), q8,
                              rows_s.reshape(db, 1, 512), win_s.reshape(db, 1, 256), cache_win.reshape(db, wl, 256),
                              gates8, bc, bs, bw, b0, ov_s, et, pe, w1, w2, gk0, p64)
    o8 = o8.reshape(db, NSA_KV, NSA_GROUP, NSA_KV, HEAD_DIM)
    o_nsa_s = jnp.stack([o8[:, 0, :, 0], o8[:, 1, :, 1]], axis=1).reshape(db, 512).astype(BF16)

    wgt = jnp.pad(w_gla_gate.T, ((0, 0), (0, 128 - GLA_RANK))).reshape(GLA_HEADS, GLA_DK, 128)
    o_gla_s, gla_state_s = _gla_step(
        gla_s[:, 0:256].reshape(db, GLA_HEADS, GLA_DK, 1), gla_s[:, 256:512].reshape(db, GLA_HEADS, GLA_DK, 1),
        gla_s[:, 1024:1152].reshape(db, 1, 128), gla_s[:, 512:1024].reshape(db, GLA_HEADS, 1, GLA_DV),
        gla_s[:, 1152:1664].reshape(db, GLA_HEADS, 1, GLA_DV), state_gla.astype(F32), wgt,
        b_gla_gate.reshape(GLA_HEADS, GLA_DK, 1), ggo)
    o_gla_s = o_gla_s.reshape(db, 512).astype(BF16)

    xq_pad = jnp.pad(xq_s.reshape(db, 1, 512), ((0, 0), (0, 15), (0, 0)))
    o_x_s = _xatt(xq_pad, cache_mem.reshape(db, -1, 1024))[:, 0]
    x1_s = _merge(o_nsa_s, o_gla_s, o_x_s, mg_s, x_sample.reshape(db, d), wn, wgo, wx, wo)
    y_s, g_new = _ffn_step(x1_s, state_conv[:, 0], state_conv[:, 1], row(g_ffn), wup, conv_w, row(conv_b), wdn)

    out_rows_s = rows_s.reshape(db, 1, 4, NSA_KV, HEAD_DIM)
    out_win_s = win_new.reshape(db, wl, 2, NSA_KV, HEAD_DIM)
    out_conv_s = jnp.stack([state_conv[:, 1], g_new], axis=1)
    return (y_p, y_s.reshape(db, 1, d), out_rows_p, out_win_p, out_gla_p, out_conv_p, out_mem_p,
            out_rows_s, out_win_s, gla_state_s, out_conv_s)
```

```python
import functools
import math

import numpy as np
import jax
import jax.numpy as jnp
from jax import lax
from jax.experimental import pallas as pl
from jax.experimental.pallas import tpu as pltpu

F32 = jnp.float32
BF16 = jnp.bfloat16

NSA_HEADS = 8
NSA_KV = 2
NSA_GROUP = 4
HEAD_DIM = 64
CMP_BLOCK = 32
CMP_STRIDE = 16
SEL_BLOCK = 64
SEL_TOP = 16
WINDOW = 512
Q_BLOCK = 128
GLA_HEADS = 4
GLA_DK = 64
GLA_DV = 128
GLA_RANK = 16
GLA_TAU = 16.0
GLA_CHUNK = 64
X_HEADS = 4
X_DIM = 128
N_BUCKETS = 32
MAX_DISTANCE = 128
EPS = 1e-6
LOG2E = math.log2(math.e)
NEG = -1e30
TINY = 1e-30
SEL_PENALTY = -1e9
MASKED_BELOW = -5e29
EXP_CLAMP = 80.0

IN_SIZES = (512, 768, 24, 256, 256, 512, 16, 512, 512, 3072)
PAD_SIZES = (512, 768, 128, 256, 256, 512, 128, 512, 512, 3072)
PAD_OFFS = tuple(int(v) for v in np.cumsum((0,) + PAD_SIZES))
D_IN_PAD = PAD_OFFS[-1]
GLA_IN_W = 256 + 256 + 512 + 128 + 512

VMEM_LIMIT = 56 * 1024 * 1024
FAR_TILE = 256
NEAR_TILE = 512
WIN_TILE = WINDOW + Q_BLOCK


def _cparams(n_axes):
    return pltpu.CompilerParams(dimension_semantics=("arbitrary",) * n_axes, vmem_limit_bytes=VMEM_LIMIT)


def _dot(a, b):
    return jnp.dot(a, b, preferred_element_type=F32)


def _dot_nt(a, b):
    return lax.dot_general(a, b, (((1,), (1,)), ((), ())), preferred_element_type=F32)


def _dot_tn(a, b):
    return lax.dot_general(a, b, (((0,), (0,)), ((), ())), preferred_element_type=F32)


def _split_dot(x, m):
    hi = x.astype(BF16)
    lo = (x - hi.astype(F32)).astype(BF16)
    return _dot(hi, m) + _dot(lo, m)


def _rms(x, g):
    return x * lax.rsqrt(jnp.mean(x * x, axis=-1, keepdims=True) + EPS) * g


def _group_rms(x, pmat, gsize, g):
    ss = _split_dot(x * x, pmat)
    return x * lax.rsqrt(ss * (1.0 / gsize) + EPS) * g


def _gelu(x):
    return 0.5 * x * (1.0 + jnp.tanh(math.sqrt(2.0 / math.pi) * (x + 0.044715 * (x * x * x))))


def _sigmoid(x):
    return 1.0 / (1.0 + jnp.exp(-x))


def _block_ones(n, gsize):
    i = np.arange(n) // gsize
    return jnp.asarray((i[:, None] == i[None, :]).astype(np.float32), dtype=BF16)


def _proj_kernel(x_ref, gmix_ref, w_ref, gq_ref, gk1_ref, gk2_ref, gxq_ref, p64_ref, p128_ref,
                 rows_ref, win_ref, qn_ref, gates_ref, gla_ref, xq_ref, mg_ref, *attn_refs):
    x = x_ref[0]
    h = _rms(x, gmix_ref[...]).astype(BF16)
    o = PAD_OFFS

    def seg(i):
        return _dot(h, w_ref[:, o[i]:o[i + 1]])

    p64 = p64_ref[...]
    p64s = p64_ref[0:128, 0:128]
    qn = _group_rms(seg(0), p64, HEAD_DIM, gq_ref[...]) * (HEAD_DIM ** -0.5)
    qn_ref[0] = qn.astype(BF16)
    if attn_refs:
        qt_ref = attn_refs[4]
        for u in range(x.shape[0] // Q_BLOCK):
            qt_ref[0, u] = (qn[Q_BLOCK * u:Q_BLOCK * (u + 1)] * LOG2E).T.astype(BF16)

    kv = seg(1)
    k_sel = _group_rms(kv[:, 256:384], p64s, HEAD_DIM, gk1_ref[...])
    k_win = _group_rms(kv[:, 512:640], p64s, HEAD_DIM, gk2_ref[...])
    rows_ref[0, :, 0:256] = kv[:, 0:256]
    rows_ref[0, :, 256:384] = k_sel
    rows_ref[0, :, 384:512] = kv[:, 384:512]
    win_ref[0, :, 0:128] = k_win
    win_ref[0, :, 128:256] = kv[:, 640:768]
    if attn_refs:
        ka_ref, kw_ref, vst_ref, vwt_ref, _ = attn_refs
        tm = x.shape[0]
        tpos = pl.program_id(1) * tm + lax.broadcasted_iota(jnp.int32, (tm, 128), 0)
        blk = lax.broadcasted_iota(jnp.int32, (tm, 128), 1)
        ka_ref[0, :, 0:128] = k_sel.astype(BF16)
        ka_ref[0, :, 128:256] = jnp.where(tpos // SEL_BLOCK == blk, 1.0, 0.0).astype(BF16)
        kw_ref[0] = k_win.astype(BF16)
        for u in range(tm // FAR_TILE):
            vst_ref[0, u] = kv[FAR_TILE * u:FAR_TILE * (u + 1), 384:512].T.astype(BF16)
        for u in range(tm // Q_BLOCK):
            vwt_ref[0, u] = kv[Q_BLOCK * u:Q_BLOCK * (u + 1), 640:768].T.astype(BF16)

    gates_ref[0] = _sigmoid(seg(2))
    gla_ref[0, :, 0:256] = seg(3) * (GLA_DK ** -0.5)
    gla_ref[0, :, 256:512] = seg(4)
    gla_ref[0, :, 512:1024] = seg(5)
    gla_ref[0, :, 1024:1152] = seg(6)
    gla_ref[0, :, 1152:1664] = seg(7)
    xq = _group_rms(seg(8), p128_ref[...], X_DIM, gxq_ref[...]) * (X_DIM ** -0.5)
    xq_ref[0] = xq.astype(BF16)
    mg_ref[0] = _sigmoid(seg(9)).astype(BF16)


def _proj_in(x, g_mix, w_pad, gq, gk1, gk2, gxq, p64, p128, attn_layout):
    b, t, d = x.shape
    tm = min(512, t)
    assert t % tm == 0
    widths = [512, 256, 512, 128, GLA_IN_W, 512, 3072]
    dtypes = [F32, F32, BF16, F32, F32, BF16, BF16]
    out_specs = [pl.BlockSpec((1, tm, w), lambda i, j: (i, j, 0)) for w in widths]
    out_shape = [jax.ShapeDtypeStruct((b, t, w), dt) for w, dt in zip(widths, dtypes)]
    if attn_layout:
        assert tm % Q_BLOCK == 0
        for w in (256, 128):
            out_specs.append(pl.BlockSpec((1, tm, w), lambda i, j: (i, j, 0)))
            out_shape.append(jax.ShapeDtypeStruct((b, t, w), BF16))
        assert tm % FAR_TILE == 0
        for rows, width in ((128, FAR_TILE), (128, Q_BLOCK), (512, Q_BLOCK)):
            out_specs.append(pl.BlockSpec((1, tm // width, rows, width), lambda i, j: (i, j, 0, 0)))
            out_shape.append(jax.ShapeDtypeStruct((b, t // width, rows, width), BF16))
    const = lambda shape: pl.BlockSpec(shape, lambda i, j: (0,) * len(shape), pipeline_mode=pl.Buffered(1))
    return pl.pallas_call(
        _proj_kernel,
        grid=(b, t // tm),
        in_specs=[pl.BlockSpec((1, tm, d), lambda i, j: (i, j, 0)),
                  const((1, d)), const((d, D_IN_PAD)), const((1, 512)), const((1, 128)), const((1, 128)),
                  const((1, 512)), const((512, 512)), const((512, 512))],
        out_specs=out_specs,
        out_shape=out_shape,
        compiler_params=_cparams(2),
    )(x, g_mix, w_pad, gq, gk1, gk2, gxq, p64, p128)


def _memkv_kernel(m_ref, g_ref, w_ref, gk_ref, p128_ref, o_ref):
    h = _rms(m_ref[0], g_ref[...]).astype(BF16)
    kv = _dot(h, w_ref[...])
    o_ref[0, :, 0:512] = _group_rms(kv[:, 0:512], p128_ref[...], X_DIM, gk_ref[...])
    o_ref[0, :, 512:1024] = kv[:, 512:1024]


def _memory_kv(mem, g_mem, w_mem, gxk, p128):
    b, m, d = mem.shape
    const = lambda shape: pl.BlockSpec(shape, lambda i: (0,) * len(shape))
    return pl.pallas_call(
        _memkv_kernel,
        grid=(b,),
        in_specs=[pl.BlockSpec((1, m, d), lambda i: (i, 0, 0)), const((1, d)), const((d, 1024)),
                  const((1, 512)), const((512, 512))],
        out_specs=pl.BlockSpec((1, m, 1024), lambda i: (i, 0, 0)),
        out_shape=jax.ShapeDtypeStruct((b, m, 1024), F32),
        compiler_params=_cparams(1),
    )(mem, g_mem, w_mem, gxk, p128)


def _compress_core(load_j, n_chunks, pe_ref, w1_ref, w2_ref, gk0_ref, p64_ref):
    xs = [load_j(j) for j in range(CMP_STRIDE)]
    halves = [_dot(jnp.concatenate([(x + pe_ref[r, j]).astype(BF16) for j, x in enumerate(xs)], axis=1), w1_ref[r])
              for r in range(2)]
    hid = halves[0] + pltpu.roll(halves[1], n_chunks - 1, 0)
    out = _dot(_gelu(hid).astype(BF16), w2_ref[...])
    row = lax.broadcasted_iota(jnp.int32, (n_chunks, 128), 0)
    live = row < n_chunks - 1
    kc = _group_rms(out[:, 0:128], p64_ref[0:128, 0:128], HEAD_DIM, gk0_ref[...])
    return jnp.where(live, kc, 0.0), jnp.where(live, out[:, 128:256], 0.0)


def _compress_kernel(rk_ref, rv_ref, pe_ref, w1_ref, w2_ref, gk0_ref, p64_ref, kc_ref, vc_ref, *, n_chunks):
    load_j = lambda j: jnp.concatenate([rk_ref[0, pl.ds(j, n_chunks, stride=CMP_STRIDE), :],
                                        rv_ref[0, pl.ds(j, n_chunks, stride=CMP_STRIDE), :]], axis=1)
    kc, vc = _compress_core(load_j, n_chunks, pe_ref, w1_ref, w2_ref, gk0_ref, p64_ref)
    kc_ref[0] = kc.astype(BF16)
    vc_ref[0] = vc.T.astype(BF16)


def _compress(rows, pe, w1, w2, gk0, p64):
    b, t, _ = rows.shape
    n_chunks = t // CMP_STRIDE
    const = lambda shape: pl.BlockSpec(shape, lambda i: (0,) * len(shape))
    return pl.pallas_call(
        functools.partial(_compress_kernel, n_chunks=n_chunks),
        grid=(b,),
        in_specs=[pl.BlockSpec((1, t, 128), lambda i: (i, 0, 0)), pl.BlockSpec((1, t, 128), lambda i: (i, 0, 1)),
                  const((2, CMP_STRIDE, 1, 256)), const((2, CMP_STRIDE * 256, 256)), const((256, 256)),
                  const((1, 128)), const((512, 512))],
        out_specs=[pl.BlockSpec((1, n_chunks, 128), lambda i: (i, 0, 0)),
                   pl.BlockSpec((1, 128, n_chunks), lambda i: (i, 0, 0))],
        out_shape=[jax.ShapeDtypeStruct((b, n_chunks, 128), BF16), jax.ShapeDtypeStruct((b, 128, n_chunks), BF16)],
        compiler_params=_cparams(1),
    )(rows, rows, pe, w1, w2, gk0, p64)


def _masked_softmax(s, axis, exp_fn=jnp.exp):
    m = jnp.maximum(jnp.max(s, axis=axis, keepdims=True), MASKED_BELOW)
    p = exp_fn(s - m)
    return p / jnp.maximum(jnp.sum(p, axis=axis, keepdims=True), TINY)


def _select_blocks(score, top, axis):
    pos = lax.broadcasted_iota(jnp.int32, score.shape, axis).astype(F32)
    sel = jnp.zeros(score.shape, jnp.bool_)
    for _ in range(top):
        mx = jnp.max(score, axis=axis, keepdims=True)
        idx = jnp.min(jnp.where(score == mx, pos, 1e9), axis=axis, keepdims=True)
        hit = pos == idx
        sel = jnp.logical_or(sel, hit)
        score = jnp.where(hit, -3e38, score)
    return sel


def _select_with_forced(imp, forced, top, axis):
    n_forced = 3
    assert top > n_forced
    return jnp.logical_or(forced, _select_blocks(jnp.where(forced, -3e38, imp), top - n_forced, axis))


def _values_times_probs(vt_tiles, p):
    out = None
    start = 0
    for vt in vt_tiles:
        part = _dot(vt, p[start:start + vt.shape[1]])
        start += vt.shape[1]
        out = part if out is None else out + part
    return out


def _softmax_stats_t(s_ref, bias, m_ref, l_ref):
    a_parts, p_parts = [], []
    for g in range(s_ref.shape[1] // Q_BLOCK):
        cs = slice(Q_BLOCK * g, Q_BLOCK * (g + 1))
        sg = s_ref[:, cs]
        if bias is not None:
            sg = sg + bias(cs)
        m_old = m_ref[:, cs]
        m_new = jnp.maximum(m_old, jnp.max(sg, axis=0, keepdims=True))
        alpha = jnp.exp2(m_old - m_new)
        p = jnp.exp2(sg - m_new)
        m_ref[:, cs] = m_new
        l_ref[:, cs] = alpha * l_ref[:, cs] + jnp.sum(p, axis=0, keepdims=True)
        a_parts.append(alpha)
        p_parts.append(p.astype(BF16))
    return jnp.concatenate(a_parts, axis=1), jnp.concatenate(p_parts, axis=1)


def _acc_update_t(acc_ref, alpha, vt_tiles, p):
    acc_ref[...] = alpha * acc_ref[...] + _values_times_probs(vt_tiles, p)


def _attn_kernel(qt_ref, kc_ref, vct_ref, ka_ref, vst_ref, kw_ref, vwt_ref, gt_ref, tsel_ref, twin_ref, acmp_ref,
                 ovt_ref, o_ref, sa_scr, sb_scr, m_scr, l_scr, acc_scr, pend_a, pend_p, *, top):
    k = pl.program_id(1)
    n = pl.program_id(2)
    cols = NSA_GROUP * Q_BLOCK
    vrow = pl.multiple_of(k * HEAD_DIM, HEAD_DIM)
    zero = jnp.zeros((HEAD_DIM, Q_BLOCK), BF16)
    parts = []
    for g in range(NSA_GROUP):
        piece = qt_ref[0, 0, HEAD_DIM * g:HEAD_DIM * (g + 1), :]
        parts.append(jnp.where(k == 0, jnp.concatenate([piece, zero], axis=0),
                               jnp.concatenate([zero, piece], axis=0)))
    qt = jnp.concatenate(parts, axis=1)

    nc = kc_ref.shape[1]
    blocks_per_q = Q_BLOCK // CMP_STRIDE
    cp = lax.broadcasted_iota(jnp.int32, (nc, 128), 0) - blocks_per_q * n + 9
    fr = lax.broadcasted_iota(jnp.int32, (nc, 128), 1)
    feat = ((fr < 32) & (cp == (fr & 15))) | ((fr == 32) & (cp > 15))
    kc_aug = jnp.concatenate([kc_ref[0], jnp.where(feat, 1.0, 0.0).astype(BF16)], axis=1)
    s_c = _dot(kc_aug, jnp.concatenate([qt, acmp_ref[0]], axis=0))
    p_c = _masked_softmax(s_c, 0, jnp.exp2).astype(BF16)
    o_c = _dot(vct_ref[0, pl.ds(vrow, HEAD_DIM), :], p_c)
    imp4 = _dot(ovt_ref[...], p_c)
    imp = imp4[:, 0:128] + imp4[:, 128:256] + imp4[:, 256:384] + imp4[:, 384:512]

    def value_tiles(ref, first, count):
        return [ref[0, first + d, pl.ds(vrow, HEAD_DIM), :] for d in range(count)]

    wt = jnp.maximum(n - WINDOW // Q_BLOCK, 0)
    s_w = _dot(kw_ref[0, pl.ds(pl.multiple_of(wt * Q_BLOCK, Q_BLOCK), WIN_TILE), :], qt) + twin_ref[0, 0]
    p_w = jnp.exp2(s_w - jnp.max(s_w, axis=0, keepdims=True))
    o_w = (_values_times_probs(value_tiles(vwt_ref, wt, WIN_TILE // Q_BLOCK), p_w.astype(BF16))
           / jnp.sum(p_w, axis=0, keepdims=True))

    jj = lax.broadcasted_iota(jnp.int32, (128, Q_BLOCK), 0)
    tpos = n * Q_BLOCK + lax.broadcasted_iota(jnp.int32, (128, Q_BLOCK), 1)
    tblk = tpos // SEL_BLOCK
    forced = (jj == 0) | (jj == tblk) | (jj == tblk - 1)
    sel_t = _select_with_forced(jnp.where(jj * SEL_BLOCK <= tpos, imp, NEG), forced, top, 0)
    pen_t = jnp.where(sel_t, 0.0, SEL_PENALTY).astype(BF16)
    rhs = jnp.concatenate([qt, jnp.concatenate([pen_t] * NSA_GROUP, axis=1)], axis=0)

    n_far = jnp.maximum(n - 2, 0) // 2
    r_near = n - 2 * n_far

    def issue_scores(tile, s_ref):
        s_ref[...] = _dot(ka_ref[0, pl.ds(pl.multiple_of(tile * FAR_TILE, FAR_TILE), FAR_TILE), :], rhs)

    def stats(s_ref, half_idx=None):
        bias = None
        if half_idx is not None:
            bias = lambda cs: tsel_ref[0, r_near, FAR_TILE * half_idx:FAR_TILE * (half_idx + 1), cs]
        return _softmax_stats_t(s_ref, bias, m_scr, l_scr)

    def accumulate(tile, alpha, p):
        _acc_update_t(acc_scr, alpha, value_tiles(vst_ref, tile, 1), p)

    m_scr[...] = jnp.full((1, cols), NEG, F32)
    l_scr[...] = jnp.zeros((1, cols), F32)
    acc_scr[...] = jnp.zeros((HEAD_DIM, cols), F32)
    pend_a[...] = jnp.ones((1, cols), F32)
    pend_p[...] = jnp.zeros((FAR_TILE, cols), BF16)
    odd = n_far % 2

    @pl.when(odd == 1)
    def _():
        issue_scores(0, sa_scr)
        accumulate(0, *stats(sa_scr))

    issue_scores(odd, sa_scr)

    @pl.loop(0, n_far // 2)
    def _(j):
        t0 = odd + 2 * j
        issue_scores(t0 + 1, sb_scr)
        alpha_a, p_a = stats(sa_scr)
        accumulate(jnp.maximum(t0 - 1, 0), pend_a[...], pend_p[...])
        issue_scores(t0 + 2, sa_scr)
        alpha_b, p_b = stats(sb_scr)
        accumulate(t0, alpha_a, p_a)
        pend_a[...] = alpha_b
        pend_p[...] = p_b

    accumulate(jnp.maximum(n_far - 1, 0), pend_a[...], pend_p[...])
    issue_scores(n_far + 1, sb_scr)
    accumulate(n_far, *stats(sa_scr, 0))
    accumulate(n_far + 1, *stats(sb_scr, 1))
    o_s = acc_scr[...] / l_scr[...]

    gtt = gt_ref[0].T

    def gate_row(branch):
        rows = [jnp.where(k == 0, gtt[3 * g + branch:3 * g + branch + 1],
                          gtt[12 + 3 * g + branch:12 + 3 * g + branch + 1]) for g in range(NSA_GROUP)]
        return jnp.concatenate(rows, axis=1)

    o_t = gate_row(0) * o_c + gate_row(1) * o_s + gate_row(2) * o_w
    left = jnp.concatenate([o_t[:, 0:128], o_t[:, 128:256]], axis=0).T
    right = jnp.concatenate([o_t[:, 256:384], o_t[:, 384:512]], axis=0).T
    o_ref[0] = jnp.concatenate([left, right], axis=1).astype(BF16)


def _nsa_prompt(qt, kc, vct, ka, kw, vst, vwt, gates, tsel, twin, acmp, ovt):
    b, t, _ = ka.shape
    nqb = t // Q_BLOCK
    nc = kc.shape[1]
    top = min(SEL_TOP, -(-t // SEL_BLOCK))
    assert t >= WIN_TILE and t % FAR_TILE == 0 and t // SEL_BLOCK <= 128
    n_win = WINDOW // Q_BLOCK
    per_b = lambda shape: pl.BlockSpec((1,) + shape, lambda i, k, n: (i,) + (0,) * len(shape))
    return pl.pallas_call(
        functools.partial(_attn_kernel, top=top),
        grid=(b, NSA_KV, nqb),
        in_specs=[pl.BlockSpec((1, 1, NSA_GROUP * HEAD_DIM, Q_BLOCK), lambda i, k, n: (i, n, k, 0)),
                  per_b((nc, 128)), per_b((128, nc)),
                  per_b((t, 256)), per_b((t // FAR_TILE, 128, FAR_TILE)), per_b((t, 128)), per_b((nqb, 128, Q_BLOCK)),
                  pl.BlockSpec((1, Q_BLOCK, 128), lambda i, k, n: (i, n, 0)),
                  pl.BlockSpec((1, 4, NEAR_TILE, 512), lambda i, k, n: (k, 0, 0, 0)),
                  pl.BlockSpec((1, 1, WIN_TILE, 512), lambda i, k, n: (k, jnp.minimum(n, n_win), 0, 0)),
                  pl.BlockSpec((1, 128, 512), lambda i, k, n: (k, 0, 0)),
                  pl.BlockSpec((128, nc), lambda i, k, n: (0, 0))],
        out_specs=pl.BlockSpec((1, Q_BLOCK, 256), lambda i, k, n: (i, n, k)),
        out_shape=jax.ShapeDtypeStruct((b, t, 512), BF16),
        scratch_shapes=[pltpu.VMEM((FAR_TILE, NSA_GROUP * Q_BLOCK), F32), pltpu.VMEM((FAR_TILE, NSA_GROUP * Q_BLOCK), F32),
                        pltpu.VMEM((1, NSA_GROUP * Q_BLOCK), F32), pltpu.VMEM((1, NSA_GROUP * Q_BLOCK), F32),
                        pltpu.VMEM((HEAD_DIM, NSA_GROUP * Q_BLOCK), F32),
                        pltpu.VMEM((1, NSA_GROUP * Q_BLOCK), F32), pltpu.VMEM((FAR_TILE, NSA_GROUP * Q_BLOCK), BF16)],
        compiler_params=_cparams(3),
    )(qt, kc, vct, ka, vst, kw, vwt, gates, tsel, twin, acmp, ovt)


def _log_sigmoid(z):
    return jnp.minimum(z, 0.0) - jnp.log1p(jnp.exp(-jnp.abs(z)))


def _gla_kernel(x_ref, s0_ref, wg_ref, bg_ref, ggo_ref, tri_ref, o_ref, st_ref, s_scr, *, n_chunks):
    @pl.when(pl.program_id(0) == 0)
    def _():
        s_scr[...] = s0_ref[...]

    c_len = GLA_CHUNK
    ct = n_chunks * c_len
    tri = tri_ref[...]
    ti = lax.broadcasted_iota(jnp.int32, (ct, ct), 0)
    si = lax.broadcasted_iota(jnp.int32, (ct, ct), 1)
    causal = (si <= ti) & (si // c_len == ti // c_len)

    def per_chunk(x, row):
        return jnp.concatenate([jnp.broadcast_to(x[c_len * c + row:c_len * c + row + 1], (c_len, x.shape[1]))
                                for c in range(n_chunks)], axis=0)

    for bi in range(x_ref.shape[0]):
        q = x_ref[bi, :, 0:256]
        kk = x_ref[bi, :, 256:512]
        v = x_ref[bi, :, 512:1024]
        lr = x_ref[bi, :, 1024:1152]
        r = x_ref[bi, :, 1152:1664]
        la = _log_sigmoid(_dot(lr.astype(BF16), wg_ref[...]) + bg_ref[...]) * (1.0 / GLA_TAU)
        a1 = la.astype(BF16)
        r1 = la - a1.astype(F32)
        a2 = r1.astype(BF16)
        a3 = (r1 - a2.astype(F32)).astype(BF16)
        cb = _dot(tri, a1) + _dot(tri, a2) + _dot(tri, a3)
        last = per_chunk(cb, c_len - 1)
        mid = per_chunk(cb, c_len // 2)
        qe = (q * jnp.exp(cb)).astype(BF16)
        qa = (q * jnp.exp(jnp.minimum(cb - mid, EXP_CLAMP))).astype(BF16)
        kb = (kk * jnp.exp(jnp.minimum(mid - cb, EXP_CLAMP))).astype(BF16)
        ke = (kk * jnp.exp(last - cb)).astype(BF16)
        for h in range(GLA_HEADS):
            ks = slice(GLA_DK * h, GLA_DK * (h + 1))
            vs = slice(GLA_DV * h, GLA_DV * (h + 1))
            att = jnp.where(causal, _dot_nt(qa[:, ks], kb[:, ks]), 0.0)
            vh = v[:, vs].astype(BF16)
            o_intra = _dot(att.astype(BF16), vh)
            st = s_scr[bi, h]
            o_inter = []
            for c in range(n_chunks):
                rs = slice(c_len * c, c_len * (c + 1))
                o_inter.append(_dot_nt(qe[rs, ks], st.astype(BF16)))
                dec = jnp.exp(cb[c_len * (c + 1) - 1:c_len * (c + 1), ks])
                st = st * dec + _dot_tn(vh[rs], ke[rs, ks])
            s_scr[bi, h] = st
            on = _rms(o_intra + jnp.concatenate(o_inter, axis=0), ggo_ref[...])
            rh = r[:, vs]
            o_ref[bi, :, vs] = (on * (rh * _sigmoid(rh))).astype(BF16)
    st_ref[...] = s_scr[...]


def _gla_prompt(gla_in, s0t, wg, bg, ggo):
    b, t, w = gla_in.shape
    ct = min(256, t)
    assert t % ct == 0 and ct % GLA_CHUNK == 0
    pos = np.arange(ct)
    tri = jnp.asarray(((pos[None, :] <= pos[:, None])
                       & (pos[None, :] // GLA_CHUNK == pos[:, None] // GLA_CHUNK)).astype(np.float32), dtype=BF16)
    const = lambda shape: pl.BlockSpec(shape, lambda j: (0,) * len(shape))
    state_shape = (b, GLA_HEADS, GLA_DV, GLA_DK)
    return pl.pallas_call(
        functools.partial(_gla_kernel, n_chunks=ct // GLA_CHUNK),
        grid=(t // ct,),
        in_specs=[pl.BlockSpec((b, ct, w), lambda j: (0, j, 0)), const(state_shape),
                  const((128, 256)), const((1, 256)), const((1, 128)), const((ct, ct))],
        out_specs=[pl.BlockSpec((b, ct, 512), lambda j: (0, j, 0)), const(state_shape)],
        out_shape=[jax.ShapeDtypeStruct((b, t, 512), BF16), jax.ShapeDtypeStruct(state_shape, F32)],
        scratch_shapes=[pltpu.VMEM(state_shape, F32)],
        compiler_params=_cparams(1),
    )(gla_in, s0t, wg, bg, ggo, tri)


def _gla_step_kernel(q_ref, k_ref, lr_ref, v_ref, r_ref, s_ref, wgt_ref, bgt_ref, ggo_ref, o_ref, sn_ref):
    for bi, h in [(bi, h) for bi in range(q_ref.shape[0]) for h in range(GLA_HEADS)]:
        lr = lr_ref[bi]
        z = jnp.sum(wgt_ref[h] * lr, axis=-1, keepdims=True) + bgt_ref[h]
        a = jnp.exp(_log_sigmoid(z) * (1.0 / GLA_TAU))
        s0 = s_ref[bi, h]
        kh = k_ref[bi, h]
        qh = q_ref[bi, h]
        vh = v_ref[bi, h]
        sn_ref[bi, h] = a * s0 + kh * vh
        o = jnp.sum((qh * a) * s0, axis=0, keepdims=True) + jnp.sum(qh * kh, axis=0, keepdims=True) * vh
        on = _rms(o, ggo_ref[...])
        rh = r_ref[bi, h]
        o_ref[bi, h] = on * (rh * _sigmoid(rh))


def _gla_step(q_col, k_col, lr, v_row, r_row, s0, wgt, bgt, ggo):
    b = q_col.shape[0]
    bb = math.gcd(b, 8)
    const = lambda shape: pl.BlockSpec(shape, lambda i: (0,) * len(shape))
    per_b = lambda shape: pl.BlockSpec((bb,) + shape, lambda i: (i,) + (0,) * len(shape))
    return pl.pallas_call(
        _gla_step_kernel,
        grid=(b // bb,),
        in_specs=[per_b((GLA_HEADS, GLA_DK, 1)), per_b((GLA_HEADS, GLA_DK, 1)), per_b((1, 128)),
                  per_b((GLA_HEADS, 1, GLA_DV)), per_b((GLA_HEADS, 1, GLA_DV)), per_b((GLA_HEADS, GLA_DK, GLA_DV)),
                  const((GLA_HEADS, GLA_DK, 128)), const((GLA_HEADS, GLA_DK, 1)), const((1, 128))],
        out_specs=[per_b((GLA_HEADS, 1, GLA_DV)), per_b((GLA_HEADS, GLA_DK, GLA_DV))],
        out_shape=[jax.ShapeDtypeStruct((b, GLA_HEADS, 1, GLA_DV), F32),
                   jax.ShapeDtypeStruct((b, GLA_HEADS, GLA_DK, GLA_DV), F32)],
        compiler_params=_cparams(1),
    )(q_col, k_col, lr, v_row, r_row, s0, wgt, bgt, ggo)


def _xatt_kernel(xq_ref, mem_ref, o_ref):
    for bi in range(xq_ref.shape[0]):
        for h in range(X_HEADS):
            ls = slice(X_DIM * h, X_DIM * (h + 1))
            kh = mem_ref[bi, :, ls].astype(BF16)
            vh = mem_ref[bi, :, 512 + X_DIM * h:512 + X_DIM * (h + 1)].astype(BF16)
            s = _dot_nt(xq_ref[bi, :, ls], kh)
            p = jnp.exp(s - jnp.max(s, axis=-1, keepdims=True))
            p = p / jnp.sum(p, axis=-1, keepdims=True)
            o_ref[bi, :, ls] = _dot(p.astype(BF16), vh).astype(BF16)


def _xatt(xq, memkv):
    b, t, _ = xq.shape
    m = memkv.shape[1]
    tq = min(512, t)
    assert t % tq == 0
    bb = math.gcd(b, max(1, 128 // tq))
    return pl.pallas_call(
        _xatt_kernel,
        grid=(b // bb, t // tq),
        in_specs=[pl.BlockSpec((bb, tq, 512), lambda i, j: (i, j, 0)),
                  pl.BlockSpec((bb, m, 1024), lambda i, j: (i, 0, 0))],
        out_specs=pl.BlockSpec((bb, tq, 512), lambda i, j: (i, j, 0)),
        out_shape=jax.ShapeDtypeStruct((b, t, 512), BF16),
        compiler_params=_cparams(2),
    )(xq, memkv)


def _merge_kernel(on_ref, og_ref, ox_ref, mg_ref, x_ref, wn_ref, wg_ref, wx_ref, wo_ref, x1_ref):
    d = x_ref.shape[-1]
    merged = (mg_ref[:, 0:d].astype(F32) * _dot(on_ref[...], wn_ref[...])
              + mg_ref[:, d:2 * d].astype(F32) * _dot(og_ref[...], wg_ref[...])
              + mg_ref[:, 2 * d:3 * d].astype(F32) * _dot(ox_ref[...], wx_ref[...]))
    x1_ref[...] = x_ref[...] + _dot(merged.astype(BF16), wo_ref[...])


def _merge(o_nsa, o_gla, o_x, mg, x, wn, wg, wx, wo):
    m, d = x.shape
    tm = min(512, m)
    assert m % tm == 0
    row = lambda w: pl.BlockSpec((tm, w), lambda i: (i, 0))
    const = lambda shape: pl.BlockSpec(shape, lambda i: (0,) * len(shape))
    return pl.pallas_call(
        _merge_kernel,
        grid=(m // tm,),
        in_specs=[row(512), row(512), row(512), row(3 * d), row(d),
                  const((512, d)), const((512, d)), const((512, d)), const((d, d))],
        out_specs=row(d),
        out_shape=jax.ShapeDtypeStruct((m, d), F32),
        compiler_params=_cparams(1),
    )(o_nsa, o_gla, o_x, mg, x, wn, wg, wx, wo)


def _ffn_seq_kernel(x_ref, past_ref, g_ref, wup_ref, cw_ref, cb_ref, wdn_ref, y_ref, tail_ref, carry_ref):
    f = cw_ref.shape[-1]
    tm = x_ref.shape[1]

    @pl.when(pl.program_id(1) == 0)
    def _():
        carry_ref[...] = jnp.zeros(carry_ref.shape, F32)
        carry_ref[6:8, :] = past_ref[0]

    x1 = x_ref[0]
    ug = _dot(_rms(x1, g_ref[...]).astype(BF16), wup_ref[...])
    u = ug[:, 0:f]
    g = ug[:, f:2 * f]
    row = lax.broadcasted_iota(jnp.int32, (tm, f), 0)
    p1 = carry_ref[7:8, :]
    p2 = carry_ref[6:7, :]
    gm1 = jnp.where(row == 0, p1, pltpu.roll(g, 1, 0))
    gm2 = jnp.where(row == 0, p2, jnp.where(row == 1, p1, pltpu.roll(g, 2, 0)))
    gc = cb_ref[...] + cw_ref[0:1, :] * gm2 + cw_ref[1:2, :] * gm1 + cw_ref[2:3, :] * g
    y_ref[0] = x1 + _dot((_gelu(gc) * u).astype(BF16), wdn_ref[...])
    carry_ref[...] = g[tm - 8:tm, :]
    tail_ref[0] = g[tm - 8:tm, :]


def _ffn_seq(x1, conv_past, g_ffn, w_up, conv_w, conv_b, w_down):
    b, t, d = x1.shape
    f = conv_w.shape[-1]
    tm = min(512, t)
    assert t % tm == 0 and tm >= 8
    const = lambda shape: pl.BlockSpec(shape, lambda i, j: (0,) * len(shape), pipeline_mode=pl.Buffered(1))
    return pl.pallas_call(
        _ffn_seq_kernel,
        grid=(b, t // tm),
        in_specs=[pl.BlockSpec((1, tm, d), lambda i, j: (i, j, 0)),
                  pl.BlockSpec((1, 2, f), lambda i, j: (i, 0, 0)),
                  const((1, d)), const((d, 2 * f)), const((3, f)), const((1, f)), const((f, d))],
        out_specs=[pl.BlockSpec((1, tm, d), lambda i, j: (i, j, 0)),
                   pl.BlockSpec((1, 8, f), lambda i, j: (i, 0, 0))],
        out_shape=[jax.ShapeDtypeStruct((b, t, d), F32), jax.ShapeDtypeStruct((b, 8, f), F32)],
        scratch_shapes=[pltpu.VMEM((8, f), F32)],
        compiler_params=_cparams(2),
    )(x1, conv_past, g_ffn, w_up, conv_w, conv_b, w_down)


def _ffn_step_kernel(x_ref, p0_ref, p1_ref, g_ref, wup_ref, cw_ref, cb_ref, wdn_ref, y_ref, gnew_ref):
    f = cw_ref.shape[-1]
    x1 = x_ref[...]
    ug = _dot(_rms(x1, g_ref[...]).astype(BF16), wup_ref[...])
    u = ug[:, 0:f]
    g = ug[:, f:2 * f]
    gc = cb_ref[...] + cw_ref[0:1, :] * p0_ref[...] + cw_ref[1:2, :] * p1_ref[...] + cw_ref[2:3, :] * g
    y_ref[...] = x1 + _dot((_gelu(gc) * u).astype(BF16), wdn_ref[...])
    gnew_ref[...] = g


def _ffn_step(x1, p0, p1, g_ffn, w_up, conv_w, conv_b, w_down):
    m, d = x1.shape
    f = conv_w.shape[-1]
    full = lambda shape: pl.BlockSpec(shape, lambda i: (0,) * len(shape))
    return pl.pallas_call(
        _ffn_step_kernel,
        grid=(1,),
        in_specs=[full((m, d)), full((m, f)), full((m, f)), full((1, d)), full((d, 2 * f)), full((3, f)),
                  full((1, f)), full((f, d))],
        out_specs=[full((m, d)), full((m, f))],
        out_shape=[jax.ShapeDtypeStruct((m, d), F32), jax.ShapeDtypeStruct((m, f), F32)],
        compiler_params=_cparams(1),
    )(x1, p0, p1, g_ffn, w_up, conv_w, conv_b, w_down)


def _decode_kernel(pt_ref, *refs, top, n_sel, n_pages):
    del pt_ref
    page_refs = refs[:n_pages]
    (q8_ref, new_ref, neww_ref, cwin_ref, gt_ref, bc_ref, bs_ref, bw_ref, b0_ref, ovt_ref, et_ref, pe_ref, w1_ref,
     w2_ref, gk0_ref, p64_ref, o_ref, wout_ref, xs_ref) = refs[n_pages:]
    page = page_refs[0].shape[1]
    length = n_pages * page
    n_chunks = length // CMP_STRIDE
    for u, pg in enumerate(page_refs):
        for c in range(2):
            xs_ref[c, page * u:page * (u + 1), :] = pg[0, :, 128 * c:128 * (c + 1)]

    def load_j(j):
        return jnp.concatenate([xs_ref[c, pl.ds(j, n_chunks, stride=CMP_STRIDE), :] for c in range(2)], axis=1)

    def token_rows(lo):
        return jnp.concatenate([pg[0, :, lo:lo + 128] for pg in page_refs], axis=0)

    q8 = q8_ref[0]
    q8f = q8.astype(F32)
    rowk = lax.broadcasted_iota(jnp.int32, (8, 128), 0) // NSA_GROUP
    lane_half = lax.broadcasted_iota(jnp.int32, (8, 128), 1) // HEAD_DIM

    def half_mask(x):
        return jnp.where(rowk == lane_half, x, 0.0)

    kc, vc = _compress_core(load_j, n_chunks, pe_ref, w1_ref, w2_ref, gk0_ref, p64_ref)
    kc = kc.astype(BF16)
    vc = vc.astype(BF16)

    p_c = _masked_softmax(_dot_nt(q8, kc) + bc_ref[...], 1).astype(BF16)
    o_c = half_mask(_dot(p_c, vc))
    pf = p_c.astype(F32)
    p2 = jnp.concatenate([jnp.sum(pf[0:4], axis=0, keepdims=True), jnp.sum(pf[4:8], axis=0, keepdims=True),
                          jnp.zeros((126, n_chunks), F32)], axis=0)
    p2_hi = p2.astype(BF16)
    p2_lo = (p2 - p2_hi.astype(F32)).astype(BF16)
    imp_t = _dot_nt(ovt_ref[...], p2_hi) + _dot_nt(ovt_ref[...], p2_lo)
    nsr = imp_t.shape[0]
    nsp = -(-nsr // 128) * 128
    jj = lax.broadcasted_iota(jnp.int32, (nsr, 128), 0)
    tblk = length // SEL_BLOCK
    forced = (jj == 0) | (jj == tblk) | (jj == tblk - 1)
    sel_t = _select_with_forced(jnp.where(jj < n_sel, imp_t, -2e38), forced, top, 0)
    pen_t = jnp.concatenate([jnp.where(sel_t, 0.0, SEL_PENALTY), jnp.zeros((nsp - nsr, 128), F32)], axis=0)
    pen = pen_t.T
    pen8 = jnp.concatenate([jnp.broadcast_to(pen[0:1], (4, nsp)), jnp.broadcast_to(pen[1:2], (4, nsp))], axis=0)

    new = new_ref[0]
    neww = neww_ref[0]
    b0 = b0_ref[...]

    def attend(s_parts, s_new, v_parts, v_new):
        m = s_new
        for s in s_parts:
            m = jnp.maximum(m, jnp.max(s, axis=-1, keepdims=True))
        pn = jnp.exp(s_new - m)
        l = pn
        o = pn.astype(BF16).astype(F32) * v_new.astype(BF16).astype(F32)
        for s, v in zip(s_parts, v_parts):
            pp = jnp.exp(s - m)
            l = l + jnp.sum(pp, axis=-1, keepdims=True)
            o = o + _dot(pp.astype(BF16), v)
        return half_mask(o / l)

    def new_score(k_new):
        return jnp.sum(q8f * k_new.astype(BF16).astype(F32), axis=-1, keepdims=True) + b0

    k_past = token_rows(256).astype(BF16)
    v_past = token_rows(384).astype(BF16)
    s_past = _dot_nt(q8, k_past) + _dot_nt(pen8[:, 0:128].astype(BF16), et_ref[...]) + bs_ref[...]
    lane = lax.broadcasted_iota(jnp.int32, (8, nsp), 1)
    pen_new = jnp.sum(jnp.where(lane == tblk, pen8, 0.0), axis=-1, keepdims=True)
    o_s = attend([s_past], new_score(new[:, 256:384]) + pen_new, [v_past], new[:, 384:512])

    cw = cwin_ref[0]
    s_w = _dot_nt(q8, cw[:, 0:128].astype(BF16)) + bw_ref[...]
    o_w = attend([s_w], new_score(neww[:, 0:128]), [cw[:, 128:256].astype(BF16)], neww[:, 128:256])

    gt = gt_ref[0]
    o_ref[0] = gt[:, 0:1] * o_c + gt[:, 1:2] * o_s + gt[:, 2:3] * o_w

    wl = cw.shape[0]
    wrow = lax.broadcasted_iota(jnp.int32, cw.shape, 0)
    wout_ref[0] = jnp.where(wrow == wl - 1, neww, pltpu.roll(cw, wl - 1, 0))


def _nsa_decode(page_table, cache2d, q8, new_rows, new_win, cache_win, gates8, bc, bs, bw, b0, ov, et,
                pe, w1, w2, gk0, p64):
    db, n_pages = page_table.shape
    page = cache2d.shape[1]
    length = n_pages * page
    n_chunks = length // CMP_STRIDE
    n_sel = -(-(length + 1) // SEL_BLOCK)
    top = min(SEL_TOP, n_sel)
    wl = cache_win.shape[1]
    nsp = ov.shape[0]
    const = lambda shape: pl.BlockSpec(shape, lambda i, pt: (0,) * len(shape))
    per_b = lambda shape: pl.BlockSpec((1,) + shape, lambda i, pt: (i,) + (0,) * len(shape))
    page_spec = lambda u: pl.BlockSpec((1, page, 512), lambda i, pt: (pt[i, u], 0, 0))
    grid_spec = pltpu.PrefetchScalarGridSpec(
        num_scalar_prefetch=1,
        grid=(db,),
        in_specs=[page_spec(u) for u in range(n_pages)] + [
                  per_b((8, 128)), per_b((1, 512)), per_b((1, 256)), per_b((wl, 256)), per_b((8, 128)),
                  const((8, n_chunks)), const((8, length)), const((8, wl)), const((8, 1)),
                  const((nsp, n_chunks)), const((length, 128)),
                  const((2, CMP_STRIDE, 1, 256)), const((2, CMP_STRIDE * 256, 256)), const((256, 256)),
                  const((1, 128)), const((512, 512))],
        out_specs=[per_b((8, 128)), per_b((wl, 256))],
        scratch_shapes=[pltpu.VMEM((2, length, 128), F32)],
    )
    return pl.pallas_call(
        functools.partial(_decode_kernel, top=top, n_sel=n_sel, n_pages=n_pages),
        grid_spec=grid_spec,
        out_shape=[jax.ShapeDtypeStruct((db, 8, 128), F32), jax.ShapeDtypeStruct((db, wl, 256), F32)],
        compiler_params=_cparams(1),
    )(page_table, *([cache2d] * n_pages), q8, new_rows, new_win, cache_win, gates8, bc, bs, bw, b0, ov, et,
      pe, w1, w2, gk0, p64)


def _bucket_table():
    n = np.arange(MAX_DISTANCE + 1)
    max_exact = N_BUCKETS // 2
    nf = np.maximum(n, 1).astype(np.float32)
    large = max_exact + (np.log(nf / np.float32(max_exact)) / np.float32(math.log(MAX_DISTANCE / max_exact))
                         * np.float32(N_BUCKETS - max_exact)).astype(np.int32)
    return np.where(n < max_exact, n, np.minimum(large, N_BUCKETS - 1)).astype(np.int32)


def _bias_lookup(tb, rel, valid):
    idx = np.clip(rel, 0, MAX_DISTANCE)
    vals = jnp.moveaxis(tb[idx], -1, 0)
    return jnp.where(jnp.asarray(valid)[None], vals, NEG)


def _overlap(n_cmp_pad, n_cmp, n_sel_pad, n_sel):
    cs = (np.arange(n_cmp_pad) * CMP_STRIDE)[:, None]
    ss = (np.arange(n_sel_pad) * SEL_BLOCK)[None, :]
    ov = (cs < ss + SEL_BLOCK) & (cs + CMP_BLOCK > ss)
    ov &= (np.arange(n_cmp_pad) < n_cmp)[:, None] & (np.arange(n_sel_pad) < n_sel)[None, :]
    return jnp.asarray(ov.astype(np.float32), dtype=BF16)


def _stack_rows(x):
    return x.reshape(NSA_KV, NSA_GROUP * Q_BLOCK, x.shape[-1])


def _toeplitz(tbr, shift, width, max_valid):
    n = width + Q_BLOCK - 1
    u = np.arange(n)
    xs = shift - np.where(u < width, u, u - n)
    fvec = _bias_lookup(tbr, xs, (xs >= 0) & (xs <= max_valid))
    h = fvec.shape[0]
    return jnp.tile(fvec, (1, Q_BLOCK))[:, :Q_BLOCK * (n - 1)].reshape(h, Q_BLOCK, n - 1)[:, :, :width]


def _bias_descending(tb, top):
    far = jnp.broadcast_to(tb[MAX_DISTANCE][:, None], (tb.shape[1], top - MAX_DISTANCE + 1))
    return jnp.concatenate([far, tb[MAX_DISTANCE - 1:0:-1].T], axis=1)


def _prompt_tables(tb):
    tbr = (tb - tb[MAX_DISTANCE][None, :]) * LOG2E
    i = np.arange(Q_BLOCK)[:, None]

    def table(rel, valid):
        return _stack_rows(_bias_lookup(tbr, rel, valid))

    n_r = 4
    m_sel = _stack_rows(_toeplitz(tbr, Q_BLOCK * (n_r - 1), NEAR_TILE + Q_BLOCK * (n_r - 1), 1 << 30))
    tsel = jnp.stack([m_sel[:, :, Q_BLOCK * (n_r - 1 - r):Q_BLOCK * (n_r - 1 - r) + NEAR_TILE] for r in range(n_r)],
                     axis=1)
    n_v = WINDOW // Q_BLOCK + 1
    m_win = _stack_rows(_toeplitz(tbr, Q_BLOCK * (n_v - 1), WIN_TILE + Q_BLOCK * (n_v - 1), WINDOW - 1))
    twin = jnp.stack([m_win[:, :, Q_BLOCK * (n_v - 1 - v):Q_BLOCK * (n_v - 1 - v) + WIN_TILE] for v in range(n_v)],
                     axis=1)
    w = np.arange(16)[None, :] - 9
    rel = i - CMP_STRIDE * w - (CMP_BLOCK - 1)
    a = table(rel, rel >= 0)
    hi = a.astype(BF16)
    lo = (a - hi.astype(F32)).astype(BF16)
    acmp = jnp.concatenate([hi, lo, jnp.full(hi.shape[:2] + (1,), NEG, BF16),
                            jnp.zeros(hi.shape[:2] + (128 - 33,), BF16)], axis=-1)
    return jnp.swapaxes(tsel, 2, 3), jnp.swapaxes(twin, 2, 3), jnp.swapaxes(acmp, 1, 2)


def kernel(x_prompt, x_sample, cache_kv, cache_win, state_gla, state_conv, cache_mem, page_table, mem_prompt,
           g_mix, w_in, g_nsa_q, g_nsa_k, cmp_k_pe, cmp_k_w1, cmp_k_w2, cmp_v_pe, cmp_v_w1, cmp_v_w2,
           rel_bias, w_gla_gate, b_gla_gate, g_gla_o, g_mem, w_mem_kv, g_x_q, g_x_k,
           w_nsa_out, w_gla_out, w_x_out, w_o, g_ffn, w_up, conv_w, conv_b, w_down):
    bp, t, d = x_prompt.shape
    db = x_sample.shape[0]
    f = conv_w.shape[-1]

    offs = np.cumsum((0,) + IN_SIZES)
    segs = [w_in[:, offs[i]:offs[i + 1]] for i in range(len(IN_SIZES))]
    w_pad = jnp.concatenate([jnp.pad(s, ((0, 0), (0, pw - s.shape[1]))) for s, pw in zip(segs, PAD_SIZES)],
                            axis=1).astype(BF16)
    row = lambda v: v.reshape(1, -1).astype(F32)
    gq = row(jnp.tile(g_nsa_q, NSA_HEADS))
    gk0 = row(jnp.tile(g_nsa_k[0], NSA_KV))
    gk1 = row(jnp.tile(g_nsa_k[1], NSA_KV))
    gk2 = row(jnp.tile(g_nsa_k[2], NSA_KV))
    gxq = row(jnp.tile(g_x_q, X_HEADS))
    gxk = row(jnp.tile(g_x_k, X_HEADS))
    p64 = _block_ones(512, HEAD_DIM)
    p128 = _block_ones(512, X_DIM)
    bd2 = lambda a: jnp.concatenate([jnp.concatenate([a, jnp.zeros_like(a)], -1),
                                     jnp.concatenate([jnp.zeros_like(a), a], -1)], -2)
    bd4 = lambda a, c: jnp.concatenate([jnp.concatenate([bd2(a), jnp.zeros_like(bd2(a))], -1),
                                        jnp.concatenate([jnp.zeros_like(bd2(c)), bd2(c)], -1)], -2)
    pe = jnp.concatenate([jnp.tile(cmp_k_pe, (1, NSA_KV)), jnp.tile(cmp_v_pe, (1, NSA_KV))],
                         axis=-1).reshape(2, CMP_STRIDE, 1, 256)
    w1 = bd4(cmp_k_w1, cmp_v_w1).reshape(2, CMP_STRIDE * 256, 256).astype(BF16)
    w2 = bd4(cmp_k_w2, cmp_v_w2).astype(BF16)
    wg_pad = jnp.pad(w_gla_gate, ((0, 128 - GLA_RANK), (0, 0))).astype(BF16)
    tb = rel_bias.astype(F32)[_bucket_table()]
    wn, wgo, wx, wo = (w.astype(BF16) for w in (w_nsa_out, w_gla_out, w_x_out, w_o))
    wup = w_up.astype(BF16)
    wdn = w_down.astype(BF16)
    ggo = row(g_gla_o)

    rows_p, win_p, _, gates, gla_in, xq, mg, ka, kw, vst, vwt, qt = _proj_in(
        x_prompt, row(g_mix), w_pad, gq, gk1, gk2, gxq, p64, p128, True)
    memkv_p = _memory_kv(mem_prompt, row(g_mem), w_mem_kv.astype(BF16), gxk, p128)
    kc, vct = _compress(rows_p, pe, w1, w2, gk0, p64)
    n_chunks = t // CMP_STRIDE
    n_sel = -(-t // SEL_BLOCK)
    tsel, twin, acmp = _prompt_tables(tb)
    ovt = _overlap(n_chunks, n_chunks - 1, 128, n_sel).T
    o_nsa = _nsa_prompt(qt, kc, vct, ka, kw, vst, vwt, gates, tsel, twin, acmp, ovt)
    s0t = jnp.zeros((bp, GLA_HEADS, GLA_DV, GLA_DK), F32)
    o_gla, st = _gla_prompt(gla_in, s0t, wg_pad, row(b_gla_gate), ggo)
    o_x = _xatt(xq, memkv_p)
    m = bp * t
    x1 = _merge(o_nsa.reshape(m, 512), o_gla.reshape(m, 512), o_x.reshape(m, 512), mg.reshape(m, 3 * d),
                x_prompt.reshape(m, d), wn, wgo, wx, wo)
    y_p, tail = _ffn_seq(x1.reshape(bp, t, d), jnp.zeros((bp, 2, f), F32), row(g_ffn), wup, conv_w, row(conv_b), wdn)
    wl_p = min(WINDOW, t)
    out_rows_p = rows_p.reshape(bp, t, 4, NSA_KV, HEAD_DIM)
    out_win_p = win_p[:, t - wl_p:].reshape(bp, wl_p, 2, NSA_KV, HEAD_DIM)
    out_gla_p = jnp.swapaxes(st, 2, 3)
    out_conv_p = tail[:, 6:8]
    out_mem_p = memkv_p.reshape(bp, -1, 2, X_HEADS, X_DIM)

    n_pages = page_table.shape[1]
    page = cache_kv.shape[1]
    length = n_pages * page
    wl = cache_win.shape[1]
    rows_s, win_s, qn_s, gates_s, gla_s, xq_s, mg_s = (a[0] for a in _proj_in(
        x_sample.reshape(1, db, d), row(g_mix), w_pad, gq, gk1, gk2, gxq, p64, p128, False))

    eye = jnp.eye(NSA_KV, dtype=BF16)
    q8 = (qn_s.reshape(db, NSA_KV, NSA_GROUP, 1, HEAD_DIM) * eye[None, :, None, :, None]).reshape(db, 8, 128)
    gates8 = jnp.pad(gates_s[:, 0:24].reshape(db, 8, 3), ((0, 0), (0, 0), (0, 125)))
    n_chunks_s = length // CMP_STRIDE
    n_sel_s = -(-(length + 1) // SEL_BLOCK)
    nsp = -(-n_sel_s // 8) * 8
    cidx = np.arange(n_chunks_s)
    rel_c = length - (cidx * CMP_STRIDE + CMP_BLOCK - 1)
    bc = _bias_lookup(tb, rel_c, (rel_c >= 0) & (cidx < n_chunks_s - 1))
    bs = _bias_descending(tb, length)
    bw = jnp.where(jnp.asarray(np.arange(wl, 0, -1) < WINDOW)[None], _bias_descending(tb, wl), NEG)
    b0 = tb[0].reshape(8, 1)
    ov_s = _overlap(n_chunks_s, n_chunks_s - 1, nsp, n_sel_s).T
    assert length // SEL_BLOCK <= 128
    et = jnp.asarray((np.arange(length)[:, None] // SEL_BLOCK == np.arange(128)[None, :]).astype(np.float32),
                     dtype=BF16)
    o8, win_new = _nsa_decode(page_table, cache_kv.reshape(cache_kv.shape[0], page, 512), q8,
                              rows_s.reshape(db, 1, 512), win_s.reshape(db, 1, 256), cache_win.reshape(db, wl, 256),
                              gates8, bc, bs, bw, b0, ov_s, et, pe, w1, w2, gk0, p64)
    o8 = o8.reshape(db, NSA_KV, NSA_GROUP, NSA_KV, HEAD_DIM)
    o_nsa_s = jnp.stack([o8[:, 0, :, 0], o8[:, 1, :, 1]], axis=1).reshape(db, 512).astype(BF16)

    wgt = jnp.pad(w_gla_gate.T, ((0, 0), (0, 128 - GLA_RANK))).reshape(GLA_HEADS, GLA_DK, 128)
    o_gla_s, gla_state_s = _gla_step(
        gla_s[:, 0:256].reshape(db, GLA_HEADS, GLA_DK, 1), gla_s[:, 256:512].reshape(db, GLA_HEADS, GLA_DK, 1),
        gla_s[:, 1024:1152].reshape(db, 1, 128), gla_s[:, 512:1024].reshape(db, GLA_HEADS, 1, GLA_DV),
        gla_s[:, 1152:1664].reshape(db, GLA_HEADS, 1, GLA_DV), state_gla.astype(F32), wgt,
        b_gla_gate.reshape(GLA_HEADS, GLA_DK, 1), ggo)
    o_gla_s = o_gla_s.reshape(db, 512).astype(BF16)

    xq_pad = jnp.pad(xq_s.reshape(db, 1, 512), ((0, 0), (0, 15), (0, 0)))
    o_x_s = _xatt(xq_pad, cache_mem.reshape(db, -1, 1024))[:, 0]
    x1_s = _merge(o_nsa_s, o_gla_s, o_x_s, mg_s, x_sample.reshape(db, d), wn, wgo, wx, wo)
    y_s, g_new = _ffn_step(x1_s, state_conv[:, 0], state_conv[:, 1], row(g_ffn), wup, conv_w, row(conv_b), wdn)

    out_rows_s = rows_s.reshape(db, 1, 4, NSA_KV, HEAD_DIM)
    out_win_s = win_new.reshape(db, wl, 2, NSA_KV, HEAD_DIM)
    out_conv_s = jnp.stack([state_conv[:, 1], g_new], axis=1)
    return (y_p, y_s.reshape(db, 1, d), out_rows_p, out_win_p, out_gla_p, out_conv_p, out_mem_p,
            out_rows_s, out_win_s, gla_state_s, out_conv_s)
```

```python
import functools
import math

import numpy as np
import jax
import jax.numpy as jnp
from jax import lax
from jax.experimental import pallas as pl
from jax.experimental.pallas import tpu as pltpu

F32 = jnp.float32
BF16 = jnp.bfloat16

NSA_HEADS = 8
NSA_KV = 2
NSA_GROUP = 4
HEAD_DIM = 64
CMP_BLOCK = 32
CMP_STRIDE = 16
SEL_BLOCK = 64
SEL_TOP = 16
WINDOW = 512
Q_BLOCK = 128
GLA_HEADS = 4
GLA_DK = 64
GLA_DV = 128
GLA_RANK = 16
GLA_TAU = 16.0
GLA_CHUNK = 64
X_HEADS = 4
X_DIM = 128
N_BUCKETS = 32
MAX_DISTANCE = 128
EPS = 1e-6
LOG2E = math.log2(math.e)
NEG = -1e30
TINY = 1e-30
SEL_PENALTY = -1e9
MASKED_BELOW = -5e29
EXP_CLAMP = 80.0

IN_SIZES = (512, 768, 24, 256, 256, 512, 16, 512, 512, 3072)
PAD_SIZES = (512, 768, 128, 256, 256, 512, 128, 512, 512, 3072)
PAD_OFFS = tuple(int(v) for v in np.cumsum((0,) + PAD_SIZES))
D_IN_PAD = PAD_OFFS[-1]
GLA_IN_W = 256 + 256 + 512 + 128 + 512

VMEM_LIMIT = 56 * 1024 * 1024
FAR_TILE = 256
NEAR_TILE = 512
WIN_TILE = WINDOW + Q_BLOCK


def _cparams(n_axes):
    return pltpu.CompilerParams(dimension_semantics=("arbitrary",) * n_axes, vmem_limit_bytes=VMEM_LIMIT)


def _dot(a, b):
    return jnp.dot(a, b, preferred_element_type=F32)


def _dot_nt(a, b):
    return lax.dot_general(a, b, (((1,), (1,)), ((), ())), preferred_element_type=F32)


def _dot_tn(a, b):
    return lax.dot_general(a, b, (((0,), (0,)), ((), ())), preferred_element_type=F32)


def _split_dot(x, m):
    hi = x.astype(BF16)
    lo = (x - hi.astype(F32)).astype(BF16)
    return _dot(hi, m) + _dot(lo, m)


def _rms(x, g):
    return x * lax.rsqrt(jnp.mean(x * x, axis=-1, keepdims=True) + EPS) * g


def _group_rms(x, pmat, gsize, g):
    ss = _split_dot(x * x, pmat)
    return x * lax.rsqrt(ss * (1.0 / gsize) + EPS) * g


def _gelu(x):
    return 0.5 * x * (1.0 + jnp.tanh(math.sqrt(2.0 / math.pi) * (x + 0.044715 * (x * x * x))))


def _sigmoid(x):
    return 1.0 / (1.0 + jnp.exp(-x))


def _block_ones(n, gsize):
    i = np.arange(n) // gsize
    return jnp.asarray((i[:, None] == i[None, :]).astype(np.float32), dtype=BF16)


def _proj_kernel(x_ref, gmix_ref, w_ref, gq_ref, gk1_ref, gk2_ref, gxq_ref, p64_ref, p128_ref,
                 rows_ref, win_ref, qn_ref, gates_ref, gla_ref, xq_ref, mg_ref, *attn_refs):
    x = x_ref[0]
    h = _rms(x, gmix_ref[...]).astype(BF16)
    o = PAD_OFFS

    def seg(i):
        return _dot(h, w_ref[:, o[i]:o[i + 1]])

    p64 = p64_ref[...]
    p64s = p64_ref[0:128, 0:128]
    qn = _group_rms(seg(0), p64, HEAD_DIM, gq_ref[...]) * (HEAD_DIM ** -0.5)
    qn_ref[0] = qn.astype(BF16)
    if attn_refs:
        qt_ref = attn_refs[4]
        for u in range(x.shape[0] // Q_BLOCK):
            qt_ref[0, u] = (qn[Q_BLOCK * u:Q_BLOCK * (u + 1)] * LOG2E).T.astype(BF16)

    kv = seg(1)
    k_sel = _group_rms(kv[:, 256:384], p64s, HEAD_DIM, gk1_ref[...])
    k_win = _group_rms(kv[:, 512:640], p64s, HEAD_DIM, gk2_ref[...])
    rows_ref[0, :, 0:256] = kv[:, 0:256]
    rows_ref[0, :, 256:384] = k_sel
    rows_ref[0, :, 384:512] = kv[:, 384:512]
    win_ref[0, :, 0:128] = k_win
    win_ref[0, :, 128:256] = kv[:, 640:768]
    if attn_refs:
        ka_ref, kw_ref, vst_ref, vwt_ref, _ = attn_refs
        tm = x.shape[0]
        tpos = pl.program_id(1) * tm + lax.broadcasted_iota(jnp.int32, (tm, 128), 0)
        blk = lax.broadcasted_iota(jnp.int32, (tm, 128), 1)
        ka_ref[0, :, 0:128] = k_sel.astype(BF16)
        ka_ref[0, :, 128:256] = jnp.where(tpos // SEL_BLOCK == blk, 1.0, 0.0).astype(BF16)
        kw_ref[0] = k_win.astype(BF16)
        for u in range(tm // FAR_TILE):
            vst_ref[0, u] = kv[FAR_TILE * u:FAR_TILE * (u + 1), 384:512].T.astype(BF16)
        for u in range(tm // Q_BLOCK):
            vwt_ref[0, u] = kv[Q_BLOCK * u:Q_BLOCK * (u + 1), 640:768].T.astype(BF16)

    gates_ref[0] = _sigmoid(seg(2))
    gla_ref[0, :, 0:256] = seg(3) * (GLA_DK ** -0.5)
    gla_ref[0, :, 256:512] = seg(4)
    gla_ref[0, :, 512:1024] = seg(5)
    gla_ref[0, :, 1024:1152] = seg(6)
    gla_ref[0, :, 1152:1664] = seg(7)
    xq = _group_rms(seg(8), p128_ref[...], X_DIM, gxq_ref[...]) * (X_DIM ** -0.5)
    xq_ref[0] = xq.astype(BF16)
    mg_ref[0] = _sigmoid(seg(9)).astype(BF16)


def _proj_in(x, g_mix, w_pad, gq, gk1, gk2, gxq, p64, p128, attn_layout):
    b, t, d = x.shape
    tm = min(512, t)
    assert t % tm == 0
    widths = [512, 256, 512, 128, GLA_IN_W, 512, 3072]
    dtypes = [F32, F32, BF16, F32, F32, BF16, BF16]
    out_specs = [pl.BlockSpec((1, tm, w), lambda i, j: (i, j, 0)) for w in widths]
    out_shape = [jax.ShapeDtypeStruct((b, t, w), dt) for w, dt in zip(widths, dtypes)]
    if attn_layout:
        assert tm % Q_BLOCK == 0
        for w in (256, 128):
            out_specs.append(pl.BlockSpec((1, tm, w), lambda i, j: (i, j, 0)))
            out_shape.append(jax.ShapeDtypeStruct((b, t, w), BF16))
        assert tm % FAR_TILE == 0
        for rows, width in ((128, FAR_TILE), (128, Q_BLOCK), (512, Q_BLOCK)):
            out_specs.append(pl.BlockSpec((1, tm // width, rows, width), lambda i, j: (i, j, 0, 0)))
            out_shape.append(jax.ShapeDtypeStruct((b, t // width, rows, width), BF16))
    const = lambda shape: pl.BlockSpec(shape, lambda i, j: (0,) * len(shape), pipeline_mode=pl.Buffered(1))
    return pl.pallas_call(
        _proj_kernel,
        grid=(b, t // tm),
        in_specs=[pl.BlockSpec((1, tm, d), lambda i, j: (i, j, 0)),
                  const((1, d)), const((d, D_IN_PAD)), const((1, 512)), const((1, 128)), const((1, 128)),
                  const((1, 512)), const((512, 512)), const((512, 512))],
        out_specs=out_specs,
        out_shape=out_shape,
        compiler_params=_cparams(2),
    )(x, g_mix, w_pad, gq, gk1, gk2, gxq, p64, p128)


def _memkv_kernel(m_ref, g_ref, w_ref, gk_ref, p128_ref, o_ref):
    h = _rms(m_ref[0], g_ref[...]).astype(BF16)
    kv = _dot(h, w_ref[...])
    o_ref[0, :, 0:512] = _group_rms(kv[:, 0:512], p128_ref[...], X_DIM, gk_ref[...])
    o_ref[0, :, 512:1024] = kv[:, 512:1024]


def _memory_kv(mem, g_mem, w_mem, gxk, p128):
    b, m, d = mem.shape
    const = lambda shape: pl.BlockSpec(shape, lambda i: (0,) * len(shape))
    return pl.pallas_call(
        _memkv_kernel,
        grid=(b,),
        in_specs=[pl.BlockSpec((1, m, d), lambda i: (i, 0, 0)), const((1, d)), const((d, 1024)),
                  const((1, 512)), const((512, 512))],
        out_specs=pl.BlockSpec((1, m, 1024), lambda i: (i, 0, 0)),
        out_shape=jax.ShapeDtypeStruct((b, m, 1024), F32),
        compiler_params=_cparams(1),
    )(mem, g_mem, w_mem, gxk, p128)


def _compress_core(load_j, n_chunks, pe_ref, w1_ref, w2_ref, gk0_ref, p64_ref):
    xs = [load_j(j) for j in range(CMP_STRIDE)]
    halves = [_dot(jnp.concatenate([(x + pe_ref[r, j]).astype(BF16) for j, x in enumerate(xs)], axis=1), w1_ref[r])
              for r in range(2)]
    hid = halves[0] + pltpu.roll(halves[1], n_chunks - 1, 0)
    out = _dot(_gelu(hid).astype(BF16), w2_ref[...])
    row = lax.broadcasted_iota(jnp.int32, (n_chunks, 128), 0)
    live = row < n_chunks - 1
    kc = _group_rms(out[:, 0:128], p64_ref[0:128, 0:128], HEAD_DIM, gk0_ref[...])
    return jnp.where(live, kc, 0.0), jnp.where(live, out[:, 128:256], 0.0)


def _compress_kernel(rk_ref, rv_ref, pe_ref, w1_ref, w2_ref, gk0_ref, p64_ref, kc_ref, vc_ref, *, n_chunks):
    load_j = lambda j: jnp.concatenate([rk_ref[0, pl.ds(j, n_chunks, stride=CMP_STRIDE), :],
                                        rv_ref[0, pl.ds(j, n_chunks, stride=CMP_STRIDE), :]], axis=1)
    kc, vc = _compress_core(load_j, n_chunks, pe_ref, w1_ref, w2_ref, gk0_ref, p64_ref)
    kc_ref[0] = kc.astype(BF16)
    vc_ref[0] = vc.T.astype(BF16)


def _compress(rows, pe, w1, w2, gk0, p64):
    b, t, _ = rows.shape
    n_chunks = t // CMP_STRIDE
    const = lambda shape: pl.BlockSpec(shape, lambda i: (0,) * len(shape))
    return pl.pallas_call(
        functools.partial(_compress_kernel, n_chunks=n_chunks),
        grid=(b,),
        in_specs=[pl.BlockSpec((1, t, 128), lambda i: (i, 0, 0)), pl.BlockSpec((1, t, 128), lambda i: (i, 0, 1)),
                  const((2, CMP_STRIDE, 1, 256)), const((2, CMP_STRIDE * 256, 256)), const((256, 256)),
                  const((1, 128)), const((512, 512))],
        out_specs=[pl.BlockSpec((1, n_chunks, 128), lambda i: (i, 0, 0)),
                   pl.BlockSpec((1, 128, n_chunks), lambda i: (i, 0, 0))],
        out_shape=[jax.ShapeDtypeStruct((b, n_chunks, 128), BF16), jax.ShapeDtypeStruct((b, 128, n_chunks), BF16)],
        compiler_params=_cparams(1),
    )(rows, rows, pe, w1, w2, gk0, p64)


def _masked_softmax(s, axis, exp_fn=jnp.exp):
    m = jnp.maximum(jnp.max(s, axis=axis, keepdims=True), MASKED_BELOW)
    p = exp_fn(s - m)
    return p / jnp.maximum(jnp.sum(p, axis=axis, keepdims=True), TINY)


def _select_blocks(score, top, axis):
    pos = lax.broadcasted_iota(jnp.int32, score.shape, axis).astype(F32)
    sel = jnp.zeros(score.shape, jnp.bool_)
    for _ in range(top):
        mx = jnp.max(score, axis=axis, keepdims=True)
        idx = jnp.min(jnp.where(score == mx, pos, 1e9), axis=axis, keepdims=True)
        hit = pos == idx
        sel = jnp.logical_or(sel, hit)
        score = jnp.where(hit, -3e38, score)
    return sel


def _select_with_forced(imp, forced, top, axis):
    n_forced = 3
    assert top > n_forced
    return jnp.logical_or(forced, _select_blocks(jnp.where(forced, -3e38, imp), top - n_forced, axis))


def _values_times_probs(vt_tiles, p):
    out = None
    start = 0
    for vt in vt_tiles:
        part = _dot(vt, p[start:start + vt.shape[1]])
        start += vt.shape[1]
        out = part if out is None else out + part
    return out


def _softmax_stats_t(s_ref, bias, m_ref, l_ref):
    a_parts, p_parts = [], []
    for g in range(s_ref.shape[1] // Q_BLOCK):
        cs = slice(Q_BLOCK * g, Q_BLOCK * (g + 1))
        sg = s_ref[:, cs]
        if bias is not None:
            sg = sg + bias(cs)
        m_old = m_ref[:, cs]
        m_new = jnp.maximum(m_old, jnp.max(sg, axis=0, keepdims=True))
        alpha = jnp.exp2(m_old - m_new)
        p = jnp.exp2(sg - m_new)
        m_ref[:, cs] = m_new
        l_ref[:, cs] = alpha * l_ref[:, cs] + jnp.sum(p, axis=0, keepdims=True)
        a_parts.append(alpha)
        p_parts.append(p.astype(BF16))
    return jnp.concatenate(a_parts, axis=1), jnp.concatenate(p_parts, axis=1)


def _acc_update_t(acc_ref, alpha, vt_tiles, p):
    acc_ref[...] = alpha * acc_ref[...] + _values_times_probs(vt_tiles, p)


def _attn_kernel(qt_ref, kc_ref, vct_ref, ka_ref, vst_ref, kw_ref, vwt_ref, gt_ref, tsel_ref, twin_ref, acmp_ref,
                 ovt_ref, o_ref, sa_scr, sb_scr, m_scr, l_scr, acc_scr, pend_a, pend_p, *, top):
    n = pl.program_id(1)
    cols = NSA_GROUP * Q_BLOCK
    n_far = jnp.maximum(n - 2, 0) // 2
    r_near = n - 2 * n_far
    heads = [_attn_head_setup(k, n, qt_ref, kc_ref, vct_ref, kw_ref, vwt_ref, twin_ref, acmp_ref, ovt_ref, top)
             for k in range(NSA_KV)]
    kv_heads = range(NSA_KV)

    def value_tile(k, tile):
        return [vst_ref[0, tile, HEAD_DIM * k:HEAD_DIM * (k + 1), :]]

    def issue_scores(k, tile, s_scr):
        s_scr[k] = _dot(ka_ref[0, pl.ds(pl.multiple_of(tile * FAR_TILE, FAR_TILE), FAR_TILE), :], heads[k][0])

    def stats(k, s_scr, half_idx=None):
        bias = None
        if half_idx is not None:
            bias = lambda cs: tsel_ref[k, r_near, FAR_TILE * half_idx:FAR_TILE * (half_idx + 1), cs]
        return _softmax_stats_t(s_scr.at[k], bias, m_scr.at[k], l_scr.at[k])

    def accumulate(k, tile, alpha, p):
        _acc_update_t(acc_scr.at[k], alpha, value_tile(k, tile), p)

    m_scr[...] = jnp.full(m_scr.shape, NEG, F32)
    l_scr[...] = jnp.zeros(l_scr.shape, F32)
    acc_scr[...] = jnp.zeros(acc_scr.shape, F32)
    pend_a[...] = jnp.ones(pend_a.shape, F32)
    pend_p[...] = jnp.zeros(pend_p.shape, BF16)
    odd = n_far % 2

    @pl.when(odd == 1)
    def _():
        for k in kv_heads:
            issue_scores(k, 0, sa_scr)
        for k in kv_heads:
            accumulate(k, 0, *stats(k, sa_scr))

    for k in kv_heads:
        issue_scores(k, odd, sa_scr)

    @pl.loop(0, n_far // 2)
    def _(j):
        t0 = odd + 2 * j
        for k in kv_heads:
            issue_scores(k, t0 + 1, sb_scr)
        first = [stats(k, sa_scr) for k in kv_heads]
        for k in kv_heads:
            accumulate(k, jnp.maximum(t0 - 1, 0), pend_a[k], pend_p[k])
            issue_scores(k, t0 + 2, sa_scr)
        second = [stats(k, sb_scr) for k in kv_heads]
        for k in kv_heads:
            accumulate(k, t0, *first[k])
            pend_a[k] = second[k][0]
            pend_p[k] = second[k][1]

    for k in kv_heads:
        accumulate(k, jnp.maximum(n_far - 1, 0), pend_a[k], pend_p[k])
        issue_scores(k, n_far + 1, sb_scr)
    for k in kv_heads:
        accumulate(k, n_far, *stats(k, sa_scr, 0))
    for k in kv_heads:
        accumulate(k, n_far + 1, *stats(k, sb_scr, 1))

    gtt = gt_ref[0].T
    for k in kv_heads:
        _, o_c, o_w = heads[k]
        o_s = acc_scr[k] / l_scr[k]

        def gate_row(branch):
            return jnp.concatenate([gtt[12 * k + 3 * g + branch:12 * k + 3 * g + branch + 1]
                                    for g in range(NSA_GROUP)], axis=1)

        o_t = gate_row(0) * o_c + gate_row(1) * o_s + gate_row(2) * o_w
        left = jnp.concatenate([o_t[:, 0:128], o_t[:, 128:256]], axis=0).T
        right = jnp.concatenate([o_t[:, 256:384], o_t[:, 384:512]], axis=0).T
        o_ref[0, :, 256 * k:256 * (k + 1)] = jnp.concatenate([left, right], axis=1).astype(BF16)


def _attn_head_setup(k, n, qt_ref, kc_ref, vct_ref, kw_ref, vwt_ref, twin_ref, acmp_ref, ovt_ref, top):
    cols = NSA_GROUP * Q_BLOCK
    vrows = slice(HEAD_DIM * k, HEAD_DIM * (k + 1))
    zero = jnp.zeros((HEAD_DIM, Q_BLOCK), BF16)
    parts = []
    for g in range(NSA_GROUP):
        h = NSA_GROUP * k + g
        piece = qt_ref[0, 0, HEAD_DIM * h:HEAD_DIM * (h + 1), :]
        parts.append(jnp.concatenate([piece, zero] if k == 0 else [zero, piece], axis=0))
    qt = jnp.concatenate(parts, axis=1)

    nc = kc_ref.shape[1]
    blocks_per_q = Q_BLOCK // CMP_STRIDE
    cp = lax.broadcasted_iota(jnp.int32, (nc, 128), 0) - blocks_per_q * n + 9
    fr = lax.broadcasted_iota(jnp.int32, (nc, 128), 1)
    feat = ((fr < 32) & (cp == (fr & 15))) | ((fr == 32) & (cp > 15))
    kc_aug = jnp.concatenate([kc_ref[0], jnp.where(feat, 1.0, 0.0).astype(BF16)], axis=1)
    s_c = _dot(kc_aug, jnp.concatenate([qt, acmp_ref[k]], axis=0))
    p_c = _masked_softmax(s_c, 0, jnp.exp2).astype(BF16)
    o_c = _dot(vct_ref[0, vrows, :], p_c)
    imp4 = _dot(ovt_ref[...], p_c)
    imp = imp4[:, 0:128] + imp4[:, 128:256] + imp4[:, 256:384] + imp4[:, 384:512]

    def value_tiles(ref, first, count):
        return [ref[0, first + d, vrows, :] for d in range(count)]

    wt = jnp.maximum(n - WINDOW // Q_BLOCK, 0)
    s_w = _dot(kw_ref[0, pl.ds(pl.multiple_of(wt * Q_BLOCK, Q_BLOCK), WIN_TILE), :], qt) + twin_ref[k, 0]
    p_w = jnp.exp2(s_w - jnp.max(s_w, axis=0, keepdims=True))
    o_w = (_values_times_probs(value_tiles(vwt_ref, wt, WIN_TILE // Q_BLOCK), p_w.astype(BF16))
           / jnp.sum(p_w, axis=0, keepdims=True))

    jj = lax.broadcasted_iota(jnp.int32, (128, Q_BLOCK), 0)
    tpos = n * Q_BLOCK + lax.broadcasted_iota(jnp.int32, (128, Q_BLOCK), 1)
    tblk = tpos // SEL_BLOCK
    forced = (jj == 0) | (jj == tblk) | (jj == tblk - 1)
    sel_t = _select_with_forced(jnp.where(jj * SEL_BLOCK <= tpos, imp, NEG), forced, top, 0)
    pen_t = jnp.where(sel_t, 0.0, SEL_PENALTY).astype(BF16)
    rhs = jnp.concatenate([qt, jnp.concatenate([pen_t] * NSA_GROUP, axis=1)], axis=0)
    return rhs, o_c, o_w


def _nsa_prompt(qt, kc, vct, ka, kw, vst, vwt, gates, tsel, twin, acmp, ovt):
    b, t, _ = ka.shape
    nqb = t // Q_BLOCK
    nc = kc.shape[1]
    top = min(SEL_TOP, -(-t // SEL_BLOCK))
    assert t >= WIN_TILE and t % FAR_TILE == 0 and t // SEL_BLOCK <= 128
    n_win = WINDOW // Q_BLOCK
    per_b = lambda shape: pl.BlockSpec((1,) + shape, lambda i, n: (i,) + (0,) * len(shape))
    const = lambda shape: pl.BlockSpec(shape, lambda i, n: (0,) * len(shape), pipeline_mode=pl.Buffered(1))
    cols = NSA_GROUP * Q_BLOCK
    per_kv = lambda rows, dt: pltpu.VMEM((NSA_KV, rows, cols), dt)
    return pl.pallas_call(
        functools.partial(_attn_kernel, top=top),
        grid=(b, nqb),
        in_specs=[pl.BlockSpec((1, 1, NSA_HEADS * HEAD_DIM, Q_BLOCK), lambda i, n: (i, n, 0, 0)),
                  per_b((nc, 128)), per_b((128, nc)),
                  per_b((t, 256)), per_b((t // FAR_TILE, 128, FAR_TILE)), per_b((t, 128)), per_b((nqb, 128, Q_BLOCK)),
                  pl.BlockSpec((1, Q_BLOCK, 128), lambda i, n: (i, n, 0)),
                  const((NSA_KV, 4, NEAR_TILE, 512)),
                  pl.BlockSpec((NSA_KV, 1, WIN_TILE, 512), lambda i, n: (0, jnp.minimum(n, n_win), 0, 0)),
                  const((NSA_KV, 128, 512)), const((128, nc))],
        out_specs=pl.BlockSpec((1, Q_BLOCK, 512), lambda i, n: (i, n, 0)),
        out_shape=jax.ShapeDtypeStruct((b, t, 512), BF16),
        scratch_shapes=[per_kv(FAR_TILE, F32), per_kv(FAR_TILE, F32), per_kv(1, F32), per_kv(1, F32),
                        per_kv(HEAD_DIM, F32), per_kv(1, F32), per_kv(FAR_TILE, BF16)],
        compiler_params=_cparams(2),
    )(qt, kc, vct, ka, vst, kw, vwt, gates, tsel, twin, acmp, ovt)


def _log_sigmoid(z):
    return jnp.minimum(z, 0.0) - jnp.log1p(jnp.exp(-jnp.abs(z)))


def _gla_kernel(x_ref, s0_ref, wg_ref, bg_ref, ggo_ref, tri_ref, o_ref, st_ref, s_scr, *, n_chunks):
    @pl.when(pl.program_id(0) == 0)
    def _():
        s_scr[...] = s0_ref[...]

    c_len = GLA_CHUNK
    ct = n_chunks * c_len
    tri = tri_ref[...]
    ti = lax.broadcasted_iota(jnp.int32, (ct, ct), 0)
    si = lax.broadcasted_iota(jnp.int32, (ct, ct), 1)
    causal = (si <= ti) & (si // c_len == ti // c_len)

    def per_chunk(x, row):
        return jnp.concatenate([jnp.broadcast_to(x[c_len * c + row:c_len * c + row + 1], (c_len, x.shape[1]))
                                for c in range(n_chunks)], axis=0)

    for bi in range(x_ref.shape[0]):
        q = x_ref[bi, :, 0:256]
        kk = x_ref[bi, :, 256:512]
        v = x_ref[bi, :, 512:1024]
        lr = x_ref[bi, :, 1024:1152]
        r = x_ref[bi, :, 1152:1664]
        la = _log_sigmoid(_dot(lr.astype(BF16), wg_ref[...]) + bg_ref[...]) * (1.0 / GLA_TAU)
        a1 = la.astype(BF16)
        r1 = la - a1.astype(F32)
        a2 = r1.astype(BF16)
        a3 = (r1 - a2.astype(F32)).astype(BF16)
        cb = _dot(tri, a1) + _dot(tri, a2) + _dot(tri, a3)
        last = per_chunk(cb, c_len - 1)
        mid = per_chunk(cb, c_len // 2)
        qe = (q * jnp.exp(cb)).astype(BF16)
        qa = (q * jnp.exp(jnp.minimum(cb - mid, EXP_CLAMP))).astype(BF16)
        kb = (kk * jnp.exp(jnp.minimum(mid - cb, EXP_CLAMP))).astype(BF16)
        ke = (kk * jnp.exp(last - cb)).astype(BF16)
        for h in range(GLA_HEADS):
            ks = slice(GLA_DK * h, GLA_DK * (h + 1))
            vs = slice(GLA_DV * h, GLA_DV * (h + 1))
            att = jnp.where(causal, _dot_nt(qa[:, ks], kb[:, ks]), 0.0)
            vh = v[:, vs].astype(BF16)
            o_intra = _dot(att.astype(BF16), vh)
            st = s_scr[bi, h]
            o_inter = []
            for c in range(n_chunks):
                rs = slice(c_len * c, c_len * (c + 1))
                o_inter.append(_dot_nt(qe[rs, ks], st.astype(BF16)))
                dec = jnp.exp(cb[c_len * (c + 1) - 1:c_len * (c + 1), ks])
                st = st * dec + _dot_tn(vh[rs], ke[rs, ks])
            s_scr[bi, h] = st
            on = _rms(o_intra + jnp.concatenate(o_inter, axis=0), ggo_ref[...])
            rh = r[:, vs]
            o_ref[bi, :, vs] = (on * (rh * _sigmoid(rh))).astype(BF16)
    st_ref[...] = s_scr[...]


def _gla_prompt(gla_in, s0t, wg, bg, ggo):
    b, t, w = gla_in.shape
    ct = min(256, t)
    assert t % ct == 0 and ct % GLA_CHUNK == 0
    pos = np.arange(ct)
    tri = jnp.asarray(((pos[None, :] <= pos[:, None])
                       & (pos[None, :] // GLA_CHUNK == pos[:, None] // GLA_CHUNK)).astype(np.float32), dtype=BF16)
    const = lambda shape: pl.BlockSpec(shape, lambda j: (0,) * len(shape))
    state_shape = (b, GLA_HEADS, GLA_DV, GLA_DK)
    return pl.pallas_call(
        functools.partial(_gla_kernel, n_chunks=ct // GLA_CHUNK),
        grid=(t // ct,),
        in_specs=[pl.BlockSpec((b, ct, w), lambda j: (0, j, 0)), const(state_shape),
                  const((128, 256)), const((1, 256)), const((1, 128)), const((ct, ct))],
        out_specs=[pl.BlockSpec((b, ct, 512), lambda j: (0, j, 0)), const(state_shape)],
        out_shape=[jax.ShapeDtypeStruct((b, t, 512), BF16), jax.ShapeDtypeStruct(state_shape, F32)],
        scratch_shapes=[pltpu.VMEM(state_shape, F32)],
        compiler_params=_cparams(1),
    )(gla_in, s0t, wg, bg, ggo, tri)


def _gla_step_kernel(q_ref, k_ref, lr_ref, v_ref, r_ref, s_ref, wgt_ref, bgt_ref, ggo_ref, o_ref, sn_ref):
    for bi, h in [(bi, h) for bi in range(q_ref.shape[0]) for h in range(GLA_HEADS)]:
        lr = lr_ref[bi]
        z = jnp.sum(wgt_ref[h] * lr, axis=-1, keepdims=True) + bgt_ref[h]
        a = jnp.exp(_log_sigmoid(z) * (1.0 / GLA_TAU))
        s0 = s_ref[bi, h]
        kh = k_ref[bi, h]
        qh = q_ref[bi, h]
        vh = v_ref[bi, h]
        sn_ref[bi, h] = a * s0 + kh * vh
        o = jnp.sum((qh * a) * s0, axis=0, keepdims=True) + jnp.sum(qh * kh, axis=0, keepdims=True) * vh
        on = _rms(o, ggo_ref[...])
        rh = r_ref[bi, h]
        o_ref[bi, h] = on * (rh * _sigmoid(rh))


def _gla_step(q_col, k_col, lr, v_row, r_row, s0, wgt, bgt, ggo):
    b = q_col.shape[0]
    bb = math.gcd(b, 8)
    const = lambda shape: pl.BlockSpec(shape, lambda i: (0,) * len(shape))
    per_b = lambda shape: pl.BlockSpec((bb,) + shape, lambda i: (i,) + (0,) * len(shape))
    return pl.pallas_call(
        _gla_step_kernel,
        grid=(b // bb,),
        in_specs=[per_b((GLA_HEADS, GLA_DK, 1)), per_b((GLA_HEADS, GLA_DK, 1)), per_b((1, 128)),
                  per_b((GLA_HEADS, 1, GLA_DV)), per_b((GLA_HEADS, 1, GLA_DV)), per_b((GLA_HEADS, GLA_DK, GLA_DV)),
                  const((GLA_HEADS, GLA_DK, 128)), const((GLA_HEADS, GLA_DK, 1)), const((1, 128))],
        out_specs=[per_b((GLA_HEADS, 1, GLA_DV)), per_b((GLA_HEADS, GLA_DK, GLA_DV))],
        out_shape=[jax.ShapeDtypeStruct((b, GLA_HEADS, 1, GLA_DV), F32),
                   jax.ShapeDtypeStruct((b, GLA_HEADS, GLA_DK, GLA_DV), F32)],
        compiler_params=_cparams(1),
    )(q_col, k_col, lr, v_row, r_row, s0, wgt, bgt, ggo)


def _xatt_kernel(xq_ref, mem_ref, o_ref):
    for bi in range(xq_ref.shape[0]):
        for h in range(X_HEADS):
            ls = slice(X_DIM * h, X_DIM * (h + 1))
            kh = mem_ref[bi, :, ls].astype(BF16)
            vh = mem_ref[bi, :, 512 + X_DIM * h:512 + X_DIM * (h + 1)].astype(BF16)
            s = _dot_nt(xq_ref[bi, :, ls], kh)
            p = jnp.exp(s - jnp.max(s, axis=-1, keepdims=True))
            p = p / jnp.sum(p, axis=-1, keepdims=True)
            o_ref[bi, :, ls] = _dot(p.astype(BF16), vh).astype(BF16)


def _xatt(xq, memkv):
    b, t, _ = xq.shape
    m = memkv.shape[1]
    tq = min(512, t)
    assert t % tq == 0
    bb = math.gcd(b, max(1, 128 // tq))
    return pl.pallas_call(
        _xatt_kernel,
        grid=(b // bb, t // tq),
        in_specs=[pl.BlockSpec((bb, tq, 512), lambda i, j: (i, j, 0)),
                  pl.BlockSpec((bb, m, 1024), lambda i, j: (i, 0, 0))],
        out_specs=pl.BlockSpec((bb, tq, 512), lambda i, j: (i, j, 0)),
        out_shape=jax.ShapeDtypeStruct((b, t, 512), BF16),
        compiler_params=_cparams(2),
    )(xq, memkv)


def _merge_kernel(on_ref, og_ref, ox_ref, mg_ref, x_ref, wn_ref, wg_ref, wx_ref, wo_ref, x1_ref):
    d = x_ref.shape[-1]
    merged = (mg_ref[:, 0:d].astype(F32) * _dot(on_ref[...], wn_ref[...])
              + mg_ref[:, d:2 * d].astype(F32) * _dot(og_ref[...], wg_ref[...])
              + mg_ref[:, 2 * d:3 * d].astype(F32) * _dot(ox_ref[...], wx_ref[...]))
    x1_ref[...] = x_ref[...] + _dot(merged.astype(BF16), wo_ref[...])


def _merge(o_nsa, o_gla, o_x, mg, x, wn, wg, wx, wo):
    m, d = x.shape
    tm = min(512, m)
    assert m % tm == 0
    row = lambda w: pl.BlockSpec((tm, w), lambda i: (i, 0))
    const = lambda shape: pl.BlockSpec(shape, lambda i: (0,) * len(shape))
    return pl.pallas_call(
        _merge_kernel,
        grid=(m // tm,),
        in_specs=[row(512), row(512), row(512), row(3 * d), row(d),
                  const((512, d)), const((512, d)), const((512, d)), const((d, d))],
        out_specs=row(d),
        out_shape=jax.ShapeDtypeStruct((m, d), F32),
        compiler_params=_cparams(1),
    )(o_nsa, o_gla, o_x, mg, x, wn, wg, wx, wo)


def _ffn_seq_kernel(x_ref, past_ref, g_ref, wup_ref, cw_ref, cb_ref, wdn_ref, y_ref, tail_ref, carry_ref):
    f = cw_ref.shape[-1]
    tm = x_ref.shape[1]

    @pl.when(pl.program_id(1) == 0)
    def _():
        carry_ref[...] = jnp.zeros(carry_ref.shape, F32)
        carry_ref[6:8, :] = past_ref[0]

    x1 = x_ref[0]
    ug = _dot(_rms(x1, g_ref[...]).astype(BF16), wup_ref[...])
    u = ug[:, 0:f]
    g = ug[:, f:2 * f]
    row = lax.broadcasted_iota(jnp.int32, (tm, f), 0)
    p1 = carry_ref[7:8, :]
    p2 = carry_ref[6:7, :]
    gm1 = jnp.where(row == 0, p1, pltpu.roll(g, 1, 0))
    gm2 = jnp.where(row == 0, p2, jnp.where(row == 1, p1, pltpu.roll(g, 2, 0)))
    gc = cb_ref[...] + cw_ref[0:1, :] * gm2 + cw_ref[1:2, :] * gm1 + cw_ref[2:3, :] * g
    y_ref[0] = x1 + _dot((_gelu(gc) * u).astype(BF16), wdn_ref[...])
    carry_ref[...] = g[tm - 8:tm, :]
    tail_ref[0] = g[tm - 8:tm, :]


def _ffn_seq(x1, conv_past, g_ffn, w_up, conv_w, conv_b, w_down):
    b, t, d = x1.shape
    f = conv_w.shape[-1]
    tm = min(512, t)
    assert t % tm == 0 and tm >= 8
    const = lambda shape: pl.BlockSpec(shape, lambda i, j: (0,) * len(shape), pipeline_mode=pl.Buffered(1))
    return pl.pallas_call(
        _ffn_seq_kernel,
        grid=(b, t // tm),
        in_specs=[pl.BlockSpec((1, tm, d), lambda i, j: (i, j, 0)),
                  pl.BlockSpec((1, 2, f), lambda i, j: (i, 0, 0)),
                  const((1, d)), const((d, 2 * f)), const((3, f)), const((1, f)), const((f, d))],
        out_specs=[pl.BlockSpec((1, tm, d), lambda i, j: (i, j, 0)),
                   pl.BlockSpec((1, 8, f), lambda i, j: (i, 0, 0))],
        out_shape=[jax.ShapeDtypeStruct((b, t, d), F32), jax.ShapeDtypeStruct((b, 8, f), F32)],
        scratch_shapes=[pltpu.VMEM((8, f), F32)],
        compiler_params=_cparams(2),
    )(x1, conv_past, g_ffn, w_up, conv_w, conv_b, w_down)


def _ffn_step_kernel(x_ref, p0_ref, p1_ref, g_ref, wup_ref, cw_ref, cb_ref, wdn_ref, y_ref, gnew_ref):
    f = cw_ref.shape[-1]
    x1 = x_ref[...]
    ug = _dot(_rms(x1, g_ref[...]).astype(BF16), wup_ref[...])
    u = ug[:, 0:f]
    g = ug[:, f:2 * f]
    gc = cb_ref[...] + cw_ref[0:1, :] * p0_ref[...] + cw_ref[1:2, :] * p1_ref[...] + cw_ref[2:3, :] * g
    y_ref[...] = x1 + _dot((_gelu(gc) * u).astype(BF16), wdn_ref[...])
    gnew_ref[...] = g


def _ffn_step(x1, p0, p1, g_ffn, w_up, conv_w, conv_b, w_down):
    m, d = x1.shape
    f = conv_w.shape[-1]
    full = lambda shape: pl.BlockSpec(shape, lambda i: (0,) * len(shape))
    return pl.pallas_call(
        _ffn_step_kernel,
        grid=(1,),
        in_specs=[full((m, d)), full((m, f)), full((m, f)), full((1, d)), full((d, 2 * f)), full((3, f)),
                  full((1, f)), full((f, d))],
        out_specs=[full((m, d)), full((m, f))],
        out_shape=[jax.ShapeDtypeStruct((m, d), F32), jax.ShapeDtypeStruct((m, f), F32)],
        compiler_params=_cparams(1),
    )(x1, p0, p1, g_ffn, w_up, conv_w, conv_b, w_down)


def _decode_kernel(pt_ref, *refs, top, n_sel, n_pages):
    del pt_ref
    page_refs = refs[:n_pages]
    (q8_ref, new_ref, neww_ref, cwin_ref, gt_ref, bc_ref, bs_ref, bw_ref, b0_ref, ovt_ref, et_ref, pe_ref, w1_ref,
     w2_ref, gk0_ref, p64_ref, o_ref, wout_ref, xs_ref) = refs[n_pages:]
    page = page_refs[0].shape[1]
    length = n_pages * page
    n_chunks = length // CMP_STRIDE
    for u, pg in enumerate(page_refs):
        for c in range(2):
            xs_ref[c, page * u:page * (u + 1), :] = pg[0, :, 128 * c:128 * (c + 1)]

    def load_j(j):
        return jnp.concatenate([xs_ref[c, pl.ds(j, n_chunks, stride=CMP_STRIDE), :] for c in range(2)], axis=1)

    def token_rows(lo):
        return jnp.concatenate([pg[0, :, lo:lo + 128] for pg in page_refs], axis=0)

    q8 = q8_ref[0]
    q8f = q8.astype(F32)
    rowk = lax.broadcasted_iota(jnp.int32, (8, 128), 0) // NSA_GROUP
    lane_half = lax.broadcasted_iota(jnp.int32, (8, 128), 1) // HEAD_DIM

    def half_mask(x):
        return jnp.where(rowk == lane_half, x, 0.0)

    kc, vc = _compress_core(load_j, n_chunks, pe_ref, w1_ref, w2_ref, gk0_ref, p64_ref)
    kc = kc.astype(BF16)
    vc = vc.astype(BF16)

    p_c = _masked_softmax(_dot_nt(q8, kc) + bc_ref[...], 1).astype(BF16)
    o_c = half_mask(_dot(p_c, vc))
    pf = p_c.astype(F32)
    p2 = jnp.concatenate([jnp.sum(pf[0:4], axis=0, keepdims=True), jnp.sum(pf[4:8], axis=0, keepdims=True),
                          jnp.zeros((126, n_chunks), F32)], axis=0)
    p2_hi = p2.astype(BF16)
    p2_lo = (p2 - p2_hi.astype(F32)).astype(BF16)
    imp_t = _dot_nt(ovt_ref[...], p2_hi) + _dot_nt(ovt_ref[...], p2_lo)
    nsr = imp_t.shape[0]
    nsp = -(-nsr // 128) * 128
    jj = lax.broadcasted_iota(jnp.int32, (nsr, 128), 0)
    tblk = length // SEL_BLOCK
    forced = (jj == 0) | (jj == tblk) | (jj == tblk - 1)
    sel_t = _select_with_forced(jnp.where(jj < n_sel, imp_t, -2e38), forced, top, 0)
    pen_t = jnp.concatenate([jnp.where(sel_t, 0.0, SEL_PENALTY), jnp.zeros((nsp - nsr, 128), F32)], axis=0)
    pen = pen_t.T
    pen8 = jnp.concatenate([jnp.broadcast_to(pen[0:1], (4, nsp)), jnp.broadcast_to(pen[1:2], (4, nsp))], axis=0)

    new = new_ref[0]
    neww = neww_ref[0]
    b0 = b0_ref[...]

    def attend(s_parts, s_new, v_parts, v_new):
        m = s_new
        for s in s_parts:
            m = jnp.maximum(m, jnp.max(s, axis=-1, keepdims=True))
        pn = jnp.exp(s_new - m)
        l = pn
        o = pn.astype(BF16).astype(F32) * v_new.astype(BF16).astype(F32)
        for s, v in zip(s_parts, v_parts):
            pp = jnp.exp(s - m)
            l = l + jnp.sum(pp, axis=-1, keepdims=True)
            o = o + _dot(pp.astype(BF16), v)
        return half_mask(o / l)

    def new_score(k_new):
        return jnp.sum(q8f * k_new.astype(BF16).astype(F32), axis=-1, keepdims=True) + b0

    k_past = token_rows(256).astype(BF16)
    v_past = token_rows(384).astype(BF16)
    s_past = _dot_nt(q8, k_past) + _dot_nt(pen8[:, 0:128].astype(BF16), et_ref[...]) + bs_ref[...]
    lane = lax.broadcasted_iota(jnp.int32, (8, nsp), 1)
    pen_new = jnp.sum(jnp.where(lane == tblk, pen8, 0.0), axis=-1, keepdims=True)
    o_s = attend([s_past], new_score(new[:, 256:384]) + pen_new, [v_past], new[:, 384:512])

    cw = cwin_ref[0]
    s_w = _dot_nt(q8, cw[:, 0:128].astype(BF16)) + bw_ref[...]
    o_w = attend([s_w], new_score(neww[:, 0:128]), [cw[:, 128:256].astype(BF16)], neww[:, 128:256])

    gt = gt_ref[0]
    o_ref[0] = gt[:, 0:1] * o_c + gt[:, 1:2] * o_s + gt[:, 2:3] * o_w

    wl = cw.shape[0]
    wrow = lax.broadcasted_iota(jnp.int32, cw.shape, 0)
    wout_ref[0] = jnp.where(wrow == wl - 1, neww, pltpu.roll(cw, wl - 1, 0))


def _nsa_decode(page_table, cache2d, q8, new_rows, new_win, cache_win, gates8, bc, bs, bw, b0, ov, et,
                pe, w1, w2, gk0, p64):
    db, n_pages = page_table.shape
    page = cache2d.shape[1]
    length = n_pages * page
    n_chunks = length // CMP_STRIDE
    n_sel = -(-(length + 1) // SEL_BLOCK)
    top = min(SEL_TOP, n_sel)
    wl = cache_win.shape[1]
    nsp = ov.shape[0]
    const = lambda shape: pl.BlockSpec(shape, lambda i, pt: (0,) * len(shape))
    per_b = lambda shape: pl.BlockSpec((1,) + shape, lambda i, pt: (i,) + (0,) * len(shape))
    page_spec = lambda u: pl.BlockSpec((1, page, 512), lambda i, pt: (pt[i, u], 0, 0))
    grid_spec = pltpu.PrefetchScalarGridSpec(
        num_scalar_prefetch=1,
        grid=(db,),
        in_specs=[page_spec(u) for u in range(n_pages)] + [
                  per_b((8, 128)), per_b((1, 512)), per_b((1, 256)), per_b((wl, 256)), per_b((8, 128)),
                  const((8, n_chunks)), const((8, length)), const((8, wl)), const((8, 1)),
                  const((nsp, n_chunks)), const((length, 128)),
                  const((2, CMP_STRIDE, 1, 256)), const((2, CMP_STRIDE * 256, 256)), const((256, 256)),
                  const((1, 128)), const((512, 512))],
        out_specs=[per_b((8, 128)), per_b((wl, 256))],
        scratch_shapes=[pltpu.VMEM((2, length, 128), F32)],
    )
    return pl.pallas_call(
        functools.partial(_decode_kernel, top=top, n_sel=n_sel, n_pages=n_pages),
        grid_spec=grid_spec,
        out_shape=[jax.ShapeDtypeStruct((db, 8, 128), F32), jax.ShapeDtypeStruct((db, wl, 256), F32)],
        compiler_params=_cparams(1),
    )(page_table, *([cache2d] * n_pages), q8, new_rows, new_win, cache_win, gates8, bc, bs, bw, b0, ov, et,
      pe, w1, w2, gk0, p64)


def _bucket_table():
    n = np.arange(MAX_DISTANCE + 1)
    max_exact = N_BUCKETS // 2
    nf = np.maximum(n, 1).astype(np.float32)
    large = max_exact + (np.log(nf / np.float32(max_exact)) / np.float32(math.log(MAX_DISTANCE / max_exact))
                         * np.float32(N_BUCKETS - max_exact)).astype(np.int32)
    return np.where(n < max_exact, n, np.minimum(large, N_BUCKETS - 1)).astype(np.int32)


def _bias_lookup(tb, rel, valid):
    idx = np.clip(rel, 0, MAX_DISTANCE)
    vals = jnp.moveaxis(tb[idx], -1, 0)
    return jnp.where(jnp.asarray(valid)[None], vals, NEG)


def _overlap(n_cmp_pad, n_cmp, n_sel_pad, n_sel):
    cs = (np.arange(n_cmp_pad) * CMP_STRIDE)[:, None]
    ss = (np.arange(n_sel_pad) * SEL_BLOCK)[None, :]
    ov = (cs < ss + SEL_BLOCK) & (cs + CMP_BLOCK > ss)
    ov &= (np.arange(n_cmp_pad) < n_cmp)[:, None] & (np.arange(n_sel_pad) < n_sel)[None, :]
    return jnp.asarray(ov.astype(np.float32), dtype=BF16)


def _stack_rows(x):
    return x.reshape(NSA_KV, NSA_GROUP * Q_BLOCK, x.shape[-1])


def _toeplitz(tbr, shift, width, max_valid):
    n = width + Q_BLOCK - 1
    u = np.arange(n)
    xs = shift - np.where(u < width, u, u - n)
    fvec = _bias_lookup(tbr, xs, (xs >= 0) & (xs <= max_valid))
    h = fvec.shape[0]
    return jnp.tile(fvec, (1, Q_BLOCK))[:, :Q_BLOCK * (n - 1)].reshape(h, Q_BLOCK, n - 1)[:, :, :width]


def _bias_descending(tb, top):
    far = jnp.broadcast_to(tb[MAX_DISTANCE][:, None], (tb.shape[1], top - MAX_DISTANCE + 1))
    return jnp.concatenate([far, tb[MAX_DISTANCE - 1:0:-1].T], axis=1)


def _prompt_tables(tb):
    tbr = (tb - tb[MAX_DISTANCE][None, :]) * LOG2E
    i = np.arange(Q_BLOCK)[:, None]

    def table(rel, valid):
        return _stack_rows(_bias_lookup(tbr, rel, valid))

    n_r = 4
    m_sel = _stack_rows(_toeplitz(tbr, Q_BLOCK * (n_r - 1), NEAR_TILE + Q_BLOCK * (n_r - 1), 1 << 30))
    tsel = jnp.stack([m_sel[:, :, Q_BLOCK * (n_r - 1 - r):Q_BLOCK * (n_r - 1 - r) + NEAR_TILE] for r in range(n_r)],
                     axis=1)
    n_v = WINDOW // Q_BLOCK + 1
    m_win = _stack_rows(_toeplitz(tbr, Q_BLOCK * (n_v - 1), WIN_TILE + Q_BLOCK * (n_v - 1), WINDOW - 1))
    twin = jnp.stack([m_win[:, :, Q_BLOCK * (n_v - 1 - v):Q_BLOCK * (n_v - 1 - v) + WIN_TILE] for v in range(n_v)],
                     axis=1)
    w = np.arange(16)[None, :] - 9
    rel = i - CMP_STRIDE * w - (CMP_BLOCK - 1)
    a = table(rel, rel >= 0)
    hi = a.astype(BF16)
    lo = (a - hi.astype(F32)).astype(BF16)
    acmp = jnp.concatenate([hi, lo, jnp.full(hi.shape[:2] + (1,), NEG, BF16),
                            jnp.zeros(hi.shape[:2] + (128 - 33,), BF16)], axis=-1)
    return jnp.swapaxes(tsel, 2, 3), jnp.swapaxes(twin, 2, 3), jnp.swapaxes(acmp, 1, 2)


def kernel(x_prompt, x_sample, cache_kv, cache_win, state_gla, state_conv, cache_mem, page_table, mem_prompt,
           g_mix, w_in, g_nsa_q, g_nsa_k, cmp_k_pe, cmp_k_w1, cmp_k_w2, cmp_v_pe, cmp_v_w1, cmp_v_w2,
           rel_bias, w_gla_gate, b_gla_gate, g_gla_o, g_mem, w_mem_kv, g_x_q, g_x_k,
           w_nsa_out, w_gla_out, w_x_out, w_o, g_ffn, w_up, conv_w, conv_b, w_down):
    bp, t, d = x_prompt.shape
    db = x_sample.shape[0]
    f = conv_w.shape[-1]

    offs = np.cumsum((0,) + IN_SIZES)
    segs = [w_in[:, offs[i]:offs[i + 1]] for i in range(len(IN_SIZES))]
    w_pad = jnp.concatenate([jnp.pad(s, ((0, 0), (0, pw - s.shape[1]))) for s, pw in zip(segs, PAD_SIZES)],
                            axis=1).astype(BF16)
    row = lambda v: v.reshape(1, -1).astype(F32)
    gq = row(jnp.tile(g_nsa_q, NSA_HEADS))
    gk0 = row(jnp.tile(g_nsa_k[0], NSA_KV))
    gk1 = row(jnp.tile(g_nsa_k[1], NSA_KV))
    gk2 = row(jnp.tile(g_nsa_k[2], NSA_KV))
    gxq = row(jnp.tile(g_x_q, X_HEADS))
    gxk = row(jnp.tile(g_x_k, X_HEADS))
    p64 = _block_ones(512, HEAD_DIM)
    p128 = _block_ones(512, X_DIM)
    bd2 = lambda a: jnp.concatenate([jnp.concatenate([a, jnp.zeros_like(a)], -1),
                                     jnp.concatenate([jnp.zeros_like(a), a], -1)], -2)
    bd4 = lambda a, c: jnp.concatenate([jnp.concatenate([bd2(a), jnp.zeros_like(bd2(a))], -1),
                                        jnp.concatenate([jnp.zeros_like(bd2(c)), bd2(c)], -1)], -2)
    pe = jnp.concatenate([jnp.tile(cmp_k_pe, (1, NSA_KV)), jnp.tile(cmp_v_pe, (1, NSA_KV))],
                         axis=-1).reshape(2, CMP_STRIDE, 1, 256)
    w1 = bd4(cmp_k_w1, cmp_v_w1).reshape(2, CMP_STRIDE * 256, 256).astype(BF16)
    w2 = bd4(cmp_k_w2, cmp_v_w2).astype(BF16)
    wg_pad = jnp.pad(w_gla_gate, ((0, 128 - GLA_RANK), (0, 0))).astype(BF16)
    tb = rel_bias.astype(F32)[_bucket_table()]
    wn, wgo, wx, wo = (w.astype(BF16) for w in (w_nsa_out, w_gla_out, w_x_out, w_o))
    wup = w_up.astype(BF16)
    wdn = w_down.astype(BF16)
    ggo = row(g_gla_o)

    rows_p, win_p, _, gates, gla_in, xq, mg, ka, kw, vst, vwt, qt = _proj_in(
        x_prompt, row(g_mix), w_pad, gq, gk1, gk2, gxq, p64, p128, True)
    memkv_p = _memory_kv(mem_prompt, row(g_mem), w_mem_kv.astype(BF16), gxk, p128)
    kc, vct = _compress(rows_p, pe, w1, w2, gk0, p64)
    n_chunks = t // CMP_STRIDE
    n_sel = -(-t // SEL_BLOCK)
    tsel, twin, acmp = _prompt_tables(tb)
    ovt = _overlap(n_chunks, n_chunks - 1, 128, n_sel).T
    o_nsa = _nsa_prompt(qt, kc, vct, ka, kw, vst, vwt, gates, tsel, twin, acmp, ovt)
    s0t = jnp.zeros((bp, GLA_HEADS, GLA_DV, GLA_DK), F32)
    o_gla, st = _gla_prompt(gla_in, s0t, wg_pad, row(b_gla_gate), ggo)
    o_x = _xatt(xq, memkv_p)
    m = bp * t
    x1 = _merge(o_nsa.reshape(m, 512), o_gla.reshape(m, 512), o_x.reshape(m, 512), mg.reshape(m, 3 * d),
                x_prompt.reshape(m, d), wn, wgo, wx, wo)
    y_p, tail = _ffn_seq(x1.reshape(bp, t, d), jnp.zeros((bp, 2, f), F32), row(g_ffn), wup, conv_w, row(conv_b), wdn)
    wl_p = min(WINDOW, t)
    out_rows_p = rows_p.reshape(bp, t, 4, NSA_KV, HEAD_DIM)
    out_win_p = win_p[:, t - wl_p:].reshape(bp, wl_p, 2, NSA_KV, HEAD_DIM)
    out_gla_p = jnp.swapaxes(st, 2, 3)
    out_conv_p = tail[:, 6:8]
    out_mem_p = memkv_p.reshape(bp, -1, 2, X_HEADS, X_DIM)

    n_pages = page_table.shape[1]
    page = cache_kv.shape[1]
    length = n_pages * page
    wl = cache_win.shape[1]
    rows_s, win_s, qn_s, gates_s, gla_s, xq_s, mg_s = (a[0] for a in _proj_in(
        x_sample.reshape(1, db, d), row(g_mix), w_pad, gq, gk1, gk2, gxq, p64, p128, False))

    eye = jnp.eye(NSA_KV, dtype=BF16)
    q8 = (qn_s.reshape(db, NSA_KV, NSA_GROUP, 1, HEAD_DIM) * eye[None, :, None, :, None]).reshape(db, 8, 128)
    gates8 = jnp.pad(gates_s[:, 0:24].reshape(db, 8, 3), ((0, 0), (0, 0), (0, 125)))
    n_chunks_s = length // CMP_STRIDE
    n_sel_s = -(-(length + 1) // SEL_BLOCK)
    nsp = -(-n_sel_s // 8) * 8
    cidx = np.arange(n_chunks_s)
    rel_c = length - (cidx * CMP_STRIDE + CMP_BLOCK - 1)
    bc = _bias_lookup(tb, rel_c, (rel_c >= 0) & (cidx < n_chunks_s - 1))
    bs = _bias_descending(tb, length)
    bw = jnp.where(jnp.asarray(np.arange(wl, 0, -1) < WINDOW)[None], _bias_descending(tb, wl), NEG)
    b0 = tb[0].reshape(8, 1)
    ov_s = _overlap(n_chunks_s, n_chunks_s - 1, nsp, n_sel_s).T
    assert length // SEL_BLOCK <= 128
    et = jnp.asarray((np.arange(length)[:, None] // SEL_BLOCK == np.arange(128)[None, :]).astype(np.float32),
                     dtype=BF16)
    o8, win_new = _nsa_decode(page_table, cache_kv.reshape(cache_kv.shape[0], page, 512), q8,
                              rows_s.reshape(db, 1, 512), win_s.reshape(db, 1, 256), cache_win.reshape(db, wl, 256),
                              gates8, bc, bs, bw, b0, ov_s, et, pe, w1, w2, gk0, p64)
    o8 = o8.reshape(db, NSA_KV, NSA_GROUP, NSA_KV, HEAD_DIM)
    o_nsa_s = jnp.stack([o8[:, 0, :, 0], o8[:, 1, :, 1]], axis=1).reshape(db, 512).astype(BF16)

    wgt = jnp.pad(w_gla_gate.T, ((0, 0), (0, 128 - GLA_RANK))).reshape(GLA_HEADS, GLA_DK, 128)
    o_gla_s, gla_state_s = _gla_step(
        gla_s[:, 0:256].reshape(db, GLA_HEADS, GLA_DK, 1), gla_s[:, 256:512].reshape(db, GLA_HEADS, GLA_DK, 1),
        gla_s[:, 1024:1152].reshape(db, 1, 128), gla_s[:, 512:1024].reshape(db, GLA_HEADS, 1, GLA_DV),
        gla_s[:, 1152:1664].reshape(db, GLA_HEADS, 1, GLA_DV), state_gla.astype(F32), wgt,
        b_gla_gate.reshape(GLA_HEADS, GLA_DK, 1), ggo)
    o_gla_s = o_gla_s.reshape(db, 512).astype(BF16)

    xq_pad = jnp.pad(xq_s.reshape(db, 1, 512), ((0, 0), (0, 15), (0, 0)))
    o_x_s = _xatt(xq_pad, cache_mem.reshape(db, -1, 1024))[:, 0]
    x1_s = _merge(o_nsa_s, o_gla_s, o_x_s, mg_s, x_sample.reshape(db, d), wn, wgo, wx, wo)
    y_s, g_new = _ffn_step(x1_s, state_conv[:, 0], state_conv[:, 1], row(g_ffn), wup, conv_w, row(conv_b), wdn)

    out_rows_s = rows_s.reshape(db, 1, 4, NSA_KV, HEAD_DIM)
    out_win_s = win_new.reshape(db, wl, 2, NSA_KV, HEAD_DIM)
    out_conv_s = jnp.stack([state_conv[:, 1], g_new], axis=1)
    return (y_p, y_s.reshape(db, 1, d), out_rows_p, out_win_p, out_gla_p, out_conv_p, out_mem_p,
            out_rows_s, out_win_s, gla_state_s, out_conv_s)
```

```python
import functools
import math

import numpy as np
import jax
import jax.numpy as jnp
from jax import lax
from jax.experimental import pallas as pl
from jax.experimental.pallas import tpu as pltpu

F32 = jnp.float32
BF16 = jnp.bfloat16

NSA_HEADS = 8
NSA_KV = 2
NSA_GROUP = 4
HEAD_DIM = 64
CMP_BLOCK = 32
CMP_STRIDE = 16
SEL_BLOCK = 64
SEL_TOP = 16
WINDOW = 512
Q_BLOCK = 128
GLA_HEADS = 4
GLA_DK = 64
GLA_DV = 128
GLA_RANK = 16
GLA_TAU = 16.0
GLA_CHUNK = 64
X_HEADS = 4
X_DIM = 128
N_BUCKETS = 32
MAX_DISTANCE = 128
EPS = 1e-6
LOG2E = math.log2(math.e)
NEG = -1e30
TINY = 1e-30
SEL_PENALTY = -1e9
MASKED_BELOW = -5e29
EXP_CLAMP = 80.0

IN_SIZES = (512, 768, 24, 256, 256, 512, 16, 512, 512, 3072)
PAD_SIZES = (512, 768, 128, 256, 256, 512, 128, 512, 512, 3072)
PAD_OFFS = tuple(int(v) for v in np.cumsum((0,) + PAD_SIZES))
D_IN_PAD = PAD_OFFS[-1]
GLA_IN_W = 256 + 256 + 512 + 128 + 512

VMEM_LIMIT = 56 * 1024 * 1024
FAR_TILE = 256
NEAR_TILE = 512
WIN_TILE = WINDOW + Q_BLOCK


def _cparams(n_axes):
    return pltpu.CompilerParams(dimension_semantics=("arbitrary",) * n_axes, vmem_limit_bytes=VMEM_LIMIT)


def _dot(a, b):
    return jnp.dot(a, b, preferred_element_type=F32)


def _dot_nt(a, b):
    return lax.dot_general(a, b, (((1,), (1,)), ((), ())), preferred_element_type=F32)


def _dot_tn(a, b):
    return lax.dot_general(a, b, (((0,), (0,)), ((), ())), preferred_element_type=F32)


def _split_dot(x, m):
    hi = x.astype(BF16)
    lo = (x - hi.astype(F32)).astype(BF16)
    return _dot(hi, m) + _dot(lo, m)


def _rms(x, g):
    return x * lax.rsqrt(jnp.mean(x * x, axis=-1, keepdims=True) + EPS) * g


def _group_rms(x, pmat, gsize, g):
    ss = _split_dot(x * x, pmat)
    return x * lax.rsqrt(ss * (1.0 / gsize) + EPS) * g


def _gelu(x):
    return 0.5 * x * (1.0 + jnp.tanh(math.sqrt(2.0 / math.pi) * (x + 0.044715 * (x * x * x))))


def _sigmoid(x):
    return 1.0 / (1.0 + jnp.exp(-x))


def _block_ones(n, gsize):
    i = np.arange(n) // gsize
    return jnp.asarray((i[:, None] == i[None, :]).astype(np.float32), dtype=BF16)


def _proj_kernel(x_ref, gmix_ref, w_ref, gq_ref, gk1_ref, gk2_ref, gxq_ref, p64_ref, p128_ref,
                 rows_ref, win_ref, qn_ref, gates_ref, gla_ref, xq_ref, mg_ref, *attn_refs):
    x = x_ref[0]
    h = _rms(x, gmix_ref[...]).astype(BF16)
    o = PAD_OFFS

    def seg(i):
        return _dot(h, w_ref[:, o[i]:o[i + 1]])

    p64 = p64_ref[...]
    p64s = p64_ref[0:128, 0:128]
    qn = _group_rms(seg(0), p64, HEAD_DIM, gq_ref[...]) * (HEAD_DIM ** -0.5)
    qn_ref[0] = qn.astype(BF16)
    if attn_refs:
        qt_ref = attn_refs[4]
        for u in range(x.shape[0] // Q_BLOCK):
            qt_ref[0, u] = (qn[Q_BLOCK * u:Q_BLOCK * (u + 1)] * LOG2E).T.astype(BF16)

    kv = seg(1)
    k_sel = _group_rms(kv[:, 256:384], p64s, HEAD_DIM, gk1_ref[...])
    k_win = _group_rms(kv[:, 512:640], p64s, HEAD_DIM, gk2_ref[...])
    rows_ref[0, :, 0:256] = kv[:, 0:256]
    rows_ref[0, :, 256:384] = k_sel
    rows_ref[0, :, 384:512] = kv[:, 384:512]
    win_ref[0, :, 0:128] = k_win
    win_ref[0, :, 128:256] = kv[:, 640:768]
    if attn_refs:
        ka_ref, kw_ref, vst_ref, vwt_ref, _ = attn_refs
        tm = x.shape[0]
        tpos = pl.program_id(1) * tm + lax.broadcasted_iota(jnp.int32, (tm, 128), 0)
        blk = lax.broadcasted_iota(jnp.int32, (tm, 128), 1)
        ka_ref[0, :, 0:128] = k_sel.astype(BF16)
        ka_ref[0, :, 128:256] = jnp.where(tpos // SEL_BLOCK == blk, 1.0, 0.0).astype(BF16)
        kw_ref[0] = k_win.astype(BF16)
        for u in range(tm // FAR_TILE):
            vst_ref[0, u] = kv[FAR_TILE * u:FAR_TILE * (u + 1), 384:512].T.astype(BF16)
        for u in range(tm // Q_BLOCK):
            vwt_ref[0, u] = kv[Q_BLOCK * u:Q_BLOCK * (u + 1), 640:768].T.astype(BF16)

    gates_ref[0] = _sigmoid(seg(2))
    gla_ref[0, :, 0:256] = seg(3) * (GLA_DK ** -0.5)
    gla_ref[0, :, 256:512] = seg(4)
    gla_ref[0, :, 512:1024] = seg(5)
    gla_ref[0, :, 1024:1152] = seg(6)
    gla_ref[0, :, 1152:1664] = seg(7)
    xq = _group_rms(seg(8), p128_ref[...], X_DIM, gxq_ref[...]) * (X_DIM ** -0.5)
    xq_ref[0] = xq.astype(BF16)
    mg_ref[0] = _sigmoid(seg(9)).astype(BF16)


def _proj_in(x, g_mix, w_pad, gq, gk1, gk2, gxq, p64, p128, attn_layout):
    b, t, d = x.shape
    tm = min(512, t)
    assert t % tm == 0
    widths = [512, 256, 512, 128, GLA_IN_W, 512, 3072]
    dtypes = [F32, F32, BF16, F32, F32, BF16, BF16]
    out_specs = [pl.BlockSpec((1, tm, w), lambda i, j: (i, j, 0)) for w in widths]
    out_shape = [jax.ShapeDtypeStruct((b, t, w), dt) for w, dt in zip(widths, dtypes)]
    if attn_layout:
        assert tm % Q_BLOCK == 0
        for w in (256, 128):
            out_specs.append(pl.BlockSpec((1, tm, w), lambda i, j: (i, j, 0)))
            out_shape.append(jax.ShapeDtypeStruct((b, t, w), BF16))
        assert tm % FAR_TILE == 0
        for rows, width in ((128, FAR_TILE), (128, Q_BLOCK), (512, Q_BLOCK)):
            out_specs.append(pl.BlockSpec((1, tm // width, rows, width), lambda i, j: (i, j, 0, 0)))
            out_shape.append(jax.ShapeDtypeStruct((b, t // width, rows, width), BF16))
    const = lambda shape: pl.BlockSpec(shape, lambda i, j: (0,) * len(shape), pipeline_mode=pl.Buffered(1))
    return pl.pallas_call(
        _proj_kernel,
        grid=(b, t // tm),
        in_specs=[pl.BlockSpec((1, tm, d), lambda i, j: (i, j, 0)),
                  const((1, d)), const((d, D_IN_PAD)), const((1, 512)), const((1, 128)), const((1, 128)),
                  const((1, 512)), const((512, 512)), const((512, 512))],
        out_specs=out_specs,
        out_shape=out_shape,
        compiler_params=_cparams(2),
    )(x, g_mix, w_pad, gq, gk1, gk2, gxq, p64, p128)


def _memkv_kernel(m_ref, g_ref, w_ref, gk_ref, p128_ref, o_ref):
    h = _rms(m_ref[0], g_ref[...]).astype(BF16)
    kv = _dot(h, w_ref[...])
    o_ref[0, :, 0:512] = _group_rms(kv[:, 0:512], p128_ref[...], X_DIM, gk_ref[...])
    o_ref[0, :, 512:1024] = kv[:, 512:1024]


def _memory_kv(mem, g_mem, w_mem, gxk, p128):
    b, m, d = mem.shape
    const = lambda shape: pl.BlockSpec(shape, lambda i: (0,) * len(shape))
    return pl.pallas_call(
        _memkv_kernel,
        grid=(b,),
        in_specs=[pl.BlockSpec((1, m, d), lambda i: (i, 0, 0)), const((1, d)), const((d, 1024)),
                  const((1, 512)), const((512, 512))],
        out_specs=pl.BlockSpec((1, m, 1024), lambda i: (i, 0, 0)),
        out_shape=jax.ShapeDtypeStruct((b, m, 1024), F32),
        compiler_params=_cparams(1),
    )(mem, g_mem, w_mem, gxk, p128)


def _compress_core(load_j, n_chunks, pe_ref, w1_ref, w2_ref, gk0_ref, p64_ref):
    xs = [load_j(j) for j in range(CMP_STRIDE)]
    halves = [_dot(jnp.concatenate([(x + pe_ref[r, j]).astype(BF16) for j, x in enumerate(xs)], axis=1), w1_ref[r])
              for r in range(2)]
    hid = halves[0] + pltpu.roll(halves[1], n_chunks - 1, 0)
    out = _dot(_gelu(hid).astype(BF16), w2_ref[...])
    row = lax.broadcasted_iota(jnp.int32, (n_chunks, 128), 0)
    live = row < n_chunks - 1
    kc = _group_rms(out[:, 0:128], p64_ref[0:128, 0:128], HEAD_DIM, gk0_ref[...])
    return jnp.where(live, kc, 0.0), jnp.where(live, out[:, 128:256], 0.0)


def _compress_kernel(rk_ref, rv_ref, pe_ref, w1_ref, w2_ref, gk0_ref, p64_ref, kc_ref, vc_ref, *, n_chunks):
    load_j = lambda j: jnp.concatenate([rk_ref[0, pl.ds(j, n_chunks, stride=CMP_STRIDE), :],
                                        rv_ref[0, pl.ds(j, n_chunks, stride=CMP_STRIDE), :]], axis=1)
    kc, vc = _compress_core(load_j, n_chunks, pe_ref, w1_ref, w2_ref, gk0_ref, p64_ref)
    kc_ref[0] = kc.astype(BF16)
    vc_ref[0] = vc.T.astype(BF16)


def _compress(rows, pe, w1, w2, gk0, p64):
    b, t, _ = rows.shape
    n_chunks = t // CMP_STRIDE
    const = lambda shape: pl.BlockSpec(shape, lambda i: (0,) * len(shape))
    return pl.pallas_call(
        functools.partial(_compress_kernel, n_chunks=n_chunks),
        grid=(b,),
        in_specs=[pl.BlockSpec((1, t, 128), lambda i: (i, 0, 0)), pl.BlockSpec((1, t, 128), lambda i: (i, 0, 1)),
                  const((2, CMP_STRIDE, 1, 256)), const((2, CMP_STRIDE * 256, 256)), const((256, 256)),
                  const((1, 128)), const((512, 512))],
        out_specs=[pl.BlockSpec((1, n_chunks, 128), lambda i: (i, 0, 0)),
                   pl.BlockSpec((1, 128, n_chunks), lambda i: (i, 0, 0))],
        out_shape=[jax.ShapeDtypeStruct((b, n_chunks, 128), BF16), jax.ShapeDtypeStruct((b, 128, n_chunks), BF16)],
        compiler_params=_cparams(1),
    )(rows, rows, pe, w1, w2, gk0, p64)


def _masked_softmax(s, axis, exp_fn=jnp.exp):
    m = jnp.maximum(jnp.max(s, axis=axis, keepdims=True), MASKED_BELOW)
    p = exp_fn(s - m)
    return p / jnp.maximum(jnp.sum(p, axis=axis, keepdims=True), TINY)


def _select_blocks(score, top, axis):
    pos = lax.broadcasted_iota(jnp.int32, score.shape, axis).astype(F32)
    sel = jnp.zeros(score.shape, jnp.bool_)
    for _ in range(top):
        mx = jnp.max(score, axis=axis, keepdims=True)
        idx = jnp.min(jnp.where(score == mx, pos, 1e9), axis=axis, keepdims=True)
        hit = pos == idx
        sel = jnp.logical_or(sel, hit)
        score = jnp.where(hit, -3e38, score)
    return sel


def _select_with_forced(imp, forced, top, axis):
    n_forced = 3
    assert top > n_forced
    return jnp.logical_or(forced, _select_blocks(jnp.where(forced, -3e38, imp), top - n_forced, axis))


def _values_times_probs(vt_tiles, p):
    out = None
    start = 0
    for vt in vt_tiles:
        part = _dot(vt, p[start:start + vt.shape[1]])
        start += vt.shape[1]
        out = part if out is None else out + part
    return out


def _softmax_stats_t(s_ref, bias, m_ref, l_ref):
    a_parts, p_parts = [], []
    for g in range(s_ref.shape[1] // Q_BLOCK):
        cs = slice(Q_BLOCK * g, Q_BLOCK * (g + 1))
        sg = s_ref[:, cs]
        if bias is not None:
            sg = sg + bias(cs)
        m_old = m_ref[:, cs]
        m_new = jnp.maximum(m_old, jnp.max(sg, axis=0, keepdims=True))
        alpha = jnp.exp2(m_old - m_new)
        p = jnp.exp2(sg - m_new)
        m_ref[:, cs] = m_new
        l_ref[:, cs] = alpha * l_ref[:, cs] + jnp.sum(p, axis=0, keepdims=True)
        a_parts.append(alpha)
        p_parts.append(p.astype(BF16))
    return jnp.concatenate(a_parts, axis=1), jnp.concatenate(p_parts, axis=1)


def _acc_update_t(acc_ref, alpha, vt_tiles, p):
    acc_ref[...] = alpha * acc_ref[...] + _values_times_probs(vt_tiles, p)


def _attn_kernel(qt_ref, kc_ref, vct_ref, ka_ref, vst_ref, kw_ref, vwt_ref, gt_ref, tsel_ref, twin_ref, acmp_ref,
                 ovt_ref, o_ref, sa_scr, sb_scr, m_scr, l_scr, acc_scr, pend_a, pend_p, *, top):
    n = pl.program_id(1)
    cols = NSA_GROUP * Q_BLOCK
    n_far = jnp.maximum(n - 2, 0) // 2
    r_near = n - 2 * n_far
    heads = [_attn_head_setup(k, n, qt_ref, kc_ref, vct_ref, kw_ref, vwt_ref, twin_ref, acmp_ref, ovt_ref, top)
             for k in range(NSA_KV)]
    kv_heads = range(NSA_KV)

    def value_tile(k, tile):
        return [vst_ref[0, tile, HEAD_DIM * k:HEAD_DIM * (k + 1), :]]

    def issue_scores(k, tile, s_scr):
        s_scr[k] = _dot(ka_ref[0, pl.ds(pl.multiple_of(tile * FAR_TILE, FAR_TILE), FAR_TILE), :], heads[k][0])

    def stats(k, s_scr, half_idx=None):
        bias = None
        if half_idx is not None:
            bias = lambda cs: tsel_ref[k, r_near, FAR_TILE * half_idx:FAR_TILE * (half_idx + 1), cs]
        return _softmax_stats_t(s_scr.at[k], bias, m_scr.at[k], l_scr.at[k])

    def accumulate(k, tile, alpha, p):
        _acc_update_t(acc_scr.at[k], alpha, value_tile(k, tile), p)

    m_scr[...] = jnp.full(m_scr.shape, NEG, F32)
    l_scr[...] = jnp.zeros(l_scr.shape, F32)
    acc_scr[...] = jnp.zeros(acc_scr.shape, F32)
    pend_a[...] = jnp.ones(pend_a.shape, F32)
    pend_p[...] = jnp.zeros(pend_p.shape, BF16)
    odd = n_far % 2

    @pl.when(odd == 1)
    def _():
        for k in kv_heads:
            issue_scores(k, 0, sa_scr)
        for k in kv_heads:
            accumulate(k, 0, *stats(k, sa_scr))

    for k in kv_heads:
        issue_scores(k, odd, sa_scr)

    @pl.loop(0, n_far // 2)
    def _(j):
        t0 = odd + 2 * j
        for k in kv_heads:
            issue_scores(k, t0 + 1, sb_scr)
        first = [stats(k, sa_scr) for k in kv_heads]
        for k in kv_heads:
            accumulate(k, jnp.maximum(t0 - 1, 0), pend_a[k], pend_p[k])
            issue_scores(k, t0 + 2, sa_scr)
        second = [stats(k, sb_scr) for k in kv_heads]
        for k in kv_heads:
            accumulate(k, t0, *first[k])
            pend_a[k] = second[k][0]
            pend_p[k] = second[k][1]

    for k in kv_heads:
        accumulate(k, jnp.maximum(n_far - 1, 0), pend_a[k], pend_p[k])
        issue_scores(k, n_far + 1, sb_scr)
    for k in kv_heads:
        accumulate(k, n_far, *stats(k, sa_scr, 0))
    for k in kv_heads:
        accumulate(k, n_far + 1, *stats(k, sb_scr, 1))

    gtt = gt_ref[0].T
    for k in kv_heads:
        _, o_c, o_w = heads[k]
        o_s = acc_scr[k] / l_scr[k]

        def gate_row(branch):
            return jnp.concatenate([gtt[12 * k + 3 * g + branch:12 * k + 3 * g + branch + 1]
                                    for g in range(NSA_GROUP)], axis=1)

        o_t = gate_row(0) * o_c + gate_row(1) * o_s + gate_row(2) * o_w
        left = jnp.concatenate([o_t[:, 0:128], o_t[:, 128:256]], axis=0).T
        right = jnp.concatenate([o_t[:, 256:384], o_t[:, 384:512]], axis=0).T
        o_ref[0, :, 256 * k:256 * (k + 1)] = jnp.concatenate([left, right], axis=1).astype(BF16)


def _attn_head_setup(k, n, qt_ref, kc_ref, vct_ref, kw_ref, vwt_ref, twin_ref, acmp_ref, ovt_ref, top):
    cols = NSA_GROUP * Q_BLOCK
    vrows = slice(HEAD_DIM * k, HEAD_DIM * (k + 1))
    zero = jnp.zeros((HEAD_DIM, Q_BLOCK), BF16)
    parts = []
    for g in range(NSA_GROUP):
        h = NSA_GROUP * k + g
        piece = qt_ref[0, 0, HEAD_DIM * h:HEAD_DIM * (h + 1), :]
        parts.append(jnp.concatenate([piece, zero] if k == 0 else [zero, piece], axis=0))
    qt = jnp.concatenate(parts, axis=1)

    nc = kc_ref.shape[1]
    blocks_per_q = Q_BLOCK // CMP_STRIDE
    cp = lax.broadcasted_iota(jnp.int32, (nc, 128), 0) - blocks_per_q * n + 9
    fr = lax.broadcasted_iota(jnp.int32, (nc, 128), 1)
    feat = ((fr < 32) & (cp == (fr & 15))) | ((fr == 32) & (cp > 15))
    kc_aug = jnp.concatenate([kc_ref[0], jnp.where(feat, 1.0, 0.0).astype(BF16)], axis=1)
    s_c = _dot(kc_aug, jnp.concatenate([qt, acmp_ref[k]], axis=0))
    p_c = _masked_softmax(s_c, 0, jnp.exp2).astype(BF16)
    o_c = _dot(vct_ref[0, vrows, :], p_c)
    imp4 = _dot(ovt_ref[...], p_c)
    imp = imp4[:, 0:128] + imp4[:, 128:256] + imp4[:, 256:384] + imp4[:, 384:512]

    def value_tiles(ref, first, count):
        return [ref[0, first + d, vrows, :] for d in range(count)]

    wt = jnp.maximum(n - WINDOW // Q_BLOCK, 0)
    s_w = _dot(kw_ref[0, pl.ds(pl.multiple_of(wt * Q_BLOCK, Q_BLOCK), WIN_TILE), :], qt) + twin_ref[k, 0]
    p_w = jnp.exp2(s_w - jnp.max(s_w, axis=0, keepdims=True))
    o_w = (_values_times_probs(value_tiles(vwt_ref, wt, WIN_TILE // Q_BLOCK), p_w.astype(BF16))
           / jnp.sum(p_w, axis=0, keepdims=True))

    jj = lax.broadcasted_iota(jnp.int32, (128, Q_BLOCK), 0)
    tpos = n * Q_BLOCK + lax.broadcasted_iota(jnp.int32, (128, Q_BLOCK), 1)
    tblk = tpos // SEL_BLOCK
    forced = (jj == 0) | (jj == tblk) | (jj == tblk - 1)
    sel_t = _select_with_forced(jnp.where(jj * SEL_BLOCK <= tpos, imp, NEG), forced, top, 0)
    pen_t = jnp.where(sel_t, 0.0, SEL_PENALTY).astype(BF16)
    rhs = jnp.concatenate([qt, jnp.concatenate([pen_t] * NSA_GROUP, axis=1)], axis=0)
    return rhs, o_c, o_w


def _nsa_prompt(qt, kc, vct, ka, kw, vst, vwt, gates, tsel, twin, acmp, ovt):
    b, t, _ = ka.shape
    nqb = t // Q_BLOCK
    nc = kc.shape[1]
    top = min(SEL_TOP, -(-t // SEL_BLOCK))
    assert t >= WIN_TILE and t % FAR_TILE == 0 and t // SEL_BLOCK <= 128
    n_win = WINDOW // Q_BLOCK
    per_b = lambda shape: pl.BlockSpec((1,) + shape, lambda i, n: (i,) + (0,) * len(shape))
    const = lambda shape: pl.BlockSpec(shape, lambda i, n: (0,) * len(shape), pipeline_mode=pl.Buffered(1))
    cols = NSA_GROUP * Q_BLOCK
    per_kv = lambda rows, dt: pltpu.VMEM((NSA_KV, rows, cols), dt)
    return pl.pallas_call(
        functools.partial(_attn_kernel, top=top),
        grid=(b, nqb),
        in_specs=[pl.BlockSpec((1, 1, NSA_HEADS * HEAD_DIM, Q_BLOCK), lambda i, n: (i, n, 0, 0)),
                  per_b((nc, 128)), per_b((128, nc)),
                  per_b((t, 256)), per_b((t // FAR_TILE, 128, FAR_TILE)), per_b((t, 128)), per_b((nqb, 128, Q_BLOCK)),
                  pl.BlockSpec((1, Q_BLOCK, 128), lambda i, n: (i, n, 0)),
                  const((NSA_KV, 4, NEAR_TILE, 512)),
                  pl.BlockSpec((NSA_KV, 1, WIN_TILE, 512), lambda i, n: (0, jnp.minimum(n, n_win), 0, 0)),
                  const((NSA_KV, 128, 512)), const((128, nc))],
        out_specs=pl.BlockSpec((1, Q_BLOCK, 512), lambda i, n: (i, n, 0)),
        out_shape=jax.ShapeDtypeStruct((b, t, 512), BF16),
        scratch_shapes=[per_kv(FAR_TILE, F32), per_kv(FAR_TILE, F32), per_kv(1, F32), per_kv(1, F32),
                        per_kv(HEAD_DIM, F32), per_kv(1, F32), per_kv(FAR_TILE, BF16)],
        compiler_params=_cparams(2),
    )(qt, kc, vct, ka, vst, kw, vwt, gates, tsel, twin, acmp, ovt)


def _log_sigmoid(z):
    return jnp.minimum(z, 0.0) - jnp.log1p(jnp.exp(-jnp.abs(z)))


def _gla_kernel(x_ref, s0_ref, wg_ref, bg_ref, ggo_ref, tri_ref, o_ref, st_ref, s_scr, *, n_chunks):
    @pl.when(pl.program_id(0) == 0)
    def _():
        s_scr[...] = s0_ref[...]

    c_len = GLA_CHUNK
    ct = n_chunks * c_len
    tri = tri_ref[...]
    ti = lax.broadcasted_iota(jnp.int32, (ct, ct), 0)
    si = lax.broadcasted_iota(jnp.int32, (ct, ct), 1)
    causal = (si <= ti) & (si // c_len == ti // c_len)

    def per_chunk(x, row):
        return jnp.concatenate([jnp.broadcast_to(x[c_len * c + row:c_len * c + row + 1], (c_len, x.shape[1]))
                                for c in range(n_chunks)], axis=0)

    for bi in range(x_ref.shape[0]):
        q = x_ref[bi, :, 0:256]
        kk = x_ref[bi, :, 256:512]
        v = x_ref[bi, :, 512:1024]
        lr = x_ref[bi, :, 1024:1152]
        r = x_ref[bi, :, 1152:1664]
        la = _log_sigmoid(_dot(lr.astype(BF16), wg_ref[...]) + bg_ref[...]) * (1.0 / GLA_TAU)
        a1 = la.astype(BF16)
        r1 = la - a1.astype(F32)
        a2 = r1.astype(BF16)
        a3 = (r1 - a2.astype(F32)).astype(BF16)
        cb = _dot(tri, a1) + _dot(tri, a2) + _dot(tri, a3)
        last = per_chunk(cb, c_len - 1)
        mid = per_chunk(cb, c_len // 2)
        qe = (q * jnp.exp(cb)).astype(BF16)
        qa = (q * jnp.exp(jnp.minimum(cb - mid, EXP_CLAMP))).astype(BF16)
        kb = (kk * jnp.exp(jnp.minimum(mid - cb, EXP_CLAMP))).astype(BF16)
        ke = (kk * jnp.exp(last - cb)).astype(BF16)
        for h in range(GLA_HEADS):
            ks = slice(GLA_DK * h, GLA_DK * (h + 1))
            vs = slice(GLA_DV * h, GLA_DV * (h + 1))
            att = jnp.where(causal, _dot_nt(qa[:, ks], kb[:, ks]), 0.0)
            vh = v[:, vs].astype(BF16)
            o_intra = _dot(att.astype(BF16), vh)
            st = s_scr[bi, h]
            o_inter = []
            for c in range(n_chunks):
                rs = slice(c_len * c, c_len * (c + 1))
                o_inter.append(_dot_nt(qe[rs, ks], st.astype(BF16)))
                dec = jnp.exp(cb[c_len * (c + 1) - 1:c_len * (c + 1), ks])
                st = st * dec + _dot_tn(vh[rs], ke[rs, ks])
            s_scr[bi, h] = st
            on = _rms(o_intra + jnp.concatenate(o_inter, axis=0), ggo_ref[...])
            rh = r[:, vs]
            o_ref[bi, :, vs] = (on * (rh * _sigmoid(rh))).astype(BF16)
    st_ref[...] = s_scr[...]


def _gla_prompt(gla_in, s0t, wg, bg, ggo):
    b, t, w = gla_in.shape
    ct = min(256, t)
    assert t % ct == 0 and ct % GLA_CHUNK == 0
    pos = np.arange(ct)
    tri = jnp.asarray(((pos[None, :] <= pos[:, None])
                       & (pos[None, :] // GLA_CHUNK == pos[:, None] // GLA_CHUNK)).astype(np.float32), dtype=BF16)
    const = lambda shape: pl.BlockSpec(shape, lambda j: (0,) * len(shape))
    state_shape = (b, GLA_HEADS, GLA_DV, GLA_DK)
    return pl.pallas_call(
        functools.partial(_gla_kernel, n_chunks=ct // GLA_CHUNK),
        grid=(t // ct,),
        in_specs=[pl.BlockSpec((b, ct, w), lambda j: (0, j, 0)), const(state_shape),
                  const((128, 256)), const((1, 256)), const((1, 128)), const((ct, ct))],
        out_specs=[pl.BlockSpec((b, ct, 512), lambda j: (0, j, 0)), const(state_shape)],
        out_shape=[jax.ShapeDtypeStruct((b, t, 512), BF16), jax.ShapeDtypeStruct(state_shape, F32)],
        scratch_shapes=[pltpu.VMEM(state_shape, F32)],
        compiler_params=_cparams(1),
    )(gla_in, s0t, wg, bg, ggo, tri)


def _gla_step_kernel(q_ref, k_ref, lr_ref, v_ref, r_ref, s_ref, wgt_ref, bgt_ref, ggo_ref, o_ref, sn_ref):
    for bi, h in [(bi, h) for bi in range(q_ref.shape[0]) for h in range(GLA_HEADS)]:
        lr = lr_ref[bi]
        z = jnp.sum(wgt_ref[h] * lr, axis=-1, keepdims=True) + bgt_ref[h]
        a = jnp.exp(_log_sigmoid(z) * (1.0 / GLA_TAU))
        s0 = s_ref[bi, h]
        kh = k_ref[bi, h]
        qh = q_ref[bi, h]
        vh = v_ref[bi, h]
        sn_ref[bi, h] = a * s0 + kh * vh
        o = jnp.sum((qh * a) * s0, axis=0, keepdims=True) + jnp.sum(qh * kh, axis=0, keepdims=True) * vh
        on = _rms(o, ggo_ref[...])
        rh = r_ref[bi, h]
        o_ref[bi, h] = on * (rh * _sigmoid(rh))


def _gla_step(q_col, k_col, lr, v_row, r_row, s0, wgt, bgt, ggo):
    b = q_col.shape[0]
    bb = math.gcd(b, 8)
    const = lambda shape: pl.BlockSpec(shape, lambda i: (0,) * len(shape))
    per_b = lambda shape: pl.BlockSpec((bb,) + shape, lambda i: (i,) + (0,) * len(shape))
    return pl.pallas_call(
        _gla_step_kernel,
        grid=(b // bb,),
        in_specs=[per_b((GLA_HEADS, GLA_DK, 1)), per_b((GLA_HEADS, GLA_DK, 1)), per_b((1, 128)),
                  per_b((GLA_HEADS, 1, GLA_DV)), per_b((GLA_HEADS, 1, GLA_DV)), per_b((GLA_HEADS, GLA_DK, GLA_DV)),
                  const((GLA_HEADS, GLA_DK, 128)), const((GLA_HEADS, GLA_DK, 1)), const((1, 128))],
        out_specs=[per_b((GLA_HEADS, 1, GLA_DV)), per_b((GLA_HEADS, GLA_DK, GLA_DV))],
        out_shape=[jax.ShapeDtypeStruct((b, GLA_HEADS, 1, GLA_DV), F32),
                   jax.ShapeDtypeStruct((b, GLA_HEADS, GLA_DK, GLA_DV), F32)],
        compiler_params=_cparams(1),
    )(q_col, k_col, lr, v_row, r_row, s0, wgt, bgt, ggo)


def _xatt_kernel(xq_ref, mem_ref, o_ref):
    for bi in range(xq_ref.shape[0]):
        for h in range(X_HEADS):
            ls = slice(X_DIM * h, X_DIM * (h + 1))
            kh = mem_ref[bi, :, ls].astype(BF16)
            vh = mem_ref[bi, :, 512 + X_DIM * h:512 + X_DIM * (h + 1)].astype(BF16)
            s = _dot_nt(xq_ref[bi, :, ls], kh)
            p = jnp.exp(s - jnp.max(s, axis=-1, keepdims=True))
            p = p / jnp.sum(p, axis=-1, keepdims=True)
            o_ref[bi, :, ls] = _dot(p.astype(BF16), vh).astype(BF16)


def _xatt(xq, memkv):
    b, t, _ = xq.shape
    m = memkv.shape[1]
    tq = min(512, t)
    assert t % tq == 0
    bb = math.gcd(b, max(2, 128 // tq))
    return pl.pallas_call(
        _xatt_kernel,
        grid=(b // bb, t // tq),
        in_specs=[pl.BlockSpec((bb, tq, 512), lambda i, j: (i, j, 0)),
                  pl.BlockSpec((bb, m, 1024), lambda i, j: (i, 0, 0))],
        out_specs=pl.BlockSpec((bb, tq, 512), lambda i, j: (i, j, 0)),
        out_shape=jax.ShapeDtypeStruct((b, t, 512), BF16),
        compiler_params=_cparams(2),
    )(xq, memkv)


def _merge_kernel(on_ref, og_ref, ox_ref, mg_ref, x_ref, wn_ref, wg_ref, wx_ref, wo_ref, x1_ref):
    d = x_ref.shape[-1]
    merged = (mg_ref[:, 0:d].astype(F32) * _dot(on_ref[...], wn_ref[...])
              + mg_ref[:, d:2 * d].astype(F32) * _dot(og_ref[...], wg_ref[...])
              + mg_ref[:, 2 * d:3 * d].astype(F32) * _dot(ox_ref[...], wx_ref[...]))
    x1_ref[...] = x_ref[...] + _dot(merged.astype(BF16), wo_ref[...])


def _merge(o_nsa, o_gla, o_x, mg, x, wn, wg, wx, wo):
    m, d = x.shape
    tm = min(512, m)
    assert m % tm == 0
    row = lambda w: pl.BlockSpec((tm, w), lambda i: (i, 0))
    const = lambda shape: pl.BlockSpec(shape, lambda i: (0,) * len(shape))
    return pl.pallas_call(
        _merge_kernel,
        grid=(m // tm,),
        in_specs=[row(512), row(512), row(512), row(3 * d), row(d),
                  const((512, d)), const((512, d)), const((512, d)), const((d, d))],
        out_specs=row(d),
        out_shape=jax.ShapeDtypeStruct((m, d), F32),
        compiler_params=_cparams(1),
    )(o_nsa, o_gla, o_x, mg, x, wn, wg, wx, wo)


def _ffn_seq_kernel(x_ref, past_ref, g_ref, wup_ref, cw_ref, cb_ref, wdn_ref, y_ref, tail_ref, carry_ref):
    f = cw_ref.shape[-1]
    tm = x_ref.shape[1]

    @pl.when(pl.program_id(1) == 0)
    def _():
        carry_ref[...] = jnp.zeros(carry_ref.shape, F32)
        carry_ref[6:8, :] = past_ref[0]

    x1 = x_ref[0]
    ug = _dot(_rms(x1, g_ref[...]).astype(BF16), wup_ref[...])
    u = ug[:, 0:f]
    g = ug[:, f:2 * f]
    row = lax.broadcasted_iota(jnp.int32, (tm, f), 0)
    p1 = carry_ref[7:8, :]
    p2 = carry_ref[6:7, :]
    gm1 = jnp.where(row == 0, p1, pltpu.roll(g, 1, 0))
    gm2 = jnp.where(row == 0, p2, jnp.where(row == 1, p1, pltpu.roll(g, 2, 0)))
    gc = cb_ref[...] + cw_ref[0:1, :] * gm2 + cw_ref[1:2, :] * gm1 + cw_ref[2:3, :] * g
    y_ref[0] = x1 + _dot((_gelu(gc) * u).astype(BF16), wdn_ref[...])
    carry_ref[...] = g[tm - 8:tm, :]
    tail_ref[0] = g[tm - 8:tm, :]


def _ffn_seq(x1, conv_past, g_ffn, w_up, conv_w, conv_b, w_down):
    b, t, d = x1.shape
    f = conv_w.shape[-1]
    tm = min(512, t)
    assert t % tm == 0 and tm >= 8
    const = lambda shape: pl.BlockSpec(shape, lambda i, j: (0,) * len(shape), pipeline_mode=pl.Buffered(1))
    return pl.pallas_call(
        _ffn_seq_kernel,
        grid=(b, t // tm),
        in_specs=[pl.BlockSpec((1, tm, d), lambda i, j: (i, j, 0)),
                  pl.BlockSpec((1, 2, f), lambda i, j: (i, 0, 0)),
                  const((1, d)), const((d, 2 * f)), const((3, f)), const((1, f)), const((f, d))],
        out_specs=[pl.BlockSpec((1, tm, d), lambda i, j: (i, j, 0)),
                   pl.BlockSpec((1, 8, f), lambda i, j: (i, 0, 0))],
        out_shape=[jax.ShapeDtypeStruct((b, t, d), F32), jax.ShapeDtypeStruct((b, 8, f), F32)],
        scratch_shapes=[pltpu.VMEM((8, f), F32)],
        compiler_params=_cparams(2),
    )(x1, conv_past, g_ffn, w_up, conv_w, conv_b, w_down)


def _ffn_step_kernel(x_ref, p0_ref, p1_ref, g_ref, wup_ref, cw_ref, cb_ref, wdn_ref, y_ref, gnew_ref):
    f = cw_ref.shape[-1]
    x1 = x_ref[...]
    ug = _dot(_rms(x1, g_ref[...]).astype(BF16), wup_ref[...])
    u = ug[:, 0:f]
    g = ug[:, f:2 * f]
    gc = cb_ref[...] + cw_ref[0:1, :] * p0_ref[...] + cw_ref[1:2, :] * p1_ref[...] + cw_ref[2:3, :] * g
    y_ref[...] = x1 + _dot((_gelu(gc) * u).astype(BF16), wdn_ref[...])
    gnew_ref[...] = g


def _ffn_step(x1, p0, p1, g_ffn, w_up, conv_w, conv_b, w_down):
    m, d = x1.shape
    f = conv_w.shape[-1]
    full = lambda shape: pl.BlockSpec(shape, lambda i: (0,) * len(shape))
    return pl.pallas_call(
        _ffn_step_kernel,
        grid=(1,),
        in_specs=[full((m, d)), full((m, f)), full((m, f)), full((1, d)), full((d, 2 * f)), full((3, f)),
                  full((1, f)), full((f, d))],
        out_specs=[full((m, d)), full((m, f))],
        out_shape=[jax.ShapeDtypeStruct((m, d), F32), jax.ShapeDtypeStruct((m, f), F32)],
        compiler_params=_cparams(1),
    )(x1, p0, p1, g_ffn, w_up, conv_w, conv_b, w_down)


def _decode_kernel(pt_ref, *refs, top, n_sel, n_pages):
    del pt_ref
    page_refs = refs[:n_pages]
    (q8_ref, new_ref, neww_ref, cwin_ref, gt_ref, bc_ref, bs_ref, bw_ref, b0_ref, ovt_ref, et_ref, pe_ref, w1_ref,
     w2_ref, gk0_ref, p64_ref, o_ref, wout_ref, xs_ref) = refs[n_pages:]
    page = page_refs[0].shape[1]
    length = n_pages * page
    n_chunks = length // CMP_STRIDE
    for u, pg in enumerate(page_refs):
        for c in range(2):
            xs_ref[c, page * u:page * (u + 1), :] = pg[0, :, 128 * c:128 * (c + 1)]

    def load_j(j):
        return jnp.concatenate([xs_ref[c, pl.ds(j, n_chunks, stride=CMP_STRIDE), :] for c in range(2)], axis=1)

    def token_rows(lo):
        return jnp.concatenate([pg[0, :, lo:lo + 128] for pg in page_refs], axis=0)

    q8 = q8_ref[0]
    q8f = q8.astype(F32)
    rowk = lax.broadcasted_iota(jnp.int32, (8, 128), 0) // NSA_GROUP
    lane_half = lax.broadcasted_iota(jnp.int32, (8, 128), 1) // HEAD_DIM

    def half_mask(x):
        return jnp.where(rowk == lane_half, x, 0.0)

    kc, vc = _compress_core(load_j, n_chunks, pe_ref, w1_ref, w2_ref, gk0_ref, p64_ref)
    kc = kc.astype(BF16)
    vc = vc.astype(BF16)

    p_c = _masked_softmax(_dot_nt(q8, kc) + bc_ref[...], 1).astype(BF16)
    o_c = half_mask(_dot(p_c, vc))
    pf = p_c.astype(F32)
    p2 = jnp.concatenate([jnp.sum(pf[0:4], axis=0, keepdims=True), jnp.sum(pf[4:8], axis=0, keepdims=True),
                          jnp.zeros((126, n_chunks), F32)], axis=0)
    p2_hi = p2.astype(BF16)
    p2_lo = (p2 - p2_hi.astype(F32)).astype(BF16)
    imp_t = _dot_nt(ovt_ref[...], p2_hi) + _dot_nt(ovt_ref[...], p2_lo)
    nsr = imp_t.shape[0]
    nsp = -(-nsr // 128) * 128
    jj = lax.broadcasted_iota(jnp.int32, (nsr, 128), 0)
    tblk = length // SEL_BLOCK
    forced = (jj == 0) | (jj == tblk) | (jj == tblk - 1)
    sel_t = _select_with_forced(jnp.where(jj < n_sel, imp_t, -2e38), forced, top, 0)
    pen_t = jnp.concatenate([jnp.where(sel_t, 0.0, SEL_PENALTY), jnp.zeros((nsp - nsr, 128), F32)], axis=0)
    pen = pen_t.T
    pen8 = jnp.concatenate([jnp.broadcast_to(pen[0:1], (4, nsp)), jnp.broadcast_to(pen[1:2], (4, nsp))], axis=0)

    new = new_ref[0]
    neww = neww_ref[0]
    b0 = b0_ref[...]

    def attend(s_parts, s_new, v_parts, v_new):
        m = s_new
        for s in s_parts:
            m = jnp.maximum(m, jnp.max(s, axis=-1, keepdims=True))
        pn = jnp.exp(s_new - m)
        l = pn
        o = pn.astype(BF16).astype(F32) * v_new.astype(BF16).astype(F32)
        for s, v in zip(s_parts, v_parts):
            pp = jnp.exp(s - m)
            l = l + jnp.sum(pp, axis=-1, keepdims=True)
            o = o + _dot(pp.astype(BF16), v)
        return half_mask(o / l)

    def new_score(k_new):
        return jnp.sum(q8f * k_new.astype(BF16).astype(F32), axis=-1, keepdims=True) + b0

    k_past = token_rows(256).astype(BF16)
    v_past = token_rows(384).astype(BF16)
    s_past = _dot_nt(q8, k_past) + _dot_nt(pen8[:, 0:128].astype(BF16), et_ref[...]) + bs_ref[...]
    lane = lax.broadcasted_iota(jnp.int32, (8, nsp), 1)
    pen_new = jnp.sum(jnp.where(lane == tblk, pen8, 0.0), axis=-1, keepdims=True)
    o_s = attend([s_past], new_score(new[:, 256:384]) + pen_new, [v_past], new[:, 384:512])

    cw = cwin_ref[0]
    s_w = _dot_nt(q8, cw[:, 0:128].astype(BF16)) + bw_ref[...]
    o_w = attend([s_w], new_score(neww[:, 0:128]), [cw[:, 128:256].astype(BF16)], neww[:, 128:256])

    gt = gt_ref[0]
    o_ref[0] = gt[:, 0:1] * o_c + gt[:, 1:2] * o_s + gt[:, 2:3] * o_w

    wl = cw.shape[0]
    wrow = lax.broadcasted_iota(jnp.int32, cw.shape, 0)
    wout_ref[0] = jnp.where(wrow == wl - 1, neww, pltpu.roll(cw, wl - 1, 0))


def _nsa_decode(page_table, cache2d, q8, new_rows, new_win, cache_win, gates8, bc, bs, bw, b0, ov, et,
                pe, w1, w2, gk0, p64):
    db, n_pages = page_table.shape
    page = cache2d.shape[1]
    length = n_pages * page
    n_chunks = length // CMP_STRIDE
    n_sel = -(-(length + 1) // SEL_BLOCK)
    top = min(SEL_TOP, n_sel)
    wl = cache_win.shape[1]
    nsp = ov.shape[0]
    const = lambda shape: pl.BlockSpec(shape, lambda i, pt: (0,) * len(shape))
    per_b = lambda shape: pl.BlockSpec((1,) + shape, lambda i, pt: (i,) + (0,) * len(shape))
    page_spec = lambda u: pl.BlockSpec((1, page, 512), lambda i, pt: (pt[i, u], 0, 0))
    grid_spec = pltpu.PrefetchScalarGridSpec(
        num_scalar_prefetch=1,
        grid=(db,),
        in_specs=[page_spec(u) for u in range(n_pages)] + [
                  per_b((8, 128)), per_b((1, 512)), per_b((1, 256)), per_b((wl, 256)), per_b((8, 128)),
                  const((8, n_chunks)), const((8, length)), const((8, wl)), const((8, 1)),
                  const((nsp, n_chunks)), const((length, 128)),
                  const((2, CMP_STRIDE, 1, 256)), const((2, CMP_STRIDE * 256, 256)), const((256, 256)),
                  const((1, 128)), const((512, 512))],
        out_specs=[per_b((8, 128)), per_b((wl, 256))],
        scratch_shapes=[pltpu.VMEM((2, length, 128), F32)],
    )
    return pl.pallas_call(
        functools.partial(_decode_kernel, top=top, n_sel=n_sel, n_pages=n_pages),
        grid_spec=grid_spec,
        out_shape=[jax.ShapeDtypeStruct((db, 8, 128), F32), jax.ShapeDtypeStruct((db, wl, 256), F32)],
        compiler_params=_cparams(1),
    )(page_table, *([cache2d] * n_pages), q8, new_rows, new_win, cache_win, gates8, bc, bs, bw, b0, ov, et,
      pe, w1, w2, gk0, p64)


def _bucket_table():
    n = np.arange(MAX_DISTANCE + 1)
    max_exact = N_BUCKETS // 2
    nf = np.maximum(n, 1).astype(np.float32)
    large = max_exact + (np.log(nf / np.float32(max_exact)) / np.float32(math.log(MAX_DISTANCE / max_exact))
                         * np.float32(N_BUCKETS - max_exact)).astype(np.int32)
    return np.where(n < max_exact, n, np.minimum(large, N_BUCKETS - 1)).astype(np.int32)


def _bias_lookup(tb, rel, valid):
    idx = np.clip(rel, 0, MAX_DISTANCE)
    vals = jnp.moveaxis(tb[idx], -1, 0)
    return jnp.where(jnp.asarray(valid)[None], vals, NEG)


def _overlap(n_cmp_pad, n_cmp, n_sel_pad, n_sel):
    cs = (np.arange(n_cmp_pad) * CMP_STRIDE)[:, None]
    ss = (np.arange(n_sel_pad) * SEL_BLOCK)[None, :]
    ov = (cs < ss + SEL_BLOCK) & (cs + CMP_BLOCK > ss)
    ov &= (np.arange(n_cmp_pad) < n_cmp)[:, None] & (np.arange(n_sel_pad) < n_sel)[None, :]
    return jnp.asarray(ov.astype(np.float32), dtype=BF16)


def _stack_rows(x):
    return x.reshape(NSA_KV, NSA_GROUP * Q_BLOCK, x.shape[-1])


def _toeplitz(tbr, shift, width, max_valid):
    n = width + Q_BLOCK - 1
    u = np.arange(n)
    xs = shift - np.where(u < width, u, u - n)
    fvec = _bias_lookup(tbr, xs, (xs >= 0) & (xs <= max_valid))
    h = fvec.shape[0]
    return jnp.tile(fvec, (1, Q_BLOCK))[:, :Q_BLOCK * (n - 1)].reshape(h, Q_BLOCK, n - 1)[:, :, :width]


def _bias_descending(tb, top):
    far = jnp.broadcast_to(tb[MAX_DISTANCE][:, None], (tb.shape[1], top - MAX_DISTANCE + 1))
    return jnp.concatenate([far, tb[MAX_DISTANCE - 1:0:-1].T], axis=1)


def _prompt_tables(tb):
    tbr = (tb - tb[MAX_DISTANCE][None, :]) * LOG2E
    i = np.arange(Q_BLOCK)[:, None]

    def table(rel, valid):
        return _stack_rows(_bias_lookup(tbr, rel, valid))

    n_r = 4
    m_sel = _stack_rows(_toeplitz(tbr, Q_BLOCK * (n_r - 1), NEAR_TILE + Q_BLOCK * (n_r - 1), 1 << 30))
    tsel = jnp.stack([m_sel[:, :, Q_BLOCK * (n_r - 1 - r):Q_BLOCK * (n_r - 1 - r) + NEAR_TILE] for r in range(n_r)],
                     axis=1)
    n_v = WINDOW // Q_BLOCK + 1
    m_win = _stack_rows(_toeplitz(tbr, Q_BLOCK * (n_v - 1), WIN_TILE + Q_BLOCK * (n_v - 1), WINDOW - 1))
    twin = jnp.stack([m_win[:, :, Q_BLOCK * (n_v - 1 - v):Q_BLOCK * (n_v - 1 - v) + WIN_TILE] for v in range(n_v)],
                     axis=1)
    w = np.arange(16)[None, :] - 9
    rel = i - CMP_STRIDE * w - (CMP_BLOCK - 1)
    a = table(rel, rel >= 0)
    hi = a.astype(BF16)
    lo = (a - hi.astype(F32)).astype(BF16)
    acmp = jnp.concatenate([hi, lo, jnp.full(hi.shape[:2] + (1,), NEG, BF16),
                            jnp.zeros(hi.shape[:2] + (128 - 33,), BF16)], axis=-1)
    return jnp.swapaxes(tsel, 2, 3), jnp.swapaxes(twin, 2, 3), jnp.swapaxes(acmp, 1, 2)


def kernel(x_prompt, x_sample, cache_kv, cache_win, state_gla, state_conv, cache_mem, page_table, mem_prompt,
           g_mix, w_in, g_nsa_q, g_nsa_k, cmp_k_pe, cmp_k_w1, cmp_k_w2, cmp_v_pe, cmp_v_w1, cmp_v_w2,
           rel_bias, w_gla_gate, b_gla_gate, g_gla_o, g_mem, w_mem_kv, g_x_q, g_x_k,
           w_nsa_out, w_gla_out, w_x_out, w_o, g_ffn, w_up, conv_w, conv_b, w_down):
    bp, t, d = x_prompt.shape
    db = x_sample.shape[0]
    f = conv_w.shape[-1]

    offs = np.cumsum((0,) + IN_SIZES)
    segs = [w_in[:, offs[i]:offs[i + 1]] for i in range(len(IN_SIZES))]
    w_pad = jnp.concatenate([jnp.pad(s, ((0, 0), (0, pw - s.shape[1]))) for s, pw in zip(segs, PAD_SIZES)],
                            axis=1).astype(BF16)
    row = lambda v: v.reshape(1, -1).astype(F32)
    gq = row(jnp.tile(g_nsa_q, NSA_HEADS))
    gk0 = row(jnp.tile(g_nsa_k[0], NSA_KV))
    gk1 = row(jnp.tile(g_nsa_k[1], NSA_KV))
    gk2 = row(jnp.tile(g_nsa_k[2], NSA_KV))
    gxq = row(jnp.tile(g_x_q, X_HEADS))
    gxk = row(jnp.tile(g_x_k, X_HEADS))
    p64 = _block_ones(512, HEAD_DIM)
    p128 = _block_ones(512, X_DIM)
    bd2 = lambda a: jnp.concatenate([jnp.concatenate([a, jnp.zeros_like(a)], -1),
                                     jnp.concatenate([jnp.zeros_like(a), a], -1)], -2)
    bd4 = lambda a, c: jnp.concatenate([jnp.concatenate([bd2(a), jnp.zeros_like(bd2(a))], -1),
                                        jnp.concatenate([jnp.zeros_like(bd2(c)), bd2(c)], -1)], -2)
    pe = jnp.concatenate([jnp.tile(cmp_k_pe, (1, NSA_KV)), jnp.tile(cmp_v_pe, (1, NSA_KV))],
                         axis=-1).reshape(2, CMP_STRIDE, 1, 256)
    w1 = bd4(cmp_k_w1, cmp_v_w1).reshape(2, CMP_STRIDE * 256, 256).astype(BF16)
    w2 = bd4(cmp_k_w2, cmp_v_w2).astype(BF16)
    wg_pad = jnp.pad(w_gla_gate, ((0, 128 - GLA_RANK), (0, 0))).astype(BF16)
    tb = rel_bias.astype(F32)[_bucket_table()]
    wn, wgo, wx, wo = (w.astype(BF16) for w in (w_nsa_out, w_gla_out, w_x_out, w_o))
    wup = w_up.astype(BF16)
    wdn = w_down.astype(BF16)
    ggo = row(g_gla_o)

    rows_p, win_p, _, gates, gla_in, xq, mg, ka, kw, vst, vwt, qt = _proj_in(
        x_prompt, row(g_mix), w_pad, gq, gk1, gk2, gxq, p64, p128, True)
    memkv_p = _memory_kv(mem_prompt, row(g_mem), w_mem_kv.astype(BF16), gxk, p128)
    kc, vct = _compress(rows_p, pe, w1, w2, gk0, p64)
    n_chunks = t // CMP_STRIDE
    n_sel = -(-t // SEL_BLOCK)
    tsel, twin, acmp = _prompt_tables(tb)
    ovt = _overlap(n_chunks, n_chunks - 1, 128, n_sel).T
    o_nsa = _nsa_prompt(qt, kc, vct, ka, kw, vst, vwt, gates, tsel, twin, acmp, ovt)
    s0t = jnp.zeros((bp, GLA_HEADS, GLA_DV, GLA_DK), F32)
    o_gla, st = _gla_prompt(gla_in, s0t, wg_pad, row(b_gla_gate), ggo)
    o_x = _xatt(xq, memkv_p)
    m = bp * t
    x1 = _merge(o_nsa.reshape(m, 512), o_gla.reshape(m, 512), o_x.reshape(m, 512), mg.reshape(m, 3 * d),
                x_prompt.reshape(m, d), wn, wgo, wx, wo)
    y_p, tail = _ffn_seq(x1.reshape(bp, t, d), jnp.zeros((bp, 2, f), F32), row(g_ffn), wup, conv_w, row(conv_b), wdn)
    wl_p = min(WINDOW, t)
    out_rows_p = rows_p.reshape(bp, t, 4, NSA_KV, HEAD_DIM)
    out_win_p = win_p[:, t - wl_p:].reshape(bp, wl_p, 2, NSA_KV, HEAD_DIM)
    out_gla_p = jnp.swapaxes(st, 2, 3)
    out_conv_p = tail[:, 6:8]
    out_mem_p = memkv_p.reshape(bp, -1, 2, X_HEADS, X_DIM)

    n_pages = page_table.shape[1]
    page = cache_kv.shape[1]
    length = n_pages * page
    wl = cache_win.shape[1]
    rows_s, win_s, qn_s, gates_s, gla_s, xq_s, mg_s = (a[0] for a in _proj_in(
        x_sample.reshape(1, db, d), row(g_mix), w_pad, gq, gk1, gk2, gxq, p64, p128, False))

    eye = jnp.eye(NSA_KV, dtype=BF16)
    q8 = (qn_s.reshape(db, NSA_KV, NSA_GROUP, 1, HEAD_DIM) * eye[None, :, None, :, None]).reshape(db, 8, 128)
    gates8 = jnp.pad(gates_s[:, 0:24].reshape(db, 8, 3), ((0, 0), (0, 0), (0, 125)))
    n_chunks_s = length // CMP_STRIDE
    n_sel_s = -(-(length + 1) // SEL_BLOCK)
    nsp = -(-n_sel_s // 8) * 8
    cidx = np.arange(n_chunks_s)
    rel_c = length - (cidx * CMP_STRIDE + CMP_BLOCK - 1)
    bc = _bias_lookup(tb, rel_c, (rel_c >= 0) & (cidx < n_chunks_s - 1))
    bs = _bias_descending(tb, length)
    bw = jnp.where(jnp.asarray(np.arange(wl, 0, -1) < WINDOW)[None], _bias_descending(tb, wl), NEG)
    b0 = tb[0].reshape(8, 1)
    ov_s = _overlap(n_chunks_s, n_chunks_s - 1, nsp, n_sel_s).T
    assert length // SEL_BLOCK <= 128
    et = jnp.asarray((np.arange(length)[:, None] // SEL_BLOCK == np.arange(128)[None, :]).astype(np.float32),
                     dtype=BF16)
    o8, win_new = _nsa_decode(page_table, cache_kv.reshape(cache_kv.shape[0], page, 512), q8,
                              rows_s.reshape(db, 1, 512), win_s.reshape(db, 1, 256), cache_win.reshape(db, wl, 256),
                              gates8, bc, bs, bw, b0, ov_s, et, pe, w1, w2, gk0, p64)
    o8 = o8.reshape(db, NSA_KV, NSA_GROUP, NSA_KV, HEAD_DIM)
    o_nsa_s = jnp.stack([o8[:, 0, :, 0], o8[:, 1, :, 1]], axis=1).reshape(db, 512).astype(BF16)

    wgt = jnp.pad(w_gla_gate.T, ((0, 0), (0, 128 - GLA_RANK))).reshape(GLA_HEADS, GLA_DK, 128)
    o_gla_s, gla_state_s = _gla_step(
        gla_s[:, 0:256].reshape(db, GLA_HEADS, GLA_DK, 1), gla_s[:, 256:512].reshape(db, GLA_HEADS, GLA_DK, 1),
        gla_s[:, 1024:1152].reshape(db, 1, 128), gla_s[:, 512:1024].reshape(db, GLA_HEADS, 1, GLA_DV),
        gla_s[:, 1152:1664].reshape(db, GLA_HEADS, 1, GLA_DV), state_gla.astype(F32), wgt,
        b_gla_gate.reshape(GLA_HEADS, GLA_DK, 1), ggo)
    o_gla_s = o_gla_s.reshape(db, 512).astype(BF16)

    xq_pad = jnp.pad(xq_s.reshape(db, 1, 512), ((0, 0), (0, 15), (0, 0)))
    o_x_s = _xatt(xq_pad, cache_mem.reshape(db, -1, 1024))[:, 0]
    x1_s = _merge(o_nsa_s, o_gla_s, o_x_s, mg_s, x_sample.reshape(db, d), wn, wgo, wx, wo)
    y_s, g_new = _ffn_step(x1_s, state_conv[:, 0], state_conv[:, 1], row(g_ffn), wup, conv_w, row(conv_b), wdn)

    out_rows_s = rows_s.reshape(db, 1, 4, NSA_KV, HEAD_DIM)
    out_win_s = win_new.reshape(db, wl, 2, NSA_KV, HEAD_DIM)
    out_conv_s = jnp.stack([state_conv[:, 1], g_new], axis=1)
    return (y_p, y_s.reshape(db, 1, d), out_rows_p, out_win_p, out_gla_p, out_conv_p, out_mem_p,
            out_rows_s, out_win_s, gla_state_s, out_conv_s)
```
